```python
import jax, jax.numpy as jnp
from jax import lax
import numpy as np

D_MODEL = 1024
BATCH = 32
SEQ = 256
DEPTH = 2
DEC_BATCH = 8
DEC_SEQ = 1024
PAST_LEN = 256

GRID_W = 64
BRANCH_WIDTH = D_MODEL // 2
N_BRANCH = 3
NA_HEADS = 8
NA_HEAD_DIM = BRANCH_WIDTH // NA_HEADS
NA_WIDTH = NA_HEADS * NA_HEAD_DIM
NA_WIN_H_MAX = 8
NA_WIN_W = 16
NA_QCOL_BLOCK = 16
NA_KCOL_BLOCK = NA_QCOL_BLOCK + NA_WIN_W
SGU_GROUPS = 4
SGU_CHUNK = 128
SGU_WIDTH = BRANCH_WIDTH
SGU_GROUP_DIM = SGU_WIDTH // SGU_GROUPS
MLA_HEADS = 8
MLA_NOPE = 64
MLA_ROPE = 32
MLA_V = BRANCH_WIDTH // MLA_HEADS
MLA_WIDTH = MLA_HEADS * MLA_V
MLA_Q_LORA = 384
MLA_KV_LORA = 256
ROPE_THETA = 10000.0
Q_BLOCK = 128
IN_COLS = 4 * NA_WIDTH + 3 * SGU_WIDTH + MLA_Q_LORA + MLA_KV_LORA + MLA_ROPE + MLA_WIDTH + N_BRANCH * D_MODEL
EPS = 1e-6
NEG_INF = -1e30

kernel_name = 'hybrid_diffusion_na_sgu_mla_step'


def _rmsnorm(x, g):
    xf = x.astype(jnp.float32)
    y = xf * lax.rsqrt(jnp.mean(xf * xf, axis=-1, keepdims=True) + EPS)
    return (y * g.astype(jnp.float32)).astype(x.dtype)


def _layernorm(x):
    xf = x.astype(jnp.float32)
    mu = jnp.mean(xf, axis=-1, keepdims=True)
    var = jnp.mean(jnp.square(xf - mu), axis=-1, keepdims=True)
    return ((xf - mu) * lax.rsqrt(var + EPS)).astype(x.dtype)


def _modulation(cond, w_mod, b_mod):
    m = jnp.einsum('...d,de->...e', jax.nn.silu(cond), w_mod) + b_mod
    return jnp.split(m, 3, axis=-1)


def _split_cols(p):
    sizes = (NA_WIDTH,) * 4 + (SGU_WIDTH,) * 3 + (MLA_Q_LORA, MLA_KV_LORA, MLA_ROPE, MLA_WIDTH, N_BRANCH * D_MODEL)
    return jnp.split(p, np.cumsum(sizes)[:-1].tolist(), axis=-1)


def _pre_mixer(x, shift, scale, norm_g, w_in):
    h = _rmsnorm(x, norm_g) * (1 + scale) + shift
    return _split_cols(jnp.einsum('bnd,de->bne', h, w_in))


def _axial_rope(n_tokens):
    pos = jnp.arange(n_tokens, dtype=jnp.int32)
    row = (pos // GRID_W).astype(jnp.float32)
    col = (pos % GRID_W).astype(jnp.float32)
    n_freq = MLA_ROPE // 4
    inv = ROPE_THETA ** (-jnp.arange(n_freq, dtype=jnp.float32) / n_freq)
    ang = jnp.concatenate([row[:, None] * inv, col[:, None] * inv], axis=-1)
    return jnp.cos(ang), jnp.sin(ang)


def _rope(x, cos, sin):
    x1, x2 = jnp.split(x.astype(jnp.float32), 2, axis=-1)
    return jnp.concatenate([x1 * cos - x2 * sin, x1 * sin + x2 * cos], axis=-1).astype(x.dtype)


def _over_query_blocks(fn, q):
    B, N = q.shape[:2]
    qb = jnp.moveaxis(q.reshape((B, N // Q_BLOCK, Q_BLOCK) + q.shape[2:]), 1, 0)
    o = lax.map(fn, qb)
    return jnp.moveaxis(o, 0, 1).reshape((B, N) + o.shape[3:])


def _context_attention(q, k, v):
    scale = q.shape[-1] ** -0.5

    def block(qi):
        s = jnp.einsum('bqhd,bmhd->bhqm', qi, k).astype(jnp.float32) * scale
        p = jax.nn.softmax(s, axis=-1).astype(v.dtype)
        return jnp.einsum('bhqm,bmhd->bqhd', p, v)

    o = _over_query_blocks(block, q)
    return o.reshape(o.shape[0], o.shape[1], -1)


def _na_latent_attention(q, k, v, k_ctx, v_ctx, rpb):
    B, N, H, dh = q.shape
    rows = N // GRID_W
    wh = min(NA_WIN_H_MAX, rows)
    n_cb = GRID_W // NA_QCOL_BLOCK
    r = np.arange(rows)
    row_idx = np.clip(r - wh // 2, 0, rows - wh)[:, None] + np.arange(wh)[None, :]
    qcol = np.arange(GRID_W).reshape(n_cb, NA_QCOL_BLOCK)
    col_idx = np.clip(qcol[:, 0] - NA_WIN_W // 2, 0, GRID_W - NA_KCOL_BLOCK)[:, None] + np.arange(NA_KCOL_BLOCK)[None, :]
    win_lo = np.clip(qcol - NA_WIN_W // 2, 0, GRID_W - NA_WIN_W)
    in_win = (col_idx[:, None, :] >= win_lo[:, :, None]) & (col_idx[:, None, :] < win_lo[:, :, None] + NA_WIN_W)
    rel_r = row_idx - r[:, None] + NA_WIN_H_MAX - 1
    rel_c = np.clip(col_idx[:, None, :] - qcol[:, :, None] + NA_WIN_W - 1, 0, 2 * NA_WIN_W - 2)
    bias = rpb.astype(jnp.float32)[:, rel_r[:, None, None, :, None], rel_c[None, :, :, None, :]]
    bias = jnp.where(in_win[None, None, :, :, None, :], bias, NEG_INF)
    qb = q.reshape(B, rows, n_cb, NA_QCOL_BLOCK, H, dh)
    gather_r = row_idx[:, :, None, None]
    gather_c = col_idx[None, None, :, :]
    kg = k.reshape(B, rows, GRID_W, H, dh)[:, gather_r, gather_c]
    vg = v.reshape(B, rows, GRID_W, H, dh)[:, gather_r, gather_c]
    scale = dh ** -0.5
    s_loc = jnp.einsum('brjqhd,brwjkhd->bhrjqwk', qb, kg).astype(jnp.float32) * scale + bias
    s_ctx = jnp.einsum('brjqhd,bmhd->bhrjqm', qb, k_ctx).astype(jnp.float32) * scale
    n_loc = wh * NA_KCOL_BLOCK
    s = jnp.concatenate([s_loc.reshape(B, H, rows, n_cb, NA_QCOL_BLOCK, n_loc), s_ctx], axis=-1)
    p = jax.nn.softmax(s, axis=-1).astype(v.dtype)
    p_loc = p[..., :n_loc].reshape(B, H, rows, n_cb, NA_QCOL_BLOCK, wh, NA_KCOL_BLOCK)
    p_ctx = p[..., n_loc:]
    o = jnp.einsum('bhrjqwk,brwjkhd->brjqhd', p_loc, vg) + jnp.einsum('bhrjqm,bmhd->brjqhd', p_ctx, v_ctx)
    return o.reshape(B, N, H * dh)


def _sgu(u, v, w_s, b_s):
    B, N, _ = u.shape
    nc = N // SGU_CHUNK
    u = jax.nn.gelu(u)
    v = _layernorm(jax.nn.gelu(v))
    vg = v.reshape(B, nc, SGU_CHUNK, SGU_GROUPS, SGU_GROUP_DIM)
    mixed = jnp.einsum('gpq,bnqgc->bnpgc', w_s, vg) + b_s.T[:, :, None]
    return u * mixed.reshape(B, N, SGU_WIDTH)


def _mla_q(dq, q_norm, w_uq):
    B, N, _ = dq.shape
    q = jnp.einsum('bnr,re->bne', _rmsnorm(dq, q_norm), w_uq)
    return q.reshape(B, N, MLA_HEADS, MLA_NOPE + MLA_ROPE)


def _mla_expand(c_kv, w_ukv):
    B, L, _ = c_kv.shape
    kv = jnp.einsum('blr,re->ble', c_kv, w_ukv).reshape(B, L, MLA_HEADS, MLA_NOPE + MLA_V)
    return kv[..., :MLA_NOPE], kv[..., MLA_NOPE:]


def _mla_attention(q, k_nope, k_rope, v):
    scale = (MLA_NOPE + MLA_ROPE) ** -0.5

    def block(qi):
        s = (jnp.einsum('bqhd,bmhd->bhqm', qi[..., :MLA_NOPE], k_nope)
             + jnp.einsum('bqhr,bmr->bhqm', qi[..., MLA_NOPE:], k_rope)).astype(jnp.float32) * scale
        p = jax.nn.softmax(s, axis=-1).astype(v.dtype)
        return jnp.einsum('bhqm,bmhd->bqhd', p, v)

    o = _over_query_blocks(block, q)
    return o.reshape(o.shape[0], o.shape[1], MLA_WIDTH)


def _merge(ys, zs, merge_logits, w_branch, w_out):
    B, N, _ = merge_logits.shape
    gated = jnp.stack([y * jax.nn.silu(z) for y, z in zip(ys, zs)], axis=2)
    branch = jnp.einsum('bnkw,kwd->bnkd', gated, w_branch)
    gates = jax.nn.sigmoid(merge_logits).reshape(B, N, N_BRANCH, D_MODEL)
    merged = jnp.sum(gates * branch, axis=2)
    return jnp.einsum('bnd,de->bne', merged, w_out)


def _context_layer(x, c_ctx, lp):
    norm_g, w_mod, b_mod, w_in, na_rpb, sgu_w, sgu_b, q_norm, w_uq, kv_norm, w_ukv, w_branch, w_out = lp
    shift, scale, gate = _modulation(c_ctx, w_mod, b_mod)
    na_q, na_k, na_v, na_z, su, sv, sz, dq, dkv, kr, mz, mg = _pre_mixer(x, shift, scale, norm_g, w_in)
    B, L, _ = x.shape
    hs = (B, L, NA_HEADS, NA_HEAD_DIM)
    k_c, v_c = na_k.reshape(hs), na_v.reshape(hs)
    y_na = _context_attention(na_q.reshape(hs), k_c, v_c)
    y_sgu = _sgu(su, sv, sgu_w, sgu_b)
    q = _mla_q(dq, q_norm, w_uq)
    ckv = _rmsnorm(dkv, kv_norm)
    k_nope, v = _mla_expand(ckv, w_ukv)
    y_mla = _mla_attention(q, k_nope, kr, v)
    out = _merge((y_na, y_sgu, y_mla), (na_z, sz, mz), mg, w_branch, w_out)
    return x + gate * out, k_c, v_c, ckv, kr


def _latent_layer(x, c, k_ctx, v_ctx, ckv_ctx, kr_ctx, cos, sin, lp):
    norm_g, w_mod, b_mod, w_in, na_rpb, sgu_w, sgu_b, q_norm, w_uq, kv_norm, w_ukv, w_branch, w_out = lp
    shift, scale, gate = (m[:, None, :] for m in _modulation(c, w_mod, b_mod))
    na_q, na_k, na_v, na_z, su, sv, sz, dq, dkv, kr, mz, mg = _pre_mixer(x, shift, scale, norm_g, w_in)
    B, N, _ = x.shape
    hs = (B, N, NA_HEADS, NA_HEAD_DIM)
    y_na = _na_latent_attention(na_q.reshape(hs), na_k.reshape(hs), na_v.reshape(hs), k_ctx, v_ctx, na_rpb)
    y_sgu = _sgu(su, sv, sgu_w, sgu_b)
    q = _mla_q(dq, q_norm, w_uq)
    q = jnp.concatenate([q[..., :MLA_NOPE], _rope(q[..., MLA_NOPE:], cos[:, None, :], sin[:, None, :])], axis=-1)
    ckv_all = jnp.concatenate([ckv_ctx, _rmsnorm(dkv, kv_norm)], axis=1)
    kr_all = jnp.concatenate([kr_ctx, _rope(kr, cos, sin)], axis=1)
    k_nope, v = _mla_expand(ckv_all, w_ukv)
    y_mla = _mla_attention(q, k_nope, kr_all, v)
    out = _merge((y_na, y_sgu, y_mla), (na_z, sz, mz), mg, w_branch, w_out)
    return x + gate * out


def setup_inputs(seed: int = 0) -> dict:
    key = jax.random.key(seed)
    ks = jax.random.split(key, 24)
    f32 = jnp.float32

    def nrm(k, shape, s):
        return jax.random.normal(k, shape, f32) * s

    return {
        'x_prompt': nrm(ks[0], (BATCH, SEQ, D_MODEL), 1.0),
        'x_sample': nrm(ks[1], (DEC_BATCH, DEC_SEQ, D_MODEL), 1.0),
        'cache_na_k': nrm(ks[2], (DEC_BATCH, DEPTH, PAST_LEN, NA_HEADS, NA_HEAD_DIM), 1.0),
        'cache_na_v': nrm(ks[3], (DEC_BATCH, DEPTH, PAST_LEN, NA_HEADS, NA_HEAD_DIM), 1.0),
        'cache_mla_ckv': nrm(ks[4], (DEC_BATCH, DEPTH, PAST_LEN, MLA_KV_LORA), 1.0),
        'cache_mla_krope': nrm(ks[5], (DEC_BATCH, DEPTH, PAST_LEN, MLA_ROPE), 1.0),
        'c': nrm(ks[6], (DEC_BATCH, D_MODEL), 1.0),
        'c_ctx': nrm(ks[7], (D_MODEL,), 1.0),
        'norm_g': 1.0 + nrm(ks[8], (DEPTH, D_MODEL), 0.02),
        'w_mod': nrm(ks[9], (DEPTH, D_MODEL, 3 * D_MODEL), D_MODEL ** -0.5),
        'b_mod': nrm(ks[10], (DEPTH, 3 * D_MODEL), 0.02),
        'w_in': nrm(ks[11], (DEPTH, D_MODEL, IN_COLS), D_MODEL ** -0.5),
        'na_rpb': nrm(ks[12], (DEPTH, NA_HEADS, 2 * NA_WIN_H_MAX - 1, 2 * NA_WIN_W - 1), 0.1),
        'sgu_w': nrm(ks[13], (DEPTH, SGU_GROUPS, SGU_CHUNK, SGU_CHUNK), SGU_CHUNK ** -0.5),
        'sgu_b': nrm(ks[14], (DEPTH, SGU_GROUPS, SGU_CHUNK), 0.02),
        'mla_q_norm': 1.0 + nrm(ks[15], (DEPTH, MLA_Q_LORA), 0.02),
        'mla_w_uq': nrm(ks[16], (DEPTH, MLA_Q_LORA, MLA_HEADS * (MLA_NOPE + MLA_ROPE)), MLA_Q_LORA ** -0.5),
        'mla_kv_norm': 1.0 + nrm(ks[17], (DEPTH, MLA_KV_LORA), 0.02),
        'mla_w_ukv': nrm(ks[18], (DEPTH, MLA_KV_LORA, MLA_HEADS * (MLA_NOPE + MLA_V)), MLA_KV_LORA ** -0.5),
        'w_branch': nrm(ks[19], (DEPTH, N_BRANCH, BRANCH_WIDTH, D_MODEL), BRANCH_WIDTH ** -0.5),
        'w_out': nrm(ks[20], (DEPTH, D_MODEL, D_MODEL), D_MODEL ** -0.5),
        'final_norm_g': 1.0 + nrm(ks[21], (D_MODEL,), 0.02),
    }


def reference(x_prompt, x_sample, cache_na_k, cache_na_v, cache_mla_ckv, cache_mla_krope, c, c_ctx,
              norm_g, w_mod, b_mod, w_in, na_rpb, sgu_w, sgu_b, mla_q_norm, mla_w_uq, mla_kv_norm,
              mla_w_ukv, w_branch, w_out, final_norm_g):
    xp = x_prompt
    xs = x_sample
    cos, sin = _axial_rope(x_sample.shape[1])
    ks_, vs_, ckvs_, krs_ = [], [], [], []
    for l in range(DEPTH):
        lp = (norm_g[l], w_mod[l], b_mod[l], w_in[l], na_rpb[l], sgu_w[l], sgu_b[l], mla_q_norm[l],
              mla_w_uq[l], mla_kv_norm[l], mla_w_ukv[l], w_branch[l], w_out[l])
        xp, k_c, v_c, ckv_c, kr_c = _context_layer(xp, c_ctx, lp)
        ks_.append(k_c)
        vs_.append(v_c)
        ckvs_.append(ckv_c)
        krs_.append(kr_c)
        xs = _latent_layer(xs, c, cache_na_k[:, l], cache_na_v[:, l], cache_mla_ckv[:, l],
                           cache_mla_krope[:, l], cos, sin, lp)
    y_prompt = _rmsnorm(xp, final_norm_g)
    y_sample = _rmsnorm(xs, final_norm_g)
    state_na_k = jnp.stack(ks_, axis=1)
    state_na_v = jnp.stack(vs_, axis=1)
    state_mla_ckv = jnp.stack(ckvs_, axis=1)
    state_mla_krope = jnp.stack(krs_, axis=1)
    return (y_prompt, y_sample, state_na_k, state_na_v, state_mla_ckv, state_mla_krope)
```

```python
import functools

import numpy as np
import jax
import jax.numpy as jnp
from jax import lax
from jax.experimental import pallas as pl
from jax.experimental.pallas import tpu as pltpu

F32 = jnp.float32
BF16 = jnp.bfloat16

D_MODEL = 1024
DEPTH = 2
GRID_W = 64
BRANCH_WIDTH = 512
N_BRANCH = 3
NA_HEADS = 8
NA_HEAD_DIM = 64
NA_WIN_H = 8
NA_WIN_W = 16
SGU_GROUPS = 4
SGU_CHUNK = 128
MLA_HEADS = 8
MLA_NOPE = 64
MLA_ROPE = 32
MLA_V = 64
MLA_Q_LORA = 384
MLA_KV_LORA = 256
ROPE_THETA = 10000.0
EPS = 1e-6
NEG_INF = -1e30

LANES = 128
TILE = 256
HEAD_PAIRS = NA_HEADS // 2
NA_KEY_ROWS = 12
NA_LOC_KEYS = NA_KEY_ROWS * GRID_W
ROPE_LANE0 = MLA_NOPE

C_NAQ, C_NAK, C_NAV, C_NAZ, C_SU, C_SV, C_SZ, C_MZ, C_MG = (512 * i for i in (0, 1, 2, 3, 4, 5, 6, 7, 8))
C_DQ = C_MG + N_BRANCH * D_MODEL
C_DKV = C_DQ + MLA_Q_LORA
C_KR = C_DKV + MLA_KV_LORA
W_COLS = C_KR + LANES
P_COLS = C_DQ + 512
CHUNK = 512

VMEM_LIMIT = 56 * 1024 * 1024


def _dot(a, b):
    return jnp.dot(a, b, preferred_element_type=F32)


def _dot_nt(a, b):
    return lax.dot_general(a, b, (((1,), (1,)), ((), ())), preferred_element_type=F32)


def _rms(x, g):
    return x * lax.rsqrt(jnp.mean(x * x, axis=-1, keepdims=True) + EPS) * g


def _rope128(x, cos, sin):
    lane = lax.broadcasted_iota(jnp.int32, x.shape, 1)
    swapped = jnp.where(lane < ROPE_LANE0 + MLA_ROPE // 2,
                        pltpu.roll(x, LANES - MLA_ROPE // 2, axis=1),
                        pltpu.roll(x, MLA_ROPE // 2, axis=1))
    return x * cos + swapped * sin


def _softmax_pv(scores, values):
    m = scores[0].max(axis=-1, keepdims=True)
    for s in scores[1:]:
        m = jnp.maximum(m, s.max(axis=-1, keepdims=True))
    den = None
    acc = None
    for s, v in zip(scores, values):
        e = jnp.exp(s - m)
        d = e.sum(axis=-1, keepdims=True)
        o = _dot(e.astype(BF16), v)
        den = d if den is None else den + d
        acc = o if acc is None else acc + o
    return acc / den


def _const_spec(shape, index):
    return pl.BlockSpec(shape, lambda *_: index, pipeline_mode=pl.Buffered(1))


def _params(semantics):
    return pltpu.CompilerParams(dimension_semantics=semantics, vmem_limit_bytes=VMEM_LIMIT)


def _mod_kernel(c_ref, w_ref, b_ref, o_ref):
    c = c_ref[...]
    s = c * jax.nn.sigmoid(c)
    o_ref[...] = jnp.dot(s, w_ref[...], preferred_element_type=F32, precision=lax.Precision.HIGHEST) + b_ref[...]


def _modulation(cond, w_mod, b_mod):
    rows = cond.shape[0]
    return pl.pallas_call(
        _mod_kernel,
        grid=(DEPTH, 3),
        in_specs=[pl.BlockSpec((rows, D_MODEL), lambda l, j: (0, 0)),
                  pl.BlockSpec((None, D_MODEL, D_MODEL), lambda l, j: (l, 0, j)),
                  pl.BlockSpec((None, 1, D_MODEL), lambda l, j: (l, 0, j))],
        out_specs=pl.BlockSpec((None, rows, D_MODEL), lambda l, j: (l, 0, j)),
        out_shape=jax.ShapeDtypeStruct((DEPTH, rows, 3 * D_MODEL), F32),
        compiler_params=_params(("parallel", "parallel")),
        name="modulation",
    )(cond, w_mod, b_mod.reshape(DEPTH, 1, 3 * D_MODEL))


def _inproj_kernel(*refs, emit_state, rope, n_alias):
    x_ref, mod_ref, g_ref, w_ref, kvn_ref = refs[:5]
    refs = refs[5:]
    if rope:
        cos_ref, sin_ref = refs[:2]
        refs = refs[2:]
    refs = refs[n_alias:]
    p_ref, ckv_ref, kr_ref = refs[:3]
    if emit_state:
        sk_ref, sv_ref, sckv_ref, skr_ref = refs[3:7]

    mod = mod_ref[...]
    shift = mod[:, 0:D_MODEL]
    scale = mod[:, D_MODEL:2 * D_MODEL]
    h = (_rms(x_ref[...], g_ref[...]) * (1.0 + scale) + shift).astype(BF16)

    for c0 in range(0, P_COLS, CHUNK):
        acc = _dot(h, w_ref[:, c0:c0 + CHUNK])
        p_ref[:, c0:c0 + CHUNK] = acc.astype(BF16)
        if emit_state and c0 == C_NAK:
            sk_ref[...] = acc
        if emit_state and c0 == C_NAV:
            sv_ref[...] = acc

    ckv = _rms(_dot(h, w_ref[:, C_DKV:C_KR]), kvn_ref[...])
    ckv_ref[...] = ckv.astype(BF16)
    kr = _dot(h, w_ref[:, C_KR:W_COLS])
    if emit_state:
        sckv_ref[...] = ckv
        skr_ref[...] = kr[:, ROPE_LANE0:ROPE_LANE0 + MLA_ROPE]
    if rope:
        kr = _rope128(kr, cos_ref[...], sin_ref[...])
    kr_ref[...] = kr.astype(BF16)


def _inproj(x, mod, mod_row, layer, norm_g, w_packed, kv_norm, rope_tabs, state_in, emit_state):
    n_tok = x.shape[0]
    n_tiles = n_tok // TILE
    rope = rope_tabs is not None
    in_specs = [pl.BlockSpec((TILE, D_MODEL), lambda i: (i, 0)),
                pl.BlockSpec((None, None, 1, 3 * D_MODEL), lambda i: (layer, mod_row(i), 0, 0)),
                _const_spec((None, 1, D_MODEL), (layer, 0, 0)),
                _const_spec((None, D_MODEL, W_COLS), (layer, 0, 0)),
                _const_spec((None, 1, MLA_KV_LORA), (layer, 0, 0))]
    args = [x, mod, norm_g, w_packed, kv_norm]
    if rope:
        tiles_per_seq = rope_tabs[0].shape[0] // TILE
        in_specs += [pl.BlockSpec((TILE, LANES), lambda i: (i % tiles_per_seq, 0))] * 2
        args += list(rope_tabs)
    out_specs = [pl.BlockSpec((TILE, P_COLS), lambda i: (i, 0)),
                 pl.BlockSpec((TILE, MLA_KV_LORA), lambda i: (i, 0)),
                 pl.BlockSpec((TILE, LANES), lambda i: (i, 0))]
    out_shape = [jax.ShapeDtypeStruct((n_tok, P_COLS), BF16),
                 jax.ShapeDtypeStruct((n_tok, MLA_KV_LORA), BF16),
                 jax.ShapeDtypeStruct((n_tok, LANES), BF16)]
    aliases = {}
    if emit_state:
        widths = (BRANCH_WIDTH, BRANCH_WIDTH, MLA_KV_LORA, MLA_ROPE)
        for k, w in enumerate(widths):
            out_specs.append(pl.BlockSpec((None, None, TILE, w), lambda i: (i, layer, 0, 0)))
            out_shape.append(jax.ShapeDtypeStruct((n_tiles, DEPTH, TILE, w), F32))
        if state_in is not None:
            first = len(args)
            in_specs += [pl.BlockSpec(memory_space=pl.ANY)] * 4
            args += list(state_in)
            aliases = {first + k: 3 + k for k in range(4)}
    kern = functools.partial(_inproj_kernel, emit_state=emit_state, rope=rope, n_alias=len(aliases))
    return pl.pallas_call(
        kern, grid=(n_tiles,), in_specs=in_specs, out_specs=out_specs, out_shape=out_shape,
        input_output_aliases=aliases, compiler_params=_params(("parallel",)),
        name="inproj",
    )(*args)


def _na_heads(q_ref, o_ref, score_fn, value_list_fn):
    lane = lax.broadcasted_iota(jnp.int32, (TILE, LANES), 1)
    low = lane < NA_HEAD_DIM
    for hp in range(HEAD_PAIRS):
        sl = slice(LANES * hp, LANES * (hp + 1))
        q2 = q_ref[:, sl] * (NA_HEAD_DIM ** -0.5)
        values = value_list_fn(sl)
        outs = []
        for half in range(2):
            qm = jnp.where(low if half == 0 else jnp.logical_not(low), q2, jnp.zeros_like(q2))
            outs.append(_softmax_pv(score_fn(qm, sl, 2 * hp + half), values))
        o_ref[:, sl] = jnp.where(low, outs[0], outs[1]).astype(BF16)


def _na_ctx_kernel(q_ref, k_ref, v_ref, o_ref):
    _na_heads(q_ref, o_ref,
              lambda qm, sl, h: [_dot_nt(qm, k_ref[:, sl])],
              lambda sl: [v_ref[:, sl]])


def _na_lat_kernel(q_ref, k_ref, v_ref, kc_ref, vc_ref, bias_ref, o_ref):
    t = pl.program_id(0)
    start = pl.multiple_of(jnp.where(t < 2, 0, TILE), TILE)
    rows = pl.ds(start, NA_LOC_KEYS)

    def scores(qm, sl, h):
        return [_dot_nt(qm, k_ref[rows, sl]) + bias_ref[h], _dot_nt(qm, kc_ref[:, sl].astype(BF16))]

    _na_heads(q_ref, o_ref, scores, lambda sl: [v_ref[rows, sl], vc_ref[:, sl].astype(BF16)])


def _na_ctx(p, n_seq):
    blk = lambda c: pl.BlockSpec((TILE, BRANCH_WIDTH), lambda i: (i, c))
    return pl.pallas_call(
        _na_ctx_kernel, grid=(n_seq,),
        in_specs=[blk(C_NAQ // 512), blk(C_NAK // 512), blk(C_NAV // 512)],
        out_specs=pl.BlockSpec((TILE, BRANCH_WIDTH), lambda i: (i, 0)),
        out_shape=jax.ShapeDtypeStruct((n_seq * TILE, BRANCH_WIDTH), BF16),
        compiler_params=_params(("parallel",)), name="na_ctx",
    )(p, p, p)


def _na_lat(p, cache_k, cache_v, bias, layer, n_seq, seq):
    tiles = seq // TILE
    kv = lambda c: pl.BlockSpec((seq, BRANCH_WIDTH), lambda t, b: (b, c))
    cache = pl.BlockSpec((None, None, cache_k.shape[2], BRANCH_WIDTH), lambda t, b: (b, layer, 0, 0))
    return pl.pallas_call(
        _na_lat_kernel, grid=(tiles, n_seq),
        in_specs=[pl.BlockSpec((TILE, BRANCH_WIDTH), lambda t, b: (b * tiles + t, C_NAQ // 512)),
                  kv(C_NAK // 512), kv(C_NAV // 512), cache, cache,
                  pl.BlockSpec((None, NA_HEADS, TILE, NA_LOC_KEYS), lambda t, b: (t, 0, 0, 0),
                               pipeline_mode=pl.Buffered(1))],
        out_specs=pl.BlockSpec((TILE, BRANCH_WIDTH), lambda t, b: (b * tiles + t, 0)),
        out_shape=jax.ShapeDtypeStruct((n_seq * seq, BRANCH_WIDTH), BF16),
        compiler_params=_params(("parallel", "parallel")), name="na_lat",
    )(p, p, p, cache_k, cache_v, bias)


def _na_bias(rpb, seq):
    tiles = seq // TILE
    rows_per_tile = TILE // GRID_W
    n_rows = seq // GRID_W
    out = []
    for t in range(tiles):
        key_row0 = 0 if t < 2 else TILE // GRID_W
        rq = t * rows_per_tile + np.arange(TILE) // GRID_W
        cq = np.arange(TILE) % GRID_W
        rk = key_row0 + np.arange(NA_LOC_KEYS) // GRID_W
        ck = np.arange(NA_LOC_KEYS) % GRID_W
        row_lo = np.clip(rq - NA_WIN_H // 2, 0, n_rows - NA_WIN_H)
        col_lo = np.clip(cq - NA_WIN_W // 2, 0, GRID_W - NA_WIN_W)
        ok = ((rk[None, :] >= row_lo[:, None]) & (rk[None, :] < row_lo[:, None] + NA_WIN_H)
              & (ck[None, :] >= col_lo[:, None]) & (ck[None, :] < col_lo[:, None] + NA_WIN_W))
        rel_r = np.clip(rk[None, :] - rq[:, None] + NA_WIN_H - 1, 0, 2 * NA_WIN_H - 2)
        rel_c = np.clip(ck[None, :] - cq[:, None] + NA_WIN_W - 1, 0, 2 * NA_WIN_W - 2)
        out.append(jnp.where(ok[None], rpb[:, rel_r, rel_c], NEG_INF))
    return jnp.stack(out)


def _mla_kernel(*refs, n_cache, n_lat, rope):
    dq_ref, ckv_ref, kr_ref = refs[:3]
    refs = refs[3:]
    if n_cache:
        cckv_ref, ckr_ref = refs[:2]
        refs = refs[2:]
    qn_ref, wuq_ref, wk_ref, wv_ref = refs[:4]
    refs = refs[4:]
    if rope:
        cos_ref, sin_ref = refs[:2]
        refs = refs[2:]
    o_ref, kx_ref, vx_ref = refs

    @pl.when(pl.program_id(1) == 0)
    def _expand_keys():
        def fill(r0, ckv, kr):
            n = ckv.shape[0]
            kk = _dot(ckv, wk_ref[...])
            for h in range(MLA_HEADS):
                sl = slice(LANES * h, LANES * (h + 1))
                kx_ref[r0:r0 + n, sl] = (kk[:, sl] + kr).astype(BF16)
            vx_ref[r0:r0 + n, :] = _dot(ckv, wv_ref[...]).astype(BF16)

        if n_cache:
            fill(0, cckv_ref[...].astype(BF16), ckr_ref[...])
        for r0 in range(0, n_lat, TILE):
            fill(n_cache + r0, ckv_ref[r0:r0 + TILE, :], kr_ref[r0:r0 + TILE, :].astype(F32))

    dqn = _rms(dq_ref[:, :MLA_Q_LORA].astype(F32), qn_ref[...])
    q = _dot(dqn.astype(BF16), wuq_ref[...])
    lane = lax.broadcasted_iota(jnp.int32, (TILE, LANES), 1)
    low = lane < MLA_V
    scale = (MLA_NOPE + MLA_ROPE) ** -0.5
    for hp in range(HEAD_PAIRS):
        v2 = vx_ref[:, LANES * hp:LANES * (hp + 1)]
        outs = []
        for half in range(2):
            sl = slice(LANES * (2 * hp + half), LANES * (2 * hp + half + 1))
            qh = q[:, sl]
            if rope:
                qh = _rope128(qh, cos_ref[...], sin_ref[...])
            qh = (qh * scale).astype(BF16)
            outs.append(_softmax_pv([_dot_nt(qh, kx_ref[:, sl])], [v2]))
        o_ref[:, LANES * hp:LANES * (hp + 1)] = jnp.where(low, outs[0], outs[1]).astype(BF16)


def _mla(p, ckv, kr, cache_ckv, cache_kr, layer, q_norm, w_uq, w_k, w_v, rope_tabs, n_seq, seq):
    tiles = seq // TILE
    n_cache = 0 if cache_ckv is None else cache_ckv.shape[2]
    rope = rope_tabs is not None
    in_specs = [pl.BlockSpec((TILE, CHUNK), lambda b, t: (b * tiles + t, C_DQ // CHUNK)),
                pl.BlockSpec((seq, MLA_KV_LORA), lambda b, t: (b, 0)),
                pl.BlockSpec((seq, LANES), lambda b, t: (b, 0))]
    args = [p, ckv, kr]
    if n_cache:
        in_specs += [pl.BlockSpec((None, None, n_cache, MLA_KV_LORA), lambda b, t: (b, layer, 0, 0)),
                     pl.BlockSpec((None, None, n_cache, LANES), lambda b, t: (b, layer, 0, 0))]
        args += [cache_ckv, cache_kr]
    in_specs += [_const_spec((None, 1, MLA_Q_LORA), (layer, 0, 0)),
                 _const_spec((None, MLA_Q_LORA, MLA_HEADS * LANES), (layer, 0, 0)),
                 _const_spec((None, MLA_KV_LORA, MLA_HEADS * LANES), (layer, 0, 0)),
                 _const_spec((None, MLA_KV_LORA, MLA_HEADS * MLA_V), (layer, 0, 0))]
    args += [q_norm, w_uq, w_k, w_v]
    if rope:
        in_specs += [pl.BlockSpec((TILE, LANES), lambda b, t: (t, 0))] * 2
        args += list(rope_tabs)
    n_keys = n_cache + seq
    return pl.pallas_call(
        functools.partial(_mla_kernel, n_cache=n_cache, n_lat=seq, rope=rope),
        grid=(n_seq, tiles), in_specs=in_specs,
        out_specs=pl.BlockSpec((TILE, BRANCH_WIDTH), lambda b, t: (b * tiles + t, 0)),
        out_shape=jax.ShapeDtypeStruct((n_seq * seq, BRANCH_WIDTH), BF16),
        scratch_shapes=[pltpu.VMEM((n_keys, MLA_HEADS * LANES), BF16), pltpu.VMEM((n_keys, MLA_HEADS * MLA_V), BF16)],
        compiler_params=_params(("parallel", "arbitrary")), name="mla",
    )(*args)


def _merge_kernel(x_ref, mod_ref, yna_ref, ymla_ref, naz_ref, su_ref, sv_ref, sz_ref, mz_ref, mg0_ref, mg1_ref, mg2_ref,
                  sw_ref, sb_ref, wb_ref, wo_ref, fg_ref, o_ref, *, final):
    u = jax.nn.gelu(su_ref[...].astype(F32))
    v = jax.nn.gelu(sv_ref[...].astype(F32))
    mu = jnp.mean(v, axis=-1, keepdims=True)
    vc = v - mu
    vn = (vc * lax.rsqrt(jnp.mean(vc * vc, axis=-1, keepdims=True) + EPS)).astype(BF16)
    rows = []
    for c0 in range(0, TILE, SGU_CHUNK):
        cols = [_dot(sw_ref[g], vn[c0:c0 + SGU_CHUNK, LANES * g:LANES * (g + 1)]) for g in range(SGU_GROUPS)]
        rows.append(jnp.concatenate(cols, axis=1) + sb_ref[...])
    y_sgu = u * jnp.concatenate(rows, axis=0)

    def silu(z_ref):
        z = z_ref[...].astype(F32)
        return z * jax.nn.sigmoid(z)

    branches = ((yna_ref[...].astype(F32), naz_ref, mg0_ref), (y_sgu, sz_ref, mg1_ref),
                (ymla_ref[...].astype(F32), mz_ref, mg2_ref))
    merged = None
    for k, (y, z_ref, mg_ref) in enumerate(branches):
        br = _dot((y * silu(z_ref)).astype(BF16), wb_ref[k])
        term = jax.nn.sigmoid(mg_ref[...].astype(F32)) * br
        merged = term if merged is None else merged + term
    out = _dot(merged.astype(BF16), wo_ref[...])
    gate = mod_ref[:, 2 * D_MODEL:3 * D_MODEL]
    xn = x_ref[...] + gate * out
    if final:
        xn = _rms(xn, fg_ref[...])
    o_ref[...] = xn


def _merge(x, mod, mod_row, layer, p, y_na, y_mla, sgu_w, sgu_b, w_branch, w_out, final_g, final):
    n_tok = x.shape[0]
    half = lambda c: pl.BlockSpec((TILE, BRANCH_WIDTH), lambda i: (i, c))
    full = lambda c: pl.BlockSpec((TILE, D_MODEL), lambda i: (i, c))
    in_specs = [full(0),
                pl.BlockSpec((None, None, 1, 3 * D_MODEL), lambda i: (layer, mod_row(i), 0, 0)),
                half(0), half(0),
                half(C_NAZ // 512), half(C_SU // 512), half(C_SV // 512), half(C_SZ // 512), half(C_MZ // 512),
                full(C_MG // D_MODEL), full(C_MG // D_MODEL + 1), full(C_MG // D_MODEL + 2),
                _const_spec((None, SGU_GROUPS, SGU_CHUNK, SGU_CHUNK), (layer, 0, 0, 0)),
                _const_spec((None, SGU_CHUNK, BRANCH_WIDTH), (layer, 0, 0)),
                _const_spec((None, N_BRANCH, BRANCH_WIDTH, D_MODEL), (layer, 0, 0, 0)),
                _const_spec((None, D_MODEL, D_MODEL), (layer, 0, 0)),
                _const_spec((1, D_MODEL), (0, 0))]
    return pl.pallas_call(
        functools.partial(_merge_kernel, final=final),
        grid=(n_tok // TILE,), in_specs=in_specs,
        out_specs=full(0), out_shape=jax.ShapeDtypeStruct((n_tok, D_MODEL), F32),
        compiler_params=_params(("parallel",)), name="merge",
    )(x, mod, y_na, y_mla, p, p, p, p, p, p, p, p, sgu_w, sgu_b, w_branch, w_out, final_g)


def _pack_params(w_in, sgu_w, sgu_b, mla_w_uq, mla_w_ukv, w_branch, w_out):
    o = np.cumsum([0, 2048, 1536, MLA_Q_LORA, MLA_KV_LORA, MLA_ROPE, BRANCH_WIDTH, N_BRANCH * D_MODEL])
    na, sgu, dq, dkv, kr, mz, mg = (w_in[:, :, o[i]:o[i + 1]] for i in range(7))
    zeros = lambda n: jnp.zeros((DEPTH, D_MODEL, n), w_in.dtype)
    w_packed = jnp.concatenate([na, sgu, mz, mg, dq, dkv, zeros(ROPE_LANE0), kr,
                                zeros(LANES - ROPE_LANE0 - MLA_ROPE)], axis=2).astype(BF16)
    uq = mla_w_uq.reshape(DEPTH, MLA_Q_LORA, MLA_HEADS, MLA_NOPE + MLA_ROPE)
    uq = jnp.pad(uq, ((0, 0), (0, 0), (0, 0), (0, LANES - MLA_NOPE - MLA_ROPE)))
    uq = uq.reshape(DEPTH, MLA_Q_LORA, MLA_HEADS * LANES).astype(BF16)
    ukv = mla_w_ukv.reshape(DEPTH, MLA_KV_LORA, MLA_HEADS, MLA_NOPE + MLA_V)
    w_k = jnp.pad(ukv[..., :MLA_NOPE], ((0, 0), (0, 0), (0, 0), (0, LANES - MLA_NOPE)))
    w_k = w_k.reshape(DEPTH, MLA_KV_LORA, MLA_HEADS * LANES).astype(BF16)
    w_v = ukv[..., MLA_NOPE:].reshape(DEPTH, MLA_KV_LORA, MLA_HEADS * MLA_V).astype(BF16)
    sgu_bias = jnp.repeat(jnp.swapaxes(sgu_b, 1, 2), BRANCH_WIDTH // SGU_GROUPS, axis=2)
    return w_packed, uq, w_k, w_v, sgu_w.astype(BF16), sgu_bias, w_branch.astype(BF16), w_out.astype(BF16)


def _rope_tables(n_tokens):
    pos = jnp.arange(n_tokens, dtype=jnp.int32)
    row = (pos // GRID_W).astype(F32)
    col = (pos % GRID_W).astype(F32)
    n_freq = MLA_ROPE // 4
    inv = ROPE_THETA ** (-jnp.arange(n_freq, dtype=F32) / n_freq)
    ang = jnp.concatenate([row[:, None] * inv, col[:, None] * inv], axis=-1)
    cos, sin = jnp.cos(ang), jnp.sin(ang)
    pad_l, pad_r = ROPE_LANE0, LANES - ROPE_LANE0 - MLA_ROPE
    cos_t = jnp.pad(jnp.concatenate([cos, cos], axis=1), ((0, 0), (pad_l, pad_r)), constant_values=1.0)
    sin_t = jnp.pad(jnp.concatenate([-sin, sin], axis=1), ((0, 0), (pad_l, pad_r)))
    return cos_t, sin_t


def kernel(x_prompt, x_sample, cache_na_k, cache_na_v, cache_mla_ckv, cache_mla_krope, c, c_ctx, norm_g, w_mod, b_mod, w_in, na_rpb, sgu_w, sgu_b, mla_q_norm, mla_w_uq, mla_kv_norm, mla_w_ukv, w_branch, w_out, final_norm_g):
    n_ctx, ctx_len, _ = x_prompt.shape
    n_lat, lat_len, _ = x_sample.shape
    past = cache_na_k.shape[2]
    assert ctx_len == TILE and lat_len % TILE == 0 and lat_len // GRID_W == 16 and past == TILE

    w_packed, w_uq, w_k, w_v, sgu_wb, sgu_bias, w_br, w_o = _pack_params(
        w_in, sgu_w, sgu_b, mla_w_uq, mla_w_ukv, w_branch, w_out)
    rope_tabs = _rope_tables(lat_len)
    norm_g3 = norm_g.reshape(DEPTH, 1, D_MODEL)
    kv_norm3 = mla_kv_norm.reshape(DEPTH, 1, MLA_KV_LORA)
    q_norm3 = mla_q_norm.reshape(DEPTH, 1, MLA_Q_LORA)
    final_g = final_norm_g.reshape(1, D_MODEL)

    cond_rows = 16
    cond = jnp.concatenate([c, c_ctx[None, :], jnp.zeros((cond_rows - n_lat - 1, D_MODEL), F32)], axis=0)
    mod = _modulation(cond, w_mod, b_mod).reshape(DEPTH, cond_rows, 1, 3 * D_MODEL)
    ctx_row = lambda i: n_lat
    lat_tiles = lat_len // TILE
    lat_row = lambda i: i // lat_tiles

    cache_k = cache_na_k.reshape(n_lat, DEPTH, past, BRANCH_WIDTH)
    cache_v = cache_na_v.reshape(n_lat, DEPTH, past, BRANCH_WIDTH)
    cache_kr = jnp.pad(cache_mla_krope, ((0, 0), (0, 0), (0, 0), (ROPE_LANE0, LANES - ROPE_LANE0 - MLA_ROPE)))

    xp = x_prompt.reshape(n_ctx * ctx_len, D_MODEL)
    xs = x_sample.reshape(n_lat * lat_len, D_MODEL)
    state = None
    for l in range(DEPTH):
        final = l == DEPTH - 1
        p, ckv, kr, *state = _inproj(xp, mod, ctx_row, l, norm_g3, w_packed, kv_norm3, None, state, True)
        y_na = _na_ctx(p, n_ctx)
        y_mla = _mla(p, ckv, kr, None, None, l, q_norm3, w_uq, w_k, w_v, None, n_ctx, ctx_len)
        xp = _merge(xp, mod, ctx_row, l, p, y_na, y_mla, sgu_wb, sgu_bias, w_br, w_o, final_g, final)
        p, ckv, kr = _inproj(xs, mod, lat_row, l, norm_g3, w_packed, kv_norm3, rope_tabs, None, False)
        y_na = _na_lat(p, cache_k, cache_v, _na_bias(na_rpb[l], lat_len), l, n_lat, lat_len)
        y_mla = _mla(p, ckv, kr, cache_mla_ckv, cache_kr, l, q_norm3, w_uq, w_k, w_v, rope_tabs, n_lat, lat_len)
        xs = _merge(xs, mod, lat_row, l, p, y_na, y_mla, sgu_wb, sgu_bias, w_br, w_o, final_g, final)

    s_k, s_v, s_ckv, s_kr = state
    return (xp.reshape(n_ctx, ctx_len, D_MODEL), xs.reshape(n_lat, lat_len, D_MODEL),
            s_k.reshape(n_ctx, DEPTH, ctx_len, NA_HEADS, NA_HEAD_DIM),
            s_v.reshape(n_ctx, DEPTH, ctx_len, NA_HEADS, NA_HEAD_DIM), s_ckv, s_kr)
```

```python
import functools

import numpy as np
import jax
import jax.numpy as jnp
from jax import lax
from jax.experimental import pallas as pl
from jax.experimental.pallas import tpu as pltpu

F32 = jnp.float32
BF16 = jnp.bfloat16

D_MODEL = 1024
DEPTH = 2
GRID_W = 64
BRANCH_WIDTH = 512
N_BRANCH = 3
NA_HEADS = 8
NA_HEAD_DIM = 64
NA_WIN_H = 8
NA_WIN_W = 16
SGU_GROUPS = 4
SGU_CHUNK = 128
MLA_HEADS = 8
MLA_NOPE = 64
MLA_ROPE = 32
MLA_V = 64
MLA_Q_LORA = 384
MLA_KV_LORA = 256
ROPE_THETA = 10000.0
EPS = 1e-6
NEG_INF = -1e30

LANES = 128
TILE = 256
HEAD_PAIRS = NA_HEADS // 2
NA_KEY_ROWS = 12
NA_LOC_KEYS = NA_KEY_ROWS * GRID_W
ROWS_PER_TILE = TILE // GRID_W
TAB_PAD = 4
TAB_BLOCKS = 24
ROPE_LANE0 = MLA_NOPE

C_NAQ, C_NAK, C_NAV, C_NAZ, C_SU, C_SV, C_SZ, C_MZ, C_MG = (512 * i for i in (0, 1, 2, 3, 4, 5, 6, 7, 8))
C_DQ = C_MG + N_BRANCH * D_MODEL
C_DKV = C_DQ + MLA_Q_LORA
C_KR = C_DKV + MLA_KV_LORA
W_COLS = C_KR + LANES
P_COLS = C_DQ + 512
CHUNK = 512

VMEM_LIMIT = 56 * 1024 * 1024


def _dot(a, b):
    return jnp.dot(a, b, preferred_element_type=F32)


def _dot_nt(a, b):
    return lax.dot_general(a, b, (((1,), (1,)), ((), ())), preferred_element_type=F32)


def _rms(x, g):
    return x * lax.rsqrt(jnp.mean(x * x, axis=-1, keepdims=True) + EPS) * g


def _rope128(x, cos, sin):
    lane = lax.broadcasted_iota(jnp.int32, x.shape, 1)
    swapped = jnp.where(lane < ROPE_LANE0 + MLA_ROPE // 2,
                        pltpu.roll(x, LANES - MLA_ROPE // 2, axis=1),
                        pltpu.roll(x, MLA_ROPE // 2, axis=1))
    return x * cos + swapped * sin


def _softmax_pv(scores, values):
    m = scores[0].max(axis=-1, keepdims=True)
    for s in scores[1:]:
        m = jnp.maximum(m, s.max(axis=-1, keepdims=True))
    den = None
    acc = None
    for s, v in zip(scores, values):
        e = jnp.exp(s - m)
        d = e.sum(axis=-1, keepdims=True)
        o = _dot(e.astype(BF16), v)
        den = d if den is None else den + d
        acc = o if acc is None else acc + o
    return acc / den


def _const_spec(shape, index):
    return pl.BlockSpec(shape, lambda *_: index, pipeline_mode=pl.Buffered(1))


def _params(semantics):
    return pltpu.CompilerParams(dimension_semantics=semantics, vmem_limit_bytes=VMEM_LIMIT)


def _mod_kernel(c_ref, w_ref, b_ref, o_ref):
    c = c_ref[...]
    s = c * jax.nn.sigmoid(c)
    o_ref[...] = jnp.dot(s, w_ref[...], preferred_element_type=F32, precision=lax.Precision.HIGHEST) + b_ref[...]


def _modulation(cond, w_mod, b_mod):
    rows = cond.shape[0]
    return pl.pallas_call(
        _mod_kernel,
        grid=(DEPTH, 3),
        in_specs=[pl.BlockSpec((rows, D_MODEL), lambda l, j: (0, 0)),
                  pl.BlockSpec((None, D_MODEL, D_MODEL), lambda l, j: (l, 0, j)),
                  pl.BlockSpec((None, 1, D_MODEL), lambda l, j: (l, 0, j))],
        out_specs=pl.BlockSpec((None, rows, D_MODEL), lambda l, j: (l, 0, j)),
        out_shape=jax.ShapeDtypeStruct((DEPTH, rows, 3 * D_MODEL), F32),
        compiler_params=_params(("parallel", "parallel")),
        name="modulation",
    )(cond, w_mod, b_mod.reshape(DEPTH, 1, 3 * D_MODEL))


def _inproj_kernel(*refs, emit_state, rope, n_alias):
    x_ref, mod_ref, g_ref, w_ref, kvn_ref = refs[:5]
    refs = refs[5:]
    if rope:
        cos_ref, sin_ref = refs[:2]
        refs = refs[2:]
    refs = refs[n_alias:]
    p_ref, ckv_ref, kr_ref = refs[:3]
    if emit_state:
        sk_ref, sv_ref, sckv_ref, skr_ref = refs[3:7]

    mod = mod_ref[...]
    shift = mod[:, 0:D_MODEL]
    scale = mod[:, D_MODEL:2 * D_MODEL]
    h = (_rms(x_ref[...], g_ref[...]) * (1.0 + scale) + shift).astype(BF16)

    for c0 in range(0, P_COLS, CHUNK):
        acc = _dot(h, w_ref[:, c0:c0 + CHUNK])
        p_ref[:, c0:c0 + CHUNK] = acc.astype(BF16)
        if emit_state and c0 == C_NAK:
            sk_ref[...] = acc
        if emit_state and c0 == C_NAV:
            sv_ref[...] = acc

    ckv = _rms(_dot(h, w_ref[:, C_DKV:C_KR]), kvn_ref[...])
    ckv_ref[...] = ckv.astype(BF16)
    kr = _dot(h, w_ref[:, C_KR:W_COLS])
    if emit_state:
        sckv_ref[...] = ckv
        skr_ref[...] = kr[:, ROPE_LANE0:ROPE_LANE0 + MLA_ROPE]
    if rope:
        kr = _rope128(kr, cos_ref[...], sin_ref[...])
    kr_ref[...] = kr.astype(BF16)


def _inproj(x, mod, mod_row, layer, norm_g, w_packed, kv_norm, rope_tabs, state_in, emit_state):
    n_tok = x.shape[0]
    n_tiles = n_tok // TILE
    rope = rope_tabs is not None
    in_specs = [pl.BlockSpec((TILE, D_MODEL), lambda i: (i, 0)),
                pl.BlockSpec((None, None, 1, 3 * D_MODEL), lambda i: (layer, mod_row(i), 0, 0)),
                _const_spec((None, 1, D_MODEL), (layer, 0, 0)),
                _const_spec((None, D_MODEL, W_COLS), (layer, 0, 0)),
                _const_spec((None, 1, MLA_KV_LORA), (layer, 0, 0))]
    args = [x, mod, norm_g, w_packed, kv_norm]
    if rope:
        tiles_per_seq = rope_tabs[0].shape[0] // TILE
        in_specs += [pl.BlockSpec((TILE, LANES), lambda i: (i % tiles_per_seq, 0))] * 2
        args += list(rope_tabs)
    out_specs = [pl.BlockSpec((TILE, P_COLS), lambda i: (i, 0)),
                 pl.BlockSpec((TILE, MLA_KV_LORA), lambda i: (i, 0)),
                 pl.BlockSpec((TILE, LANES), lambda i: (i, 0))]
    out_shape = [jax.ShapeDtypeStruct((n_tok, P_COLS), BF16),
                 jax.ShapeDtypeStruct((n_tok, MLA_KV_LORA), BF16),
                 jax.ShapeDtypeStruct((n_tok, LANES), BF16)]
    aliases = {}
    if emit_state:
        widths = (BRANCH_WIDTH, BRANCH_WIDTH, MLA_KV_LORA, MLA_ROPE)
        for k, w in enumerate(widths):
            out_specs.append(pl.BlockSpec((None, None, TILE, w), lambda i: (i, layer, 0, 0)))
            out_shape.append(jax.ShapeDtypeStruct((n_tiles, DEPTH, TILE, w), F32))
        if state_in is not None:
            first = len(args)
            in_specs += [pl.BlockSpec(memory_space=pl.ANY)] * 4
            args += list(state_in)
            aliases = {first + k: 3 + k for k in range(4)}
    kern = functools.partial(_inproj_kernel, emit_state=emit_state, rope=rope, n_alias=len(aliases))
    return pl.pallas_call(
        kern, grid=(n_tiles,), in_specs=in_specs, out_specs=out_specs, out_shape=out_shape,
        input_output_aliases=aliases, compiler_params=_params(("parallel",)),
        name="inproj",
    )(*args)


def _na_heads(q_ref, o_ref, score_fn, value_list_fn):
    lane = lax.broadcasted_iota(jnp.int32, (TILE, LANES), 1)
    low = lane < NA_HEAD_DIM
    for hp in range(HEAD_PAIRS):
        sl = slice(LANES * hp, LANES * (hp + 1))
        q2 = q_ref[:, sl] * (NA_HEAD_DIM ** -0.5)
        values = value_list_fn(sl)
        outs = []
        for half in range(2):
            qm = jnp.where(low if half == 0 else jnp.logical_not(low), q2, jnp.zeros_like(q2))
            outs.append(_softmax_pv(score_fn(qm, sl, 2 * hp + half), values))
        o_ref[:, sl] = jnp.where(low, outs[0], outs[1]).astype(BF16)


def _na_ctx_kernel(q_ref, k_ref, v_ref, o_ref):
    _na_heads(q_ref, o_ref,
              lambda qm, sl, h: [_dot_nt(qm, k_ref[:, sl])],
              lambda sl: [v_ref[:, sl]])


def _na_band_row0(tile_row0, n_rows):
    return min(max(tile_row0 - NA_WIN_H // 2, 0), n_rows - NA_KEY_ROWS)


def _na_band_plans(seq):
    n_rows = seq // GRID_W
    plans = []
    for t in range(seq // TILE):
        band0 = _na_band_row0(t * ROWS_PER_TILE, n_rows)
        plan = []
        for r in range(ROWS_PER_TILE):
            rq = t * ROWS_PER_TILE + r
            row_lo = min(max(rq - NA_WIN_H // 2, 0), n_rows - NA_WIN_H)
            plan.append((band0 - rq + NA_WIN_H - 1 + TAB_PAD, row_lo - band0))
        plans.append(tuple(plan))
    return tuple(plans)


def _na_lat_kernel(q_ref, k_ref, v_ref, kc_ref, vc_ref, tab_ref, tab_odd_ref, o_ref, bias_ref, *, plans, n_rows):
    t = pl.program_id(0)
    lane = lax.broadcasted_iota(jnp.int32, (GRID_W, NA_LOC_KEYS), 1)

    for plan in sorted(set(plans)):
        is_tile = functools.reduce(jnp.logical_or, [t == i for i, p in enumerate(plans) if p == plan])

        @pl.when(jnp.logical_and(pl.program_id(1) == 0, is_tile))
        def _build_bias(plan=plan):
            for r, (blk0, j0) in enumerate(plan):
                src, b0 = (tab_ref, blk0) if blk0 % 2 == 0 else (tab_odd_ref, blk0 - 1)
                in_rows = jnp.logical_and(lane >= j0 * GRID_W, lane < (j0 + NA_WIN_H) * GRID_W)
                for h in range(NA_HEADS):
                    band = src[h, :, b0 * GRID_W:b0 * GRID_W + NA_LOC_KEYS]
                    bias_ref[h, r * GRID_W:(r + 1) * GRID_W, :] = jnp.where(in_rows, band, NEG_INF)

    band_row0 = jnp.clip(t * ROWS_PER_TILE - NA_WIN_H // 2, 0, n_rows - NA_KEY_ROWS)
    rows = pl.ds(pl.multiple_of(band_row0 * GRID_W, TILE), NA_LOC_KEYS)

    def scores(qm, sl, h):
        return [_dot_nt(qm, k_ref[rows, sl]) + bias_ref[h], _dot_nt(qm, kc_ref[:, sl].astype(BF16))]

    _na_heads(q_ref, o_ref, scores, lambda sl: [v_ref[rows, sl], vc_ref[:, sl].astype(BF16)])


def _na_ctx(p, n_seq):
    blk = lambda c: pl.BlockSpec((TILE, BRANCH_WIDTH), lambda i: (i, c))
    return pl.pallas_call(
        _na_ctx_kernel, grid=(n_seq,),
        in_specs=[blk(C_NAQ // 512), blk(C_NAK // 512), blk(C_NAV // 512)],
        out_specs=pl.BlockSpec((TILE, BRANCH_WIDTH), lambda i: (i, 0)),
        out_shape=jax.ShapeDtypeStruct((n_seq * TILE, BRANCH_WIDTH), BF16),
        compiler_params=_params(("parallel",)), name="na_ctx",
    )(p, p, p)


def _na_lat(p, cache_k, cache_v, tab, tab_odd, layer, n_seq, seq):
    tiles = seq // TILE
    kv = lambda c: pl.BlockSpec((seq, BRANCH_WIDTH), lambda t, b: (b, c))
    cache = pl.BlockSpec((None, None, cache_k.shape[2], BRANCH_WIDTH), lambda t, b: (b, layer, 0, 0))
    table = _const_spec((None, NA_HEADS, GRID_W, TAB_BLOCKS * GRID_W), (layer, 0, 0, 0))
    return pl.pallas_call(
        functools.partial(_na_lat_kernel, plans=_na_band_plans(seq), n_rows=seq // GRID_W),
        grid=(tiles, n_seq),
        in_specs=[pl.BlockSpec((TILE, BRANCH_WIDTH), lambda t, b: (b * tiles + t, C_NAQ // 512)),
                  kv(C_NAK // 512), kv(C_NAV // 512), cache, cache, table, table],
        out_specs=pl.BlockSpec((TILE, BRANCH_WIDTH), lambda t, b: (b * tiles + t, 0)),
        out_shape=jax.ShapeDtypeStruct((n_seq * seq, BRANCH_WIDTH), BF16),
        scratch_shapes=[pltpu.VMEM((NA_HEADS, TILE, NA_LOC_KEYS), F32)],
        compiler_params=_params(("parallel", "arbitrary")), name="na_lat",
    )(p, p, p, cache_k, cache_v, tab, tab_odd)


def _na_bias_tables(rpb):
    cq = np.arange(GRID_W)[:, None]
    ck = np.arange(GRID_W)[None, :]
    col_lo = np.clip(cq - NA_WIN_W // 2, 0, GRID_W - NA_WIN_W)
    ok = (ck >= col_lo) & (ck < col_lo + NA_WIN_W)
    rel_c = ck - cq + NA_WIN_W - 1
    onehot = ((rel_c[:, :, None] == np.arange(2 * NA_WIN_W - 1)) & ok[:, :, None]).astype(np.float32)
    tab = jnp.einsum('lhab,uvb->lhuav', rpb, onehot, precision=lax.Precision.HIGHEST)
    tab = jnp.where(ok[None, None, :, None, :], tab, NEG_INF)
    n_rel = 2 * NA_WIN_H - 1
    tab = jnp.pad(tab, ((0, 0), (0, 0), (0, 0), (TAB_PAD, TAB_BLOCKS - n_rel - TAB_PAD), (0, 0)),
                  constant_values=NEG_INF)
    tab = tab.reshape(DEPTH, NA_HEADS, GRID_W, TAB_BLOCKS * GRID_W)
    tab_odd = jnp.pad(tab[..., GRID_W:], ((0, 0), (0, 0), (0, 0), (0, GRID_W)), constant_values=NEG_INF)
    return tab, tab_odd


def _mla_kernel(*refs, n_cache, n_lat, rope):
    dq_ref, ckv_ref, kr_ref = refs[:3]
    refs = refs[3:]
    if n_cache:
        cckv_ref, ckr_ref = refs[:2]
        refs = refs[2:]
    qn_ref, wuq_ref, wk_ref, wv_ref = refs[:4]
    refs = refs[4:]
    if rope:
        cos_ref, sin_ref = refs[:2]
        refs = refs[2:]
    o_ref, kx_ref, vx_ref = refs

    @pl.when(pl.program_id(1) == 0)
    def _expand_keys():
        def fill(r0, ckv, kr):
            n = ckv.shape[0]
            kk = _dot(ckv, wk_ref[...])
            for h in range(MLA_HEADS):
                sl = slice(LANES * h, LANES * (h + 1))
                kx_ref[r0:r0 + n, sl] = (kk[:, sl] + kr).astype(BF16)
            vx_ref[r0:r0 + n, :] = _dot(ckv, wv_ref[...]).astype(BF16)

        if n_cache:
            fill(0, cckv_ref[...].astype(BF16), ckr_ref[...])
        for r0 in range(0, n_lat, TILE):
            fill(n_cache + r0, ckv_ref[r0:r0 + TILE, :], kr_ref[r0:r0 + TILE, :].astype(F32))

    dqn = _rms(dq_ref[:, :MLA_Q_LORA].astype(F32), qn_ref[...])
    q = _dot(dqn.astype(BF16), wuq_ref[...])
    lane = lax.broadcasted_iota(jnp.int32, (TILE, LANES), 1)
    low = lane < MLA_V
    scale = (MLA_NOPE + MLA_ROPE) ** -0.5
    for hp in range(HEAD_PAIRS):
        v2 = vx_ref[:, LANES * hp:LANES * (hp + 1)]
        outs = []
        for half in range(2):
            sl = slice(LANES * (2 * hp + half), LANES * (2 * hp + half + 1))
            qh = q[:, sl]
            if rope:
                qh = _rope128(qh, cos_ref[...], sin_ref[...])
            qh = (qh * scale).astype(BF16)
            outs.append(_softmax_pv([_dot_nt(qh, kx_ref[:, sl])], [v2]))
        o_ref[:, LANES * hp:LANES * (hp + 1)] = jnp.where(low, outs[0], outs[1]).astype(BF16)


def _mla(p, ckv, kr, cache_ckv, cache_kr, layer, q_norm, w_uq, w_k, w_v, rope_tabs, n_seq, seq):
    tiles = seq // TILE
    n_cache = 0 if cache_ckv is None else cache_ckv.shape[2]
    rope = rope_tabs is not None
    in_specs = [pl.BlockSpec((TILE, CHUNK), lambda b, t: (b * tiles + t, C_DQ // CHUNK)),
                pl.BlockSpec((seq, MLA_KV_LORA), lambda b, t: (b, 0)),
                pl.BlockSpec((seq, LANES), lambda b, t: (b, 0))]
    args = [p, ckv, kr]
    if n_cache:
        in_specs += [pl.BlockSpec((None, None, n_cache, MLA_KV_LORA), lambda b, t: (b, layer, 0, 0)),
                     pl.BlockSpec((None, None, n_cache, LANES), lambda b, t: (b, layer, 0, 0))]
        args += [cache_ckv, cache_kr]
    in_specs += [_const_spec((None, 1, MLA_Q_LORA), (layer, 0, 0)),
                 _const_spec((None, MLA_Q_LORA, MLA_HEADS * LANES), (layer, 0, 0)),
                 _const_spec((None, MLA_KV_LORA, MLA_HEADS * LANES), (layer, 0, 0)),
                 _const_spec((None, MLA_KV_LORA, MLA_HEADS * MLA_V), (layer, 0, 0))]
    args += [q_norm, w_uq, w_k, w_v]
    if rope:
        in_specs += [pl.BlockSpec((TILE, LANES), lambda b, t: (t, 0))] * 2
        args += list(rope_tabs)
    n_keys = n_cache + seq
    return pl.pallas_call(
        functools.partial(_mla_kernel, n_cache=n_cache, n_lat=seq, rope=rope),
        grid=(n_seq, tiles), in_specs=in_specs,
        out_specs=pl.BlockSpec((TILE, BRANCH_WIDTH), lambda b, t: (b * tiles + t, 0)),
        out_shape=jax.ShapeDtypeStruct((n_seq * seq, BRANCH_WIDTH), BF16),
        scratch_shapes=[pltpu.VMEM((n_keys, MLA_HEADS * LANES), BF16), pltpu.VMEM((n_keys, MLA_HEADS * MLA_V), BF16)],
        compiler_params=_params(("parallel", "arbitrary")), name="mla",
    )(*args)


def _merge_kernel(x_ref, mod_ref, yna_ref, ymla_ref, naz_ref, su_ref, sv_ref, sz_ref, mz_ref, mg0_ref, mg1_ref, mg2_ref,
                  sw_ref, sb_ref, wb_ref, wo_ref, fg_ref, o_ref, *, final):
    u = jax.nn.gelu(su_ref[...].astype(F32))
    v = jax.nn.gelu(sv_ref[...].astype(F32))
    mu = jnp.mean(v, axis=-1, keepdims=True)
    vc = v - mu
    vn = (vc * lax.rsqrt(jnp.mean(vc * vc, axis=-1, keepdims=True) + EPS)).astype(BF16)
    rows = []
    for c0 in range(0, TILE, SGU_CHUNK):
        cols = [_dot(sw_ref[g], vn[c0:c0 + SGU_CHUNK, LANES * g:LANES * (g + 1)]) for g in range(SGU_GROUPS)]
        rows.append(jnp.concatenate(cols, axis=1) + sb_ref[...])
    y_sgu = u * jnp.concatenate(rows, axis=0)

    def silu(z_ref):
        z = z_ref[...].astype(F32)
        return z * jax.nn.sigmoid(z)

    branches = ((yna_ref[...].astype(F32), naz_ref, mg0_ref), (y_sgu, sz_ref, mg1_ref),
                (ymla_ref[...].astype(F32), mz_ref, mg2_ref))
    merged = None
    for k, (y, z_ref, mg_ref) in enumerate(branches):
        br = _dot((y * silu(z_ref)).astype(BF16), wb_ref[k])
        term = jax.nn.sigmoid(mg_ref[...].astype(F32)) * br
        merged = term if merged is None else merged + term
    out = _dot(merged.astype(BF16), wo_ref[...])
    gate = mod_ref[:, 2 * D_MODEL:3 * D_MODEL]
    xn = x_ref[...] + gate * out
    if final:
        xn = _rms(xn, fg_ref[...])
    o_ref[...] = xn


def _merge(x, mod, mod_row, layer, p, y_na, y_mla, sgu_w, sgu_b, w_branch, w_out, final_g, final):
    n_tok = x.shape[0]
    half = lambda c: pl.BlockSpec((TILE, BRANCH_WIDTH), lambda i: (i, c))
    full = lambda c: pl.BlockSpec((TILE, D_MODEL), lambda i: (i, c))
    in_specs = [full(0),
                pl.BlockSpec((None, None, 1, 3 * D_MODEL), lambda i: (layer, mod_row(i), 0, 0)),
                half(0), half(0),
                half(C_NAZ // 512), half(C_SU // 512), half(C_SV // 512), half(C_SZ // 512), half(C_MZ // 512),
                full(C_MG // D_MODEL), full(C_MG // D_MODEL + 1), full(C_MG // D_MODEL + 2),
                _const_spec((None, SGU_GROUPS, SGU_CHUNK, SGU_CHUNK), (layer, 0, 0, 0)),
                _const_spec((None, SGU_CHUNK, BRANCH_WIDTH), (layer, 0, 0)),
                _const_spec((None, N_BRANCH, BRANCH_WIDTH, D_MODEL), (layer, 0, 0, 0)),
                _const_spec((None, D_MODEL, D_MODEL), (layer, 0, 0)),
                _const_spec((1, D_MODEL), (0, 0))]
    return pl.pallas_call(
        functools.partial(_merge_kernel, final=final),
        grid=(n_tok // TILE,), in_specs=in_specs,
        out_specs=full(0), out_shape=jax.ShapeDtypeStruct((n_tok, D_MODEL), F32),
        compiler_params=_params(("parallel",)), name="merge",
    )(x, mod, y_na, y_mla, p, p, p, p, p, p, p, p, sgu_w, sgu_b, w_branch, w_out, final_g)


def _pack_params(w_in, sgu_w, sgu_b, mla_w_uq, mla_w_ukv, w_branch, w_out):
    o = np.cumsum([0, 2048, 1536, MLA_Q_LORA, MLA_KV_LORA, MLA_ROPE, BRANCH_WIDTH, N_BRANCH * D_MODEL])
    na, sgu, dq, dkv, kr, mz, mg = (w_in[:, :, o[i]:o[i + 1]] for i in range(7))
    zeros = lambda n: jnp.zeros((DEPTH, D_MODEL, n), w_in.dtype)
    w_packed = jnp.concatenate([na, sgu, mz, mg, dq, dkv, zeros(ROPE_LANE0), kr,
                                zeros(LANES - ROPE_LANE0 - MLA_ROPE)], axis=2).astype(BF16)
    uq = mla_w_uq.reshape(DEPTH, MLA_Q_LORA, MLA_HEADS, MLA_NOPE + MLA_ROPE)
    uq = jnp.pad(uq, ((0, 0), (0, 0), (0, 0), (0, LANES - MLA_NOPE - MLA_ROPE)))
    uq = uq.reshape(DEPTH, MLA_Q_LORA, MLA_HEADS * LANES).astype(BF16)
    ukv = mla_w_ukv.reshape(DEPTH, MLA_KV_LORA, MLA_HEADS, MLA_NOPE + MLA_V)
    w_k = jnp.pad(ukv[..., :MLA_NOPE], ((0, 0), (0, 0), (0, 0), (0, LANES - MLA_NOPE)))
    w_k = w_k.reshape(DEPTH, MLA_KV_LORA, MLA_HEADS * LANES).astype(BF16)
    w_v = ukv[..., MLA_NOPE:].reshape(DEPTH, MLA_KV_LORA, MLA_HEADS * MLA_V).astype(BF16)
    sgu_bias = jnp.repeat(jnp.swapaxes(sgu_b, 1, 2), BRANCH_WIDTH // SGU_GROUPS, axis=2)
    return w_packed, uq, w_k, w_v, sgu_w.astype(BF16), sgu_bias, w_branch.astype(BF16), w_out.astype(BF16)


def _rope_tables(n_tokens):
    pos = jnp.arange(n_tokens, dtype=jnp.int32)
    row = (pos // GRID_W).astype(F32)
    col = (pos % GRID_W).astype(F32)
    n_freq = MLA_ROPE // 4
    inv = ROPE_THETA ** (-jnp.arange(n_freq, dtype=F32) / n_freq)
    ang = jnp.concatenate([row[:, None] * inv, col[:, None] * inv], axis=-1)
    cos, sin = jnp.cos(ang), jnp.sin(ang)
    pad_l, pad_r = ROPE_LANE0, LANES - ROPE_LANE0 - MLA_ROPE
    cos_t = jnp.pad(jnp.concatenate([cos, cos], axis=1), ((0, 0), (pad_l, pad_r)), constant_values=1.0)
    sin_t = jnp.pad(jnp.concatenate([-sin, sin], axis=1), ((0, 0), (pad_l, pad_r)))
    return cos_t, sin_t


def kernel(x_prompt, x_sample, cache_na_k, cache_na_v, cache_mla_ckv, cache_mla_krope, c, c_ctx, norm_g, w_mod, b_mod, w_in, na_rpb, sgu_w, sgu_b, mla_q_norm, mla_w_uq, mla_kv_norm, mla_w_ukv, w_branch, w_out, final_norm_g):
    n_ctx, ctx_len, _ = x_prompt.shape
    n_lat, lat_len, _ = x_sample.shape
    past = cache_na_k.shape[2]
    assert ctx_len == TILE and lat_len % TILE == 0 and lat_len // GRID_W == 16 and past == TILE

    w_packed, w_uq, w_k, w_v, sgu_wb, sgu_bias, w_br, w_o = _pack_params(
        w_in, sgu_w, sgu_b, mla_w_uq, mla_w_ukv, w_branch, w_out)
    rope_tabs = _rope_tables(lat_len)
    tab, tab_odd = _na_bias_tables(na_rpb)
    norm_g3 = norm_g.reshape(DEPTH, 1, D_MODEL)
    kv_norm3 = mla_kv_norm.reshape(DEPTH, 1, MLA_KV_LORA)
    q_norm3 = mla_q_norm.reshape(DEPTH, 1, MLA_Q_LORA)
    final_g = final_norm_g.reshape(1, D_MODEL)

    cond_rows = 16
    cond = jnp.concatenate([c, c_ctx[None, :], jnp.zeros((cond_rows - n_lat - 1, D_MODEL), F32)], axis=0)
    mod = _modulation(cond, w_mod, b_mod).reshape(DEPTH, cond_rows, 1, 3 * D_MODEL)
    ctx_row = lambda i: n_lat
    lat_tiles = lat_len // TILE
    lat_row = lambda i: i // lat_tiles

    cache_k = cache_na_k.reshape(n_lat, DEPTH, past, BRANCH_WIDTH)
    cache_v = cache_na_v.reshape(n_lat, DEPTH, past, BRANCH_WIDTH)
    cache_kr = jnp.pad(cache_mla_krope, ((0, 0), (0, 0), (0, 0), (ROPE_LANE0, LANES - ROPE_LANE0 - MLA_ROPE)))

    xp = x_prompt.reshape(n_ctx * ctx_len, D_MODEL)
    xs = x_sample.reshape(n_lat * lat_len, D_MODEL)
    state = None
    for l in range(DEPTH):
        final = l == DEPTH - 1
        p, ckv, kr, *state = _inproj(xp, mod, ctx_row, l, norm_g3, w_packed, kv_norm3, None, state, True)
        y_na = _na_ctx(p, n_ctx)
        y_mla = _mla(p, ckv, kr, None, None, l, q_norm3, w_uq, w_k, w_v, None, n_ctx, ctx_len)
        xp = _merge(xp, mod, ctx_row, l, p, y_na, y_mla, sgu_wb, sgu_bias, w_br, w_o, final_g, final)
        p, ckv, kr = _inproj(xs, mod, lat_row, l, norm_g3, w_packed, kv_norm3, rope_tabs, None, False)
        y_na = _na_lat(p, cache_k, cache_v, tab, tab_odd, l, n_lat, lat_len)
        y_mla = _mla(p, ckv, kr, cache_mla_ckv, cache_kr, l, q_norm3, w_uq, w_k, w_v, rope_tabs, n_lat, lat_len)
        xs = _merge(xs, mod, lat_row, l, p, y_na, y_mla, sgu_wb, sgu_bias, w_br, w_o, final_g, final)

    s_k, s_v, s_ckv, s_kr = state
    return (xp.reshape(n_ctx, ctx_len, D_MODEL), xs.reshape(n_lat, lat_len, D_MODEL),
            s_k.reshape(n_ctx, DEPTH, ctx_len, NA_HEADS, NA_HEAD_DIM),
            s_v.reshape(n_ctx, DEPTH, ctx_len, NA_HEADS, NA_HEAD_DIM), s_ckv, s_kr)
```

```python
import functools

import numpy as np
import jax
import jax.numpy as jnp
from jax import lax
from jax.experimental import pallas as pl
from jax.experimental.pallas import tpu as pltpu

F32 = jnp.float32
BF16 = jnp.bfloat16

D_MODEL = 1024
DEPTH = 2
GRID_W = 64
BRANCH_WIDTH = 512
N_BRANCH = 3
NA_HEADS = 8
NA_HEAD_DIM = 64
NA_WIN_H = 8
NA_WIN_W = 16
SGU_GROUPS = 4
SGU_CHUNK = 128
MLA_HEADS = 8
MLA_NOPE = 64
MLA_ROPE = 32
MLA_V = 64
MLA_Q_LORA = 384
MLA_KV_LORA = 256
ROPE_THETA = 10000.0
EPS = 1e-6
NEG_INF = -1e30

LANES = 128
TILE = 256
INPROJ_TILE = 512
HEAD_PAIRS = NA_HEADS // 2
NA_KEY_ROWS = 12
NA_LOC_KEYS = NA_KEY_ROWS * GRID_W
ROWS_PER_TILE = TILE // GRID_W
TAB_PAD = 4
TAB_BLOCKS = 24
ROPE_LANE0 = MLA_NOPE

C_NAQ, C_NAK, C_NAV, C_NAZ, C_SU, C_SV, C_SZ, C_MZ, C_MG = (512 * i for i in (0, 1, 2, 3, 4, 5, 6, 7, 8))
C_DQ = C_MG + N_BRANCH * D_MODEL
C_DKV = C_DQ + MLA_Q_LORA
C_KR = C_DKV + MLA_KV_LORA
W_COLS = C_KR + LANES
P_COLS = C_DQ + 512
CHUNK = 512

VMEM_LIMIT = 56 * 1024 * 1024


def _dot(a, b):
    return jnp.dot(a, b, preferred_element_type=F32)


def _dot_nt(a, b):
    return lax.dot_general(a, b, (((1,), (1,)), ((), ())), preferred_element_type=F32)


def _rms(x, g):
    return x * lax.rsqrt(jnp.mean(x * x, axis=-1, keepdims=True) + EPS) * g


def _center_norm(x):
    c = x - jnp.mean(x, axis=-1, keepdims=True)
    return c * lax.rsqrt(jnp.mean(c * c, axis=-1, keepdims=True) + EPS)


def _rope128(x, cos, sin):
    lane = lax.broadcasted_iota(jnp.int32, x.shape, 1)
    swapped = jnp.where(lane < ROPE_LANE0 + MLA_ROPE // 2,
                        pltpu.roll(x, LANES - MLA_ROPE // 2, axis=1),
                        pltpu.roll(x, MLA_ROPE // 2, axis=1))
    return x * cos + swapped * sin


def _softmax_pv(scores, values):
    m = scores[0].max(axis=-1, keepdims=True)
    for s in scores[1:]:
        m = jnp.maximum(m, s.max(axis=-1, keepdims=True))
    den = None
    acc = None
    for s, v in zip(scores, values):
        e = jnp.exp(s - m)
        d = e.sum(axis=-1, keepdims=True)
        o = _dot(e.astype(BF16), v)
        den = d if den is None else den + d
        acc = o if acc is None else acc + o
    return acc / den


def _const_spec(shape, index):
    return pl.BlockSpec(shape, lambda *_: index, pipeline_mode=pl.Buffered(1))


def _params(semantics):
    return pltpu.CompilerParams(dimension_semantics=semantics, vmem_limit_bytes=VMEM_LIMIT)


def _mod_kernel(c_ref, w_ref, b_ref, o_ref):
    c = c_ref[...]
    s = c * jax.nn.sigmoid(c)
    o_ref[...] = jnp.dot(s, w_ref[...], preferred_element_type=F32, precision=lax.Precision.HIGHEST) + b_ref[...]


def _modulation(cond, w_mod, b_mod):
    rows = cond.shape[0]
    return pl.pallas_call(
        _mod_kernel,
        grid=(DEPTH, 3),
        in_specs=[pl.BlockSpec((rows, D_MODEL), lambda l, j: (0, 0)),
                  pl.BlockSpec((None, D_MODEL, D_MODEL), lambda l, j: (l, 0, j)),
                  pl.BlockSpec((None, 1, D_MODEL), lambda l, j: (l, 0, j))],
        out_specs=pl.BlockSpec((None, rows, D_MODEL), lambda l, j: (l, 0, j)),
        out_shape=jax.ShapeDtypeStruct((DEPTH, rows, 3 * D_MODEL), F32),
        compiler_params=_params(("parallel", "parallel")),
        name="modulation",
    )(cond, w_mod, b_mod.reshape(DEPTH, 1, 3 * D_MODEL))


def _inproj_kernel(*refs, emit_state, rope, n_alias):
    x_ref, mod_ref, g_ref, w_ref, kvn_ref = refs[:5]
    refs = refs[5:]
    if rope:
        cos_ref, sin_ref = refs[:2]
        refs = refs[2:]
    refs = refs[n_alias:]
    p_ref, ckv_ref, kr_ref = refs[:3]
    if emit_state:
        sk_ref, sv_ref, sckv_ref, skr_ref = refs[3:7]

    shift = mod_ref[:, 0:D_MODEL]
    scale = mod_ref[:, D_MODEL:2 * D_MODEL]
    tm = x_ref.shape[0]
    h = (_rms(x_ref[...], g_ref[...]) * (1.0 + scale) + shift).astype(BF16)

    def state_view(a):
        return a.reshape(tm // TILE, TILE, a.shape[-1])

    ckv = _rms(_dot(h, w_ref[:, C_DKV:C_KR]), kvn_ref[...])
    ckv_ref[...] = ckv.astype(BF16)
    kr = _dot(h, w_ref[:, C_KR:W_COLS])
    if emit_state:
        sckv_ref[...] = state_view(ckv)
        skr_ref[...] = state_view(kr[:, ROPE_LANE0:ROPE_LANE0 + MLA_ROPE])
    if rope:
        kr = _rope128(kr, cos_ref[...], sin_ref[...])
    kr_ref[...] = kr.astype(BF16)

    for c0 in range(0, P_COLS, CHUNK):
        acc = _dot(h, w_ref[:, c0:c0 + CHUNK])
        if emit_state and c0 == C_NAK:
            sk_ref[...] = state_view(acc)
        if emit_state and c0 == C_NAV:
            sv_ref[...] = state_view(acc)
        if c0 in (C_NAZ, C_SZ, C_MZ):
            acc = acc * jax.nn.sigmoid(acc)
        elif c0 == C_SU:
            acc = jax.nn.gelu(acc)
        elif c0 == C_SV:
            acc = _center_norm(jax.nn.gelu(acc))
        elif C_MG <= c0 < C_DQ:
            acc = jax.nn.sigmoid(acc)
        p_ref[:, c0:c0 + CHUNK] = acc.astype(BF16)


def _inproj(x, mod, mod_row, layer, norm_g, w_packed, kv_norm, rope_tabs, state_in, emit_state):
    n_tok = x.shape[0]
    tm = INPROJ_TILE
    n_tiles = n_tok // tm
    rope = rope_tabs is not None
    in_specs = [pl.BlockSpec((tm, D_MODEL), lambda i: (i, 0)),
                pl.BlockSpec((None, None, 1, 3 * D_MODEL), lambda i: (layer, mod_row(i * tm), 0, 0)),
                _const_spec((None, 1, D_MODEL), (layer, 0, 0)),
                _const_spec((None, D_MODEL, W_COLS), (layer, 0, 0)),
                _const_spec((None, 1, MLA_KV_LORA), (layer, 0, 0))]
    args = [x, mod, norm_g, w_packed, kv_norm]
    if rope:
        tiles_per_seq = rope_tabs[0].shape[0] // tm
        in_specs += [pl.BlockSpec((tm, LANES), lambda i: (i % tiles_per_seq, 0))] * 2
        args += list(rope_tabs)
    out_specs = [pl.BlockSpec((tm, P_COLS), lambda i: (i, 0)),
                 pl.BlockSpec((tm, MLA_KV_LORA), lambda i: (i, 0)),
                 pl.BlockSpec((tm, LANES), lambda i: (i, 0))]
    out_shape = [jax.ShapeDtypeStruct((n_tok, P_COLS), BF16),
                 jax.ShapeDtypeStruct((n_tok, MLA_KV_LORA), BF16),
                 jax.ShapeDtypeStruct((n_tok, LANES), BF16)]
    aliases = {}
    if emit_state:
        widths = (BRANCH_WIDTH, BRANCH_WIDTH, MLA_KV_LORA, MLA_ROPE)
        for k, w in enumerate(widths):
            out_specs.append(pl.BlockSpec((tm // TILE, None, TILE, w), lambda i: (i, layer, 0, 0)))
            out_shape.append(jax.ShapeDtypeStruct((n_tok // TILE, DEPTH, TILE, w), F32))
        if state_in is not None:
            first = len(args)
            in_specs += [pl.BlockSpec(memory_space=pl.ANY)] * 4
            args += list(state_in)
            aliases = {first + k: 3 + k for k in range(4)}
    kern = functools.partial(_inproj_kernel, emit_state=emit_state, rope=rope, n_alias=len(aliases))
    return pl.pallas_call(
        kern, grid=(n_tiles,), in_specs=in_specs, out_specs=out_specs, out_shape=out_shape,
        input_output_aliases=aliases, compiler_params=_params(("parallel",)),
        name="inproj",
    )(*args)


def _na_heads(q_ref, o_ref, score_fn, value_list_fn):
    lane = lax.broadcasted_iota(jnp.int32, (TILE, LANES), 1)
    low = lane < NA_HEAD_DIM
    for hp in range(HEAD_PAIRS):
        sl = slice(LANES * hp, LANES * (hp + 1))
        q2 = q_ref[:, sl] * (NA_HEAD_DIM ** -0.5)
        values = value_list_fn(sl)
        outs = []
        for half in range(2):
            qm = jnp.where(low if half == 0 else jnp.logical_not(low), q2, jnp.zeros_like(q2))
            outs.append(_softmax_pv(score_fn(qm, sl, 2 * hp + half), values))
        o_ref[:, sl] = jnp.where(low, outs[0], outs[1]).astype(BF16)


def _na_ctx_kernel(q_ref, k_ref, v_ref, o_ref):
    _na_heads(q_ref, o_ref,
              lambda qm, sl, h: [_dot_nt(qm, k_ref[:, sl])],
              lambda sl: [v_ref[:, sl]])


def _na_band_row0(tile_row0, n_rows):
    return min(max(tile_row0 - NA_WIN_H // 2, 0), n_rows - NA_KEY_ROWS)


def _na_band_plans(seq):
    n_rows = seq // GRID_W
    plans = []
    for t in range(seq // TILE):
        band0 = _na_band_row0(t * ROWS_PER_TILE, n_rows)
        plan = []
        for r in range(ROWS_PER_TILE):
            rq = t * ROWS_PER_TILE + r
            row_lo = min(max(rq - NA_WIN_H // 2, 0), n_rows - NA_WIN_H)
            plan.append((band0 - rq + NA_WIN_H - 1 + TAB_PAD, row_lo - band0))
        plans.append(tuple(plan))
    return tuple(plans)


def _na_lat_kernel(q_ref, k_ref, v_ref, kc_ref, vc_ref, tab_ref, tab_odd_ref, o_ref, bias_ref, *, plans, n_rows):
    t = pl.program_id(0)
    lane = lax.broadcasted_iota(jnp.int32, (GRID_W, NA_LOC_KEYS), 1)

    for plan in sorted(set(plans)):
        is_tile = functools.reduce(jnp.logical_or, [t == i for i, p in enumerate(plans) if p == plan])

        @pl.when(jnp.logical_and(pl.program_id(1) == 0, is_tile))
        def _build_bias(plan=plan):
            for r, (blk0, j0) in enumerate(plan):
                src, b0 = (tab_ref, blk0) if blk0 % 2 == 0 else (tab_odd_ref, blk0 - 1)
                in_rows = jnp.logical_and(lane >= j0 * GRID_W, lane < (j0 + NA_WIN_H) * GRID_W)
                for h in range(NA_HEADS):
                    band = src[h, :, b0 * GRID_W:b0 * GRID_W + NA_LOC_KEYS]
                    bias_ref[h, r * GRID_W:(r + 1) * GRID_W, :] = jnp.where(in_rows, band, NEG_INF)

    band_row0 = jnp.clip(t * ROWS_PER_TILE - NA_WIN_H // 2, 0, n_rows - NA_KEY_ROWS)
    rows = pl.ds(pl.multiple_of(band_row0 * GRID_W, TILE), NA_LOC_KEYS)

    def scores(qm, sl, h):
        return [_dot_nt(qm, k_ref[rows, sl]) + bias_ref[h], _dot_nt(qm, kc_ref[:, sl].astype(BF16))]

    _na_heads(q_ref, o_ref, scores, lambda sl: [v_ref[rows, sl], vc_ref[:, sl].astype(BF16)])


def _na_ctx(p, n_seq):
    blk = lambda c: pl.BlockSpec((TILE, BRANCH_WIDTH), lambda i: (i, c))
    return pl.pallas_call(
        _na_ctx_kernel, grid=(n_seq,),
        in_specs=[blk(C_NAQ // 512), blk(C_NAK // 512), blk(C_NAV // 512)],
        out_specs=pl.BlockSpec((TILE, BRANCH_WIDTH), lambda i: (i, 0)),
        out_shape=jax.ShapeDtypeStruct((n_seq * TILE, BRANCH_WIDTH), BF16),
        compiler_params=_params(("parallel",)), name="na_ctx",
    )(p, p, p)


def _na_lat(p, cache_k, cache_v, tab, tab_odd, layer, n_seq, seq):
    tiles = seq // TILE
    kv = lambda c: pl.BlockSpec((seq, BRANCH_WIDTH), lambda t, b: (b, c))
    cache = pl.BlockSpec((None, None, cache_k.shape[2], BRANCH_WIDTH), lambda t, b: (b, layer, 0, 0))
    table = _const_spec((None, NA_HEADS, GRID_W, TAB_BLOCKS * GRID_W), (layer, 0, 0, 0))
    return pl.pallas_call(
        functools.partial(_na_lat_kernel, plans=_na_band_plans(seq), n_rows=seq // GRID_W),
        grid=(tiles, n_seq),
        in_specs=[pl.BlockSpec((TILE, BRANCH_WIDTH), lambda t, b: (b * tiles + t, C_NAQ // 512)),
                  kv(C_NAK // 512), kv(C_NAV // 512), cache, cache, table, table],
        out_specs=pl.BlockSpec((TILE, BRANCH_WIDTH), lambda t, b: (b * tiles + t, 0)),
        out_shape=jax.ShapeDtypeStruct((n_seq * seq, BRANCH_WIDTH), BF16),
        scratch_shapes=[pltpu.VMEM((NA_HEADS, TILE, NA_LOC_KEYS), F32)],
        compiler_params=_params(("parallel", "arbitrary")), name="na_lat",
    )(p, p, p, cache_k, cache_v, tab, tab_odd)


def _na_bias_tables(rpb):
    cq = np.arange(GRID_W)[:, None]
    ck = np.arange(GRID_W)[None, :]
    col_lo = np.clip(cq - NA_WIN_W // 2, 0, GRID_W - NA_WIN_W)
    ok = (ck >= col_lo) & (ck < col_lo + NA_WIN_W)
    rel_c = ck - cq + NA_WIN_W - 1
    onehot = ((rel_c[:, :, None] == np.arange(2 * NA_WIN_W - 1)) & ok[:, :, None]).astype(np.float32)
    tab = jnp.einsum('lhab,uvb->lhuav', rpb, onehot, precision=lax.Precision.HIGHEST)
    tab = jnp.where(ok[None, None, :, None, :], tab, NEG_INF)
    n_rel = 2 * NA_WIN_H - 1
    tab = jnp.pad(tab, ((0, 0), (0, 0), (0, 0), (TAB_PAD, TAB_BLOCKS - n_rel - TAB_PAD), (0, 0)),
                  constant_values=NEG_INF)
    tab = tab.reshape(DEPTH, NA_HEADS, GRID_W, TAB_BLOCKS * GRID_W)
    tab_odd = jnp.pad(tab[..., GRID_W:], ((0, 0), (0, 0), (0, 0), (0, GRID_W)), constant_values=NEG_INF)
    return tab, tab_odd


def _mla_kernel(*refs, n_cache, n_lat, rope):
    dq_ref, ckv_ref, kr_ref = refs[:3]
    refs = refs[3:]
    if n_cache:
        cckv_ref, ckr_ref = refs[:2]
        refs = refs[2:]
    qn_ref, wuq_ref, wk_ref, wv_ref = refs[:4]
    refs = refs[4:]
    if rope:
        cos_ref, sin_ref = refs[:2]
        refs = refs[2:]
    o_ref, kx_ref, vx_ref = refs

    @pl.when(pl.program_id(1) == 0)
    def _expand_keys():
        def fill(r0, ckv, kr):
            n = ckv.shape[0]
            kk = _dot(ckv, wk_ref[...])
            for h in range(MLA_HEADS):
                sl = slice(LANES * h, LANES * (h + 1))
                kx_ref[r0:r0 + n, sl] = (kk[:, sl] + kr).astype(BF16)
            vx_ref[r0:r0 + n, :] = _dot(ckv, wv_ref[...]).astype(BF16)

        if n_cache:
            fill(0, cckv_ref[...].astype(BF16), ckr_ref[...])
        for r0 in range(0, n_lat, TILE):
            fill(n_cache + r0, ckv_ref[r0:r0 + TILE, :], kr_ref[r0:r0 + TILE, :].astype(F32))

    dqn = _rms(dq_ref[:, :MLA_Q_LORA].astype(F32), qn_ref[...])
    q = _dot(dqn.astype(BF16), wuq_ref[...])
    lane = lax.broadcasted_iota(jnp.int32, (TILE, LANES), 1)
    low = lane < MLA_V
    scale = (MLA_NOPE + MLA_ROPE) ** -0.5
    for hp in range(HEAD_PAIRS):
        v2 = vx_ref[:, LANES * hp:LANES * (hp + 1)]
        outs = []
        for half in range(2):
            sl = slice(LANES * (2 * hp + half), LANES * (2 * hp + half + 1))
            qh = q[:, sl]
            if rope:
                qh = _rope128(qh, cos_ref[...], sin_ref[...])
            qh = (qh * scale).astype(BF16)
            outs.append(_softmax_pv([_dot_nt(qh, kx_ref[:, sl])], [v2]))
        o_ref[:, LANES * hp:LANES * (hp + 1)] = jnp.where(low, outs[0], outs[1]).astype(BF16)


def _mla(p, ckv, kr, cache_ckv, cache_kr, layer, q_norm, w_uq, w_k, w_v, rope_tabs, n_seq, seq):
    tiles = seq // TILE
    n_cache = 0 if cache_ckv is None else cache_ckv.shape[2]
    rope = rope_tabs is not None
    in_specs = [pl.BlockSpec((TILE, CHUNK), lambda b, t: (b * tiles + t, C_DQ // CHUNK)),
                pl.BlockSpec((seq, MLA_KV_LORA), lambda b, t: (b, 0)),
                pl.BlockSpec((seq, LANES), lambda b, t: (b, 0))]
    args = [p, ckv, kr]
    if n_cache:
        in_specs += [pl.BlockSpec((None, None, n_cache, MLA_KV_LORA), lambda b, t: (b, layer, 0, 0)),
                     pl.BlockSpec((None, None, n_cache, LANES), lambda b, t: (b, layer, 0, 0))]
        args += [cache_ckv, cache_kr]
    in_specs += [_const_spec((None, 1, MLA_Q_LORA), (layer, 0, 0)),
                 _const_spec((None, MLA_Q_LORA, MLA_HEADS * LANES), (layer, 0, 0)),
                 _const_spec((None, MLA_KV_LORA, MLA_HEADS * LANES), (layer, 0, 0)),
                 _const_spec((None, MLA_KV_LORA, MLA_HEADS * MLA_V), (layer, 0, 0))]
    args += [q_norm, w_uq, w_k, w_v]
    if rope:
        in_specs += [pl.BlockSpec((TILE, LANES), lambda b, t: (t, 0))] * 2
        args += list(rope_tabs)
    n_keys = n_cache + seq
    return pl.pallas_call(
        functools.partial(_mla_kernel, n_cache=n_cache, n_lat=seq, rope=rope),
        grid=(n_seq, tiles), in_specs=in_specs,
        out_specs=pl.BlockSpec((TILE, BRANCH_WIDTH), lambda b, t: (b * tiles + t, 0)),
        out_shape=jax.ShapeDtypeStruct((n_seq * seq, BRANCH_WIDTH), BF16),
        scratch_shapes=[pltpu.VMEM((n_keys, MLA_HEADS * LANES), BF16), pltpu.VMEM((n_keys, MLA_HEADS * MLA_V), BF16)],
        compiler_params=_params(("parallel", "arbitrary")), name="mla",
    )(*args)


def _merge_kernel(x_ref, mod_ref, yna_ref, ymla_ref, naz_ref, su_ref, sv_ref, sz_ref, mz_ref, mg0_ref, mg1_ref, mg2_ref,
                  sw_ref, sb_ref, wb_ref, wo_ref, fg_ref, o_ref, *, final):
    rows = []
    for c0 in range(0, TILE, SGU_CHUNK):
        cols = [_dot(sw_ref[g], sv_ref[c0:c0 + SGU_CHUNK, LANES * g:LANES * (g + 1)]) for g in range(SGU_GROUPS)]
        rows.append(jnp.concatenate(cols, axis=1) + sb_ref[...])
    y_sgu = su_ref[...].astype(F32) * jnp.concatenate(rows, axis=0)

    gated = (yna_ref[...] * naz_ref[...],
             (y_sgu * sz_ref[...].astype(F32)).astype(BF16),
             ymla_ref[...] * mz_ref[...])
    merged = None
    for k, mg_ref in enumerate((mg0_ref, mg1_ref, mg2_ref)):
        term = mg_ref[...].astype(F32) * _dot(gated[k], wb_ref[k])
        merged = term if merged is None else merged + term
    out = _dot(merged.astype(BF16), wo_ref[...])
    gate = mod_ref[:, 2 * D_MODEL:3 * D_MODEL]
    xn = x_ref[...] + gate * out
    if final:
        xn = _rms(xn, fg_ref[...])
    o_ref[...] = xn


def _merge(x, mod, mod_row, layer, p, y_na, y_mla, sgu_w, sgu_b, w_branch, w_out, final_g, final):
    n_tok = x.shape[0]
    half = lambda c: pl.BlockSpec((TILE, BRANCH_WIDTH), lambda i: (i, c))
    full = lambda c: pl.BlockSpec((TILE, D_MODEL), lambda i: (i, c))
    in_specs = [full(0),
                pl.BlockSpec((None, None, 1, 3 * D_MODEL), lambda i: (layer, mod_row(i * TILE), 0, 0)),
                half(0), half(0),
                half(C_NAZ // 512), half(C_SU // 512), half(C_SV // 512), half(C_SZ // 512), half(C_MZ // 512),
                full(C_MG // D_MODEL), full(C_MG // D_MODEL + 1), full(C_MG // D_MODEL + 2),
                _const_spec((None, SGU_GROUPS, SGU_CHUNK, SGU_CHUNK), (layer, 0, 0, 0)),
                _const_spec((None, SGU_CHUNK, BRANCH_WIDTH), (layer, 0, 0)),
                _const_spec((None, N_BRANCH, BRANCH_WIDTH, D_MODEL), (layer, 0, 0, 0)),
                _const_spec((None, D_MODEL, D_MODEL), (layer, 0, 0)),
                _const_spec((1, D_MODEL), (0, 0))]
    return pl.pallas_call(
        functools.partial(_merge_kernel, final=final),
        grid=(n_tok // TILE,), in_specs=in_specs,
        out_specs=full(0), out_shape=jax.ShapeDtypeStruct((n_tok, D_MODEL), F32),
        compiler_params=_params(("parallel",)), name="merge",
    )(x, mod, y_na, y_mla, p, p, p, p, p, p, p, p, sgu_w, sgu_b, w_branch, w_out, final_g)


def _pack_params(w_in, sgu_w, sgu_b, mla_w_uq, mla_w_ukv, w_branch, w_out):
    o = np.cumsum([0, 2048, 1536, MLA_Q_LORA, MLA_KV_LORA, MLA_ROPE, BRANCH_WIDTH, N_BRANCH * D_MODEL])
    na, sgu, dq, dkv, kr, mz, mg = (w_in[:, :, o[i]:o[i + 1]] for i in range(7))
    zeros = lambda n: jnp.zeros((DEPTH, D_MODEL, n), w_in.dtype)
    w_packed = jnp.concatenate([na, sgu, mz, mg, dq, dkv, zeros(ROPE_LANE0), kr,
                                zeros(LANES - ROPE_LANE0 - MLA_ROPE)], axis=2).astype(BF16)
    uq = mla_w_uq.reshape(DEPTH, MLA_Q_LORA, MLA_HEADS, MLA_NOPE + MLA_ROPE)
    uq = jnp.pad(uq, ((0, 0), (0, 0), (0, 0), (0, LANES - MLA_NOPE - MLA_ROPE)))
    uq = uq.reshape(DEPTH, MLA_Q_LORA, MLA_HEADS * LANES).astype(BF16)
    ukv = mla_w_ukv.reshape(DEPTH, MLA_KV_LORA, MLA_HEADS, MLA_NOPE + MLA_V)
    w_k = jnp.pad(ukv[..., :MLA_NOPE], ((0, 0), (0, 0), (0, 0), (0, LANES - MLA_NOPE)))
    w_k = w_k.reshape(DEPTH, MLA_KV_LORA, MLA_HEADS * LANES).astype(BF16)
    w_v = ukv[..., MLA_NOPE:].reshape(DEPTH, MLA_KV_LORA, MLA_HEADS * MLA_V).astype(BF16)
    sgu_bias = jnp.repeat(jnp.swapaxes(sgu_b, 1, 2), BRANCH_WIDTH // SGU_GROUPS, axis=2)
    return w_packed, uq, w_k, w_v, sgu_w.astype(BF16), sgu_bias, w_branch.astype(BF16), w_out.astype(BF16)


def _rope_tables(n_tokens):
    pos = jnp.arange(n_tokens, dtype=jnp.int32)
    row = (pos // GRID_W).astype(F32)
    col = (pos % GRID_W).astype(F32)
    n_freq = MLA_ROPE // 4
    inv = ROPE_THETA ** (-jnp.arange(n_freq, dtype=F32) / n_freq)
    ang = jnp.concatenate([row[:, None] * inv, col[:, None] * inv], axis=-1)
    cos, sin = jnp.cos(ang), jnp.sin(ang)
    pad_l, pad_r = ROPE_LANE0, LANES - ROPE_LANE0 - MLA_ROPE
    cos_t = jnp.pad(jnp.concatenate([cos, cos], axis=1), ((0, 0), (pad_l, pad_r)), constant_values=1.0)
    sin_t = jnp.pad(jnp.concatenate([-sin, sin], axis=1), ((0, 0), (pad_l, pad_r)))
    return cos_t, sin_t


def kernel(x_prompt, x_sample, cache_na_k, cache_na_v, cache_mla_ckv, cache_mla_krope, c, c_ctx, norm_g, w_mod, b_mod, w_in, na_rpb, sgu_w, sgu_b, mla_q_norm, mla_w_uq, mla_kv_norm, mla_w_ukv, w_branch, w_out, final_norm_g):
    n_ctx, ctx_len, _ = x_prompt.shape
    n_lat, lat_len, _ = x_sample.shape
    past = cache_na_k.shape[2]
    assert ctx_len == TILE and lat_len % TILE == 0 and lat_len // GRID_W == 16 and past == TILE

    w_packed, w_uq, w_k, w_v, sgu_wb, sgu_bias, w_br, w_o = _pack_params(
        w_in, sgu_w, sgu_b, mla_w_uq, mla_w_ukv, w_branch, w_out)
    rope_tabs = _rope_tables(lat_len)
    tab, tab_odd = _na_bias_tables(na_rpb)
    norm_g3 = norm_g.reshape(DEPTH, 1, D_MODEL)
    kv_norm3 = mla_kv_norm.reshape(DEPTH, 1, MLA_KV_LORA)
    q_norm3 = mla_q_norm.reshape(DEPTH, 1, MLA_Q_LORA)
    final_g = final_norm_g.reshape(1, D_MODEL)

    cond_rows = 16
    cond = jnp.concatenate([c, c_ctx[None, :], jnp.zeros((cond_rows - n_lat - 1, D_MODEL), F32)], axis=0)
    mod = _modulation(cond, w_mod, b_mod).reshape(DEPTH, cond_rows, 1, 3 * D_MODEL)
    ctx_row = lambda token: n_lat
    lat_row = lambda token: token // lat_len

    cache_k = cache_na_k.reshape(n_lat, DEPTH, past, BRANCH_WIDTH)
    cache_v = cache_na_v.reshape(n_lat, DEPTH, past, BRANCH_WIDTH)
    cache_kr = jnp.pad(cache_mla_krope, ((0, 0), (0, 0), (0, 0), (ROPE_LANE0, LANES - ROPE_LANE0 - MLA_ROPE)))

    xp = x_prompt.reshape(n_ctx * ctx_len, D_MODEL)
    xs = x_sample.reshape(n_lat * lat_len, D_MODEL)
    state = None
    for l in range(DEPTH):
        final = l == DEPTH - 1
        p, ckv, kr, *state = _inproj(xp, mod, ctx_row, l, norm_g3, w_packed, kv_norm3, None, state, True)
        y_na = _na_ctx(p, n_ctx)
        y_mla = _mla(p, ckv, kr, None, None, l, q_norm3, w_uq, w_k, w_v, None, n_ctx, ctx_len)
        xp = _merge(xp, mod, ctx_row, l, p, y_na, y_mla, sgu_wb, sgu_bias, w_br, w_o, final_g, final)
        p, ckv, kr = _inproj(xs, mod, lat_row, l, norm_g3, w_packed, kv_norm3, rope_tabs, None, False)
        y_na = _na_lat(p, cache_k, cache_v, tab, tab_odd, l, n_lat, lat_len)
        y_mla = _mla(p, ckv, kr, cache_mla_ckv, cache_kr, l, q_norm3, w_uq, w_k, w_v, rope_tabs, n_lat, lat_len)
        xs = _merge(xs, mod, lat_row, l, p, y_na, y_mla, sgu_wb, sgu_bias, w_br, w_o, final_g, final)

    s_k, s_v, s_ckv, s_kr = state
    return (xp.reshape(n_ctx, ctx_len, D_MODEL), xs.reshape(n_lat, lat_len, D_MODEL),
            s_k.reshape(n_ctx, DEPTH, ctx_len, NA_HEADS, NA_HEAD_DIM),
            s_v.reshape(n_ctx, DEPTH, ctx_len, NA_HEADS, NA_HEAD_DIM), s_ckv, s_kr)
```

```python
import functools

import numpy as np
import jax
import jax.numpy as jnp
from jax import lax
from jax.experimental import pallas as pl
from jax.experimental.pallas import tpu as pltpu

F32 = jnp.float32
BF16 = jnp.bfloat16

D_MODEL = 1024
DEPTH = 2
GRID_W = 64
BRANCH_WIDTH = 512
N_BRANCH = 3
NA_HEADS = 8
NA_HEAD_DIM = 64
NA_WIN_H = 8
NA_WIN_W = 16
SGU_GROUPS = 4
SGU_CHUNK = 128
MLA_HEADS = 8
MLA_NOPE = 64
MLA_ROPE = 32
MLA_V = 64
MLA_Q_LORA = 384
MLA_KV_LORA = 256
ROPE_THETA = 10000.0
EPS = 1e-6
NEG_INF = -1e30

LANES = 128
TILE = 256
INPROJ_TILE = 512
STRIP = 16
HEAD_PAIRS = NA_HEADS // 2
NA_KEY_ROWS = 12
NA_LOC_KEYS = NA_KEY_ROWS * GRID_W
ROWS_PER_TILE = TILE // GRID_W
TAB_PAD = 4
TAB_BLOCKS = 24
ROPE_LANE0 = MLA_NOPE

C_NAQ, C_NAK, C_NAV, C_NAZ, C_SU, C_SV, C_SZ, C_MZ, C_MG = (512 * i for i in (0, 1, 2, 3, 4, 5, 6, 7, 8))
C_DQ = C_MG + N_BRANCH * D_MODEL
C_DKV = C_DQ + MLA_Q_LORA
C_KR = C_DKV + MLA_KV_LORA
W_COLS = C_KR + LANES
P_COLS = C_DQ + 512
CHUNK = 512

VMEM_LIMIT = 56 * 1024 * 1024


def _dot(a, b):
    return jnp.dot(a, b, preferred_element_type=F32)


def _dot_nt(a, b):
    return lax.dot_general(a, b, (((1,), (1,)), ((), ())), preferred_element_type=F32)


def _rms(x, g):
    return x * lax.rsqrt(jnp.mean(x * x, axis=-1, keepdims=True) + EPS) * g


def _center_norm(x):
    c = x - jnp.mean(x, axis=-1, keepdims=True)
    return c * lax.rsqrt(jnp.mean(c * c, axis=-1, keepdims=True) + EPS)


def _rope128(x, cos, sin):
    lane = lax.broadcasted_iota(jnp.int32, x.shape, 1)
    swapped = jnp.where(lane < ROPE_LANE0 + MLA_ROPE // 2,
                        pltpu.roll(x, LANES - MLA_ROPE // 2, axis=1),
                        pltpu.roll(x, MLA_ROPE // 2, axis=1))
    return x * cos + swapped * sin


def _pair_softmax_pv(s_ref, p_ref, den_ref, segments, values, bias=None):
    for i in range(s_ref.shape[0] // STRIP):
        r = slice(i * STRIP, (i + 1) * STRIP)
        parts = [s_ref[r, c0:c1] for c0, c1 in segments]
        if bias is not None:
            parts[0] = parts[0] + bias(i)
        m = functools.reduce(jnp.maximum, [s.max(axis=-1, keepdims=True) for s in parts])
        den = None
        for (c0, c1), s in zip(segments, parts):
            e = jnp.exp(s - m)
            d = e.sum(axis=-1, keepdims=True)
            den = d if den is None else den + d
            p_ref[r, c0:c1] = e.astype(BF16)
        den_ref[r, :] = jnp.broadcast_to(den, (STRIP, LANES))
    o = None
    for (c0, c1), v in zip(segments, values):
        t = _dot(p_ref[:, c0:c1], v)
        o = t if o is None else o + t
    o = o / den_ref[...]
    lane = lax.broadcasted_iota(jnp.int32, (TILE, LANES), 1)
    return jnp.where(lane < LANES // 2, o[:TILE], o[TILE:])


def _pair_scratch(n_keys):
    return [pltpu.VMEM((2, 2 * TILE, n_keys), F32), pltpu.VMEM((2, 2 * TILE, n_keys), BF16),
            pltpu.VMEM((2, 2 * TILE, LANES), F32)]


def _const_spec(shape, index):
    return pl.BlockSpec(shape, lambda *_: index, pipeline_mode=pl.Buffered(1))


def _params(semantics):
    return pltpu.CompilerParams(dimension_semantics=semantics, vmem_limit_bytes=VMEM_LIMIT)


def _mod_kernel(c_ref, w_ref, b_ref, o_ref):
    c = c_ref[...]
    s = c * jax.nn.sigmoid(c)
    o_ref[...] = jnp.dot(s, w_ref[...], preferred_element_type=F32, precision=lax.Precision.HIGHEST) + b_ref[...]


def _modulation(cond, w_mod, b_mod):
    rows = cond.shape[0]
    return pl.pallas_call(
        _mod_kernel,
        grid=(DEPTH, 3),
        in_specs=[pl.BlockSpec((rows, D_MODEL), lambda l, j: (0, 0)),
                  pl.BlockSpec((None, D_MODEL, D_MODEL), lambda l, j: (l, 0, j)),
                  pl.BlockSpec((None, 1, D_MODEL), lambda l, j: (l, 0, j))],
        out_specs=pl.BlockSpec((None, rows, D_MODEL), lambda l, j: (l, 0, j)),
        out_shape=jax.ShapeDtypeStruct((DEPTH, rows, 3 * D_MODEL), F32),
        compiler_params=_params(("parallel", "parallel")),
        name="modulation",
    )(cond, w_mod, b_mod.reshape(DEPTH, 1, 3 * D_MODEL))


def _inproj_kernel(*refs, emit_state, rope, n_alias):
    x_ref, mod_ref, g_ref, w_ref, kvn_ref = refs[:5]
    refs = refs[5:]
    if rope:
        cos_ref, sin_ref = refs[:2]
        refs = refs[2:]
    refs = refs[n_alias:]
    p_ref, ckv_ref, kr_ref = refs[:3]
    if emit_state:
        sk_ref, sv_ref, sckv_ref, skr_ref = refs[3:7]

    shift = mod_ref[:, 0:D_MODEL]
    scale = mod_ref[:, D_MODEL:2 * D_MODEL]
    tm = x_ref.shape[0]
    h = (_rms(x_ref[...], g_ref[...]) * (1.0 + scale) + shift).astype(BF16)

    def state_view(a):
        return a.reshape(tm // TILE, TILE, a.shape[-1])

    ckv = _rms(_dot(h, w_ref[:, C_DKV:C_KR]), kvn_ref[...])
    ckv_ref[...] = ckv.astype(BF16)
    kr = _dot(h, w_ref[:, C_KR:W_COLS])
    if emit_state:
        sckv_ref[...] = state_view(ckv)
        skr_ref[...] = state_view(kr[:, ROPE_LANE0:ROPE_LANE0 + MLA_ROPE])
    if rope:
        kr = _rope128(kr, cos_ref[...], sin_ref[...])
    kr_ref[...] = kr.astype(BF16)

    for c0 in range(0, P_COLS, CHUNK):
        acc = _dot(h, w_ref[:, c0:c0 + CHUNK])
        if emit_state and c0 == C_NAK:
            sk_ref[...] = state_view(acc)
        if emit_state and c0 == C_NAV:
            sv_ref[...] = state_view(acc)
        if c0 in (C_NAZ, C_SZ, C_MZ):
            acc = acc * jax.nn.sigmoid(acc)
        elif c0 == C_SU:
            acc = jax.nn.gelu(acc)
        elif c0 == C_SV:
            acc = _center_norm(jax.nn.gelu(acc))
        elif C_MG <= c0 < C_DQ:
            acc = jax.nn.sigmoid(acc)
        p_ref[:, c0:c0 + CHUNK] = acc.astype(BF16)


def _inproj(x, mod, mod_row, layer, norm_g, w_packed, kv_norm, rope_tabs, state_in, emit_state):
    n_tok = x.shape[0]
    tm = INPROJ_TILE
    n_tiles = n_tok // tm
    rope = rope_tabs is not None
    in_specs = [pl.BlockSpec((tm, D_MODEL), lambda i: (i, 0)),
                pl.BlockSpec((None, None, 1, 3 * D_MODEL), lambda i: (layer, mod_row(i * tm), 0, 0)),
                _const_spec((None, 1, D_MODEL), (layer, 0, 0)),
                _const_spec((None, D_MODEL, W_COLS), (layer, 0, 0)),
                _const_spec((None, 1, MLA_KV_LORA), (layer, 0, 0))]
    args = [x, mod, norm_g, w_packed, kv_norm]
    if rope:
        tiles_per_seq = rope_tabs[0].shape[0] // tm
        in_specs += [pl.BlockSpec((tm, LANES), lambda i: (i % tiles_per_seq, 0))] * 2
        args += list(rope_tabs)
    out_specs = [pl.BlockSpec((tm, P_COLS), lambda i: (i, 0)),
                 pl.BlockSpec((tm, MLA_KV_LORA), lambda i: (i, 0)),
                 pl.BlockSpec((tm, LANES), lambda i: (i, 0))]
    out_shape = [jax.ShapeDtypeStruct((n_tok, P_COLS), BF16),
                 jax.ShapeDtypeStruct((n_tok, MLA_KV_LORA), BF16),
                 jax.ShapeDtypeStruct((n_tok, LANES), BF16)]
    aliases = {}
    if emit_state:
        widths = (BRANCH_WIDTH, BRANCH_WIDTH, MLA_KV_LORA, MLA_ROPE)
        for k, w in enumerate(widths):
            out_specs.append(pl.BlockSpec((tm // TILE, None, TILE, w), lambda i: (i, layer, 0, 0)))
            out_shape.append(jax.ShapeDtypeStruct((n_tok // TILE, DEPTH, TILE, w), F32))
        if state_in is not None:
            first = len(args)
            in_specs += [pl.BlockSpec(memory_space=pl.ANY)] * 4
            args += list(state_in)
            aliases = {first + k: 3 + k for k in range(4)}
    kern = functools.partial(_inproj_kernel, emit_state=emit_state, rope=rope, n_alias=len(aliases))
    return pl.pallas_call(
        kern, grid=(n_tiles,), in_specs=in_specs, out_specs=out_specs, out_shape=out_shape,
        input_output_aliases=aliases, compiler_params=_params(("parallel",)),
        name="inproj",
    )(*args)


def _na_heads(q_ref, o_ref, scratch, keys_fn, values_fn, bias_fn=None):
    s_ref, p_ref, den_ref = scratch
    lane = lax.broadcasted_iota(jnp.int32, (TILE, LANES), 1)
    low = lane < NA_HEAD_DIM
    for hp in range(HEAD_PAIRS):
        sl = slice(LANES * hp, LANES * (hp + 1))
        q2 = q_ref[:, sl] * (NA_HEAD_DIM ** -0.5)
        zero = jnp.zeros_like(q2)
        q_stack = jnp.concatenate([jnp.where(low, q2, zero), jnp.where(low, zero, q2)], axis=0)
        s_pair, p_pair, den_pair = s_ref.at[hp % 2], p_ref.at[hp % 2], den_ref.at[hp % 2]
        segments = []
        c0 = 0
        for k in keys_fn(sl):
            segments.append((c0, c0 + k.shape[0]))
            s_pair[:, c0:c0 + k.shape[0]] = _dot_nt(q_stack, k)
            c0 += k.shape[0]
        bias = None if bias_fn is None else functools.partial(bias_fn, 2 * hp)
        o_ref[:, sl] = _pair_softmax_pv(s_pair, p_pair, den_pair, segments, values_fn(sl), bias).astype(BF16)


def _na_ctx_kernel(q_ref, k_ref, v_ref, o_ref, *scratch):
    _na_heads(q_ref, o_ref, scratch, lambda sl: [k_ref[:, sl]], lambda sl: [v_ref[:, sl]])


def _na_band_row0(tile_row0, n_rows):
    return min(max(tile_row0 - NA_WIN_H // 2, 0), n_rows - NA_KEY_ROWS)


def _na_band_plans(seq):
    n_rows = seq // GRID_W
    plans = []
    for t in range(seq // TILE):
        band0 = _na_band_row0(t * ROWS_PER_TILE, n_rows)
        plan = []
        for r in range(ROWS_PER_TILE):
            rq = t * ROWS_PER_TILE + r
            row_lo = min(max(rq - NA_WIN_H // 2, 0), n_rows - NA_WIN_H)
            plan.append((band0 - rq + NA_WIN_H - 1 + TAB_PAD, row_lo - band0))
        plans.append(tuple(plan))
    return tuple(plans)


def _na_lat_kernel(q_ref, k_ref, v_ref, kc_ref, vc_ref, tab_ref, tab_odd_ref, o_ref, bias_ref, *scratch, plans, n_rows):
    t = pl.program_id(0)
    lane = lax.broadcasted_iota(jnp.int32, (GRID_W, NA_LOC_KEYS), 1)

    for plan in sorted(set(plans)):
        is_tile = functools.reduce(jnp.logical_or, [t == i for i, p in enumerate(plans) if p == plan])

        @pl.when(jnp.logical_and(pl.program_id(1) == 0, is_tile))
        def _build_bias(plan=plan):
            for r, (blk0, j0) in enumerate(plan):
                src, b0 = (tab_ref, blk0) if blk0 % 2 == 0 else (tab_odd_ref, blk0 - 1)
                in_rows = jnp.logical_and(lane >= j0 * GRID_W, lane < (j0 + NA_WIN_H) * GRID_W)
                for h in range(NA_HEADS):
                    band = src[h, :, b0 * GRID_W:b0 * GRID_W + NA_LOC_KEYS]
                    bias_ref[h, r * GRID_W:(r + 1) * GRID_W, :] = jnp.where(in_rows, band, NEG_INF)

    band_row0 = jnp.clip(t * ROWS_PER_TILE - NA_WIN_H // 2, 0, n_rows - NA_KEY_ROWS)
    rows = pl.ds(pl.multiple_of(band_row0 * GRID_W, TILE), NA_LOC_KEYS)

    def bias(head0, strip):
        row = strip * STRIP
        return bias_ref[head0 + row // TILE, row % TILE:row % TILE + STRIP, :]

    _na_heads(q_ref, o_ref, scratch,
              lambda sl: [k_ref[rows, sl], kc_ref[:, sl].astype(BF16)],
              lambda sl: [v_ref[rows, sl], vc_ref[:, sl].astype(BF16)], bias)


def _na_ctx(p, n_seq):
    blk = lambda c: pl.BlockSpec((TILE, BRANCH_WIDTH), lambda i: (i, c))
    return pl.pallas_call(
        _na_ctx_kernel, grid=(n_seq,),
        in_specs=[blk(C_NAQ // 512), blk(C_NAK // 512), blk(C_NAV // 512)],
        out_specs=pl.BlockSpec((TILE, BRANCH_WIDTH), lambda i: (i, 0)),
        out_shape=jax.ShapeDtypeStruct((n_seq * TILE, BRANCH_WIDTH), BF16),
        scratch_shapes=_pair_scratch(TILE),
        compiler_params=_params(("parallel",)), name="na_ctx",
    )(p, p, p)


def _na_lat(p, cache_k, cache_v, tab, tab_odd, layer, n_seq, seq):
    tiles = seq // TILE
    kv = lambda c: pl.BlockSpec((seq, BRANCH_WIDTH), lambda t, b: (b, c))
    cache = pl.BlockSpec((None, None, cache_k.shape[2], BRANCH_WIDTH), lambda t, b: (b, layer, 0, 0))
    table = _const_spec((None, NA_HEADS, GRID_W, TAB_BLOCKS * GRID_W), (layer, 0, 0, 0))
    return pl.pallas_call(
        functools.partial(_na_lat_kernel, plans=_na_band_plans(seq), n_rows=seq // GRID_W),
        grid=(tiles, n_seq),
        in_specs=[pl.BlockSpec((TILE, BRANCH_WIDTH), lambda t, b: (b * tiles + t, C_NAQ // 512)),
                  kv(C_NAK // 512), kv(C_NAV // 512), cache, cache, table, table],
        out_specs=pl.BlockSpec((TILE, BRANCH_WIDTH), lambda t, b: (b * tiles + t, 0)),
        out_shape=jax.ShapeDtypeStruct((n_seq * seq, BRANCH_WIDTH), BF16),
        scratch_shapes=[pltpu.VMEM((NA_HEADS, TILE, NA_LOC_KEYS), F32)]
        + _pair_scratch(NA_LOC_KEYS + cache_k.shape[2]),
        compiler_params=_params(("parallel", "arbitrary")), name="na_lat",
    )(p, p, p, cache_k, cache_v, tab, tab_odd)


def _na_bias_tables(rpb):
    cq = np.arange(GRID_W)[:, None]
    ck = np.arange(GRID_W)[None, :]
    col_lo = np.clip(cq - NA_WIN_W // 2, 0, GRID_W - NA_WIN_W)
    ok = (ck >= col_lo) & (ck < col_lo + NA_WIN_W)
    rel_c = ck - cq + NA_WIN_W - 1
    onehot = ((rel_c[:, :, None] == np.arange(2 * NA_WIN_W - 1)) & ok[:, :, None]).astype(np.float32)
    tab = jnp.einsum('lhab,uvb->lhuav', rpb, onehot, precision=lax.Precision.HIGHEST)
    tab = jnp.where(ok[None, None, :, None, :], tab, NEG_INF)
    n_rel = 2 * NA_WIN_H - 1
    tab = jnp.pad(tab, ((0, 0), (0, 0), (0, 0), (TAB_PAD, TAB_BLOCKS - n_rel - TAB_PAD), (0, 0)),
                  constant_values=NEG_INF)
    tab = tab.reshape(DEPTH, NA_HEADS, GRID_W, TAB_BLOCKS * GRID_W)
    tab_odd = jnp.pad(tab[..., GRID_W:], ((0, 0), (0, 0), (0, 0), (0, GRID_W)), constant_values=NEG_INF)
    return tab, tab_odd


def _mla_kernel(*refs, n_cache, n_lat, rope):
    dq_ref, ckv_ref, kr_ref = refs[:3]
    refs = refs[3:]
    if n_cache:
        cckv_ref, ckr_ref = refs[:2]
        refs = refs[2:]
    qn_ref, wuq_ref, wk_ref, wv_ref = refs[:4]
    refs = refs[4:]
    if rope:
        cos_ref, sin_ref = refs[:2]
        refs = refs[2:]
    o_ref, kx_ref, vx_ref, s_ref, p_ref, den_ref = refs

    @pl.when(pl.program_id(1) == 0)
    def _expand_keys():
        def fill(r0, ckv, kr):
            n = ckv.shape[0]
            kk = _dot(ckv, wk_ref[...])
            for h in range(MLA_HEADS):
                sl = slice(LANES * h, LANES * (h + 1))
                kx_ref[r0:r0 + n, sl] = (kk[:, sl] + kr).astype(BF16)
            vx_ref[r0:r0 + n, :] = _dot(ckv, wv_ref[...]).astype(BF16)

        if n_cache:
            fill(0, cckv_ref[...].astype(BF16), ckr_ref[...])
        for r0 in range(0, n_lat, TILE):
            fill(n_cache + r0, ckv_ref[r0:r0 + TILE, :], kr_ref[r0:r0 + TILE, :].astype(F32))

    dqn = _rms(dq_ref[:, :MLA_Q_LORA].astype(F32), qn_ref[...])
    q = _dot(dqn.astype(BF16), wuq_ref[...])
    scale = (MLA_NOPE + MLA_ROPE) ** -0.5
    n_keys = kx_ref.shape[0]
    for hp in range(HEAD_PAIRS):
        s_pair, p_pair, den_pair = s_ref.at[hp % 2], p_ref.at[hp % 2], den_ref.at[hp % 2]
        for half in range(2):
            sl = slice(LANES * (2 * hp + half), LANES * (2 * hp + half + 1))
            qh = q[:, sl]
            if rope:
                qh = _rope128(qh, cos_ref[...], sin_ref[...])
            qh = (qh * scale).astype(BF16)
            s_pair[half * TILE:(half + 1) * TILE, :] = _dot_nt(qh, kx_ref[:, sl])
        v2 = vx_ref[:, LANES * hp:LANES * (hp + 1)]
        o_ref[:, LANES * hp:LANES * (hp + 1)] = _pair_softmax_pv(
            s_pair, p_pair, den_pair, [(0, n_keys)], [v2]).astype(BF16)


def _mla(p, ckv, kr, cache_ckv, cache_kr, layer, q_norm, w_uq, w_k, w_v, rope_tabs, n_seq, seq):
    tiles = seq // TILE
    n_cache = 0 if cache_ckv is None else cache_ckv.shape[2]
    rope = rope_tabs is not None
    in_specs = [pl.BlockSpec((TILE, CHUNK), lambda b, t: (b * tiles + t, C_DQ // CHUNK)),
                pl.BlockSpec((seq, MLA_KV_LORA), lambda b, t: (b, 0)),
                pl.BlockSpec((seq, LANES), lambda b, t: (b, 0))]
    args = [p, ckv, kr]
    if n_cache:
        in_specs += [pl.BlockSpec((None, None, n_cache, MLA_KV_LORA), lambda b, t: (b, layer, 0, 0)),
                     pl.BlockSpec((None, None, n_cache, LANES), lambda b, t: (b, layer, 0, 0))]
        args += [cache_ckv, cache_kr]
    in_specs += [_const_spec((None, 1, MLA_Q_LORA), (layer, 0, 0)),
                 _const_spec((None, MLA_Q_LORA, MLA_HEADS * LANES), (layer, 0, 0)),
                 _const_spec((None, MLA_KV_LORA, MLA_HEADS * LANES), (layer, 0, 0)),
                 _const_spec((None, MLA_KV_LORA, MLA_HEADS * MLA_V), (layer, 0, 0))]
    args += [q_norm, w_uq, w_k, w_v]
    if rope:
        in_specs += [pl.BlockSpec((TILE, LANES), lambda b, t: (t, 0))] * 2
        args += list(rope_tabs)
    n_keys = n_cache + seq
    return pl.pallas_call(
        functools.partial(_mla_kernel, n_cache=n_cache, n_lat=seq, rope=rope),
        grid=(n_seq, tiles), in_specs=in_specs,
        out_specs=pl.BlockSpec((TILE, BRANCH_WIDTH), lambda b, t: (b * tiles + t, 0)),
        out_shape=jax.ShapeDtypeStruct((n_seq * seq, BRANCH_WIDTH), BF16),
        scratch_shapes=[pltpu.VMEM((n_keys, MLA_HEADS * LANES), BF16), pltpu.VMEM((n_keys, MLA_HEADS * MLA_V), BF16)]
        + _pair_scratch(n_keys),
        compiler_params=_params(("parallel", "arbitrary")), name="mla",
    )(*args)


def _merge_kernel(x_ref, mod_ref, yna_ref, ymla_ref, naz_ref, su_ref, sv_ref, sz_ref, mz_ref, mg0_ref, mg1_ref, mg2_ref,
                  sw_ref, sb_ref, wb_ref, wo_ref, fg_ref, o_ref, *, final):
    rows = []
    for c0 in range(0, TILE, SGU_CHUNK):
        cols = [_dot(sw_ref[g], sv_ref[c0:c0 + SGU_CHUNK, LANES * g:LANES * (g + 1)]) for g in range(SGU_GROUPS)]
        rows.append(jnp.concatenate(cols, axis=1) + sb_ref[...])
    y_sgu = su_ref[...].astype(F32) * jnp.concatenate(rows, axis=0)

    gated = (yna_ref[...] * naz_ref[...],
             (y_sgu * sz_ref[...].astype(F32)).astype(BF16),
             ymla_ref[...] * mz_ref[...])
    merged = None
    for k, mg_ref in enumerate((mg0_ref, mg1_ref, mg2_ref)):
        term = mg_ref[...].astype(F32) * _dot(gated[k], wb_ref[k])
        merged = term if merged is None else merged + term
    out = _dot(merged.astype(BF16), wo_ref[...])
    gate = mod_ref[:, 2 * D_MODEL:3 * D_MODEL]
    xn = x_ref[...] + gate * out
    if final:
        xn = _rms(xn, fg_ref[...])
    o_ref[...] = xn


def _merge(x, mod, mod_row, layer, p, y_na, y_mla, sgu_w, sgu_b, w_branch, w_out, final_g, final):
    n_tok = x.shape[0]
    half = lambda c: pl.BlockSpec((TILE, BRANCH_WIDTH), lambda i: (i, c))
    full = lambda c: pl.BlockSpec((TILE, D_MODEL), lambda i: (i, c))
    in_specs = [full(0),
                pl.BlockSpec((None, None, 1, 3 * D_MODEL), lambda i: (layer, mod_row(i * TILE), 0, 0)),
                half(0), half(0),
                half(C_NAZ // 512), half(C_SU // 512), half(C_SV // 512), half(C_SZ // 512), half(C_MZ // 512),
                full(C_MG // D_MODEL), full(C_MG // D_MODEL + 1), full(C_MG // D_MODEL + 2),
                _const_spec((None, SGU_GROUPS, SGU_CHUNK, SGU_CHUNK), (layer, 0, 0, 0)),
                _const_spec((None, SGU_CHUNK, BRANCH_WIDTH), (layer, 0, 0)),
                _const_spec((None, N_BRANCH, BRANCH_WIDTH, D_MODEL), (layer, 0, 0, 0)),
                _const_spec((None, D_MODEL, D_MODEL), (layer, 0, 0)),
                _const_spec((1, D_MODEL), (0, 0))]
    return pl.pallas_call(
        functools.partial(_merge_kernel, final=final),
        grid=(n_tok // TILE,), in_specs=in_specs,
        out_specs=full(0), out_shape=jax.ShapeDtypeStruct((n_tok, D_MODEL), F32),
        compiler_params=_params(("parallel",)), name="merge",
    )(x, mod, y_na, y_mla, p, p, p, p, p, p, p, p, sgu_w, sgu_b, w_branch, w_out, final_g)


def _pack_w_in_kernel(w_ref, o_ref):
    src_dq = 4 * BRANCH_WIDTH + 3 * BRANCH_WIDTH
    src_dkv = src_dq + MLA_Q_LORA
    src_kr = src_dkv + MLA_KV_LORA
    src_mz = src_kr + MLA_ROPE
    src_mg = src_mz + BRANCH_WIDTH
    for dst, src, n in ((0, 0, src_dq), (C_MZ, src_mz, BRANCH_WIDTH), (C_MG, src_mg, N_BRANCH * D_MODEL),
                        (C_DQ, src_dq, MLA_Q_LORA), (C_DKV, src_dkv, MLA_KV_LORA)):
        o_ref[:, dst:dst + n] = w_ref[:, src:src + n].astype(BF16)
    rows = w_ref.shape[0]
    kr = jnp.concatenate([jnp.zeros((rows, ROPE_LANE0), F32), w_ref[:, src_kr:src_kr + MLA_ROPE],
                          jnp.zeros((rows, LANES - ROPE_LANE0 - MLA_ROPE), F32)], axis=1)
    o_ref[:, C_KR:W_COLS] = kr.astype(BF16)


def _pack_w_in(w_in):
    rows = 256
    return pl.pallas_call(
        _pack_w_in_kernel, grid=(DEPTH, D_MODEL // rows),
        in_specs=[pl.BlockSpec((None, rows, w_in.shape[2]), lambda l, i: (l, i, 0))],
        out_specs=pl.BlockSpec((None, rows, W_COLS), lambda l, i: (l, i, 0)),
        out_shape=jax.ShapeDtypeStruct((DEPTH, D_MODEL, W_COLS), BF16),
        compiler_params=_params(("parallel", "parallel")), name="pack_w_in",
    )(w_in)


def _pack_params(w_in, sgu_w, sgu_b, mla_w_uq, mla_w_ukv, w_branch, w_out):
    w_packed = _pack_w_in(w_in)
    uq = mla_w_uq.reshape(DEPTH, MLA_Q_LORA, MLA_HEADS, MLA_NOPE + MLA_ROPE)
    uq = jnp.pad(uq, ((0, 0), (0, 0), (0, 0), (0, LANES - MLA_NOPE - MLA_ROPE)))
    uq = uq.reshape(DEPTH, MLA_Q_LORA, MLA_HEADS * LANES).astype(BF16)
    ukv = mla_w_ukv.reshape(DEPTH, MLA_KV_LORA, MLA_HEADS, MLA_NOPE + MLA_V)
    w_k = jnp.pad(ukv[..., :MLA_NOPE], ((0, 0), (0, 0), (0, 0), (0, LANES - MLA_NOPE)))
    w_k = w_k.reshape(DEPTH, MLA_KV_LORA, MLA_HEADS * LANES).astype(BF16)
    w_v = ukv[..., MLA_NOPE:].reshape(DEPTH, MLA_KV_LORA, MLA_HEADS * MLA_V).astype(BF16)
    sgu_bias = jnp.repeat(jnp.swapaxes(sgu_b, 1, 2), BRANCH_WIDTH // SGU_GROUPS, axis=2)
    return w_packed, uq, w_k, w_v, sgu_w.astype(BF16), sgu_bias, w_branch.astype(BF16), w_out.astype(BF16)


def _rope_tables(n_tokens):
    pos = jnp.arange(n_tokens, dtype=jnp.int32)
    row = (pos // GRID_W).astype(F32)
    col = (pos % GRID_W).astype(F32)
    n_freq = MLA_ROPE // 4
    inv = ROPE_THETA ** (-jnp.arange(n_freq, dtype=F32) / n_freq)
    ang = jnp.concatenate([row[:, None] * inv, col[:, None] * inv], axis=-1)
    cos, sin = jnp.cos(ang), jnp.sin(ang)
    pad_l, pad_r = ROPE_LANE0, LANES - ROPE_LANE0 - MLA_ROPE
    cos_t = jnp.pad(jnp.concatenate([cos, cos], axis=1), ((0, 0), (pad_l, pad_r)), constant_values=1.0)
    sin_t = jnp.pad(jnp.concatenate([-sin, sin], axis=1), ((0, 0), (pad_l, pad_r)))
    return cos_t, sin_t


def kernel(x_prompt, x_sample, cache_na_k, cache_na_v, cache_mla_ckv, cache_mla_krope, c, c_ctx, norm_g, w_mod, b_mod, w_in, na_rpb, sgu_w, sgu_b, mla_q_norm, mla_w_uq, mla_kv_norm, mla_w_ukv, w_branch, w_out, final_norm_g):
    n_ctx, ctx_len, _ = x_prompt.shape
    n_lat, lat_len, _ = x_sample.shape
    past = cache_na_k.shape[2]
    assert ctx_len == TILE and lat_len % TILE == 0 and lat_len // GRID_W == 16 and past == TILE

    w_packed, w_uq, w_k, w_v, sgu_wb, sgu_bias, w_br, w_o = _pack_params(
        w_in, sgu_w, sgu_b, mla_w_uq, mla_w_ukv, w_branch, w_out)
    rope_tabs = _rope_tables(lat_len)
    tab, tab_odd = _na_bias_tables(na_rpb)
    norm_g3 = norm_g.reshape(DEPTH, 1, D_MODEL)
    kv_norm3 = mla_kv_norm.reshape(DEPTH, 1, MLA_KV_LORA)
    q_norm3 = mla_q_norm.reshape(DEPTH, 1, MLA_Q_LORA)
    final_g = final_norm_g.reshape(1, D_MODEL)

    cond_rows = 16
    cond = jnp.concatenate([c, c_ctx[None, :], jnp.zeros((cond_rows - n_lat - 1, D_MODEL), F32)], axis=0)
    mod = _modulation(cond, w_mod, b_mod).reshape(DEPTH, cond_rows, 1, 3 * D_MODEL)
    ctx_row = lambda token: n_lat
    lat_row = lambda token: token // lat_len

    cache_k = cache_na_k.reshape(n_lat, DEPTH, past, BRANCH_WIDTH)
    cache_v = cache_na_v.reshape(n_lat, DEPTH, past, BRANCH_WIDTH)
    cache_kr = jnp.pad(cache_mla_krope, ((0, 0), (0, 0), (0, 0), (ROPE_LANE0, LANES - ROPE_LANE0 - MLA_ROPE)))

    xp = x_prompt.reshape(n_ctx * ctx_len, D_MODEL)
    xs = x_sample.reshape(n_lat * lat_len, D_MODEL)
    state = None
    for l in range(DEPTH):
        final = l == DEPTH - 1
        p, ckv, kr, *state = _inproj(xp, mod, ctx_row, l, norm_g3, w_packed, kv_norm3, None, state, True)
        y_na = _na_ctx(p, n_ctx)
        y_mla = _mla(p, ckv, kr, None, None, l, q_norm3, w_uq, w_k, w_v, None, n_ctx, ctx_len)
        xp = _merge(xp, mod, ctx_row, l, p, y_na, y_mla, sgu_wb, sgu_bias, w_br, w_o, final_g, final)
        p, ckv, kr = _inproj(xs, mod, lat_row, l, norm_g3, w_packed, kv_norm3, rope_tabs, None, False)
        y_na = _na_lat(p, cache_k, cache_v, tab, tab_odd, l, n_lat, lat_len)
        y_mla = _mla(p, ckv, kr, cache_mla_ckv, cache_kr, l, q_norm3, w_uq, w_k, w_v, rope_tabs, n_lat, lat_len)
        xs = _merge(xs, mod, lat_row, l, p, y_na, y_mla, sgu_wb, sgu_bias, w_br, w_o, final_g, final)

    s_k, s_v, s_ckv, s_kr = state
    return (xp.reshape(n_ctx, ctx_len, D_MODEL), xs.reshape(n_lat, lat_len, D_MODEL),
            s_k.reshape(n_ctx, DEPTH, ctx_len, NA_HEADS, NA_HEAD_DIM),
            s_v.reshape(n_ctx, DEPTH, ctx_len, NA_HEADS, NA_HEAD_DIM), s_ckv, s_kr)
```

```python
import functools

import numpy as np
import jax
import jax.numpy as jnp
from jax import lax
from jax.experimental import pallas as pl
from jax.experimental.pallas import tpu as pltpu

F32 = jnp.float32
BF16 = jnp.bfloat16

D_MODEL = 1024
DEPTH = 2
GRID_W = 64
BRANCH_WIDTH = 512
N_BRANCH = 3
NA_HEADS = 8
NA_HEAD_DIM = 64
NA_WIN_H = 8
NA_WIN_W = 16
SGU_GROUPS = 4
SGU_CHUNK = 128
MLA_HEADS = 8
MLA_NOPE = 64
MLA_ROPE = 32
MLA_V = 64
MLA_Q_LORA = 384
MLA_KV_LORA = 256
ROPE_THETA = 10000.0
EPS = 1e-6
NEG_INF = -1e30

LANES = 128
TILE = 256
INPROJ_TILE = 512
STRIP = 16
HEAD_PAIRS = NA_HEADS // 2
NA_KEY_ROWS = 12
NA_LOC_KEYS = NA_KEY_ROWS * GRID_W
ROWS_PER_TILE = TILE // GRID_W
TAB_PAD = 4
TAB_BLOCKS = 24
ROPE_LANE0 = MLA_NOPE

C_NAQ, C_NAK, C_NAV, C_NAZ, C_SU, C_SV, C_SZ, C_MZ, C_MG = (512 * i for i in (0, 1, 2, 3, 4, 5, 6, 7, 8))
C_DQ = C_MG + N_BRANCH * D_MODEL
C_DKV = C_DQ + MLA_Q_LORA
C_KR = C_DKV + MLA_KV_LORA
W_COLS = C_KR + LANES
P_COLS = C_DQ + 512
CHUNK = 512

VMEM_LIMIT = 56 * 1024 * 1024


def _dot(a, b):
    return jnp.dot(a, b, preferred_element_type=F32)


def _dot_nt(a, b):
    return lax.dot_general(a, b, (((1,), (1,)), ((), ())), preferred_element_type=F32)


def _rms(x, g):
    return x * lax.rsqrt(jnp.mean(x * x, axis=-1, keepdims=True) + EPS) * g


def _center_norm(x):
    c = x - jnp.mean(x, axis=-1, keepdims=True)
    return c * lax.rsqrt(jnp.mean(c * c, axis=-1, keepdims=True) + EPS)


def _rope128(x, cos, sin):
    lane = lax.broadcasted_iota(jnp.int32, x.shape, 1)
    swapped = jnp.where(lane < ROPE_LANE0 + MLA_ROPE // 2,
                        pltpu.roll(x, LANES - MLA_ROPE // 2, axis=1),
                        pltpu.roll(x, MLA_ROPE // 2, axis=1))
    return x * cos + swapped * sin


def _pair_softmax_pv(s_ref, p_ref, den_ref, segments, values, bias=None):
    for i in range(s_ref.shape[0] // STRIP):
        r = slice(i * STRIP, (i + 1) * STRIP)
        parts = [s_ref[r, c0:c1] for c0, c1 in segments]
        if bias is not None:
            parts[0] = parts[0] + bias(i)
        m = functools.reduce(jnp.maximum, [s.max(axis=-1, keepdims=True) for s in parts])
        den = None
        for (c0, c1), s in zip(segments, parts):
            e = jnp.exp(s - m)
            d = e.sum(axis=-1, keepdims=True)
            den = d if den is None else den + d
            p_ref[r, c0:c1] = e.astype(BF16)
        den_ref[r, :] = jnp.broadcast_to(den, (STRIP, LANES))
    o = None
    for (c0, c1), (v, transposed) in zip(segments, values):
        t = _dot_nt(p_ref[:, c0:c1], v) if transposed else _dot(p_ref[:, c0:c1], v)
        o = t if o is None else o + t
    o = o / den_ref[...]
    lane = lax.broadcasted_iota(jnp.int32, (TILE, LANES), 1)
    return jnp.where(lane < LANES // 2, o[:TILE], o[TILE:])


def _pair_scratch(n_keys):
    return [pltpu.VMEM((2, 2 * TILE, n_keys), F32), pltpu.VMEM((2, 2 * TILE, n_keys), BF16),
            pltpu.VMEM((2, 2 * TILE, LANES), F32)]


def _const_spec(shape, index):
    return pl.BlockSpec(shape, lambda *_: index, pipeline_mode=pl.Buffered(1))


def _params(semantics):
    return pltpu.CompilerParams(dimension_semantics=semantics, vmem_limit_bytes=VMEM_LIMIT)


def _mod_kernel(c_ref, w_ref, b_ref, o_ref):
    c = c_ref[...]
    s = c * jax.nn.sigmoid(c)
    o_ref[...] = jnp.dot(s, w_ref[...], preferred_element_type=F32, precision=lax.Precision.HIGHEST) + b_ref[...]


def _modulation(cond, w_mod, b_mod):
    rows = cond.shape[0]
    return pl.pallas_call(
        _mod_kernel,
        grid=(DEPTH, 3),
        in_specs=[pl.BlockSpec((rows, D_MODEL), lambda l, j: (0, 0)),
                  pl.BlockSpec((None, D_MODEL, D_MODEL), lambda l, j: (l, 0, j)),
                  pl.BlockSpec((None, 1, D_MODEL), lambda l, j: (l, 0, j))],
        out_specs=pl.BlockSpec((None, rows, D_MODEL), lambda l, j: (l, 0, j)),
        out_shape=jax.ShapeDtypeStruct((DEPTH, rows, 3 * D_MODEL), F32),
        compiler_params=_params(("parallel", "parallel")),
        name="modulation",
    )(cond, w_mod, b_mod.reshape(DEPTH, 1, 3 * D_MODEL))


def _inproj_kernel(*refs, emit_state, rope, n_alias):
    x_ref, mod_ref, g_ref, w_ref, kvn_ref = refs[:5]
    refs = refs[5:]
    if rope:
        cos_ref, sin_ref = refs[:2]
        refs = refs[2:]
    refs = refs[n_alias:]
    p_ref, ckv_ref, kr_ref = refs[:3]
    if emit_state:
        sk_ref, sv_ref, sckv_ref, skr_ref = refs[3:7]

    shift = mod_ref[:, 0:D_MODEL]
    scale = mod_ref[:, D_MODEL:2 * D_MODEL]
    tm = x_ref.shape[0]
    h = (_rms(x_ref[...], g_ref[...]) * (1.0 + scale) + shift).astype(BF16)

    def state_view(a):
        return a.reshape(tm // TILE, TILE, a.shape[-1])

    ckv = _rms(_dot(h, w_ref[:, C_DKV:C_KR]), kvn_ref[...])
    ckv_ref[...] = ckv.astype(BF16)
    kr = _dot(h, w_ref[:, C_KR:W_COLS])
    if emit_state:
        sckv_ref[...] = state_view(ckv)
        skr_ref[...] = state_view(kr[:, ROPE_LANE0:ROPE_LANE0 + MLA_ROPE])
    if rope:
        kr = _rope128(kr, cos_ref[...], sin_ref[...])
    kr_ref[...] = kr.astype(BF16)

    for c0 in range(0, P_COLS, CHUNK):
        acc = _dot(h, w_ref[:, c0:c0 + CHUNK])
        if emit_state and c0 == C_NAK:
            sk_ref[...] = state_view(acc)
        if emit_state and c0 == C_NAV:
            sv_ref[...] = state_view(acc)
        if c0 in (C_NAZ, C_SZ, C_MZ):
            acc = acc * jax.nn.sigmoid(acc)
        elif c0 == C_SU:
            acc = jax.nn.gelu(acc)
        elif c0 == C_SV:
            acc = _center_norm(jax.nn.gelu(acc))
        elif C_MG <= c0 < C_DQ:
            acc = jax.nn.sigmoid(acc)
        p_ref[:, c0 // CHUNK] = state_view(acc.astype(BF16))


def _inproj(x, mod, mod_row, layer, norm_g, w_packed, kv_norm, rope_tabs, state_in, emit_state):
    n_tok = x.shape[0]
    tm = INPROJ_TILE
    n_tiles = n_tok // tm
    rope = rope_tabs is not None
    in_specs = [pl.BlockSpec((tm, D_MODEL), lambda i: (i, 0)),
                pl.BlockSpec((None, None, 1, 3 * D_MODEL), lambda i: (layer, mod_row(i * tm), 0, 0)),
                _const_spec((None, 1, D_MODEL), (layer, 0, 0)),
                _const_spec((None, D_MODEL, W_COLS), (layer, 0, 0)),
                _const_spec((None, 1, MLA_KV_LORA), (layer, 0, 0))]
    args = [x, mod, norm_g, w_packed, kv_norm]
    if rope:
        tiles_per_seq = rope_tabs[0].shape[0] // tm
        in_specs += [pl.BlockSpec((tm, LANES), lambda i: (i % tiles_per_seq, 0))] * 2
        args += list(rope_tabs)
    out_specs = [pl.BlockSpec((tm // TILE, P_COLS // CHUNK, TILE, CHUNK), lambda i: (i, 0, 0, 0)),
                 pl.BlockSpec((tm, MLA_KV_LORA), lambda i: (i, 0)),
                 pl.BlockSpec((tm, LANES), lambda i: (i, 0))]
    out_shape = [jax.ShapeDtypeStruct((n_tok // TILE, P_COLS // CHUNK, TILE, CHUNK), BF16),
                 jax.ShapeDtypeStruct((n_tok, MLA_KV_LORA), BF16),
                 jax.ShapeDtypeStruct((n_tok, LANES), BF16)]
    aliases = {}
    if emit_state:
        widths = (BRANCH_WIDTH, BRANCH_WIDTH, MLA_KV_LORA, MLA_ROPE)
        for k, w in enumerate(widths):
            out_specs.append(pl.BlockSpec((tm // TILE, None, TILE, w), lambda i: (i, layer, 0, 0)))
            out_shape.append(jax.ShapeDtypeStruct((n_tok // TILE, DEPTH, TILE, w), F32))
        if state_in is not None:
            first = len(args)
            in_specs += [pl.BlockSpec(memory_space=pl.ANY)] * 4
            args += list(state_in)
            aliases = {first + k: 3 + k for k in range(4)}
    kern = functools.partial(_inproj_kernel, emit_state=emit_state, rope=rope, n_alias=len(aliases))
    return pl.pallas_call(
        kern, grid=(n_tiles,), in_specs=in_specs, out_specs=out_specs, out_shape=out_shape,
        input_output_aliases=aliases, compiler_params=_params(("parallel",)),
        name="inproj",
    )(*args)


def _na_heads(q_ref, o_ref, scratch, keys_fn, values_fn, bias_fn=None):
    s_ref, p_ref, den_ref = scratch
    lane = lax.broadcasted_iota(jnp.int32, (TILE, LANES), 1)
    low = lane < NA_HEAD_DIM
    for hp in range(HEAD_PAIRS):
        sl = slice(LANES * hp, LANES * (hp + 1))
        q2 = q_ref[:, sl] * (NA_HEAD_DIM ** -0.5)
        zero = jnp.zeros_like(q2)
        q_stack = jnp.concatenate([jnp.where(low, q2, zero), jnp.where(low, zero, q2)], axis=0)
        s_pair, p_pair, den_pair = s_ref.at[hp % 2], p_ref.at[hp % 2], den_ref.at[hp % 2]
        segments = []
        c0 = 0
        for k, transposed in keys_fn(sl):
            n_keys = k.shape[1] if transposed else k.shape[0]
            segments.append((c0, c0 + n_keys))
            s_pair[:, c0:c0 + n_keys] = _dot(q_stack, k) if transposed else _dot_nt(q_stack, k)
            c0 += n_keys
        bias = None if bias_fn is None else functools.partial(bias_fn, 2 * hp)
        o_ref[:, sl] = _pair_softmax_pv(s_pair, p_pair, den_pair, segments, values_fn(sl), bias).astype(BF16)


def _na_ctx_kernel(q_ref, k_ref, v_ref, o_ref, *scratch):
    _na_heads(q_ref, o_ref, scratch, lambda sl: [(k_ref[:, sl], False)], lambda sl: [(v_ref[:, sl], False)])


def _na_band_row0(tile_row0, n_rows):
    return min(max(tile_row0 - NA_WIN_H // 2, 0), n_rows - NA_KEY_ROWS)


def _na_band_plans(seq):
    n_rows = seq // GRID_W
    plans = []
    for t in range(seq // TILE):
        band0 = _na_band_row0(t * ROWS_PER_TILE, n_rows)
        plan = []
        for r in range(ROWS_PER_TILE):
            rq = t * ROWS_PER_TILE + r
            row_lo = min(max(rq - NA_WIN_H // 2, 0), n_rows - NA_WIN_H)
            plan.append((band0 - rq + NA_WIN_H - 1 + TAB_PAD, row_lo - band0))
        plans.append(tuple(plan))
    return tuple(plans)


def _na_lat_kernel(q_ref, k_ref, v_ref, kc_ref, vc_ref, tab_ref, tab_odd_ref, o_ref, bias_ref, *scratch, plans, n_rows):
    t = pl.program_id(0)
    lane = lax.broadcasted_iota(jnp.int32, (GRID_W, NA_LOC_KEYS), 1)

    for plan in sorted(set(plans)):
        is_tile = functools.reduce(jnp.logical_or, [t == i for i, p in enumerate(plans) if p == plan])

        @pl.when(jnp.logical_and(pl.program_id(1) == 0, is_tile))
        def _build_bias(plan=plan):
            for r, (blk0, j0) in enumerate(plan):
                src, b0 = (tab_ref, blk0) if blk0 % 2 == 0 else (tab_odd_ref, blk0 - 1)
                in_rows = jnp.logical_and(lane >= j0 * GRID_W, lane < (j0 + NA_WIN_H) * GRID_W)
                for h in range(NA_HEADS):
                    band = src[h, :, b0 * GRID_W:b0 * GRID_W + NA_LOC_KEYS]
                    bias_ref[h, r * GRID_W:(r + 1) * GRID_W, :] = jnp.where(in_rows, band, NEG_INF)

    band_row0 = jnp.clip(t * ROWS_PER_TILE - NA_WIN_H // 2, 0, n_rows - NA_KEY_ROWS)
    band_tiles = pl.ds(band_row0 // ROWS_PER_TILE, NA_LOC_KEYS // TILE)

    def band(ref, sl):
        return ref[band_tiles, :, sl].reshape(NA_LOC_KEYS, LANES)

    def bias(head0, strip):
        row = strip * STRIP
        return bias_ref[head0 + row // TILE, row % TILE:row % TILE + STRIP, :]

    _na_heads(q_ref, o_ref, scratch,
              lambda sl: [(band(k_ref, sl), False), (kc_ref[sl, :].astype(BF16), True)],
              lambda sl: [(band(v_ref, sl), False), (vc_ref[sl, :].astype(BF16), True)], bias)


def _p_block(tile_of, col):
    return pl.BlockSpec((None, None, TILE, CHUNK), lambda *g: (tile_of(*g), col // CHUNK, 0, 0))


def _na_ctx(p, n_seq):
    blk = lambda col: _p_block(lambda i: i, col)
    return pl.pallas_call(
        _na_ctx_kernel, grid=(n_seq,),
        in_specs=[blk(C_NAQ), blk(C_NAK), blk(C_NAV)],
        out_specs=pl.BlockSpec((TILE, BRANCH_WIDTH), lambda i: (i, 0)),
        out_shape=jax.ShapeDtypeStruct((n_seq * TILE, BRANCH_WIDTH), BF16),
        scratch_shapes=_pair_scratch(TILE),
        compiler_params=_params(("parallel",)), name="na_ctx",
    )(p, p, p)


def _na_lat(p, cache_k, cache_v, tab, tab_odd, layer, n_seq, seq):
    tiles = seq // TILE
    n_past = cache_k.shape[3]
    kv = lambda col: pl.BlockSpec((tiles, None, TILE, CHUNK), lambda t, b: (b, col // CHUNK, 0, 0))
    cache = pl.BlockSpec((None, None, BRANCH_WIDTH, n_past), lambda t, b: (b, layer, 0, 0))
    table = _const_spec((None, NA_HEADS, GRID_W, TAB_BLOCKS * GRID_W), (layer, 0, 0, 0))
    return pl.pallas_call(
        functools.partial(_na_lat_kernel, plans=_na_band_plans(seq), n_rows=seq // GRID_W),
        grid=(tiles, n_seq),
        in_specs=[_p_block(lambda t, b: b * tiles + t, C_NAQ),
                  kv(C_NAK), kv(C_NAV), cache, cache, table, table],
        out_specs=pl.BlockSpec((TILE, BRANCH_WIDTH), lambda t, b: (b * tiles + t, 0)),
        out_shape=jax.ShapeDtypeStruct((n_seq * seq, BRANCH_WIDTH), BF16),
        scratch_shapes=[pltpu.VMEM((NA_HEADS, TILE, NA_LOC_KEYS), F32)]
        + _pair_scratch(NA_LOC_KEYS + n_past),
        compiler_params=_params(("parallel", "arbitrary")), name="na_lat",
    )(p, p, p, cache_k, cache_v, tab, tab_odd)


def _na_bias_tables(rpb):
    cq = np.arange(GRID_W)[:, None]
    ck = np.arange(GRID_W)[None, :]
    col_lo = np.clip(cq - NA_WIN_W // 2, 0, GRID_W - NA_WIN_W)
    ok = (ck >= col_lo) & (ck < col_lo + NA_WIN_W)
    rel_c = ck - cq + NA_WIN_W - 1
    onehot = ((rel_c[:, :, None] == np.arange(2 * NA_WIN_W - 1)) & ok[:, :, None]).astype(np.float32)
    tab = jnp.einsum('lhab,uvb->lhuav', rpb, onehot, precision=lax.Precision.HIGHEST)
    tab = jnp.where(ok[None, None, :, None, :], tab, NEG_INF)
    n_rel = 2 * NA_WIN_H - 1
    tab = jnp.pad(tab, ((0, 0), (0, 0), (0, 0), (TAB_PAD, TAB_BLOCKS - n_rel - TAB_PAD), (0, 0)),
                  constant_values=NEG_INF)
    tab = tab.reshape(DEPTH, NA_HEADS, GRID_W, TAB_BLOCKS * GRID_W)
    tab_odd = jnp.pad(tab[..., GRID_W:], ((0, 0), (0, 0), (0, 0), (0, GRID_W)), constant_values=NEG_INF)
    return tab, tab_odd


def _mla_kernel(*refs, n_cache, n_lat, rope):
    dq_ref, ckv_ref, kr_ref = refs[:3]
    refs = refs[3:]
    if n_cache:
        cckv_ref, ckr_ref = refs[:2]
        refs = refs[2:]
    qn_ref, wuq_ref, wk_ref, wv_ref = refs[:4]
    refs = refs[4:]
    if rope:
        cos_ref, sin_ref = refs[:2]
        refs = refs[2:]
    o_ref, kx_ref, vx_ref, s_ref, p_ref, den_ref = refs

    @pl.when(pl.program_id(1) == 0)
    def _expand_keys():
        def fill(r0, ckv, kr):
            n = ckv.shape[0]
            kk = _dot(ckv, wk_ref[...])
            for h in range(MLA_HEADS):
                sl = slice(LANES * h, LANES * (h + 1))
                kx_ref[r0:r0 + n, sl] = (kk[:, sl] + kr).astype(BF16)
            vx_ref[r0:r0 + n, :] = _dot(ckv, wv_ref[...]).astype(BF16)

        if n_cache:
            fill(0, cckv_ref[...].astype(BF16), ckr_ref[...])
        for r0 in range(0, n_lat, TILE):
            fill(n_cache + r0, ckv_ref[r0:r0 + TILE, :], kr_ref[r0:r0 + TILE, :].astype(F32))

    dqn = _rms(dq_ref[:, :MLA_Q_LORA].astype(F32), qn_ref[...])
    q = _dot(dqn.astype(BF16), wuq_ref[...])
    scale = (MLA_NOPE + MLA_ROPE) ** -0.5
    n_keys = kx_ref.shape[0]
    for hp in range(HEAD_PAIRS):
        s_pair, p_pair, den_pair = s_ref.at[hp % 2], p_ref.at[hp % 2], den_ref.at[hp % 2]
        for half in range(2):
            sl = slice(LANES * (2 * hp + half), LANES * (2 * hp + half + 1))
            qh = q[:, sl]
            if rope:
                qh = _rope128(qh, cos_ref[...], sin_ref[...])
            qh = (qh * scale).astype(BF16)
            s_pair[half * TILE:(half + 1) * TILE, :] = _dot_nt(qh, kx_ref[:, sl])
        v2 = vx_ref[:, LANES * hp:LANES * (hp + 1)]
        o_ref[:, LANES * hp:LANES * (hp + 1)] = _pair_softmax_pv(
            s_pair, p_pair, den_pair, [(0, n_keys)], [(v2, False)]).astype(BF16)


def _mla(p, ckv, kr, cache_ckv, cache_kr, layer, q_norm, w_uq, w_k, w_v, rope_tabs, n_seq, seq):
    tiles = seq // TILE
    n_cache = 0 if cache_ckv is None else cache_ckv.shape[2]
    rope = rope_tabs is not None
    in_specs = [_p_block(lambda b, t: b * tiles + t, C_DQ),
                pl.BlockSpec((seq, MLA_KV_LORA), lambda b, t: (b, 0)),
                pl.BlockSpec((seq, LANES), lambda b, t: (b, 0))]
    args = [p, ckv, kr]
    if n_cache:
        in_specs += [pl.BlockSpec((None, None, n_cache, MLA_KV_LORA), lambda b, t: (b, layer, 0, 0)),
                     pl.BlockSpec((None, None, n_cache, LANES), lambda b, t: (b, layer, 0, 0))]
        args += [cache_ckv, cache_kr]
    in_specs += [_const_spec((None, 1, MLA_Q_LORA), (layer, 0, 0)),
                 _const_spec((None, MLA_Q_LORA, MLA_HEADS * LANES), (layer, 0, 0)),
                 _const_spec((None, MLA_KV_LORA, MLA_HEADS * LANES), (layer, 0, 0)),
                 _const_spec((None, MLA_KV_LORA, MLA_HEADS * MLA_V), (layer, 0, 0))]
    args += [q_norm, w_uq, w_k, w_v]
    if rope:
        in_specs += [pl.BlockSpec((TILE, LANES), lambda b, t: (t, 0))] * 2
        args += list(rope_tabs)
    n_keys = n_cache + seq
    return pl.pallas_call(
        functools.partial(_mla_kernel, n_cache=n_cache, n_lat=seq, rope=rope),
        grid=(n_seq, tiles), in_specs=in_specs,
        out_specs=pl.BlockSpec((TILE, BRANCH_WIDTH), lambda b, t: (b * tiles + t, 0)),
        out_shape=jax.ShapeDtypeStruct((n_seq * seq, BRANCH_WIDTH), BF16),
        scratch_shapes=[pltpu.VMEM((n_keys, MLA_HEADS * LANES), BF16), pltpu.VMEM((n_keys, MLA_HEADS * MLA_V), BF16)]
        + _pair_scratch(n_keys),
        compiler_params=_params(("parallel", "arbitrary")), name="mla",
    )(*args)


def _merge_kernel(x_ref, mod_ref, yna_ref, ymla_ref, naz_ref, su_ref, sv_ref, sz_ref, mz_ref, mg0_ref, mg1_ref, mg2_ref,
                  sw_ref, sb_ref, wb_ref, wo_ref, fg_ref, o_ref, *, final):
    rows = []
    for c0 in range(0, TILE, SGU_CHUNK):
        cols = [_dot(sw_ref[g], sv_ref[c0:c0 + SGU_CHUNK, LANES * g:LANES * (g + 1)]) for g in range(SGU_GROUPS)]
        rows.append(jnp.concatenate(cols, axis=1) + sb_ref[...])
    y_sgu = su_ref[...].astype(F32) * jnp.concatenate(rows, axis=0)

    gated = (yna_ref[...] * naz_ref[...],
             (y_sgu * sz_ref[...].astype(F32)).astype(BF16),
             ymla_ref[...] * mz_ref[...])
    merged = None
    for k, mg_ref in enumerate((mg0_ref, mg1_ref, mg2_ref)):
        gate_k = jnp.concatenate([mg_ref[0], mg_ref[1]], axis=1)
        term = gate_k.astype(F32) * _dot(gated[k], wb_ref[k])
        merged = term if merged is None else merged + term
    out = _dot(merged.astype(BF16), wo_ref[...])
    gate = mod_ref[:, 2 * D_MODEL:3 * D_MODEL]
    xn = x_ref[...] + gate * out
    if final:
        xn = _rms(xn, fg_ref[...])
    o_ref[...] = xn


def _merge(x, mod, mod_row, layer, p, y_na, y_mla, sgu_w, sgu_b, w_branch, w_out, final_g, final):
    n_tok = x.shape[0]
    half = pl.BlockSpec((TILE, BRANCH_WIDTH), lambda i: (i, 0))
    full = lambda c: pl.BlockSpec((TILE, D_MODEL), lambda i: (i, c))
    chunk = lambda col: _p_block(lambda i: i, col)
    per_gate = D_MODEL // CHUNK
    gates = lambda k: pl.BlockSpec((None, per_gate, TILE, CHUNK), lambda i: (i, C_MG // D_MODEL + k, 0, 0))
    in_specs = [full(0),
                pl.BlockSpec((None, None, 1, 3 * D_MODEL), lambda i: (layer, mod_row(i * TILE), 0, 0)),
                half, half,
                chunk(C_NAZ), chunk(C_SU), chunk(C_SV), chunk(C_SZ), chunk(C_MZ),
                gates(0), gates(1), gates(2),
                _const_spec((None, SGU_GROUPS, SGU_CHUNK, SGU_CHUNK), (layer, 0, 0, 0)),
                _const_spec((None, SGU_CHUNK, BRANCH_WIDTH), (layer, 0, 0)),
                _const_spec((None, N_BRANCH, BRANCH_WIDTH, D_MODEL), (layer, 0, 0, 0)),
                _const_spec((None, D_MODEL, D_MODEL), (layer, 0, 0)),
                _const_spec((1, D_MODEL), (0, 0))]
    return pl.pallas_call(
        functools.partial(_merge_kernel, final=final),
        grid=(n_tok // TILE,), in_specs=in_specs,
        out_specs=full(0), out_shape=jax.ShapeDtypeStruct((n_tok, D_MODEL), F32),
        compiler_params=_params(("parallel",)), name="merge",
    )(x, mod, y_na, y_mla, p, p, p, p, p, p, p, p, sgu_w, sgu_b, w_branch, w_out, final_g)


SRC_DQ = 7 * BRANCH_WIDTH
SRC_DKV = SRC_DQ + MLA_Q_LORA
SRC_KR = SRC_DKV + MLA_KV_LORA
SRC_MZ = SRC_KR + MLA_ROPE
PACK_PLAIN = C_MZ // CHUNK
PACK_SHIFTED = (C_DQ - C_MZ) // CHUNK
PACK_SHIFT = SRC_MZ % CHUNK
assert SRC_DQ % CHUNK == 0 and (SRC_MZ - PACK_SHIFT) // CHUNK == PACK_PLAIN + 1
assert SRC_DQ + CHUNK + (C_KR - P_COLS) == SRC_KR and SRC_KR + MLA_ROPE == SRC_MZ


def _pack_w_in_kernel(a_ref, b_ref, o_ref):
    j = pl.program_id(1)

    def emit(rows):
        o_ref[...] = rows.T.astype(BF16)

    @pl.when(jnp.logical_or(j < PACK_PLAIN, j == PACK_PLAIN + PACK_SHIFTED))
    def _aligned():
        emit(a_ref[...])

    @pl.when(jnp.logical_and(j >= PACK_PLAIN, j < PACK_PLAIN + PACK_SHIFTED))
    def _shifted():
        emit(jnp.concatenate([a_ref[PACK_SHIFT:, :], b_ref[:PACK_SHIFT, :]], axis=0))

    @pl.when(j == PACK_PLAIN + PACK_SHIFTED + 1)
    def _tail():
        n_dkv = C_KR - P_COLS
        zeros = lambda n: jnp.zeros((n, D_MODEL), F32)
        emit(jnp.concatenate([a_ref[:n_dkv, :], zeros(ROPE_LANE0), a_ref[n_dkv:n_dkv + MLA_ROPE, :],
                              zeros(LANES - ROPE_LANE0 - MLA_ROPE), zeros(CHUNK - n_dkv - LANES)], axis=0))


def _pack_w_in(w_in_t):
    n_chunks = pl.cdiv(W_COLS, CHUNK)
    tail_src = SRC_DQ // CHUNK + 1

    def a_block(j):
        return jnp.where(j < PACK_PLAIN, j,
                         jnp.where(j < PACK_PLAIN + PACK_SHIFTED, j + 1,
                                   jnp.where(j == PACK_PLAIN + PACK_SHIFTED, SRC_DQ // CHUNK, tail_src)))

    def b_block(j):
        shifted = jnp.logical_and(j >= PACK_PLAIN, j < PACK_PLAIN + PACK_SHIFTED)
        return jnp.where(shifted, j + 2, PACK_PLAIN + 2)

    return pl.pallas_call(
        _pack_w_in_kernel, grid=(DEPTH, n_chunks),
        in_specs=[pl.BlockSpec((None, CHUNK, D_MODEL), lambda l, j: (l, a_block(j), 0)),
                  pl.BlockSpec((None, CHUNK, D_MODEL), lambda l, j: (l, b_block(j), 0))],
        out_specs=pl.BlockSpec((None, D_MODEL, CHUNK), lambda l, j: (l, 0, j)),
        out_shape=jax.ShapeDtypeStruct((DEPTH, D_MODEL, W_COLS), BF16),
        compiler_params=_params(("parallel", "arbitrary")), name="pack_w_in",
    )(w_in_t, w_in_t)


def _pack_params(w_in, sgu_w, sgu_b, mla_w_uq, mla_w_ukv, w_branch, w_out):
    w_packed = _pack_w_in(jnp.swapaxes(w_in, 1, 2))
    uq = mla_w_uq.reshape(DEPTH, MLA_Q_LORA, MLA_HEADS, MLA_NOPE + MLA_ROPE)
    uq = jnp.pad(uq, ((0, 0), (0, 0), (0, 0), (0, LANES - MLA_NOPE - MLA_ROPE)))
    uq = uq.reshape(DEPTH, MLA_Q_LORA, MLA_HEADS * LANES).astype(BF16)
    ukv = mla_w_ukv.reshape(DEPTH, MLA_KV_LORA, MLA_HEADS, MLA_NOPE + MLA_V)
    w_k = jnp.pad(ukv[..., :MLA_NOPE], ((0, 0), (0, 0), (0, 0), (0, LANES - MLA_NOPE)))
    w_k = w_k.reshape(DEPTH, MLA_KV_LORA, MLA_HEADS * LANES).astype(BF16)
    w_v = ukv[..., MLA_NOPE:].reshape(DEPTH, MLA_KV_LORA, MLA_HEADS * MLA_V).astype(BF16)
    sgu_bias = jnp.repeat(jnp.swapaxes(sgu_b, 1, 2), BRANCH_WIDTH // SGU_GROUPS, axis=2)
    return w_packed, uq, w_k, w_v, sgu_w.astype(BF16), sgu_bias, w_branch.astype(BF16), w_out.astype(BF16)


def _rope_tables(n_tokens):
    pos = jnp.arange(n_tokens, dtype=jnp.int32)
    row = (pos // GRID_W).astype(F32)
    col = (pos % GRID_W).astype(F32)
    n_freq = MLA_ROPE // 4
    inv = ROPE_THETA ** (-jnp.arange(n_freq, dtype=F32) / n_freq)
    ang = jnp.concatenate([row[:, None] * inv, col[:, None] * inv], axis=-1)
    cos, sin = jnp.cos(ang), jnp.sin(ang)
    pad_l, pad_r = ROPE_LANE0, LANES - ROPE_LANE0 - MLA_ROPE
    cos_t = jnp.pad(jnp.concatenate([cos, cos], axis=1), ((0, 0), (pad_l, pad_r)), constant_values=1.0)
    sin_t = jnp.pad(jnp.concatenate([-sin, sin], axis=1), ((0, 0), (pad_l, pad_r)))
    return cos_t, sin_t


def kernel(x_prompt, x_sample, cache_na_k, cache_na_v, cache_mla_ckv, cache_mla_krope, c, c_ctx, norm_g, w_mod, b_mod, w_in, na_rpb, sgu_w, sgu_b, mla_q_norm, mla_w_uq, mla_kv_norm, mla_w_ukv, w_branch, w_out, final_norm_g):
    n_ctx, ctx_len, _ = x_prompt.shape
    n_lat, lat_len, _ = x_sample.shape
    past = cache_na_k.shape[2]
    assert ctx_len == TILE and lat_len % TILE == 0 and lat_len // GRID_W == 16 and past == TILE

    w_packed, w_uq, w_k, w_v, sgu_wb, sgu_bias, w_br, w_o = _pack_params(
        w_in, sgu_w, sgu_b, mla_w_uq, mla_w_ukv, w_branch, w_out)
    rope_tabs = _rope_tables(lat_len)
    tab, tab_odd = _na_bias_tables(na_rpb)
    norm_g3 = norm_g.reshape(DEPTH, 1, D_MODEL)
    kv_norm3 = mla_kv_norm.reshape(DEPTH, 1, MLA_KV_LORA)
    q_norm3 = mla_q_norm.reshape(DEPTH, 1, MLA_Q_LORA)
    final_g = final_norm_g.reshape(1, D_MODEL)

    cond_rows = 16
    cond = jnp.concatenate([c, c_ctx[None, :], jnp.zeros((cond_rows - n_lat - 1, D_MODEL), F32)], axis=0)
    mod = _modulation(cond, w_mod, b_mod).reshape(DEPTH, cond_rows, 1, 3 * D_MODEL)
    ctx_row = lambda token: n_lat
    lat_row = lambda token: token // lat_len

    cache_k = jnp.transpose(cache_na_k, (0, 1, 3, 4, 2)).reshape(n_lat, DEPTH, BRANCH_WIDTH, past)
    cache_v = jnp.transpose(cache_na_v, (0, 1, 3, 4, 2)).reshape(n_lat, DEPTH, BRANCH_WIDTH, past)
    cache_kr = jnp.pad(cache_mla_krope, ((0, 0), (0, 0), (0, 0), (ROPE_LANE0, LANES - ROPE_LANE0 - MLA_ROPE)))

    xp = x_prompt.reshape(n_ctx * ctx_len, D_MODEL)
    xs = x_sample.reshape(n_lat * lat_len, D_MODEL)
    state = None
    for l in range(DEPTH):
        final = l == DEPTH - 1
        p, ckv, kr, *state = _inproj(xp, mod, ctx_row, l, norm_g3, w_packed, kv_norm3, None, state, True)
        y_na = _na_ctx(p, n_ctx)
        y_mla = _mla(p, ckv, kr, None, None, l, q_norm3, w_uq, w_k, w_v, None, n_ctx, ctx_len)
        xp = _merge(xp, mod, ctx_row, l, p, y_na, y_mla, sgu_wb, sgu_bias, w_br, w_o, final_g, final)
        p, ckv, kr = _inproj(xs, mod, lat_row, l, norm_g3, w_packed, kv_norm3, rope_tabs, None, False)
        y_na = _na_lat(p, cache_k, cache_v, tab, tab_odd, l, n_lat, lat_len)
        y_mla = _mla(p, ckv, kr, cache_mla_ckv, cache_kr, l, q_norm3, w_uq, w_k, w_v, rope_tabs, n_lat, lat_len)
        xs = _merge(xs, mod, lat_row, l, p, y_na, y_mla, sgu_wb, sgu_bias, w_br, w_o, final_g, final)

    s_k, s_v, s_ckv, s_kr = state
    return (xp.reshape(n_ctx, ctx_len, D_MODEL), xs.reshape(n_lat, lat_len, D_MODEL),
            s_k.reshape(n_ctx, DEPTH, ctx_len, NA_HEADS, NA_HEAD_DIM),
            s_v.reshape(n_ctx, DEPTH, ctx_len, NA_HEADS, NA_HEAD_DIM), s_ckv, s_kr)
```

```python
import functools

import numpy as np
import jax
import jax.numpy as jnp
from jax import lax
from jax.experimental import pallas as pl
from jax.experimental.pallas import tpu as pltpu

F32 = jnp.float32
BF16 = jnp.bfloat16

D_MODEL = 1024
DEPTH = 2
GRID_W = 64
BRANCH_WIDTH = 512
N_BRANCH = 3
NA_HEADS = 8
NA_HEAD_DIM = 64
NA_WIN_H = 8
NA_WIN_W = 16
SGU_GROUPS = 4
SGU_CHUNK = 128
MLA_HEADS = 8
MLA_NOPE = 64
MLA_ROPE = 32
MLA_V = 64
MLA_Q_LORA = 384
MLA_KV_LORA = 256
ROPE_THETA = 10000.0
EPS = 1e-6
NEG_INF = -1e30

LANES = 128
TILE = 256
INPROJ_TILE = 512
STRIP = 16
HEAD_PAIRS = NA_HEADS // 2
NA_KEY_ROWS = 12
NA_LOC_KEYS = NA_KEY_ROWS * GRID_W
ROWS_PER_TILE = TILE // GRID_W
TAB_PAD = 4
TAB_BLOCKS = 24
ROPE_LANE0 = MLA_NOPE

C_NAQ, C_NAK, C_NAV, C_NAZ, C_SU, C_SV, C_SZ, C_MZ, C_MG = (512 * i for i in (0, 1, 2, 3, 4, 5, 6, 7, 8))
C_DQ = C_MG + N_BRANCH * D_MODEL
C_DKV = C_DQ + MLA_Q_LORA
C_KR = C_DKV + MLA_KV_LORA
W_COLS = C_KR + LANES
P_COLS = C_DQ + 512
CHUNK = 512

VMEM_LIMIT = 56 * 1024 * 1024


def _dot(a, b):
    return jnp.dot(a, b, preferred_element_type=F32)


def _dot_nt(a, b):
    return lax.dot_general(a, b, (((1,), (1,)), ((), ())), preferred_element_type=F32)


def _rms(x, g):
    return x * lax.rsqrt(jnp.mean(x * x, axis=-1, keepdims=True) + EPS) * g


def _center_norm(x):
    c = x - jnp.mean(x, axis=-1, keepdims=True)
    return c * lax.rsqrt(jnp.mean(c * c, axis=-1, keepdims=True) + EPS)


def _rope128(x, cos, sin):
    lane = lax.broadcasted_iota(jnp.int32, x.shape, 1)
    swapped = jnp.where(lane < ROPE_LANE0 + MLA_ROPE // 2,
                        pltpu.roll(x, LANES - MLA_ROPE // 2, axis=1),
                        pltpu.roll(x, MLA_ROPE // 2, axis=1))
    return x * cos + swapped * sin


def _store_scores(s_ref, m_ref, rows, cols, scores, first):
    s_ref[rows, cols] = scores
    block_max = jnp.broadcast_to(scores.max(axis=-1, keepdims=True), (scores.shape[0], LANES))
    m_ref[rows, :] = block_max if first else jnp.maximum(m_ref[rows, :], block_max)


def _pair_softmax_pv(s_ref, m_ref, p_ref, den_ref, segments, values):
    for i in range(s_ref.shape[0] // STRIP):
        r = slice(i * STRIP, (i + 1) * STRIP)
        m = m_ref[r, :]
        den = None
        for c0, c1 in segments:
            e = jnp.exp(s_ref[r, c0:c1] - jnp.concatenate([m] * ((c1 - c0) // LANES), axis=1))
            d = e.sum(axis=-1, keepdims=True)
            den = d if den is None else den + d
            p_ref[r, c0:c1] = e.astype(BF16)
        den_ref[r, :] = jnp.broadcast_to(den, (STRIP, LANES))
    o = None
    for (c0, c1), (v, transposed) in zip(segments, values):
        t = _dot_nt(p_ref[:, c0:c1], v) if transposed else _dot(p_ref[:, c0:c1], v)
        o = t if o is None else o + t
    o = o / den_ref[...]
    lane = lax.broadcasted_iota(jnp.int32, (TILE, LANES), 1)
    return jnp.where(lane < LANES // 2, o[:TILE], o[TILE:])


def _pair_scratch(n_keys):
    return [pltpu.VMEM((2, 2 * TILE, n_keys), F32), pltpu.VMEM((2, 2 * TILE, LANES), F32),
            pltpu.VMEM((2, 2 * TILE, n_keys), BF16), pltpu.VMEM((2, 2 * TILE, LANES), F32)]


def _const_spec(shape, index):
    return pl.BlockSpec(shape, lambda *_: index, pipeline_mode=pl.Buffered(1))


def _params(semantics):
    return pltpu.CompilerParams(dimension_semantics=semantics, vmem_limit_bytes=VMEM_LIMIT)


def _mod_kernel(c_ref, w_ref, b_ref, o_ref):
    c = c_ref[...]
    s = c * jax.nn.sigmoid(c)
    o_ref[...] = jnp.dot(s, w_ref[...], preferred_element_type=F32, precision=lax.Precision.HIGHEST) + b_ref[...]


def _modulation(cond, w_mod, b_mod):
    rows = cond.shape[0]
    return pl.pallas_call(
        _mod_kernel,
        grid=(DEPTH, 3),
        in_specs=[pl.BlockSpec((rows, D_MODEL), lambda l, j: (0, 0)),
                  pl.BlockSpec((None, D_MODEL, D_MODEL), lambda l, j: (l, 0, j)),
                  pl.BlockSpec((None, 1, D_MODEL), lambda l, j: (l, 0, j))],
        out_specs=pl.BlockSpec((None, rows, D_MODEL), lambda l, j: (l, 0, j)),
        out_shape=jax.ShapeDtypeStruct((DEPTH, rows, 3 * D_MODEL), F32),
        compiler_params=_params(("parallel", "parallel")),
        name="modulation",
    )(cond, w_mod, b_mod.reshape(DEPTH, 1, 3 * D_MODEL))


def _inproj_kernel(*refs, emit_state, rope, n_alias):
    x_ref, mod_ref, g_ref, w_ref, kvn_ref = refs[:5]
    refs = refs[5:]
    if rope:
        cos_ref, sin_ref = refs[:2]
        refs = refs[2:]
    refs = refs[n_alias:]
    p_ref, ckv_ref, kr_ref = refs[:3]
    if emit_state:
        sk_ref, sv_ref, sckv_ref, skr_ref = refs[3:7]

    shift = mod_ref[:, 0:D_MODEL]
    scale = mod_ref[:, D_MODEL:2 * D_MODEL]
    tm = x_ref.shape[0]
    h = (_rms(x_ref[...], g_ref[...]) * (1.0 + scale) + shift).astype(BF16)

    def state_view(a):
        return a.reshape(tm // TILE, TILE, a.shape[-1])

    ckv = _rms(_dot(h, w_ref[:, C_DKV:C_KR]), kvn_ref[...])
    ckv_ref[...] = ckv.astype(BF16)
    kr = _dot(h, w_ref[:, C_KR:W_COLS])
    if emit_state:
        sckv_ref[...] = state_view(ckv)
        skr_ref[...] = state_view(kr[:, ROPE_LANE0:ROPE_LANE0 + MLA_ROPE])
    if rope:
        kr = _rope128(kr, cos_ref[...], sin_ref[...])
    kr_ref[...] = kr.astype(BF16)

    for c0 in range(0, P_COLS, CHUNK):
        acc = _dot(h, w_ref[:, c0:c0 + CHUNK])
        if emit_state and c0 == C_NAK:
            sk_ref[...] = state_view(acc)
        if emit_state and c0 == C_NAV:
            sv_ref[...] = state_view(acc)
        if c0 in (C_NAZ, C_SZ, C_MZ):
            acc = acc * jax.nn.sigmoid(acc)
        elif c0 == C_SU:
            acc = jax.nn.gelu(acc)
        elif c0 == C_SV:
            acc = _center_norm(jax.nn.gelu(acc))
        elif C_MG <= c0 < C_DQ:
            acc = jax.nn.sigmoid(acc)
        p_ref[:, c0 // CHUNK] = state_view(acc.astype(BF16))


def _inproj(x, mod, mod_row, layer, norm_g, w_packed, kv_norm, rope_tabs, state_in, emit_state, tm):
    n_tok = x.shape[0]
    n_tiles = n_tok // tm
    rope = rope_tabs is not None
    in_specs = [pl.BlockSpec((tm, D_MODEL), lambda i: (i, 0)),
                pl.BlockSpec((None, None, 1, 3 * D_MODEL), lambda i: (layer, mod_row(i * tm), 0, 0)),
                _const_spec((None, 1, D_MODEL), (layer, 0, 0)),
                _const_spec((None, D_MODEL, W_COLS), (layer, 0, 0)),
                _const_spec((None, 1, MLA_KV_LORA), (layer, 0, 0))]
    args = [x, mod, norm_g, w_packed, kv_norm]
    if rope:
        tiles_per_seq = rope_tabs[0].shape[0] // tm
        in_specs += [pl.BlockSpec((tm, LANES), lambda i: (i % tiles_per_seq, 0))] * 2
        args += list(rope_tabs)
    out_specs = [pl.BlockSpec((tm // TILE, P_COLS // CHUNK, TILE, CHUNK), lambda i: (i, 0, 0, 0)),
                 pl.BlockSpec((tm, MLA_KV_LORA), lambda i: (i, 0)),
                 pl.BlockSpec((tm, LANES), lambda i: (i, 0))]
    out_shape = [jax.ShapeDtypeStruct((n_tok // TILE, P_COLS // CHUNK, TILE, CHUNK), BF16),
                 jax.ShapeDtypeStruct((n_tok, MLA_KV_LORA), BF16),
                 jax.ShapeDtypeStruct((n_tok, LANES), BF16)]
    aliases = {}
    if emit_state:
        widths = (BRANCH_WIDTH, BRANCH_WIDTH, MLA_KV_LORA, MLA_ROPE)
        for k, w in enumerate(widths):
            out_specs.append(pl.BlockSpec((tm // TILE, None, TILE, w), lambda i: (i, layer, 0, 0)))
            out_shape.append(jax.ShapeDtypeStruct((n_tok // TILE, DEPTH, TILE, w), F32))
        if state_in is not None:
            first = len(args)
            in_specs += [pl.BlockSpec(memory_space=pl.ANY)] * 4
            args += list(state_in)
            aliases = {first + k: 3 + k for k in range(4)}
    kern = functools.partial(_inproj_kernel, emit_state=emit_state, rope=rope, n_alias=len(aliases))
    return pl.pallas_call(
        kern, grid=(n_tiles,), in_specs=in_specs, out_specs=out_specs, out_shape=out_shape,
        input_output_aliases=aliases, compiler_params=_params(("parallel",)),
        name="inproj",
    )(*args)


def _na_heads(q_ref, o_ref, scratch, keys_fn, values_fn, bias_fn=None):
    lane = lax.broadcasted_iota(jnp.int32, (TILE, LANES), 1)
    low = lane < NA_HEAD_DIM
    for hp in range(HEAD_PAIRS):
        sl = slice(LANES * hp, LANES * (hp + 1))
        q2 = q_ref[:, sl] * (NA_HEAD_DIM ** -0.5)
        zero = jnp.zeros_like(q2)
        q_stack = jnp.concatenate([jnp.where(low, q2, zero), jnp.where(low, zero, q2)], axis=0)
        s_pair, m_pair, p_pair, den_pair = (ref.at[hp % 2] for ref in scratch)
        segments = []
        c0 = 0
        for j, (k, transposed) in enumerate(keys_fn(sl)):
            n_keys = k.shape[1] if transposed else k.shape[0]
            scores = _dot(q_stack, k) if transposed else _dot_nt(q_stack, k)
            if j == 0 and bias_fn is not None:
                scores = scores + bias_fn(2 * hp)
            _store_scores(s_pair, m_pair, slice(None), slice(c0, c0 + n_keys), scores, j == 0)
            segments.append((c0, c0 + n_keys))
            c0 += n_keys
        o_ref[:, sl] = _pair_softmax_pv(s_pair, m_pair, p_pair, den_pair, segments, values_fn(sl)).astype(BF16)


def _na_ctx_kernel(q_ref, k_ref, v_ref, o_ref, *scratch):
    _na_heads(q_ref, o_ref, scratch, lambda sl: [(k_ref[:, sl], False)], lambda sl: [(v_ref[:, sl], False)])


def _na_band_row0(tile_row0, n_rows):
    return min(max(tile_row0 - NA_WIN_H // 2, 0), n_rows - NA_KEY_ROWS)


def _na_band_plans(seq):
    n_rows = seq // GRID_W
    plans = []
    for t in range(seq // TILE):
        band0 = _na_band_row0(t * ROWS_PER_TILE, n_rows)
        plan = []
        for r in range(ROWS_PER_TILE):
            rq = t * ROWS_PER_TILE + r
            row_lo = min(max(rq - NA_WIN_H // 2, 0), n_rows - NA_WIN_H)
            plan.append((band0 - rq + NA_WIN_H - 1 + TAB_PAD, row_lo - band0))
        plans.append(tuple(plan))
    return tuple(plans)


def _na_lat_kernel(q_ref, k_ref, v_ref, kc_ref, vc_ref, tab_ref, tab_odd_ref, o_ref, bias_ref, *scratch, plans, n_rows):
    t = pl.program_id(0)
    first_half = lax.broadcasted_iota(jnp.int32, (GRID_W, LANES), 1) < GRID_W
    masked = jnp.full((GRID_W, LANES), NEG_INF, F32)

    for plan in sorted(set(plans)):
        is_tile = functools.reduce(jnp.logical_or, [t == i for i, p in enumerate(plans) if p == plan])

        @pl.when(jnp.logical_and(pl.program_id(1) == 0, is_tile))
        def _build_bias(plan=plan):
            for r, (blk0, j0) in enumerate(plan):
                src, pair0 = (tab_ref, blk0 // 2) if blk0 % 2 == 0 else (tab_odd_ref, (blk0 - 1) // 2)
                for j in range(NA_KEY_ROWS // 2):
                    in_window = [j0 <= 2 * j + half < j0 + NA_WIN_H for half in range(2)]
                    for h in range(NA_HEADS):
                        if not any(in_window):
                            piece = masked
                        elif all(in_window):
                            piece = src[h, pair0 + j]
                        else:
                            keep = first_half if in_window[0] else jnp.logical_not(first_half)
                            piece = jnp.where(keep, src[h, pair0 + j], NEG_INF)
                        bias_ref[h, r * GRID_W:(r + 1) * GRID_W, LANES * j:LANES * (j + 1)] = piece

    band_row0 = jnp.clip(t * ROWS_PER_TILE - NA_WIN_H // 2, 0, n_rows - NA_KEY_ROWS)
    band_tiles = pl.ds(band_row0 // ROWS_PER_TILE, NA_LOC_KEYS // TILE)

    def band(ref, sl):
        return ref[band_tiles, :, sl].reshape(NA_LOC_KEYS, LANES)

    def bias(head0):
        return bias_ref[head0:head0 + 2].reshape(2 * TILE, NA_LOC_KEYS)

    _na_heads(q_ref, o_ref, scratch,
              lambda sl: [(band(k_ref, sl), False), (kc_ref[sl, :].astype(BF16), True)],
              lambda sl: [(band(v_ref, sl), False), (vc_ref[sl, :].astype(BF16), True)], bias)


def _p_block(tile_of, col):
    return pl.BlockSpec((None, None, TILE, CHUNK), lambda *g: (tile_of(*g), col // CHUNK, 0, 0))


def _na_ctx(p, n_seq):
    blk = lambda col: _p_block(lambda i: i, col)
    return pl.pallas_call(
        _na_ctx_kernel, grid=(n_seq,),
        in_specs=[blk(C_NAQ), blk(C_NAK), blk(C_NAV)],
        out_specs=pl.BlockSpec((TILE, BRANCH_WIDTH), lambda i: (i, 0)),
        out_shape=jax.ShapeDtypeStruct((n_seq * TILE, BRANCH_WIDTH), BF16),
        scratch_shapes=_pair_scratch(TILE),
        compiler_params=_params(("parallel",)), name="na_ctx",
    )(p, p, p)


def _na_lat(p, cache_k, cache_v, tab, tab_odd, layer, n_seq, seq):
    tiles = seq // TILE
    n_past = cache_k.shape[3]
    kv = lambda col: pl.BlockSpec((tiles, None, TILE, CHUNK), lambda t, b: (b, col // CHUNK, 0, 0))
    cache = pl.BlockSpec((None, None, BRANCH_WIDTH, n_past), lambda t, b: (b, layer, 0, 0))
    table = _const_spec((None, NA_HEADS, TAB_BLOCKS // 2, GRID_W, LANES), (layer, 0, 0, 0, 0))
    return pl.pallas_call(
        functools.partial(_na_lat_kernel, plans=_na_band_plans(seq), n_rows=seq // GRID_W),
        grid=(tiles, n_seq),
        in_specs=[_p_block(lambda t, b: b * tiles + t, C_NAQ),
                  kv(C_NAK), kv(C_NAV), cache, cache, table, table],
        out_specs=pl.BlockSpec((TILE, BRANCH_WIDTH), lambda t, b: (b * tiles + t, 0)),
        out_shape=jax.ShapeDtypeStruct((n_seq * seq, BRANCH_WIDTH), BF16),
        scratch_shapes=[pltpu.VMEM((NA_HEADS, TILE, NA_LOC_KEYS), F32)]
        + _pair_scratch(NA_LOC_KEYS + n_past),
        compiler_params=_params(("parallel", "arbitrary")), name="na_lat",
    )(p, p, p, cache_k, cache_v, tab, tab_odd)


def _na_bias_tables(rpb):
    cq = np.arange(GRID_W)[:, None]
    ck = np.arange(GRID_W)[None, :]
    col_lo = np.clip(cq - NA_WIN_W // 2, 0, GRID_W - NA_WIN_W)
    ok = (ck >= col_lo) & (ck < col_lo + NA_WIN_W)
    n_b = 2 * NA_WIN_W - 1
    n_rel = 2 * NA_WIN_H - 1
    onehot = ((ck - cq + NA_WIN_W - 1)[:, None, :] == np.arange(n_b)[None, :, None]) & ok[:, None, :]
    expand = np.zeros((GRID_W, 2, n_b, 2, GRID_W), np.float32)
    expand[:, 0, :, 0, :] = onehot
    expand[:, 1, :, 1, :] = onehot
    expand = expand.reshape(GRID_W, 2 * n_b, LANES)
    rows = jnp.pad(rpb, ((0, 0), (0, 0), (TAB_PAD, TAB_BLOCKS + 1 - n_rel - TAB_PAD), (0, 0)))
    real = np.zeros(TAB_BLOCKS + 1, bool)
    real[TAB_PAD:TAB_PAD + n_rel] = True

    def table(first):
        pairs = rows[:, :, first:first + TAB_BLOCKS].reshape(DEPTH, NA_HEADS, TAB_BLOCKS // 2, 2 * n_b)
        tab = jnp.einsum('lhek,ukc->lheuc', pairs, expand, precision=lax.Precision.HIGHEST)
        keep = real[first:first + TAB_BLOCKS].reshape(-1, 1, 2, 1) & ok[None, :, None, :]
        return jnp.where(keep.reshape(TAB_BLOCKS // 2, GRID_W, LANES), tab, NEG_INF)

    return table(0), table(1)


def _mla_kernel(*refs, n_cache, n_lat, rope):
    dq_ref, ckv_ref, kr_ref = refs[:3]
    refs = refs[3:]
    if n_cache:
        cckv_ref, ckr_ref = refs[:2]
        refs = refs[2:]
    qn_ref, wuq_ref, wk_ref, wv_ref = refs[:4]
    refs = refs[4:]
    if rope:
        cos_ref, sin_ref = refs[:2]
        refs = refs[2:]
    o_ref, kx_ref, vx_ref = refs[:3]
    scratch = refs[3:]

    @pl.when(pl.program_id(1) == 0)
    def _expand_keys():
        def fill(r0, ckv, kr):
            n = ckv.shape[0]
            kk = _dot(ckv, wk_ref[...])
            for h in range(MLA_HEADS):
                sl = slice(LANES * h, LANES * (h + 1))
                kx_ref[r0:r0 + n, sl] = (kk[:, sl] + kr).astype(BF16)
            vx_ref[r0:r0 + n, :] = _dot(ckv, wv_ref[...]).astype(BF16)

        if n_cache:
            fill(0, cckv_ref[...].astype(BF16), ckr_ref[...])
        for r0 in range(0, n_lat, TILE):
            fill(n_cache + r0, ckv_ref[r0:r0 + TILE, :], kr_ref[r0:r0 + TILE, :].astype(F32))

    dqn = _rms(dq_ref[:, :MLA_Q_LORA].astype(F32), qn_ref[...])
    q = _dot(dqn.astype(BF16), wuq_ref[...])
    scale = (MLA_NOPE + MLA_ROPE) ** -0.5
    n_keys = kx_ref.shape[0]
    for hp in range(HEAD_PAIRS):
        s_pair, m_pair, p_pair, den_pair = (ref.at[hp % 2] for ref in scratch)
        for half in range(2):
            sl = slice(LANES * (2 * hp + half), LANES * (2 * hp + half + 1))
            qh = q[:, sl]
            if rope:
                qh = _rope128(qh, cos_ref[...], sin_ref[...])
            qh = (qh * scale).astype(BF16)
            _store_scores(s_pair, m_pair, slice(half * TILE, (half + 1) * TILE), slice(None),
                          _dot_nt(qh, kx_ref[:, sl]), True)
        v2 = vx_ref[:, LANES * hp:LANES * (hp + 1)]
        o_ref[:, LANES * hp:LANES * (hp + 1)] = _pair_softmax_pv(
            s_pair, m_pair, p_pair, den_pair, [(0, n_keys)], [(v2, False)]).astype(BF16)


def _mla(p, ckv, kr, cache_ckv, cache_kr, layer, q_norm, w_uq, w_k, w_v, rope_tabs, n_seq, seq):
    tiles = seq // TILE
    n_cache = 0 if cache_ckv is None else cache_ckv.shape[2]
    rope = rope_tabs is not None
    in_specs = [_p_block(lambda b, t: b * tiles + t, C_DQ),
                pl.BlockSpec((seq, MLA_KV_LORA), lambda b, t: (b, 0)),
                pl.BlockSpec((seq, LANES), lambda b, t: (b, 0))]
    args = [p, ckv, kr]
    if n_cache:
        in_specs += [pl.BlockSpec((None, None, n_cache, MLA_KV_LORA), lambda b, t: (b, layer, 0, 0)),
                     pl.BlockSpec((None, None, n_cache, LANES), lambda b, t: (b, layer, 0, 0))]
        args += [cache_ckv, cache_kr]
    in_specs += [_const_spec((None, 1, MLA_Q_LORA), (layer, 0, 0)),
                 _const_spec((None, MLA_Q_LORA, MLA_HEADS * LANES), (layer, 0, 0)),
                 _const_spec((None, MLA_KV_LORA, MLA_HEADS * LANES), (layer, 0, 0)),
                 _const_spec((None, MLA_KV_LORA, MLA_HEADS * MLA_V), (layer, 0, 0))]
    args += [q_norm, w_uq, w_k, w_v]
    if rope:
        in_specs += [pl.BlockSpec((TILE, LANES), lambda b, t: (t, 0))] * 2
        args += list(rope_tabs)
    n_keys = n_cache + seq
    return pl.pallas_call(
        functools.partial(_mla_kernel, n_cache=n_cache, n_lat=seq, rope=rope),
        grid=(n_seq, tiles), in_specs=in_specs,
        out_specs=pl.BlockSpec((TILE, BRANCH_WIDTH), lambda b, t: (b * tiles + t, 0)),
        out_shape=jax.ShapeDtypeStruct((n_seq * seq, BRANCH_WIDTH), BF16),
        scratch_shapes=[pltpu.VMEM((n_keys, MLA_HEADS * LANES), BF16), pltpu.VMEM((n_keys, MLA_HEADS * MLA_V), BF16)]
        + _pair_scratch(n_keys),
        compiler_params=_params(("parallel", "arbitrary")), name="mla",
    )(*args)


def _merge_kernel(x_ref, mod_ref, yna_ref, ymla_ref, naz_ref, su_ref, sv_ref, sz_ref, mz_ref, mg0_ref, mg1_ref, mg2_ref,
                  sw_ref, sb_ref, wb_ref, wo_ref, fg_ref, o_ref, *, final):
    tm = x_ref.shape[0]

    def tokens(ref, *idx):
        return ref[(slice(None),) + idx].reshape(tm, CHUNK)

    sv = tokens(sv_ref)
    rows = []
    for c0 in range(0, tm, SGU_CHUNK):
        cols = [_dot(sw_ref[g], sv[c0:c0 + SGU_CHUNK, LANES * g:LANES * (g + 1)]) for g in range(SGU_GROUPS)]
        rows.append(jnp.concatenate(cols, axis=1) + sb_ref[...])
    y_sgu = tokens(su_ref).astype(F32) * jnp.concatenate(rows, axis=0)

    gated = (yna_ref[...] * tokens(naz_ref),
             (y_sgu * tokens(sz_ref).astype(F32)).astype(BF16),
             ymla_ref[...] * tokens(mz_ref))
    merged = None
    for k, mg_ref in enumerate((mg0_ref, mg1_ref, mg2_ref)):
        gate_k = jnp.concatenate([tokens(mg_ref, j) for j in range(mg_ref.shape[1])], axis=1)
        term = gate_k.astype(F32) * _dot(gated[k], wb_ref[k])
        merged = term if merged is None else merged + term
    out = _dot(merged.astype(BF16), wo_ref[...])
    gate = mod_ref[:, 2 * D_MODEL:3 * D_MODEL]
    xn = x_ref[...] + gate * out
    if final:
        xn = _rms(xn, fg_ref[...])
    o_ref[...] = xn


def _merge(x, mod, mod_row, layer, p, y_na, y_mla, sgu_w, sgu_b, w_branch, w_out, final_g, final, tm):
    n_tok = x.shape[0]
    n = tm // TILE
    half = pl.BlockSpec((tm, BRANCH_WIDTH), lambda i: (i, 0))
    full = lambda c: pl.BlockSpec((tm, D_MODEL), lambda i: (i, c))
    chunk = lambda col: pl.BlockSpec((n, None, TILE, CHUNK), lambda i: (i, col // CHUNK, 0, 0))
    per_gate = D_MODEL // CHUNK
    gates = lambda k: pl.BlockSpec((n, per_gate, TILE, CHUNK), lambda i: (i, C_MG // D_MODEL + k, 0, 0))
    in_specs = [full(0),
                pl.BlockSpec((None, None, 1, 3 * D_MODEL), lambda i: (layer, mod_row(i * tm), 0, 0)),
                half, half,
                chunk(C_NAZ), chunk(C_SU), chunk(C_SV), chunk(C_SZ), chunk(C_MZ),
                gates(0), gates(1), gates(2),
                _const_spec((None, SGU_GROUPS, SGU_CHUNK, SGU_CHUNK), (layer, 0, 0, 0)),
                _const_spec((None, SGU_CHUNK, BRANCH_WIDTH), (layer, 0, 0)),
                _const_spec((None, N_BRANCH, BRANCH_WIDTH, D_MODEL), (layer, 0, 0, 0)),
                _const_spec((None, D_MODEL, D_MODEL), (layer, 0, 0)),
                _const_spec((1, D_MODEL), (0, 0))]
    return pl.pallas_call(
        functools.partial(_merge_kernel, final=final),
        grid=(n_tok // tm,), in_specs=in_specs,
        out_specs=full(0), out_shape=jax.ShapeDtypeStruct((n_tok, D_MODEL), F32),
        compiler_params=_params(("parallel",)), name="merge",
    )(x, mod, y_na, y_mla, p, p, p, p, p, p, p, p, sgu_w, sgu_b, w_branch, w_out, final_g)


SRC_DQ = 7 * BRANCH_WIDTH
SRC_DKV = SRC_DQ + MLA_Q_LORA
SRC_KR = SRC_DKV + MLA_KV_LORA
SRC_MZ = SRC_KR + MLA_ROPE
PACK_PLAIN = C_MZ // CHUNK
PACK_SHIFTED = (C_DQ - C_MZ) // CHUNK
PACK_SHIFT = SRC_MZ % CHUNK
assert SRC_DQ % CHUNK == 0 and (SRC_MZ - PACK_SHIFT) // CHUNK == PACK_PLAIN + 1
assert SRC_DQ + CHUNK + (C_KR - P_COLS) == SRC_KR and SRC_KR + MLA_ROPE == SRC_MZ


def _pack_w_in_kernel(a_ref, b_ref, o_ref):
    j = pl.program_id(1)

    def emit(rows):
        o_ref[...] = rows.T.astype(BF16)

    @pl.when(jnp.logical_or(j < PACK_PLAIN, j == PACK_PLAIN + PACK_SHIFTED))
    def _aligned():
        emit(a_ref[...])

    @pl.when(jnp.logical_and(j >= PACK_PLAIN, j < PACK_PLAIN + PACK_SHIFTED))
    def _shifted():
        emit(jnp.concatenate([a_ref[PACK_SHIFT:, :], b_ref[:PACK_SHIFT, :]], axis=0))

    @pl.when(j == PACK_PLAIN + PACK_SHIFTED + 1)
    def _tail():
        n_dkv = C_KR - P_COLS
        zeros = lambda n: jnp.zeros((n, D_MODEL), F32)
        emit(jnp.concatenate([a_ref[:n_dkv, :], zeros(ROPE_LANE0), a_ref[n_dkv:n_dkv + MLA_ROPE, :],
                              zeros(LANES - ROPE_LANE0 - MLA_ROPE), zeros(CHUNK - n_dkv - LANES)], axis=0))


def _pack_w_in(w_in_t):
    n_chunks = pl.cdiv(W_COLS, CHUNK)
    tail_src = SRC_DQ // CHUNK + 1

    def a_block(j):
        return jnp.where(j < PACK_PLAIN, j,
                         jnp.where(j < PACK_PLAIN + PACK_SHIFTED, j + 1,
                                   jnp.where(j == PACK_PLAIN + PACK_SHIFTED, SRC_DQ // CHUNK, tail_src)))

    def b_block(j):
        shifted = jnp.logical_and(j >= PACK_PLAIN, j < PACK_PLAIN + PACK_SHIFTED)
        return jnp.where(shifted, j + 2, PACK_PLAIN + 2)

    return pl.pallas_call(
        _pack_w_in_kernel, grid=(DEPTH, n_chunks),
        in_specs=[pl.BlockSpec((None, CHUNK, D_MODEL), lambda l, j: (l, a_block(j), 0)),
                  pl.BlockSpec((None, CHUNK, D_MODEL), lambda l, j: (l, b_block(j), 0))],
        out_specs=pl.BlockSpec((None, D_MODEL, CHUNK), lambda l, j: (l, 0, j)),
        out_shape=jax.ShapeDtypeStruct((DEPTH, D_MODEL, W_COLS), BF16),
        compiler_params=_params(("parallel", "arbitrary")), name="pack_w_in",
    )(w_in_t, w_in_t)


def _pack_params(w_in, sgu_w, sgu_b, mla_w_uq, mla_w_ukv, w_branch, w_out):
    w_packed = _pack_w_in(jnp.swapaxes(w_in, 1, 2))
    uq = mla_w_uq.reshape(DEPTH, MLA_Q_LORA, MLA_HEADS, MLA_NOPE + MLA_ROPE)
    uq = jnp.pad(uq, ((0, 0), (0, 0), (0, 0), (0, LANES - MLA_NOPE - MLA_ROPE)))
    uq = uq.reshape(DEPTH, MLA_Q_LORA, MLA_HEADS * LANES).astype(BF16)
    ukv = mla_w_ukv.reshape(DEPTH, MLA_KV_LORA, MLA_HEADS, MLA_NOPE + MLA_V)
    w_k = jnp.pad(ukv[..., :MLA_NOPE], ((0, 0), (0, 0), (0, 0), (0, LANES - MLA_NOPE)))
    w_k = w_k.reshape(DEPTH, MLA_KV_LORA, MLA_HEADS * LANES).astype(BF16)
    w_v = ukv[..., MLA_NOPE:].reshape(DEPTH, MLA_KV_LORA, MLA_HEADS * MLA_V).astype(BF16)
    sgu_bias = jnp.repeat(jnp.swapaxes(sgu_b, 1, 2), BRANCH_WIDTH // SGU_GROUPS, axis=2)
    return w_packed, uq, w_k, w_v, sgu_w.astype(BF16), sgu_bias, w_branch.astype(BF16), w_out.astype(BF16)


def _rope_tables(n_tokens):
    pos = jnp.arange(n_tokens, dtype=jnp.int32)
    row = (pos // GRID_W).astype(F32)
    col = (pos % GRID_W).astype(F32)
    n_freq = MLA_ROPE // 4
    inv = ROPE_THETA ** (-jnp.arange(n_freq, dtype=F32) / n_freq)
    ang = jnp.concatenate([row[:, None] * inv, col[:, None] * inv], axis=-1)
    cos, sin = jnp.cos(ang), jnp.sin(ang)
    pad_l, pad_r = ROPE_LANE0, LANES - ROPE_LANE0 - MLA_ROPE
    cos_t = jnp.pad(jnp.concatenate([cos, cos], axis=1), ((0, 0), (pad_l, pad_r)), constant_values=1.0)
    sin_t = jnp.pad(jnp.concatenate([-sin, sin], axis=1), ((0, 0), (pad_l, pad_r)))
    return cos_t, sin_t


def kernel(x_prompt, x_sample, cache_na_k, cache_na_v, cache_mla_ckv, cache_mla_krope, c, c_ctx, norm_g, w_mod, b_mod, w_in, na_rpb, sgu_w, sgu_b, mla_q_norm, mla_w_uq, mla_kv_norm, mla_w_ukv, w_branch, w_out, final_norm_g):
    n_ctx, ctx_len, _ = x_prompt.shape
    n_lat, lat_len, _ = x_sample.shape
    past = cache_na_k.shape[2]
    assert ctx_len == TILE and lat_len % TILE == 0 and lat_len // GRID_W == 16 and past == TILE

    w_packed, w_uq, w_k, w_v, sgu_wb, sgu_bias, w_br, w_o = _pack_params(
        w_in, sgu_w, sgu_b, mla_w_uq, mla_w_ukv, w_branch, w_out)
    rope_tabs = _rope_tables(lat_len)
    tab, tab_odd = _na_bias_tables(na_rpb)
    norm_g3 = norm_g.reshape(DEPTH, 1, D_MODEL)
    kv_norm3 = mla_kv_norm.reshape(DEPTH, 1, MLA_KV_LORA)
    q_norm3 = mla_q_norm.reshape(DEPTH, 1, MLA_Q_LORA)
    final_g = final_norm_g.reshape(1, D_MODEL)

    cond_rows = 16
    cond = jnp.concatenate([c, c_ctx[None, :], jnp.zeros((cond_rows - n_lat - 1, D_MODEL), F32)], axis=0)
    mod = _modulation(cond, w_mod, b_mod).reshape(DEPTH, cond_rows, 1, 3 * D_MODEL)
    ctx_row = lambda token: n_lat
    lat_row = lambda token: token // lat_len

    cache_k = jnp.transpose(cache_na_k, (0, 1, 3, 4, 2)).reshape(n_lat, DEPTH, BRANCH_WIDTH, past)
    cache_v = jnp.transpose(cache_na_v, (0, 1, 3, 4, 2)).reshape(n_lat, DEPTH, BRANCH_WIDTH, past)
    cache_kr = jnp.pad(cache_mla_krope, ((0, 0), (0, 0), (0, 0), (ROPE_LANE0, LANES - ROPE_LANE0 - MLA_ROPE)))

    xp = x_prompt.reshape(n_ctx * ctx_len, D_MODEL)
    xs = x_sample.reshape(n_lat * lat_len, D_MODEL)
    state = None
    for l in range(DEPTH):
        final = l == DEPTH - 1
        p, ckv, kr, *state = _inproj(xp, mod, ctx_row, l, norm_g3, w_packed, kv_norm3, None, state, True, TILE)
        y_na = _na_ctx(p, n_ctx)
        y_mla = _mla(p, ckv, kr, None, None, l, q_norm3, w_uq, w_k, w_v, None, n_ctx, ctx_len)
        xp = _merge(xp, mod, ctx_row, l, p, y_na, y_mla, sgu_wb, sgu_bias, w_br, w_o, final_g, final, 2 * TILE)
        p, ckv, kr = _inproj(xs, mod, lat_row, l, norm_g3, w_packed, kv_norm3, rope_tabs, None, False, INPROJ_TILE)
        y_na = _na_lat(p, cache_k, cache_v, tab, tab_odd, l, n_lat, lat_len)
        y_mla = _mla(p, ckv, kr, cache_mla_ckv, cache_kr, l, q_norm3, w_uq, w_k, w_v, rope_tabs, n_lat, lat_len)
        xs = _merge(xs, mod, lat_row, l, p, y_na, y_mla, sgu_wb, sgu_bias, w_br, w_o, final_g, final, TILE)

    s_k, s_v, s_ckv, s_kr = state
    return (xp.reshape(n_ctx, ctx_len, D_MODEL), xs.reshape(n_lat, lat_len, D_MODEL),
            s_k.reshape(n_ctx, DEPTH, ctx_len, NA_HEADS, NA_HEAD_DIM),
            s_v.reshape(n_ctx, DEPTH, ctx_len, NA_HEADS, NA_HEAD_DIM), s_ckv, s_kr)
```

```python
import functools

import numpy as np
import jax
import jax.numpy as jnp
from jax import lax
from jax.experimental import pallas as pl
from jax.experimental.pallas import tpu as pltpu

F32 = jnp.float32
BF16 = jnp.bfloat16

D_MODEL = 1024
DEPTH = 2
GRID_W = 64
BRANCH_WIDTH = 512
N_BRANCH = 3
NA_HEADS = 8
NA_HEAD_DIM = 64
NA_WIN_H = 8
NA_WIN_W = 16
SGU_GROUPS = 4
SGU_CHUNK = 128
MLA_HEADS = 8
MLA_NOPE = 64
MLA_ROPE = 32
MLA_V = 64
MLA_Q_LORA = 384
MLA_KV_LORA = 256
ROPE_THETA = 10000.0
EPS = 1e-6
NEG_INF = -1e30

LANES = 128
TILE = 256
MERGE_TILE = 512
STRIP = 16
HEAD_PAIRS = NA_HEADS // 2
NA_KEY_ROWS = 12
NA_LOC_KEYS = NA_KEY_ROWS * GRID_W
ROWS_PER_TILE = TILE // GRID_W
TAB_PAD = 4
TAB_BLOCKS = 24
ROPE_LANE0 = MLA_NOPE

C_NAQ, C_NAK, C_NAV, C_NAZ, C_SU, C_SV, C_SZ, C_MZ, C_MG = (512 * i for i in (0, 1, 2, 3, 4, 5, 6, 7, 8))
C_DQ = C_MG + N_BRANCH * D_MODEL
C_DKV = C_DQ + MLA_Q_LORA
C_KR = C_DKV + MLA_KV_LORA
W_COLS = C_KR + LANES
P_COLS = C_DQ + 512
CHUNK = 512

VMEM_LIMIT = 56 * 1024 * 1024


def _dot(a, b):
    return jnp.dot(a, b, preferred_element_type=F32)


def _dot_nt(a, b):
    return lax.dot_general(a, b, (((1,), (1,)), ((), ())), preferred_element_type=F32)


def _rms(x, g):
    return x * lax.rsqrt(jnp.mean(x * x, axis=-1, keepdims=True) + EPS) * g


def _center_norm(x):
    c = x - jnp.mean(x, axis=-1, keepdims=True)
    return c * lax.rsqrt(jnp.mean(c * c, axis=-1, keepdims=True) + EPS)


def _rope128(x, cos, sin):
    lane = lax.broadcasted_iota(jnp.int32, x.shape, 1)
    swapped = jnp.where(lane < ROPE_LANE0 + MLA_ROPE // 2,
                        pltpu.roll(x, LANES - MLA_ROPE // 2, axis=1),
                        pltpu.roll(x, MLA_ROPE // 2, axis=1))
    return x * cos + swapped * sin


def _store_scores(s_ref, m_ref, rows, cols, scores, first):
    s_ref[rows, cols] = scores
    block_max = jnp.broadcast_to(scores.max(axis=-1, keepdims=True), (scores.shape[0], LANES))
    m_ref[rows, :] = block_max if first else jnp.maximum(m_ref[rows, :], block_max)


def _pair_softmax_pv(s_ref, m_ref, p_ref, den_ref, segments, values):
    for i in range(s_ref.shape[0] // STRIP):
        r = slice(i * STRIP, (i + 1) * STRIP)
        m = m_ref[r, :]
        den = None
        for c0, c1 in segments:
            e = jnp.exp(s_ref[r, c0:c1] - jnp.concatenate([m] * ((c1 - c0) // LANES), axis=1))
            d = e.sum(axis=-1, keepdims=True)
            den = d if den is None else den + d
            p_ref[r, c0:c1] = e.astype(BF16)
        den_ref[r, :] = jnp.broadcast_to(den, (STRIP, LANES))
    o = None
    for (c0, c1), (v, transposed) in zip(segments, values):
        t = _dot_nt(p_ref[:, c0:c1], v) if transposed else _dot(p_ref[:, c0:c1], v)
        o = t if o is None else o + t
    o = o / den_ref[...]
    lane = lax.broadcasted_iota(jnp.int32, (TILE, LANES), 1)
    return jnp.where(lane < LANES // 2, o[:TILE], o[TILE:])


def _pair_scratch(n_keys):
    return [pltpu.VMEM((2, 2 * TILE, n_keys), F32), pltpu.VMEM((2, 2 * TILE, LANES), F32),
            pltpu.VMEM((2, 2 * TILE, n_keys), BF16), pltpu.VMEM((2, 2 * TILE, LANES), F32)]


def _const_spec(shape, index):
    return pl.BlockSpec(shape, lambda *_: index, pipeline_mode=pl.Buffered(1))


def _params(semantics):
    return pltpu.CompilerParams(dimension_semantics=semantics, vmem_limit_bytes=VMEM_LIMIT)


def _mod_kernel(c_ref, w_ref, b_ref, o_ref):
    c = c_ref[...]
    s = c * jax.nn.sigmoid(c)
    o_ref[...] = jnp.dot(s, w_ref[...], preferred_element_type=F32, precision=lax.Precision.HIGHEST) + b_ref[...]


def _modulation(cond, w_mod, b_mod):
    rows = cond.shape[0]
    return pl.pallas_call(
        _mod_kernel,
        grid=(DEPTH, 3),
        in_specs=[pl.BlockSpec((rows, D_MODEL), lambda l, j: (0, 0)),
                  pl.BlockSpec((None, D_MODEL, D_MODEL), lambda l, j: (l, 0, j)),
                  pl.BlockSpec((None, 1, D_MODEL), lambda l, j: (l, 0, j))],
        out_specs=pl.BlockSpec((None, rows, D_MODEL), lambda l, j: (l, 0, j)),
        out_shape=jax.ShapeDtypeStruct((DEPTH, rows, 3 * D_MODEL), F32),
        compiler_params=_params(("parallel", "parallel")),
        name="modulation",
    )(cond, w_mod, b_mod.reshape(DEPTH, 1, 3 * D_MODEL))


def _inproj_kernel(*refs, emit_state, rope, n_alias):
    x_ref, mod_ref, g_ref, w_ref, kvn_ref = refs[:5]
    refs = refs[5:]
    if rope:
        cos_ref, sin_ref = refs[:2]
        refs = refs[2:]
    refs = refs[n_alias:]
    p_ref, ckv_ref, kr_ref = refs[:3]
    if emit_state:
        sk_ref, sv_ref, sckv_ref, skr_ref = refs[3:7]

    shift = mod_ref[:, 0:D_MODEL]
    scale = mod_ref[:, D_MODEL:2 * D_MODEL]
    tm = x_ref.shape[0]
    h = (_rms(x_ref[...], g_ref[...]) * (1.0 + scale) + shift).astype(BF16)

    def state_view(a):
        return a.reshape(tm // TILE, TILE, a.shape[-1])

    ckv = _rms(_dot(h, w_ref[:, C_DKV:C_KR]), kvn_ref[...])
    ckv_ref[...] = ckv.astype(BF16)
    kr = _dot(h, w_ref[:, C_KR:W_COLS])
    if emit_state:
        sckv_ref[...] = state_view(ckv)
        skr_ref[...] = state_view(kr[:, ROPE_LANE0:ROPE_LANE0 + MLA_ROPE])
    if rope:
        kr = _rope128(kr, cos_ref[...], sin_ref[...])
    kr_ref[...] = kr.astype(BF16)

    for c0 in range(0, P_COLS, CHUNK):
        acc = _dot(h, w_ref[:, c0:c0 + CHUNK])
        if emit_state and c0 == C_NAK:
            sk_ref[...] = state_view(acc)
        if emit_state and c0 == C_NAV:
            sv_ref[...] = state_view(acc)
        if c0 in (C_NAZ, C_SZ, C_MZ):
            acc = acc * jax.nn.sigmoid(acc)
        elif c0 == C_SU:
            acc = jax.nn.gelu(acc)
        elif c0 == C_SV:
            acc = _center_norm(jax.nn.gelu(acc))
        elif C_MG <= c0 < C_DQ:
            acc = jax.nn.sigmoid(acc)
        p_ref[:, c0 // CHUNK] = state_view(acc.astype(BF16))


def _inproj(x, mod, mod_row, layer, norm_g, w_packed, kv_norm, rope_tabs, state_in, emit_state, tm):
    n_tok = x.shape[0]
    n_tiles = n_tok // tm
    rope = rope_tabs is not None
    in_specs = [pl.BlockSpec((tm, D_MODEL), lambda i: (i, 0)),
                pl.BlockSpec((None, None, 1, 3 * D_MODEL), lambda i: (layer, mod_row(i * tm), 0, 0)),
                _const_spec((None, 1, D_MODEL), (layer, 0, 0)),
                _const_spec((None, D_MODEL, W_COLS), (layer, 0, 0)),
                _const_spec((None, 1, MLA_KV_LORA), (layer, 0, 0))]
    args = [x, mod, norm_g, w_packed, kv_norm]
    if rope:
        tiles_per_seq = rope_tabs[0].shape[0] // tm
        in_specs += [pl.BlockSpec((tm, LANES), lambda i: (i % tiles_per_seq, 0))] * 2
        args += list(rope_tabs)
    out_specs = [pl.BlockSpec((tm // TILE, P_COLS // CHUNK, TILE, CHUNK), lambda i: (i, 0, 0, 0)),
                 pl.BlockSpec((tm, MLA_KV_LORA), lambda i: (i, 0)),
                 pl.BlockSpec((tm, LANES), lambda i: (i, 0))]
    out_shape = [jax.ShapeDtypeStruct((n_tok // TILE, P_COLS // CHUNK, TILE, CHUNK), BF16),
                 jax.ShapeDtypeStruct((n_tok, MLA_KV_LORA), BF16),
                 jax.ShapeDtypeStruct((n_tok, LANES), BF16)]
    aliases = {}
    if emit_state:
        widths = (BRANCH_WIDTH, BRANCH_WIDTH, MLA_KV_LORA, MLA_ROPE)
        for k, w in enumerate(widths):
            out_specs.append(pl.BlockSpec((tm // TILE, None, TILE, w), lambda i: (i, layer, 0, 0)))
            out_shape.append(jax.ShapeDtypeStruct((n_tok // TILE, DEPTH, TILE, w), F32))
        if state_in is not None:
            first = len(args)
            in_specs += [pl.BlockSpec(memory_space=pl.ANY)] * 4
            args += list(state_in)
            aliases = {first + k: 3 + k for k in range(4)}
    kern = functools.partial(_inproj_kernel, emit_state=emit_state, rope=rope, n_alias=len(aliases))
    return pl.pallas_call(
        kern, grid=(n_tiles,), in_specs=in_specs, out_specs=out_specs, out_shape=out_shape,
        input_output_aliases=aliases, compiler_params=_params(("parallel",)),
        name="inproj",
    )(*args)


def _na_heads(q_ref, o_ref, scratch, keys_fn, values_fn, bias_fn=None):
    lane = lax.broadcasted_iota(jnp.int32, (TILE, LANES), 1)
    low = lane < NA_HEAD_DIM
    for hp in range(HEAD_PAIRS):
        sl = slice(LANES * hp, LANES * (hp + 1))
        q2 = q_ref[:, sl] * (NA_HEAD_DIM ** -0.5)
        zero = jnp.zeros_like(q2)
        q_stack = jnp.concatenate([jnp.where(low, q2, zero), jnp.where(low, zero, q2)], axis=0)
        s_pair, m_pair, p_pair, den_pair = (ref.at[hp % 2] for ref in scratch)
        segments = []
        c0 = 0
        for j, (k, transposed) in enumerate(keys_fn(sl)):
            n_keys = k.shape[1] if transposed else k.shape[0]
            scores = _dot(q_stack, k) if transposed else _dot_nt(q_stack, k)
            if j == 0 and bias_fn is not None:
                scores = scores + bias_fn(2 * hp)
            _store_scores(s_pair, m_pair, slice(None), slice(c0, c0 + n_keys), scores, j == 0)
            segments.append((c0, c0 + n_keys))
            c0 += n_keys
        o_ref[:, sl] = _pair_softmax_pv(s_pair, m_pair, p_pair, den_pair, segments, values_fn(sl)).astype(BF16)


def _na_ctx_kernel(q_ref, k_ref, v_ref, o_ref, *scratch):
    _na_heads(q_ref, o_ref, scratch, lambda sl: [(k_ref[:, sl], False)], lambda sl: [(v_ref[:, sl], False)])


def _na_band_row0(tile_row0, n_rows):
    return min(max(tile_row0 - NA_WIN_H // 2, 0), n_rows - NA_KEY_ROWS)


def _na_band_plans(seq):
    n_rows = seq // GRID_W
    plans = []
    for t in range(seq // TILE):
        band0 = _na_band_row0(t * ROWS_PER_TILE, n_rows)
        plan = []
        for r in range(ROWS_PER_TILE):
            rq = t * ROWS_PER_TILE + r
            row_lo = min(max(rq - NA_WIN_H // 2, 0), n_rows - NA_WIN_H)
            plan.append((band0 - rq + NA_WIN_H - 1 + TAB_PAD, row_lo - band0))
        plans.append(tuple(plan))
    return tuple(plans)


def _na_lat_kernel(q_ref, k_ref, v_ref, kc_ref, vc_ref, tab_ref, tab_odd_ref, o_ref, bias_ref, *scratch, plans, n_rows):
    t = pl.program_id(0)
    first_half = lax.broadcasted_iota(jnp.int32, (GRID_W, LANES), 1) < GRID_W
    masked = jnp.full((GRID_W, LANES), NEG_INF, F32)

    for plan in sorted(set(plans)):
        is_tile = functools.reduce(jnp.logical_or, [t == i for i, p in enumerate(plans) if p == plan])

        @pl.when(jnp.logical_and(pl.program_id(1) == 0, is_tile))
        def _build_bias(plan=plan):
            for r, (blk0, j0) in enumerate(plan):
                src, pair0 = (tab_ref, blk0 // 2) if blk0 % 2 == 0 else (tab_odd_ref, (blk0 - 1) // 2)
                for j in range(NA_KEY_ROWS // 2):
                    in_window = [j0 <= 2 * j + half < j0 + NA_WIN_H for half in range(2)]
                    for h in range(NA_HEADS):
                        if not any(in_window):
                            piece = masked
                        elif all(in_window):
                            piece = src[h, pair0 + j]
                        else:
                            keep = first_half if in_window[0] else jnp.logical_not(first_half)
                            piece = jnp.where(keep, src[h, pair0 + j], NEG_INF)
                        bias_ref[h, r * GRID_W:(r + 1) * GRID_W, LANES * j:LANES * (j + 1)] = piece

    band_row0 = jnp.clip(t * ROWS_PER_TILE - NA_WIN_H // 2, 0, n_rows - NA_KEY_ROWS)
    band_tiles = pl.ds(band_row0 // ROWS_PER_TILE, NA_LOC_KEYS // TILE)

    def band(ref, sl):
        return ref[band_tiles, :, sl].reshape(NA_LOC_KEYS, LANES)

    def bias(head0):
        return bias_ref[head0:head0 + 2].reshape(2 * TILE, NA_LOC_KEYS)

    _na_heads(q_ref, o_ref, scratch,
              lambda sl: [(band(k_ref, sl), False), (kc_ref[sl, :].astype(BF16), True)],
              lambda sl: [(band(v_ref, sl), False), (vc_ref[sl, :].astype(BF16), True)], bias)


def _p_block(tile_of, col):
    return pl.BlockSpec((None, None, TILE, CHUNK), lambda *g: (tile_of(*g), col // CHUNK, 0, 0))


def _na_ctx(p, n_seq):
    blk = lambda col: _p_block(lambda i: i, col)
    return pl.pallas_call(
        _na_ctx_kernel, grid=(n_seq,),
        in_specs=[blk(C_NAQ), blk(C_NAK), blk(C_NAV)],
        out_specs=pl.BlockSpec((TILE, BRANCH_WIDTH), lambda i: (i, 0)),
        out_shape=jax.ShapeDtypeStruct((n_seq * TILE, BRANCH_WIDTH), BF16),
        scratch_shapes=_pair_scratch(TILE),
        compiler_params=_params(("parallel",)), name="na_ctx",
    )(p, p, p)


def _na_lat(p, cache_k, cache_v, tab, tab_odd, layer, n_seq, seq):
    tiles = seq // TILE
    n_past = cache_k.shape[3]
    kv = lambda col: pl.BlockSpec((tiles, None, TILE, CHUNK), lambda t, b: (b, col // CHUNK, 0, 0))
    cache = pl.BlockSpec((None, None, BRANCH_WIDTH, n_past), lambda t, b: (b, layer, 0, 0))
    table = _const_spec((None, NA_HEADS, TAB_BLOCKS // 2, GRID_W, LANES), (layer, 0, 0, 0, 0))
    return pl.pallas_call(
        functools.partial(_na_lat_kernel, plans=_na_band_plans(seq), n_rows=seq // GRID_W),
        grid=(tiles, n_seq),
        in_specs=[_p_block(lambda t, b: b * tiles + t, C_NAQ),
                  kv(C_NAK), kv(C_NAV), cache, cache, table, table],
        out_specs=pl.BlockSpec((TILE, BRANCH_WIDTH), lambda t, b: (b * tiles + t, 0)),
        out_shape=jax.ShapeDtypeStruct((n_seq * seq, BRANCH_WIDTH), BF16),
        scratch_shapes=[pltpu.VMEM((NA_HEADS, TILE, NA_LOC_KEYS), F32)]
        + _pair_scratch(NA_LOC_KEYS + n_past),
        compiler_params=_params(("parallel", "arbitrary")), name="na_lat",
    )(p, p, p, cache_k, cache_v, tab, tab_odd)


def _na_bias_tables(rpb):
    cq = np.arange(GRID_W)[:, None]
    ck = np.arange(GRID_W)[None, :]
    col_lo = np.clip(cq - NA_WIN_W // 2, 0, GRID_W - NA_WIN_W)
    ok = (ck >= col_lo) & (ck < col_lo + NA_WIN_W)
    n_b = 2 * NA_WIN_W - 1
    n_rel = 2 * NA_WIN_H - 1
    onehot = ((ck - cq + NA_WIN_W - 1)[:, None, :] == np.arange(n_b)[None, :, None]) & ok[:, None, :]
    expand = np.zeros((GRID_W, 2, n_b, 2, GRID_W), np.float32)
    expand[:, 0, :, 0, :] = onehot
    expand[:, 1, :, 1, :] = onehot
    expand = expand.reshape(GRID_W, 2 * n_b, LANES)
    rows = jnp.pad(rpb, ((0, 0), (0, 0), (TAB_PAD, TAB_BLOCKS + 1 - n_rel - TAB_PAD), (0, 0)))
    real = np.zeros(TAB_BLOCKS + 1, bool)
    real[TAB_PAD:TAB_PAD + n_rel] = True

    def table(first):
        pairs = rows[:, :, first:first + TAB_BLOCKS].reshape(DEPTH, NA_HEADS, TAB_BLOCKS // 2, 2 * n_b)
        tab = jnp.einsum('lhek,ukc->lheuc', pairs, expand, precision=lax.Precision.HIGHEST)
        keep = real[first:first + TAB_BLOCKS].reshape(-1, 1, 2, 1) & ok[None, :, None, :]
        return jnp.where(keep.reshape(TAB_BLOCKS // 2, GRID_W, LANES), tab, NEG_INF)

    return table(0), table(1)


def _mla_kernel(*refs, n_cache, n_lat, rope):
    dq_ref, ckv_ref, kr_ref = refs[:3]
    refs = refs[3:]
    if n_cache:
        cckv_ref, ckr_ref = refs[:2]
        refs = refs[2:]
    qn_ref, wuq_ref, wk_ref, wv_ref = refs[:4]
    refs = refs[4:]
    if rope:
        cos_ref, sin_ref = refs[:2]
        refs = refs[2:]
    o_ref, kx_ref, vx_ref = refs[:3]
    scratch = refs[3:]

    @pl.when(pl.program_id(1) == 0)
    def _expand_keys():
        def fill(r0, ckv, kr):
            n = ckv.shape[0]
            kk = _dot(ckv, wk_ref[...])
            for h in range(MLA_HEADS):
                sl = slice(LANES * h, LANES * (h + 1))
                kx_ref[r0:r0 + n, sl] = (kk[:, sl] + kr).astype(BF16)
            vx_ref[r0:r0 + n, :] = _dot(ckv, wv_ref[...]).astype(BF16)

        if n_cache:
            fill(0, cckv_ref[...].astype(BF16), ckr_ref[...])
        for r0 in range(0, n_lat, TILE):
            fill(n_cache + r0, ckv_ref[r0:r0 + TILE, :], kr_ref[r0:r0 + TILE, :].astype(F32))

    dqn = _rms(dq_ref[:, :MLA_Q_LORA].astype(F32), qn_ref[...])
    q = _dot(dqn.astype(BF16), wuq_ref[...])
    scale = (MLA_NOPE + MLA_ROPE) ** -0.5
    n_keys = kx_ref.shape[0]
    for hp in range(HEAD_PAIRS):
        s_pair, m_pair, p_pair, den_pair = (ref.at[hp % 2] for ref in scratch)
        for half in range(2):
            sl = slice(LANES * (2 * hp + half), LANES * (2 * hp + half + 1))
            qh = q[:, sl]
            if rope:
                qh = _rope128(qh, cos_ref[...], sin_ref[...])
            qh = (qh * scale).astype(BF16)
            _store_scores(s_pair, m_pair, slice(half * TILE, (half + 1) * TILE), slice(None),
                          _dot_nt(qh, kx_ref[:, sl]), True)
        v2 = vx_ref[:, LANES * hp:LANES * (hp + 1)]
        o_ref[:, LANES * hp:LANES * (hp + 1)] = _pair_softmax_pv(
            s_pair, m_pair, p_pair, den_pair, [(0, n_keys)], [(v2, False)]).astype(BF16)


def _mla(p, ckv, kr, cache_ckv, cache_kr, layer, q_norm, w_uq, w_k, w_v, rope_tabs, n_seq, seq):
    tiles = seq // TILE
    n_cache = 0 if cache_ckv is None else cache_ckv.shape[2]
    rope = rope_tabs is not None
    in_specs = [_p_block(lambda b, t: b * tiles + t, C_DQ),
                pl.BlockSpec((seq, MLA_KV_LORA), lambda b, t: (b, 0)),
                pl.BlockSpec((seq, LANES), lambda b, t: (b, 0))]
    args = [p, ckv, kr]
    if n_cache:
        in_specs += [pl.BlockSpec((None, None, n_cache, MLA_KV_LORA), lambda b, t: (b, layer, 0, 0)),
                     pl.BlockSpec((None, None, n_cache, LANES), lambda b, t: (b, layer, 0, 0))]
        args += [cache_ckv, cache_kr]
    in_specs += [_const_spec((None, 1, MLA_Q_LORA), (layer, 0, 0)),
                 _const_spec((None, MLA_Q_LORA, MLA_HEADS * LANES), (layer, 0, 0)),
                 _const_spec((None, MLA_KV_LORA, MLA_HEADS * LANES), (layer, 0, 0)),
                 _const_spec((None, MLA_KV_LORA, MLA_HEADS * MLA_V), (layer, 0, 0))]
    args += [q_norm, w_uq, w_k, w_v]
    if rope:
        in_specs += [pl.BlockSpec((TILE, LANES), lambda b, t: (t, 0))] * 2
        args += list(rope_tabs)
    n_keys = n_cache + seq
    return pl.pallas_call(
        functools.partial(_mla_kernel, n_cache=n_cache, n_lat=seq, rope=rope),
        grid=(n_seq, tiles), in_specs=in_specs,
        out_specs=pl.BlockSpec((TILE, BRANCH_WIDTH), lambda b, t: (b * tiles + t, 0)),
        out_shape=jax.ShapeDtypeStruct((n_seq * seq, BRANCH_WIDTH), BF16),
        scratch_shapes=[pltpu.VMEM((n_keys, MLA_HEADS * LANES), BF16), pltpu.VMEM((n_keys, MLA_HEADS * MLA_V), BF16)]
        + _pair_scratch(n_keys),
        compiler_params=_params(("parallel", "arbitrary")), name="mla",
    )(*args)


def _merge_kernel(x_ref, mod_ref, yna_ref, ymla_ref, naz_ref, su_ref, sv_ref, sz_ref, mz_ref, mg0_ref, mg1_ref, mg2_ref,
                  sw_ref, sb_ref, wb_ref, wo_ref, fg_ref, o_ref, *, final):
    tm = x_ref.shape[0]

    def tokens(ref, *idx):
        return ref[(slice(None),) + idx].reshape(tm, CHUNK)

    sv = tokens(sv_ref)
    rows = []
    for c0 in range(0, tm, SGU_CHUNK):
        cols = [_dot(sw_ref[g], sv[c0:c0 + SGU_CHUNK, LANES * g:LANES * (g + 1)]) for g in range(SGU_GROUPS)]
        rows.append(jnp.concatenate(cols, axis=1) + sb_ref[...])
    y_sgu = tokens(su_ref).astype(F32) * jnp.concatenate(rows, axis=0)

    gated = (yna_ref[...] * tokens(naz_ref),
             (y_sgu * tokens(sz_ref).astype(F32)).astype(BF16),
             ymla_ref[...] * tokens(mz_ref))
    merged = None
    for k, mg_ref in enumerate((mg0_ref, mg1_ref, mg2_ref)):
        gate_k = jnp.concatenate([tokens(mg_ref, j) for j in range(mg_ref.shape[1])], axis=1)
        term = gate_k.astype(F32) * _dot(gated[k], wb_ref[k])
        merged = term if merged is None else merged + term
    out = _dot(merged.astype(BF16), wo_ref[...])
    gate = mod_ref[:, 2 * D_MODEL:3 * D_MODEL]
    xn = x_ref[...] + gate * out
    if final:
        xn = _rms(xn, fg_ref[...])
    o_ref[...] = xn


def _merge(x, mod, mod_row, layer, p, y_na, y_mla, sgu_w, sgu_b, w_branch, w_out, final_g, final, tm):
    n_tok = x.shape[0]
    n = tm // TILE
    half = pl.BlockSpec((tm, BRANCH_WIDTH), lambda i: (i, 0))
    full = lambda c: pl.BlockSpec((tm, D_MODEL), lambda i: (i, c))
    chunk = lambda col: pl.BlockSpec((n, None, TILE, CHUNK), lambda i: (i, col // CHUNK, 0, 0))
    per_gate = D_MODEL // CHUNK
    gates = lambda k: pl.BlockSpec((n, per_gate, TILE, CHUNK), lambda i: (i, C_MG // D_MODEL + k, 0, 0))
    in_specs = [full(0),
                pl.BlockSpec((None, None, 1, 3 * D_MODEL), lambda i: (layer, mod_row(i * tm), 0, 0)),
                half, half,
                chunk(C_NAZ), chunk(C_SU), chunk(C_SV), chunk(C_SZ), chunk(C_MZ),
                gates(0), gates(1), gates(2),
                _const_spec((None, SGU_GROUPS, SGU_CHUNK, SGU_CHUNK), (layer, 0, 0, 0)),
                _const_spec((None, SGU_CHUNK, BRANCH_WIDTH), (layer, 0, 0)),
                _const_spec((None, N_BRANCH, BRANCH_WIDTH, D_MODEL), (layer, 0, 0, 0)),
                _const_spec((None, D_MODEL, D_MODEL), (layer, 0, 0)),
                _const_spec((1, D_MODEL), (0, 0))]
    return pl.pallas_call(
        functools.partial(_merge_kernel, final=final),
        grid=(n_tok // tm,), in_specs=in_specs,
        out_specs=full(0), out_shape=jax.ShapeDtypeStruct((n_tok, D_MODEL), F32),
        compiler_params=_params(("parallel",)), name="merge",
    )(x, mod, y_na, y_mla, p, p, p, p, p, p, p, p, sgu_w, sgu_b, w_branch, w_out, final_g)


SRC_DQ = 7 * BRANCH_WIDTH
SRC_DKV = SRC_DQ + MLA_Q_LORA
SRC_KR = SRC_DKV + MLA_KV_LORA
SRC_MZ = SRC_KR + MLA_ROPE
PACK_PLAIN = C_MZ // CHUNK
PACK_SHIFTED = (C_DQ - C_MZ) // CHUNK
PACK_SHIFT = SRC_MZ % CHUNK
assert SRC_DQ % CHUNK == 0 and (SRC_MZ - PACK_SHIFT) // CHUNK == PACK_PLAIN + 1
assert SRC_DQ + CHUNK + (C_KR - P_COLS) == SRC_KR and SRC_KR + MLA_ROPE == SRC_MZ


def _pack_w_in_kernel(a_ref, b_ref, o_ref):
    j = pl.program_id(1)

    def emit(rows):
        o_ref[...] = rows.T.astype(BF16)

    @pl.when(jnp.logical_or(j < PACK_PLAIN, j == PACK_PLAIN + PACK_SHIFTED))
    def _aligned():
        emit(a_ref[...])

    @pl.when(jnp.logical_and(j >= PACK_PLAIN, j < PACK_PLAIN + PACK_SHIFTED))
    def _shifted():
        emit(jnp.concatenate([a_ref[PACK_SHIFT:, :], b_ref[:PACK_SHIFT, :]], axis=0))

    @pl.when(j == PACK_PLAIN + PACK_SHIFTED + 1)
    def _tail():
        n_dkv = C_KR - P_COLS
        zeros = lambda n: jnp.zeros((n, D_MODEL), F32)
        emit(jnp.concatenate([a_ref[:n_dkv, :], zeros(ROPE_LANE0), a_ref[n_dkv:n_dkv + MLA_ROPE, :],
                              zeros(LANES - ROPE_LANE0 - MLA_ROPE), zeros(CHUNK - n_dkv - LANES)], axis=0))


def _pack_w_in(w_in_t):
    n_chunks = pl.cdiv(W_COLS, CHUNK)
    tail_src = SRC_DQ // CHUNK + 1

    def a_block(j):
        return jnp.where(j < PACK_PLAIN, j,
                         jnp.where(j < PACK_PLAIN + PACK_SHIFTED, j + 1,
                                   jnp.where(j == PACK_PLAIN + PACK_SHIFTED, SRC_DQ // CHUNK, tail_src)))

    def b_block(j):
        shifted = jnp.logical_and(j >= PACK_PLAIN, j < PACK_PLAIN + PACK_SHIFTED)
        return jnp.where(shifted, j + 2, PACK_PLAIN + 2)

    return pl.pallas_call(
        _pack_w_in_kernel, grid=(DEPTH, n_chunks),
        in_specs=[pl.BlockSpec((None, CHUNK, D_MODEL), lambda l, j: (l, a_block(j), 0)),
                  pl.BlockSpec((None, CHUNK, D_MODEL), lambda l, j: (l, b_block(j), 0))],
        out_specs=pl.BlockSpec((None, D_MODEL, CHUNK), lambda l, j: (l, 0, j)),
        out_shape=jax.ShapeDtypeStruct((DEPTH, D_MODEL, W_COLS), BF16),
        compiler_params=_params(("parallel", "arbitrary")), name="pack_w_in",
    )(w_in_t, w_in_t)


def _pack_params(w_in, sgu_w, sgu_b, mla_w_uq, mla_w_ukv, w_branch, w_out):
    w_packed = _pack_w_in(jnp.swapaxes(w_in, 1, 2))
    uq = mla_w_uq.reshape(DEPTH, MLA_Q_LORA, MLA_HEADS, MLA_NOPE + MLA_ROPE)
    uq = jnp.pad(uq, ((0, 0), (0, 0), (0, 0), (0, LANES - MLA_NOPE - MLA_ROPE)))
    uq = uq.reshape(DEPTH, MLA_Q_LORA, MLA_HEADS * LANES).astype(BF16)
    ukv = mla_w_ukv.reshape(DEPTH, MLA_KV_LORA, MLA_HEADS, MLA_NOPE + MLA_V)
    w_k = jnp.pad(ukv[..., :MLA_NOPE], ((0, 0), (0, 0), (0, 0), (0, LANES - MLA_NOPE)))
    w_k = w_k.reshape(DEPTH, MLA_KV_LORA, MLA_HEADS * LANES).astype(BF16)
    w_v = ukv[..., MLA_NOPE:].reshape(DEPTH, MLA_KV_LORA, MLA_HEADS * MLA_V).astype(BF16)
    sgu_bias = jnp.repeat(jnp.swapaxes(sgu_b, 1, 2), BRANCH_WIDTH // SGU_GROUPS, axis=2)
    return w_packed, uq, w_k, w_v, sgu_w.astype(BF16), sgu_bias, w_branch.astype(BF16), w_out.astype(BF16)


def _rope_tables(n_tokens):
    pos = jnp.arange(n_tokens, dtype=jnp.int32)
    row = (pos // GRID_W).astype(F32)
    col = (pos % GRID_W).astype(F32)
    n_freq = MLA_ROPE // 4
    inv = ROPE_THETA ** (-jnp.arange(n_freq, dtype=F32) / n_freq)
    ang = jnp.concatenate([row[:, None] * inv, col[:, None] * inv], axis=-1)
    cos, sin = jnp.cos(ang), jnp.sin(ang)
    pad_l, pad_r = ROPE_LANE0, LANES - ROPE_LANE0 - MLA_ROPE
    cos_t = jnp.pad(jnp.concatenate([cos, cos], axis=1), ((0, 0), (pad_l, pad_r)), constant_values=1.0)
    sin_t = jnp.pad(jnp.concatenate([-sin, sin], axis=1), ((0, 0), (pad_l, pad_r)))
    return cos_t, sin_t


def kernel(x_prompt, x_sample, cache_na_k, cache_na_v, cache_mla_ckv, cache_mla_krope, c, c_ctx, norm_g, w_mod, b_mod, w_in, na_rpb, sgu_w, sgu_b, mla_q_norm, mla_w_uq, mla_kv_norm, mla_w_ukv, w_branch, w_out, final_norm_g):
    n_ctx, ctx_len, _ = x_prompt.shape
    n_lat, lat_len, _ = x_sample.shape
    past = cache_na_k.shape[2]
    assert ctx_len == TILE and lat_len % TILE == 0 and lat_len // GRID_W == 16 and past == TILE

    w_packed, w_uq, w_k, w_v, sgu_wb, sgu_bias, w_br, w_o = _pack_params(
        w_in, sgu_w, sgu_b, mla_w_uq, mla_w_ukv, w_branch, w_out)
    rope_tabs = _rope_tables(lat_len)
    tab, tab_odd = _na_bias_tables(na_rpb)
    norm_g3 = norm_g.reshape(DEPTH, 1, D_MODEL)
    kv_norm3 = mla_kv_norm.reshape(DEPTH, 1, MLA_KV_LORA)
    q_norm3 = mla_q_norm.reshape(DEPTH, 1, MLA_Q_LORA)
    final_g = final_norm_g.reshape(1, D_MODEL)

    cond_rows = 16
    cond = jnp.concatenate([c, c_ctx[None, :], jnp.zeros((cond_rows - n_lat - 1, D_MODEL), F32)], axis=0)
    mod = _modulation(cond, w_mod, b_mod).reshape(DEPTH, cond_rows, 1, 3 * D_MODEL)
    ctx_row = lambda token: n_lat
    lat_row = lambda token: token // lat_len

    cache_k = jnp.transpose(cache_na_k, (0, 1, 3, 4, 2)).reshape(n_lat, DEPTH, BRANCH_WIDTH, past)
    cache_v = jnp.transpose(cache_na_v, (0, 1, 3, 4, 2)).reshape(n_lat, DEPTH, BRANCH_WIDTH, past)
    cache_kr = jnp.pad(cache_mla_krope, ((0, 0), (0, 0), (0, 0), (ROPE_LANE0, LANES - ROPE_LANE0 - MLA_ROPE)))

    xp = x_prompt.reshape(n_ctx * ctx_len, D_MODEL)
    xs = x_sample.reshape(n_lat * lat_len, D_MODEL)
    state = None
    for l in range(DEPTH):
        final = l == DEPTH - 1
        p, ckv, kr, *state = _inproj(xp, mod, ctx_row, l, norm_g3, w_packed, kv_norm3, None, state, True, TILE)
        y_na = _na_ctx(p, n_ctx)
        y_mla = _mla(p, ckv, kr, None, None, l, q_norm3, w_uq, w_k, w_v, None, n_ctx, ctx_len)
        xp = _merge(xp, mod, ctx_row, l, p, y_na, y_mla, sgu_wb, sgu_bias, w_br, w_o, final_g, final, MERGE_TILE)
        p, ckv, kr = _inproj(xs, mod, lat_row, l, norm_g3, w_packed, kv_norm3, rope_tabs, None, False, TILE)
        y_na = _na_lat(p, cache_k, cache_v, tab, tab_odd, l, n_lat, lat_len)
        y_mla = _mla(p, ckv, kr, cache_mla_ckv, cache_kr, l, q_norm3, w_uq, w_k, w_v, rope_tabs, n_lat, lat_len)
        xs = _merge(xs, mod, lat_row, l, p, y_na, y_mla, sgu_wb, sgu_bias, w_br, w_o, final_g, final, MERGE_TILE)

    s_k, s_v, s_ckv, s_kr = state
    return (xp.reshape(n_ctx, ctx_len, D_MODEL), xs.reshape(n_lat, lat_len, D_MODEL),
            s_k.reshape(n_ctx, DEPTH, ctx_len, NA_HEADS, NA_HEAD_DIM),
            s_v.reshape(n_ctx, DEPTH, ctx_len, NA_HEADS, NA_HEAD_DIM), s_ckv, s_kr)
```

```python
import functools

import numpy as np
import jax
import jax.numpy as jnp
from jax import lax
from jax.experimental import pallas as pl
from jax.experimental.pallas import tpu as pltpu

F32 = jnp.float32
BF16 = jnp.bfloat16

D_MODEL = 1024
DEPTH = 2
GRID_W = 64
BRANCH_WIDTH = 512
N_BRANCH = 3
NA_HEADS = 8
NA_HEAD_DIM = 64
NA_WIN_H = 8
NA_WIN_W = 16
SGU_GROUPS = 4
SGU_CHUNK = 128
MLA_HEADS = 8
MLA_NOPE = 64
MLA_ROPE = 32
MLA_V = 64
MLA_Q_LORA = 384
MLA_KV_LORA = 256
ROPE_THETA = 10000.0
EPS = 1e-6
NEG_INF = -1e30

LANES = 128
TILE = 256
MERGE_TILE = 512
STRIP = 16
HEAD_PAIRS = NA_HEADS // 2
NA_KEY_ROWS = 12
NA_LOC_KEYS = NA_KEY_ROWS * GRID_W
ROWS_PER_TILE = TILE // GRID_W
TAB_PAD = 4
TAB_BLOCKS = 24
ROPE_LANE0 = MLA_NOPE

C_NAQ, C_NAK, C_NAV, C_NAZ, C_SU, C_SV, C_SZ, C_MZ, C_MG = (512 * i for i in (0, 1, 2, 3, 4, 5, 6, 7, 8))
C_DQ = C_MG + N_BRANCH * D_MODEL
C_DKV = C_DQ + MLA_Q_LORA
C_KR = C_DKV + MLA_KV_LORA
W_COLS = C_KR + LANES
P_COLS = C_DQ + 512
CHUNK = 512

VMEM_LIMIT = 56 * 1024 * 1024


def _dot(a, b):
    return jnp.dot(a, b, preferred_element_type=F32)


def _dot_nt(a, b):
    return lax.dot_general(a, b, (((1,), (1,)), ((), ())), preferred_element_type=F32)


def _rms(x, g):
    return x * lax.rsqrt(jnp.mean(x * x, axis=-1, keepdims=True) + EPS) * g


def _center_norm(x):
    c = x - jnp.mean(x, axis=-1, keepdims=True)
    return c * lax.rsqrt(jnp.mean(c * c, axis=-1, keepdims=True) + EPS)


def _rope128(x, cos, sin):
    lane = lax.broadcasted_iota(jnp.int32, x.shape, 1)
    swapped = jnp.where(lane < ROPE_LANE0 + MLA_ROPE // 2,
                        pltpu.roll(x, LANES - MLA_ROPE // 2, axis=1),
                        pltpu.roll(x, MLA_ROPE // 2, axis=1))
    return x * cos + swapped * sin


def _store_scores(s_ref, m_ref, rows, cols, scores, first):
    s_ref[rows, cols] = scores
    block_max = jnp.broadcast_to(scores.max(axis=-1, keepdims=True), (scores.shape[0], LANES))
    m_ref[rows, :] = block_max if first else jnp.maximum(m_ref[rows, :], block_max)


def _pair_softmax_pv(s_ref, m_ref, p_ref, den_ref, segments, values):
    for i in range(s_ref.shape[0] // STRIP):
        r = slice(i * STRIP, (i + 1) * STRIP)
        m = m_ref[r, :]
        den = None
        for c0, c1 in segments:
            e = jnp.exp(s_ref[r, c0:c1] - jnp.concatenate([m] * ((c1 - c0) // LANES), axis=1))
            d = e.sum(axis=-1, keepdims=True)
            den = d if den is None else den + d
            p_ref[r, c0:c1] = e.astype(BF16)
        den_ref[r, :] = jnp.broadcast_to(den, (STRIP, LANES))
    o = None
    for (c0, c1), (v, transposed) in zip(segments, values):
        t = _dot_nt(p_ref[:, c0:c1], v) if transposed else _dot(p_ref[:, c0:c1], v)
        o = t if o is None else o + t
    o = o / den_ref[...]
    lane = lax.broadcasted_iota(jnp.int32, (TILE, LANES), 1)
    return jnp.where(lane < LANES // 2, o[:TILE], o[TILE:])


def _pair_scratch(n_keys):
    return [pltpu.VMEM((2, 2 * TILE, n_keys), F32), pltpu.VMEM((2, 2 * TILE, LANES), F32),
            pltpu.VMEM((2, 2 * TILE, n_keys), BF16), pltpu.VMEM((2, 2 * TILE, LANES), F32)]


def _const_spec(shape, index):
    return pl.BlockSpec(shape, lambda *_: index, pipeline_mode=pl.Buffered(1))


def _params(semantics):
    return pltpu.CompilerParams(dimension_semantics=semantics, vmem_limit_bytes=VMEM_LIMIT)


def _mod_kernel(c_ref, w_ref, b_ref, o_ref):
    c = c_ref[...]
    s = c * jax.nn.sigmoid(c)
    o_ref[...] = jnp.dot(s, w_ref[...], preferred_element_type=F32, precision=lax.Precision.HIGHEST) + b_ref[...]


def _modulation(cond, w_mod, b_mod):
    rows = cond.shape[0]
    return pl.pallas_call(
        _mod_kernel,
        grid=(DEPTH, 3),
        in_specs=[pl.BlockSpec((rows, D_MODEL), lambda l, j: (0, 0)),
                  pl.BlockSpec((None, D_MODEL, D_MODEL), lambda l, j: (l, 0, j)),
                  pl.BlockSpec((None, 1, D_MODEL), lambda l, j: (l, 0, j))],
        out_specs=pl.BlockSpec((None, rows, D_MODEL), lambda l, j: (l, 0, j)),
        out_shape=jax.ShapeDtypeStruct((DEPTH, rows, 3 * D_MODEL), F32),
        compiler_params=_params(("parallel", "parallel")),
        name="modulation",
    )(cond, w_mod, b_mod.reshape(DEPTH, 1, 3 * D_MODEL))


def _inproj_kernel(*refs, emit_state, rope, n_alias):
    x_ref, mod_ref, g_ref, w_ref, kvn_ref = refs[:5]
    refs = refs[5:]
    if rope:
        cos_ref, sin_ref = refs[:2]
        refs = refs[2:]
    refs = refs[n_alias:]
    p_ref, ckv_ref, kr_ref = refs[:3]
    if emit_state:
        sk_ref, sv_ref, sckv_ref, skr_ref = refs[3:7]

    shift = mod_ref[:, 0:D_MODEL]
    scale = mod_ref[:, D_MODEL:2 * D_MODEL]
    tm = x_ref.shape[0]
    h = (_rms(x_ref[...], g_ref[...]) * (1.0 + scale) + shift).astype(BF16)

    def tiles(a):
        return a.reshape(tm // TILE, TILE, a.shape[-1])

    ckv = _rms(_dot(h, w_ref[:, C_DKV:C_KR]), kvn_ref[...])
    ckv_ref[...] = ckv.astype(BF16)
    kr = _dot(h, w_ref[:, C_KR:W_COLS])
    if emit_state:
        sckv_ref[...] = ckv
        skr_ref[...] = kr.T[ROPE_LANE0:ROPE_LANE0 + MLA_ROPE, :]
    if rope:
        kr = _rope128(kr, cos_ref[...], sin_ref[...])
    kr_ref[...] = kr.astype(BF16)

    for c0 in range(0, P_COLS, CHUNK):
        acc = _dot(h, w_ref[:, c0:c0 + CHUNK])
        if emit_state and c0 == C_NAK:
            sk_ref[...] = acc.T
        if emit_state and c0 == C_NAV:
            sv_ref[...] = acc.T
        if c0 in (C_NAZ, C_SZ, C_MZ):
            acc = acc * jax.nn.sigmoid(acc)
        elif c0 == C_SU:
            acc = jax.nn.gelu(acc)
        elif c0 == C_SV:
            acc = _center_norm(jax.nn.gelu(acc))
        elif C_MG <= c0 < C_DQ:
            acc = jax.nn.sigmoid(acc)
        p_ref[:, c0 // CHUNK] = tiles(acc.astype(BF16))


def _inproj(x, mod, mod_row, layer, norm_g, w_packed, kv_norm, rope_tabs, state_in, emit_state, tm):
    n_tok = x.shape[0]
    n_tiles = n_tok // tm
    rope = rope_tabs is not None
    in_specs = [pl.BlockSpec((tm, D_MODEL), lambda i: (i, 0)),
                pl.BlockSpec((None, None, 1, 3 * D_MODEL), lambda i: (layer, mod_row(i * tm), 0, 0)),
                _const_spec((None, 1, D_MODEL), (layer, 0, 0)),
                _const_spec((None, D_MODEL, W_COLS), (layer, 0, 0)),
                _const_spec((None, 1, MLA_KV_LORA), (layer, 0, 0))]
    args = [x, mod, norm_g, w_packed, kv_norm]
    if rope:
        tiles_per_seq = rope_tabs[0].shape[0] // tm
        in_specs += [pl.BlockSpec((tm, LANES), lambda i: (i % tiles_per_seq, 0))] * 2
        args += list(rope_tabs)
    out_specs = [pl.BlockSpec((tm // TILE, P_COLS // CHUNK, TILE, CHUNK), lambda i: (i, 0, 0, 0)),
                 pl.BlockSpec((tm, MLA_KV_LORA), lambda i: (i, 0)),
                 pl.BlockSpec((tm, LANES), lambda i: (i, 0))]
    out_shape = [jax.ShapeDtypeStruct((n_tok // TILE, P_COLS // CHUNK, TILE, CHUNK), BF16),
                 jax.ShapeDtypeStruct((n_tok, MLA_KV_LORA), BF16),
                 jax.ShapeDtypeStruct((n_tok, LANES), BF16)]
    aliases = {}
    if emit_state:
        assert tm == TILE
        for rows, cols in ((BRANCH_WIDTH, TILE), (BRANCH_WIDTH, TILE), (TILE, MLA_KV_LORA), (MLA_ROPE, TILE)):
            out_specs.append(pl.BlockSpec((None, None, rows, cols), lambda i: (i, layer, 0, 0)))
            out_shape.append(jax.ShapeDtypeStruct((n_tok // TILE, DEPTH, rows, cols), F32))
        if state_in is not None:
            first = len(args)
            in_specs += [pl.BlockSpec(memory_space=pl.ANY)] * 4
            args += list(state_in)
            aliases = {first + k: 3 + k for k in range(4)}
    kern = functools.partial(_inproj_kernel, emit_state=emit_state, rope=rope, n_alias=len(aliases))
    return pl.pallas_call(
        kern, grid=(n_tiles,), in_specs=in_specs, out_specs=out_specs, out_shape=out_shape,
        input_output_aliases=aliases, compiler_params=_params(("parallel",)),
        name="inproj",
    )(*args)


def _na_heads(q_ref, o_ref, scratch, keys_fn, values_fn, bias_fn=None):
    lane = lax.broadcasted_iota(jnp.int32, (TILE, LANES), 1)
    low = lane < NA_HEAD_DIM
    for hp in range(HEAD_PAIRS):
        sl = slice(LANES * hp, LANES * (hp + 1))
        q2 = q_ref[:, sl] * (NA_HEAD_DIM ** -0.5)
        zero = jnp.zeros_like(q2)
        q_stack = jnp.concatenate([jnp.where(low, q2, zero), jnp.where(low, zero, q2)], axis=0)
        s_pair, m_pair, p_pair, den_pair = (ref.at[hp % 2] for ref in scratch)
        segments = []
        c0 = 0
        for j, (k, transposed) in enumerate(keys_fn(sl)):
            n_keys = k.shape[1] if transposed else k.shape[0]
            scores = _dot(q_stack, k) if transposed else _dot_nt(q_stack, k)
            if j == 0 and bias_fn is not None:
                scores = scores + bias_fn(2 * hp)
            _store_scores(s_pair, m_pair, slice(None), slice(c0, c0 + n_keys), scores, j == 0)
            segments.append((c0, c0 + n_keys))
            c0 += n_keys
        o_ref[:, sl] = _pair_softmax_pv(s_pair, m_pair, p_pair, den_pair, segments, values_fn(sl)).astype(BF16)


def _na_ctx_kernel(q_ref, k_ref, v_ref, o_ref, *scratch):
    _na_heads(q_ref, o_ref, scratch, lambda sl: [(k_ref[:, sl], False)], lambda sl: [(v_ref[:, sl], False)])


def _na_band_row0(tile_row0, n_rows):
    return min(max(tile_row0 - NA_WIN_H // 2, 0), n_rows - NA_KEY_ROWS)


def _na_band_plans(seq):
    n_rows = seq // GRID_W
    plans = []
    for t in range(seq // TILE):
        band0 = _na_band_row0(t * ROWS_PER_TILE, n_rows)
        plan = []
        for r in range(ROWS_PER_TILE):
            rq = t * ROWS_PER_TILE + r
            row_lo = min(max(rq - NA_WIN_H // 2, 0), n_rows - NA_WIN_H)
            plan.append((band0 - rq + NA_WIN_H - 1 + TAB_PAD, row_lo - band0))
        plans.append(tuple(plan))
    return tuple(plans)


def _na_lat_kernel(q_ref, k_ref, v_ref, kc_ref, vc_ref, tab_ref, tab_odd_ref, o_ref, bias_ref, *scratch, plans, n_rows):
    t = pl.program_id(0)
    first_half = lax.broadcasted_iota(jnp.int32, (GRID_W, LANES), 1) < GRID_W
    masked = jnp.full((GRID_W, LANES), NEG_INF, F32)

    for plan in sorted(set(plans)):
        is_tile = functools.reduce(jnp.logical_or, [t == i for i, p in enumerate(plans) if p == plan])

        @pl.when(jnp.logical_and(pl.program_id(1) == 0, is_tile))
        def _build_bias(plan=plan):
            for r, (blk0, j0) in enumerate(plan):
                src, pair0 = (tab_ref, blk0 // 2) if blk0 % 2 == 0 else (tab_odd_ref, (blk0 - 1) // 2)
                for j in range(NA_KEY_ROWS // 2):
                    in_window = [j0 <= 2 * j + half < j0 + NA_WIN_H for half in range(2)]
                    for h in range(NA_HEADS):
                        if not any(in_window):
                            piece = masked
                        elif all(in_window):
                            piece = src[h, pair0 + j]
                        else:
                            keep = first_half if in_window[0] else jnp.logical_not(first_half)
                            piece = jnp.where(keep, src[h, pair0 + j], NEG_INF)
                        bias_ref[h, r * GRID_W:(r + 1) * GRID_W, LANES * j:LANES * (j + 1)] = piece

    band_row0 = jnp.clip(t * ROWS_PER_TILE - NA_WIN_H // 2, 0, n_rows - NA_KEY_ROWS)
    band_tiles = pl.ds(band_row0 // ROWS_PER_TILE, NA_LOC_KEYS // TILE)

    def band(ref, sl):
        return ref[band_tiles, :, sl].reshape(NA_LOC_KEYS, LANES)

    def bias(head0):
        return bias_ref[head0:head0 + 2].reshape(2 * TILE, NA_LOC_KEYS)

    _na_heads(q_ref, o_ref, scratch,
              lambda sl: [(band(k_ref, sl), False), (kc_ref[sl, :].astype(BF16), True)],
              lambda sl: [(band(v_ref, sl), False), (vc_ref[sl, :].astype(BF16), True)], bias)


def _p_block(tile_of, col):
    return pl.BlockSpec((None, None, TILE, CHUNK), lambda *g: (tile_of(*g), col // CHUNK, 0, 0))


def _na_ctx(p, n_seq):
    blk = lambda col: _p_block(lambda i: i, col)
    return pl.pallas_call(
        _na_ctx_kernel, grid=(n_seq,),
        in_specs=[blk(C_NAQ), blk(C_NAK), blk(C_NAV)],
        out_specs=pl.BlockSpec((TILE, BRANCH_WIDTH), lambda i: (i, 0)),
        out_shape=jax.ShapeDtypeStruct((n_seq * TILE, BRANCH_WIDTH), BF16),
        scratch_shapes=_pair_scratch(TILE),
        compiler_params=_params(("parallel",)), name="na_ctx",
    )(p, p, p)


def _na_lat(p, cache_k, cache_v, tab, tab_odd, layer, n_seq, seq):
    tiles = seq // TILE
    n_past = cache_k.shape[3]
    kv = lambda col: pl.BlockSpec((tiles, None, TILE, CHUNK), lambda t, b: (b, col // CHUNK, 0, 0))
    cache = pl.BlockSpec((None, None, BRANCH_WIDTH, n_past), lambda t, b: (b, layer, 0, 0))
    table = _const_spec((None, NA_HEADS, TAB_BLOCKS // 2, GRID_W, LANES), (layer, 0, 0, 0, 0))
    return pl.pallas_call(
        functools.partial(_na_lat_kernel, plans=_na_band_plans(seq), n_rows=seq // GRID_W),
        grid=(tiles, n_seq),
        in_specs=[_p_block(lambda t, b: b * tiles + t, C_NAQ),
                  kv(C_NAK), kv(C_NAV), cache, cache, table, table],
        out_specs=pl.BlockSpec((TILE, BRANCH_WIDTH), lambda t, b: (b * tiles + t, 0)),
        out_shape=jax.ShapeDtypeStruct((n_seq * seq, BRANCH_WIDTH), BF16),
        scratch_shapes=[pltpu.VMEM((NA_HEADS, TILE, NA_LOC_KEYS), F32)]
        + _pair_scratch(NA_LOC_KEYS + n_past),
        compiler_params=_params(("parallel", "arbitrary")), name="na_lat",
    )(p, p, p, cache_k, cache_v, tab, tab_odd)


def _na_bias_tables(rpb):
    cq = np.arange(GRID_W)[:, None]
    ck = np.arange(GRID_W)[None, :]
    col_lo = np.clip(cq - NA_WIN_W // 2, 0, GRID_W - NA_WIN_W)
    ok = (ck >= col_lo) & (ck < col_lo + NA_WIN_W)
    n_b = 2 * NA_WIN_W - 1
    n_rel = 2 * NA_WIN_H - 1
    onehot = ((ck - cq + NA_WIN_W - 1)[:, None, :] == np.arange(n_b)[None, :, None]) & ok[:, None, :]
    expand = np.zeros((GRID_W, 2, n_b, 2, GRID_W), np.float32)
    expand[:, 0, :, 0, :] = onehot
    expand[:, 1, :, 1, :] = onehot
    expand = expand.reshape(GRID_W, 2 * n_b, LANES)
    rows = jnp.pad(rpb, ((0, 0), (0, 0), (TAB_PAD, TAB_BLOCKS + 1 - n_rel - TAB_PAD), (0, 0)))
    real = np.zeros(TAB_BLOCKS + 1, bool)
    real[TAB_PAD:TAB_PAD + n_rel] = True

    def table(first):
        pairs = rows[:, :, first:first + TAB_BLOCKS].reshape(DEPTH, NA_HEADS, TAB_BLOCKS // 2, 2 * n_b)
        tab = jnp.einsum('lhek,ukc->lheuc', pairs, expand, precision=lax.Precision.HIGHEST)
        keep = real[first:first + TAB_BLOCKS].reshape(-1, 1, 2, 1) & ok[None, :, None, :]
        return jnp.where(keep.reshape(TAB_BLOCKS // 2, GRID_W, LANES), tab, NEG_INF)

    return table(0), table(1)


def _mla_kernel(*refs, n_cache, n_lat, rope):
    dq_ref, ckv_ref, kr_ref = refs[:3]
    refs = refs[3:]
    if n_cache:
        cckv_ref, ckr_ref = refs[:2]
        refs = refs[2:]
    qn_ref, wuq_ref, wk_ref, wv_ref = refs[:4]
    refs = refs[4:]
    if rope:
        cos_ref, sin_ref = refs[:2]
        refs = refs[2:]
    o_ref, kx_ref, vx_ref = refs[:3]
    scratch = refs[3:]

    @pl.when(pl.program_id(1) == 0)
    def _expand_keys():
        def fill(r0, ckv, kr):
            n = ckv.shape[0]
            kk = _dot(ckv, wk_ref[...])
            for h in range(MLA_HEADS):
                sl = slice(LANES * h, LANES * (h + 1))
                kx_ref[r0:r0 + n, sl] = (kk[:, sl] + kr).astype(BF16)
            vx_ref[r0:r0 + n, :] = _dot(ckv, wv_ref[...]).astype(BF16)

        if n_cache:
            fill(0, cckv_ref[...].astype(BF16), ckr_ref[...])
        for r0 in range(0, n_lat, TILE):
            fill(n_cache + r0, ckv_ref[r0:r0 + TILE, :], kr_ref[r0:r0 + TILE, :].astype(F32))

    dqn = _rms(dq_ref[:, :MLA_Q_LORA].astype(F32), qn_ref[...])
    q = _dot(dqn.astype(BF16), wuq_ref[...])
    scale = (MLA_NOPE + MLA_ROPE) ** -0.5
    n_keys = kx_ref.shape[0]
    for hp in range(HEAD_PAIRS):
        s_pair, m_pair, p_pair, den_pair = (ref.at[hp % 2] for ref in scratch)
        for half in range(2):
            sl = slice(LANES * (2 * hp + half), LANES * (2 * hp + half + 1))
            qh = q[:, sl]
            if rope:
                qh = _rope128(qh, cos_ref[...], sin_ref[...])
            qh = (qh * scale).astype(BF16)
            _store_scores(s_pair, m_pair, slice(half * TILE, (half + 1) * TILE), slice(None),
                          _dot_nt(qh, kx_ref[:, sl]), True)
        v2 = vx_ref[:, LANES * hp:LANES * (hp + 1)]
        o_ref[:, LANES * hp:LANES * (hp + 1)] = _pair_softmax_pv(
            s_pair, m_pair, p_pair, den_pair, [(0, n_keys)], [(v2, False)]).astype(BF16)


def _mla(p, ckv, kr, cache_ckv, cache_kr, layer, q_norm, w_uq, w_k, w_v, rope_tabs, n_seq, seq):
    tiles = seq // TILE
    n_cache = 0 if cache_ckv is None else cache_ckv.shape[2]
    rope = rope_tabs is not None
    in_specs = [_p_block(lambda b, t: b * tiles + t, C_DQ),
                pl.BlockSpec((seq, MLA_KV_LORA), lambda b, t: (b, 0)),
                pl.BlockSpec((seq, LANES), lambda b, t: (b, 0))]
    args = [p, ckv, kr]
    if n_cache:
        in_specs += [pl.BlockSpec((None, None, n_cache, MLA_KV_LORA), lambda b, t: (b, layer, 0, 0)),
                     pl.BlockSpec((None, None, n_cache, LANES), lambda b, t: (b, layer, 0, 0))]
        args += [cache_ckv, cache_kr]
    in_specs += [_const_spec((None, 1, MLA_Q_LORA), (layer, 0, 0)),
                 _const_spec((None, MLA_Q_LORA, MLA_HEADS * LANES), (layer, 0, 0)),
                 _const_spec((None, MLA_KV_LORA, MLA_HEADS * LANES), (layer, 0, 0)),
                 _const_spec((None, MLA_KV_LORA, MLA_HEADS * MLA_V), (layer, 0, 0))]
    args += [q_norm, w_uq, w_k, w_v]
    if rope:
        in_specs += [pl.BlockSpec((TILE, LANES), lambda b, t: (t, 0))] * 2
        args += list(rope_tabs)
    n_keys = n_cache + seq
    return pl.pallas_call(
        functools.partial(_mla_kernel, n_cache=n_cache, n_lat=seq, rope=rope),
        grid=(n_seq, tiles), in_specs=in_specs,
        out_specs=pl.BlockSpec((TILE, BRANCH_WIDTH), lambda b, t: (b * tiles + t, 0)),
        out_shape=jax.ShapeDtypeStruct((n_seq * seq, BRANCH_WIDTH), BF16),
        scratch_shapes=[pltpu.VMEM((n_keys, MLA_HEADS * LANES), BF16), pltpu.VMEM((n_keys, MLA_HEADS * MLA_V), BF16)]
        + _pair_scratch(n_keys),
        compiler_params=_params(("parallel", "arbitrary")), name="mla",
    )(*args)


def _merge_kernel(x_ref, mod_ref, yna_ref, ymla_ref, naz_ref, su_ref, sv_ref, sz_ref, mz_ref, mg0_ref, mg1_ref, mg2_ref,
                  sw_ref, sb_ref, wb_ref, wo_ref, fg_ref, o_ref, *, final):
    tm = x_ref.shape[0]

    def tokens(ref, *idx):
        return ref[(slice(None),) + idx].reshape(tm, CHUNK)

    sv = tokens(sv_ref)
    rows = []
    for c0 in range(0, tm, SGU_CHUNK):
        cols = [_dot(sw_ref[g], sv[c0:c0 + SGU_CHUNK, LANES * g:LANES * (g + 1)]) for g in range(SGU_GROUPS)]
        rows.append(jnp.concatenate(cols, axis=1) + sb_ref[...])
    y_sgu = tokens(su_ref).astype(F32) * jnp.concatenate(rows, axis=0)

    gated = (yna_ref[...] * tokens(naz_ref),
             (y_sgu * tokens(sz_ref).astype(F32)).astype(BF16),
             ymla_ref[...] * tokens(mz_ref))
    merged = None
    for k, mg_ref in enumerate((mg0_ref, mg1_ref, mg2_ref)):
        gate_k = jnp.concatenate([tokens(mg_ref, j) for j in range(mg_ref.shape[1])], axis=1)
        term = gate_k.astype(F32) * _dot(gated[k], wb_ref[k])
        merged = term if merged is None else merged + term
    out = _dot(merged.astype(BF16), wo_ref[...])
    gate = mod_ref[:, 2 * D_MODEL:3 * D_MODEL]
    xn = x_ref[...] + gate * out
    if final:
        xn = _rms(xn, fg_ref[...])
    o_ref[...] = xn


def _merge(x, mod, mod_row, layer, p, y_na, y_mla, sgu_w, sgu_b, w_branch, w_out, final_g, final, tm):
    n_tok = x.shape[0]
    n = tm // TILE
    half = pl.BlockSpec((tm, BRANCH_WIDTH), lambda i: (i, 0))
    full = lambda c: pl.BlockSpec((tm, D_MODEL), lambda i: (i, c))
    chunk = lambda col: pl.BlockSpec((n, None, TILE, CHUNK), lambda i: (i, col // CHUNK, 0, 0))
    per_gate = D_MODEL // CHUNK
    gates = lambda k: pl.BlockSpec((n, per_gate, TILE, CHUNK), lambda i: (i, C_MG // D_MODEL + k, 0, 0))
    in_specs = [full(0),
                pl.BlockSpec((None, None, 1, 3 * D_MODEL), lambda i: (layer, mod_row(i * tm), 0, 0)),
                half, half,
                chunk(C_NAZ), chunk(C_SU), chunk(C_SV), chunk(C_SZ), chunk(C_MZ),
                gates(0), gates(1), gates(2),
                _const_spec((None, SGU_GROUPS, SGU_CHUNK, SGU_CHUNK), (layer, 0, 0, 0)),
                _const_spec((None, SGU_CHUNK, BRANCH_WIDTH), (layer, 0, 0)),
                _const_spec((None, N_BRANCH, BRANCH_WIDTH, D_MODEL), (layer, 0, 0, 0)),
                _const_spec((None, D_MODEL, D_MODEL), (layer, 0, 0)),
                _const_spec((1, D_MODEL), (0, 0))]
    return pl.pallas_call(
        functools.partial(_merge_kernel, final=final),
        grid=(n_tok // tm,), in_specs=in_specs,
        out_specs=full(0), out_shape=jax.ShapeDtypeStruct((n_tok, D_MODEL), F32),
        compiler_params=_params(("parallel",)), name="merge",
    )(x, mod, y_na, y_mla, p, p, p, p, p, p, p, p, sgu_w, sgu_b, w_branch, w_out, final_g)


SRC_DQ = 7 * BRANCH_WIDTH
SRC_DKV = SRC_DQ + MLA_Q_LORA
SRC_KR = SRC_DKV + MLA_KV_LORA
SRC_MZ = SRC_KR + MLA_ROPE
PACK_PLAIN = C_MZ // CHUNK
PACK_SHIFTED = (C_DQ - C_MZ) // CHUNK
PACK_SHIFT = SRC_MZ % CHUNK
assert SRC_DQ % CHUNK == 0 and (SRC_MZ - PACK_SHIFT) // CHUNK == PACK_PLAIN + 1
assert SRC_DQ + CHUNK + (C_KR - P_COLS) == SRC_KR and SRC_KR + MLA_ROPE == SRC_MZ


def _pack_w_in_kernel(a_ref, b_ref, o_ref):
    j = pl.program_id(1)

    def emit(rows):
        o_ref[...] = rows.T.astype(BF16)

    @pl.when(jnp.logical_or(j < PACK_PLAIN, j == PACK_PLAIN + PACK_SHIFTED))
    def _aligned():
        emit(a_ref[...])

    @pl.when(jnp.logical_and(j >= PACK_PLAIN, j < PACK_PLAIN + PACK_SHIFTED))
    def _shifted():
        emit(jnp.concatenate([a_ref[PACK_SHIFT:, :], b_ref[:PACK_SHIFT, :]], axis=0))

    @pl.when(j == PACK_PLAIN + PACK_SHIFTED + 1)
    def _tail():
        n_dkv = C_KR - P_COLS
        zeros = lambda n: jnp.zeros((n, D_MODEL), F32)
        emit(jnp.concatenate([a_ref[:n_dkv, :], zeros(ROPE_LANE0), a_ref[n_dkv:n_dkv + MLA_ROPE, :],
                              zeros(LANES - ROPE_LANE0 - MLA_ROPE), zeros(CHUNK - n_dkv - LANES)], axis=0))


def _pack_w_in(w_in_t):
    n_chunks = pl.cdiv(W_COLS, CHUNK)
    tail_src = SRC_DQ // CHUNK + 1

    def a_block(j):
        return jnp.where(j < PACK_PLAIN, j,
                         jnp.where(j < PACK_PLAIN + PACK_SHIFTED, j + 1,
                                   jnp.where(j == PACK_PLAIN + PACK_SHIFTED, SRC_DQ // CHUNK, tail_src)))

    def b_block(j):
        shifted = jnp.logical_and(j >= PACK_PLAIN, j < PACK_PLAIN + PACK_SHIFTED)
        return jnp.where(shifted, j + 2, PACK_PLAIN + 2)

    return pl.pallas_call(
        _pack_w_in_kernel, grid=(DEPTH, n_chunks),
        in_specs=[pl.BlockSpec((None, CHUNK, D_MODEL), lambda l, j: (l, a_block(j), 0)),
                  pl.BlockSpec((None, CHUNK, D_MODEL), lambda l, j: (l, b_block(j), 0))],
        out_specs=pl.BlockSpec((None, D_MODEL, CHUNK), lambda l, j: (l, 0, j)),
        out_shape=jax.ShapeDtypeStruct((DEPTH, D_MODEL, W_COLS), BF16),
        compiler_params=_params(("parallel", "arbitrary")), name="pack_w_in",
    )(w_in_t, w_in_t)


def _pack_params(w_in, sgu_w, sgu_b, mla_w_uq, mla_w_ukv, w_branch, w_out):
    w_packed = _pack_w_in(jnp.swapaxes(w_in, 1, 2))
    uq = mla_w_uq.reshape(DEPTH, MLA_Q_LORA, MLA_HEADS, MLA_NOPE + MLA_ROPE)
    uq = jnp.pad(uq, ((0, 0), (0, 0), (0, 0), (0, LANES - MLA_NOPE - MLA_ROPE)))
    uq = uq.reshape(DEPTH, MLA_Q_LORA, MLA_HEADS * LANES).astype(BF16)
    ukv = mla_w_ukv.reshape(DEPTH, MLA_KV_LORA, MLA_HEADS, MLA_NOPE + MLA_V)
    w_k = jnp.pad(ukv[..., :MLA_NOPE], ((0, 0), (0, 0), (0, 0), (0, LANES - MLA_NOPE)))
    w_k = w_k.reshape(DEPTH, MLA_KV_LORA, MLA_HEADS * LANES).astype(BF16)
    w_v = ukv[..., MLA_NOPE:].reshape(DEPTH, MLA_KV_LORA, MLA_HEADS * MLA_V).astype(BF16)
    sgu_bias = jnp.repeat(jnp.swapaxes(sgu_b, 1, 2), BRANCH_WIDTH // SGU_GROUPS, axis=2)
    return w_packed, uq, w_k, w_v, sgu_w.astype(BF16), sgu_bias, w_branch.astype(BF16), w_out.astype(BF16)


def _rope_tables(n_tokens):
    pos = jnp.arange(n_tokens, dtype=jnp.int32)
    row = (pos // GRID_W).astype(F32)
    col = (pos % GRID_W).astype(F32)
    n_freq = MLA_ROPE // 4
    inv = ROPE_THETA ** (-jnp.arange(n_freq, dtype=F32) / n_freq)
    ang = jnp.concatenate([row[:, None] * inv, col[:, None] * inv], axis=-1)
    cos, sin = jnp.cos(ang), jnp.sin(ang)
    pad_l, pad_r = ROPE_LANE0, LANES - ROPE_LANE0 - MLA_ROPE
    cos_t = jnp.pad(jnp.concatenate([cos, cos], axis=1), ((0, 0), (pad_l, pad_r)), constant_values=1.0)
    sin_t = jnp.pad(jnp.concatenate([-sin, sin], axis=1), ((0, 0), (pad_l, pad_r)))
    return cos_t, sin_t


def kernel(x_prompt, x_sample, cache_na_k, cache_na_v, cache_mla_ckv, cache_mla_krope, c, c_ctx, norm_g, w_mod, b_mod, w_in, na_rpb, sgu_w, sgu_b, mla_q_norm, mla_w_uq, mla_kv_norm, mla_w_ukv, w_branch, w_out, final_norm_g):
    n_ctx, ctx_len, _ = x_prompt.shape
    n_lat, lat_len, _ = x_sample.shape
    past = cache_na_k.shape[2]
    assert ctx_len == TILE and lat_len % TILE == 0 and lat_len // GRID_W == 16 and past == TILE

    w_packed, w_uq, w_k, w_v, sgu_wb, sgu_bias, w_br, w_o = _pack_params(
        w_in, sgu_w, sgu_b, mla_w_uq, mla_w_ukv, w_branch, w_out)
    rope_tabs = _rope_tables(lat_len)
    tab, tab_odd = _na_bias_tables(na_rpb)
    norm_g3 = norm_g.reshape(DEPTH, 1, D_MODEL)
    kv_norm3 = mla_kv_norm.reshape(DEPTH, 1, MLA_KV_LORA)
    q_norm3 = mla_q_norm.reshape(DEPTH, 1, MLA_Q_LORA)
    final_g = final_norm_g.reshape(1, D_MODEL)

    cond_rows = 16
    cond = jnp.concatenate([c, c_ctx[None, :], jnp.zeros((cond_rows - n_lat - 1, D_MODEL), F32)], axis=0)
    mod = _modulation(cond, w_mod, b_mod).reshape(DEPTH, cond_rows, 1, 3 * D_MODEL)
    ctx_row = lambda token: n_lat
    lat_row = lambda token: token // lat_len

    cache_k = jnp.transpose(cache_na_k, (0, 1, 3, 4, 2)).reshape(n_lat, DEPTH, BRANCH_WIDTH, past)
    cache_v = jnp.transpose(cache_na_v, (0, 1, 3, 4, 2)).reshape(n_lat, DEPTH, BRANCH_WIDTH, past)
    cache_kr = jnp.pad(cache_mla_krope, ((0, 0), (0, 0), (0, 0), (ROPE_LANE0, LANES - ROPE_LANE0 - MLA_ROPE)))

    xp = x_prompt.reshape(n_ctx * ctx_len, D_MODEL)
    xs = x_sample.reshape(n_lat * lat_len, D_MODEL)
    state = None
    for l in range(DEPTH):
        final = l == DEPTH - 1
        p, ckv, kr, *state = _inproj(xp, mod, ctx_row, l, norm_g3, w_packed, kv_norm3, None, state, True, TILE)
        y_na = _na_ctx(p, n_ctx)
        y_mla = _mla(p, ckv, kr, None, None, l, q_norm3, w_uq, w_k, w_v, None, n_ctx, ctx_len)
        xp = _merge(xp, mod, ctx_row, l, p, y_na, y_mla, sgu_wb, sgu_bias, w_br, w_o, final_g, final, MERGE_TILE)
        p, ckv, kr = _inproj(xs, mod, lat_row, l, norm_g3, w_packed, kv_norm3, rope_tabs, None, False, TILE)
        y_na = _na_lat(p, cache_k, cache_v, tab, tab_odd, l, n_lat, lat_len)
        y_mla = _mla(p, ckv, kr, cache_mla_ckv, cache_kr, l, q_norm3, w_uq, w_k, w_v, rope_tabs, n_lat, lat_len)
        xs = _merge(xs, mod, lat_row, l, p, y_na, y_mla, sgu_wb, sgu_bias, w_br, w_o, final_g, final, MERGE_TILE)

    s_k, s_v, s_ckv, s_kr = state

    def heads_last(s):
        return jnp.transpose(s.reshape(n_ctx, DEPTH, NA_HEADS, NA_HEAD_DIM, ctx_len), (0, 1, 4, 2, 3))

    return (xp.reshape(n_ctx, ctx_len, D_MODEL), xs.reshape(n_lat, lat_len, D_MODEL),
            heads_last(s_k), heads_last(s_v), s_ckv, jnp.swapaxes(s_kr, 2, 3))
```

```python
import functools

import numpy as np
import jax
import jax.numpy as jnp
from jax import lax
from jax.experimental import pallas as pl
from jax.experimental.pallas import tpu as pltpu

F32 = jnp.float32
BF16 = jnp.bfloat16

D_MODEL = 1024
DEPTH = 2
GRID_W = 64
BRANCH_WIDTH = 512
N_BRANCH = 3
NA_HEADS = 8
NA_HEAD_DIM = 64
NA_WIN_H = 8
NA_WIN_W = 16
SGU_GROUPS = 4
SGU_CHUNK = 128
MLA_HEADS = 8
MLA_NOPE = 64
MLA_ROPE = 32
MLA_V = 64
MLA_Q_LORA = 384
MLA_KV_LORA = 256
ROPE_THETA = 10000.0
EPS = 1e-6
NEG_INF = -1e30

LANES = 128
TILE = 256
MERGE_TILE = 512
STRIP = 16
HEAD_PAIRS = NA_HEADS // 2
NA_KEY_ROWS = 12
NA_LOC_KEYS = NA_KEY_ROWS * GRID_W
ROWS_PER_TILE = TILE // GRID_W
TAB_PAD = 4
TAB_BLOCKS = 24
ROPE_LANE0 = MLA_NOPE

C_NAQ, C_NAK, C_NAV, C_NAZ, C_SU, C_SV, C_SZ, C_MZ, C_MG = (512 * i for i in (0, 1, 2, 3, 4, 5, 6, 7, 8))
C_DQ = C_MG + N_BRANCH * D_MODEL
C_DKV = C_DQ + MLA_Q_LORA
C_KR = C_DKV + MLA_KV_LORA
W_COLS = C_KR + LANES
P_COLS = C_DQ + 512
CHUNK = 512

VMEM_LIMIT = 56 * 1024 * 1024


def _dot(a, b):
    return jnp.dot(a, b, preferred_element_type=F32)


def _dot_nt(a, b):
    return lax.dot_general(a, b, (((1,), (1,)), ((), ())), preferred_element_type=F32)


def _rms(x, g):
    return x * lax.rsqrt(jnp.mean(x * x, axis=-1, keepdims=True) + EPS) * g


def _center_norm(x):
    c = x - jnp.mean(x, axis=-1, keepdims=True)
    return c * lax.rsqrt(jnp.mean(c * c, axis=-1, keepdims=True) + EPS)


def _rope128(x, cos, sin):
    lane = lax.broadcasted_iota(jnp.int32, x.shape, 1)
    swapped = jnp.where(lane < ROPE_LANE0 + MLA_ROPE // 2,
                        pltpu.roll(x, LANES - MLA_ROPE // 2, axis=1),
                        pltpu.roll(x, MLA_ROPE // 2, axis=1))
    return x * cos + swapped * sin


def _store_scores(s_ref, m_ref, rows, cols, scores, first):
    s_ref[rows, cols] = scores
    block_max = jnp.broadcast_to(scores.max(axis=-1, keepdims=True), (scores.shape[0], LANES))
    m_ref[rows, :] = block_max if first else jnp.maximum(m_ref[rows, :], block_max)


def _pair_softmax_pv(s_ref, m_ref, p_ref, den_ref, segments, values):
    for i in range(s_ref.shape[0] // STRIP):
        r = slice(i * STRIP, (i + 1) * STRIP)
        m = m_ref[r, :]
        den = None
        for c0, c1 in segments:
            e = jnp.exp(s_ref[r, c0:c1] - jnp.concatenate([m] * ((c1 - c0) // LANES), axis=1))
            d = e.sum(axis=-1, keepdims=True)
            den = d if den is None else den + d
            p_ref[r, c0:c1] = e.astype(BF16)
        den_ref[r, :] = jnp.broadcast_to(den, (STRIP, LANES))
    o = None
    for (c0, c1), (v, transposed) in zip(segments, values):
        t = _dot_nt(p_ref[:, c0:c1], v) if transposed else _dot(p_ref[:, c0:c1], v)
        o = t if o is None else o + t
    o = o / den_ref[...]
    lane = lax.broadcasted_iota(jnp.int32, (TILE, LANES), 1)
    return jnp.where(lane < LANES // 2, o[:TILE], o[TILE:])


def _pair_scratch(n_keys):
    return [pltpu.VMEM((2, 2 * TILE, n_keys), F32), pltpu.VMEM((2, 2 * TILE, LANES), F32),
            pltpu.VMEM((2, 2 * TILE, n_keys), BF16), pltpu.VMEM((2, 2 * TILE, LANES), F32)]


def _const_spec(shape, index):
    return pl.BlockSpec(shape, lambda *_: index, pipeline_mode=pl.Buffered(1))


def _params(semantics):
    return pltpu.CompilerParams(dimension_semantics=semantics, vmem_limit_bytes=VMEM_LIMIT)


def _mod_kernel(c_ref, w_ref, b_ref, o_ref):
    c = c_ref[...]
    s = c * jax.nn.sigmoid(c)
    o_ref[...] = jnp.dot(s, w_ref[...], preferred_element_type=F32, precision=lax.Precision.HIGHEST) + b_ref[...]


def _modulation(cond, w_mod, b_mod):
    rows = cond.shape[0]
    return pl.pallas_call(
        _mod_kernel,
        grid=(DEPTH, 3),
        in_specs=[pl.BlockSpec((rows, D_MODEL), lambda l, j: (0, 0)),
                  pl.BlockSpec((None, D_MODEL, D_MODEL), lambda l, j: (l, 0, j)),
                  pl.BlockSpec((None, 1, D_MODEL), lambda l, j: (l, 0, j))],
        out_specs=pl.BlockSpec((None, rows, D_MODEL), lambda l, j: (l, 0, j)),
        out_shape=jax.ShapeDtypeStruct((DEPTH, rows, 3 * D_MODEL), F32),
        compiler_params=_params(("parallel", "parallel")),
        name="modulation",
    )(cond, w_mod, b_mod.reshape(DEPTH, 1, 3 * D_MODEL))


def _inproj_kernel(*refs, emit_state, rope, n_alias):
    x0_ref, x_next_ref, mod0_ref, mod_next_ref, g_ref, w_ref, kvn_ref = refs[:7]
    refs = refs[7:]
    if rope:
        cos_ref, sin_ref = refs[:2]
        refs = refs[2:]
    refs = refs[n_alias:]
    p_ref, ckv_ref, kr_ref = refs[:3]
    if emit_state:
        sk_ref, sv_ref, sckv_ref, skr_ref = refs[3:7]
    h_ref = refs[-1]
    step = pl.program_id(0)
    tm = x0_ref.shape[0]

    def normalised(x_ref, mod_ref):
        shift = mod_ref[:, 0:D_MODEL]
        scale = mod_ref[:, D_MODEL:2 * D_MODEL]
        return (_rms(x_ref[...], g_ref[...]) * (1.0 + scale) + shift).astype(BF16)

    @pl.when(step == 0)
    def _first_tile():
        h_ref[0] = normalised(x0_ref, mod0_ref)

    h = h_ref[step % 2]

    def tiles(a):
        return a.reshape(tm // TILE, TILE, a.shape[-1])

    ckv = _rms(_dot(h, w_ref[:, C_DKV:C_KR]), kvn_ref[...])
    ckv_ref[...] = ckv.astype(BF16)
    kr = _dot(h, w_ref[:, C_KR:W_COLS])
    if emit_state:
        sckv_ref[...] = ckv
        skr_ref[...] = kr.T[ROPE_LANE0:ROPE_LANE0 + MLA_ROPE, :]
    if rope:
        kr = _rope128(kr, cos_ref[...], sin_ref[...])
    kr_ref[...] = kr.astype(BF16)

    h_ref[(step + 1) % 2] = normalised(x_next_ref, mod_next_ref)

    for c0 in range(0, P_COLS, CHUNK):
        acc = _dot(h, w_ref[:, c0:c0 + CHUNK])
        if emit_state and c0 == C_NAK:
            sk_ref[...] = acc.T
        if emit_state and c0 == C_NAV:
            sv_ref[...] = acc.T
        if c0 in (C_NAZ, C_SZ, C_MZ):
            acc = acc * jax.nn.sigmoid(acc)
        elif c0 == C_SU:
            acc = jax.nn.gelu(acc)
        elif c0 == C_SV:
            acc = _center_norm(jax.nn.gelu(acc))
        elif C_MG <= c0 < C_DQ:
            acc = jax.nn.sigmoid(acc)
        p_ref[:, c0 // CHUNK] = tiles(acc.astype(BF16))


def _inproj(x, mod, mod_row, layer, norm_g, w_packed, kv_norm, rope_tabs, state_in, emit_state, tm):
    n_tok = x.shape[0]
    n_tiles = n_tok // tm
    rope = rope_tabs is not None
    next_tile = lambda i: jnp.minimum(i + 1, n_tiles - 1)
    in_specs = [_const_spec((tm, D_MODEL), (0, 0)),
                pl.BlockSpec((tm, D_MODEL), lambda i: (next_tile(i), 0)),
                _const_spec((None, None, 1, 3 * D_MODEL), (layer, mod_row(0), 0, 0)),
                pl.BlockSpec((None, None, 1, 3 * D_MODEL), lambda i: (layer, mod_row(next_tile(i) * tm), 0, 0)),
                _const_spec((None, 1, D_MODEL), (layer, 0, 0)),
                _const_spec((None, D_MODEL, W_COLS), (layer, 0, 0)),
                _const_spec((None, 1, MLA_KV_LORA), (layer, 0, 0))]
    args = [x, x, mod, mod, norm_g, w_packed, kv_norm]
    if rope:
        tiles_per_seq = rope_tabs[0].shape[0] // tm
        in_specs += [pl.BlockSpec((tm, LANES), lambda i: (i % tiles_per_seq, 0))] * 2
        args += list(rope_tabs)
    out_specs = [pl.BlockSpec((tm // TILE, P_COLS // CHUNK, TILE, CHUNK), lambda i: (i, 0, 0, 0)),
                 pl.BlockSpec((tm, MLA_KV_LORA), lambda i: (i, 0)),
                 pl.BlockSpec((tm, LANES), lambda i: (i, 0))]
    out_shape = [jax.ShapeDtypeStruct((n_tok // TILE, P_COLS // CHUNK, TILE, CHUNK), BF16),
                 jax.ShapeDtypeStruct((n_tok, MLA_KV_LORA), BF16),
                 jax.ShapeDtypeStruct((n_tok, LANES), BF16)]
    aliases = {}
    if emit_state:
        assert tm == TILE
        for rows, cols in ((BRANCH_WIDTH, TILE), (BRANCH_WIDTH, TILE), (TILE, MLA_KV_LORA), (MLA_ROPE, TILE)):
            out_specs.append(pl.BlockSpec((None, None, rows, cols), lambda i: (i, layer, 0, 0)))
            out_shape.append(jax.ShapeDtypeStruct((n_tok // TILE, DEPTH, rows, cols), F32))
        if state_in is not None:
            first = len(args)
            in_specs += [pl.BlockSpec(memory_space=pl.ANY)] * 4
            args += list(state_in)
            aliases = {first + k: 3 + k for k in range(4)}
    kern = functools.partial(_inproj_kernel, emit_state=emit_state, rope=rope, n_alias=len(aliases))
    return pl.pallas_call(
        kern, grid=(n_tiles,), in_specs=in_specs, out_specs=out_specs, out_shape=out_shape,
        scratch_shapes=[pltpu.VMEM((2, tm, D_MODEL), BF16)],
        input_output_aliases=aliases, compiler_params=_params(("arbitrary",)),
        name="inproj",
    )(*args)


def _na_heads(q_ref, o_ref, scratch, keys_fn, values_fn, bias_fn=None):
    lane = lax.broadcasted_iota(jnp.int32, (TILE, LANES), 1)
    low = lane < NA_HEAD_DIM
    for hp in range(HEAD_PAIRS):
        sl = slice(LANES * hp, LANES * (hp + 1))
        q2 = q_ref[:, sl] * (NA_HEAD_DIM ** -0.5)
        zero = jnp.zeros_like(q2)
        q_stack = jnp.concatenate([jnp.where(low, q2, zero), jnp.where(low, zero, q2)], axis=0)
        s_pair, m_pair, p_pair, den_pair = (ref.at[hp % 2] for ref in scratch)
        segments = []
        c0 = 0
        for j, (k, transposed) in enumerate(keys_fn(sl)):
            n_keys = k.shape[1] if transposed else k.shape[0]
            scores = _dot(q_stack, k) if transposed else _dot_nt(q_stack, k)
            if j == 0 and bias_fn is not None:
                scores = scores + bias_fn(2 * hp)
            _store_scores(s_pair, m_pair, slice(None), slice(c0, c0 + n_keys), scores, j == 0)
            segments.append((c0, c0 + n_keys))
            c0 += n_keys
        o_ref[:, sl] = _pair_softmax_pv(s_pair, m_pair, p_pair, den_pair, segments, values_fn(sl)).astype(BF16)


def _na_ctx_kernel(q_ref, k_ref, v_ref, o_ref, *scratch):
    _na_heads(q_ref, o_ref, scratch, lambda sl: [(k_ref[:, sl], False)], lambda sl: [(v_ref[:, sl], False)])


def _na_band_row0(tile_row0, n_rows):
    return min(max(tile_row0 - NA_WIN_H // 2, 0), n_rows - NA_KEY_ROWS)


def _na_band_plans(seq):
    n_rows = seq // GRID_W
    plans = []
    for t in range(seq // TILE):
        band0 = _na_band_row0(t * ROWS_PER_TILE, n_rows)
        plan = []
        for r in range(ROWS_PER_TILE):
            rq = t * ROWS_PER_TILE + r
            row_lo = min(max(rq - NA_WIN_H // 2, 0), n_rows - NA_WIN_H)
            plan.append((band0 - rq + NA_WIN_H - 1 + TAB_PAD, row_lo - band0))
        plans.append(tuple(plan))
    return tuple(plans)


def _na_lat_kernel(q_ref, k_ref, v_ref, kc_ref, vc_ref, tab_ref, o_ref, bias_ref, *scratch, plans, n_rows):
    t = pl.program_id(0)
    first_half = lax.broadcasted_iota(jnp.int32, (GRID_W, LANES), 1) < GRID_W
    masked = jnp.full((GRID_W, LANES), NEG_INF, F32)

    def block_pair(h, blk):
        if blk % 2 == 0:
            return tab_ref[h, blk // 2]
        return jnp.where(first_half, pltpu.roll(tab_ref[h, blk // 2], GRID_W, axis=1),
                         pltpu.roll(tab_ref[h, blk // 2 + 1], GRID_W, axis=1))

    for plan in sorted(set(plans)):
        is_tile = functools.reduce(jnp.logical_or, [t == i for i, p in enumerate(plans) if p == plan])

        @pl.when(jnp.logical_and(pl.program_id(1) == 0, is_tile))
        def _build_bias(plan=plan):
            for r, (blk0, j0) in enumerate(plan):
                for j in range(NA_KEY_ROWS // 2):
                    in_window = [j0 <= 2 * j + half < j0 + NA_WIN_H for half in range(2)]
                    for h in range(NA_HEADS):
                        if not any(in_window):
                            piece = masked
                        elif all(in_window):
                            piece = block_pair(h, blk0 + 2 * j)
                        else:
                            keep = first_half if in_window[0] else jnp.logical_not(first_half)
                            piece = jnp.where(keep, block_pair(h, blk0 + 2 * j), NEG_INF)
                        bias_ref[h, r * GRID_W:(r + 1) * GRID_W, LANES * j:LANES * (j + 1)] = piece

    band_tile0 = jnp.clip(t * ROWS_PER_TILE - NA_WIN_H // 2, 0, n_rows - NA_KEY_ROWS) // ROWS_PER_TILE
    for plan in sorted(set(plans)):
        first_row = min(j0 for _, j0 in plan)
        last_row = max(j0 for _, j0 in plan) + NA_WIN_H
        skip = first_row // ROWS_PER_TILE
        n_tiles = -(-last_row // ROWS_PER_TILE) - skip
        key_tiles = pl.ds(band_tile0 + skip, n_tiles)
        bias_lanes = slice(skip * TILE, (skip + n_tiles) * TILE)

        @pl.when(functools.reduce(jnp.logical_or, [t == i for i, p in enumerate(plans) if p == plan]))
        def _attend(key_tiles=key_tiles, bias_lanes=bias_lanes, n_keys=n_tiles * TILE):
            def band(ref, sl):
                return ref[key_tiles, :, sl].reshape(n_keys, LANES)

            def bias(head0):
                return bias_ref[head0:head0 + 2, :, bias_lanes].reshape(2 * TILE, n_keys)

            _na_heads(q_ref, o_ref, scratch,
                      lambda sl: [(band(k_ref, sl), False), (kc_ref[sl, :].astype(BF16), True)],
                      lambda sl: [(band(v_ref, sl), False), (vc_ref[sl, :].astype(BF16), True)], bias)


def _p_block(tile_of, col):
    return pl.BlockSpec((None, None, TILE, CHUNK), lambda *g: (tile_of(*g), col // CHUNK, 0, 0))


def _na_ctx(p, n_seq):
    blk = lambda col: _p_block(lambda i: i, col)
    return pl.pallas_call(
        _na_ctx_kernel, grid=(n_seq,),
        in_specs=[blk(C_NAQ), blk(C_NAK), blk(C_NAV)],
        out_specs=pl.BlockSpec((TILE, BRANCH_WIDTH), lambda i: (i, 0)),
        out_shape=jax.ShapeDtypeStruct((n_seq * TILE, BRANCH_WIDTH), BF16),
        scratch_shapes=_pair_scratch(TILE),
        compiler_params=_params(("parallel",)), name="na_ctx",
    )(p, p, p)


def _na_lat(p, cache_k, cache_v, tab, layer, n_seq, seq):
    tiles = seq // TILE
    n_past = cache_k.shape[3]
    kv = lambda col: pl.BlockSpec((tiles, None, TILE, CHUNK), lambda t, b: (b, col // CHUNK, 0, 0))
    cache = pl.BlockSpec((None, None, BRANCH_WIDTH, n_past), lambda t, b: (b, layer, 0, 0))
    table = _const_spec((None, NA_HEADS, TAB_BLOCKS // 2, GRID_W, LANES), (layer, 0, 0, 0, 0))
    return pl.pallas_call(
        functools.partial(_na_lat_kernel, plans=_na_band_plans(seq), n_rows=seq // GRID_W),
        grid=(tiles, n_seq),
        in_specs=[_p_block(lambda t, b: b * tiles + t, C_NAQ),
                  kv(C_NAK), kv(C_NAV), cache, cache, table],
        out_specs=pl.BlockSpec((TILE, BRANCH_WIDTH), lambda t, b: (b * tiles + t, 0)),
        out_shape=jax.ShapeDtypeStruct((n_seq * seq, BRANCH_WIDTH), BF16),
        scratch_shapes=[pltpu.VMEM((NA_HEADS, TILE, NA_LOC_KEYS), F32)]
        + _pair_scratch(NA_LOC_KEYS + n_past),
        compiler_params=_params(("parallel", "arbitrary")), name="na_lat",
    )(p, p, p, cache_k, cache_v, tab)


def _na_bias_tables(rpb):
    cq = np.arange(GRID_W)[:, None]
    ck = np.arange(GRID_W)[None, :]
    col_lo = np.clip(cq - NA_WIN_W // 2, 0, GRID_W - NA_WIN_W)
    ok = (ck >= col_lo) & (ck < col_lo + NA_WIN_W)
    n_b = 2 * NA_WIN_W - 1
    n_rel = 2 * NA_WIN_H - 1
    onehot = ((ck - cq + NA_WIN_W - 1)[:, None, :] == np.arange(n_b)[None, :, None]) & ok[:, None, :]
    expand = np.zeros((GRID_W, 2, n_b, 2, GRID_W), np.float32)
    expand[:, 0, :, 0, :] = onehot
    expand[:, 1, :, 1, :] = onehot
    expand = expand.reshape(GRID_W, 2 * n_b, LANES)
    rows = jnp.pad(rpb, ((0, 0), (0, 0), (TAB_PAD, TAB_BLOCKS - n_rel - TAB_PAD), (0, 0)))
    real = np.zeros(TAB_BLOCKS, bool)
    real[TAB_PAD:TAB_PAD + n_rel] = True
    pairs = rows.reshape(DEPTH, NA_HEADS, TAB_BLOCKS // 2, 2 * n_b)
    tab = jnp.einsum('lhek,ukc->lheuc', pairs, expand, precision=lax.Precision.HIGHEST)
    keep = real.reshape(-1, 1, 2, 1) & ok[None, :, None, :]
    return jnp.where(keep.reshape(TAB_BLOCKS // 2, GRID_W, LANES), tab, NEG_INF)


def _mla_kernel(*refs, n_cache, n_lat, rope):
    dq_ref, ckv_ref, kr_ref = refs[:3]
    refs = refs[3:]
    if n_cache:
        cckv_ref, ckr_ref = refs[:2]
        refs = refs[2:]
    qn_ref, wuq_ref, wk_ref, wv_ref = refs[:4]
    refs = refs[4:]
    if rope:
        cos_ref, sin_ref = refs[:2]
        refs = refs[2:]
    o_ref, kx_ref, vx_ref = refs[:3]
    scratch = refs[3:]

    @pl.when(pl.program_id(1) == 0)
    def _expand_keys():
        def fill(r0, ckv, kr):
            n = ckv.shape[0]
            kk = _dot(ckv, wk_ref[...])
            for h in range(MLA_HEADS):
                sl = slice(LANES * h, LANES * (h + 1))
                kx_ref[r0:r0 + n, sl] = (kk[:, sl] + kr).astype(BF16)
            vx_ref[r0:r0 + n, :] = _dot(ckv, wv_ref[...]).astype(BF16)

        if n_cache:
            fill(0, cckv_ref[...].astype(BF16), ckr_ref[...])
        for r0 in range(0, n_lat, TILE):
            fill(n_cache + r0, ckv_ref[r0:r0 + TILE, :], kr_ref[r0:r0 + TILE, :].astype(F32))

    dqn = _rms(dq_ref[:, :MLA_Q_LORA].astype(F32), qn_ref[...])
    q = _dot(dqn.astype(BF16), wuq_ref[...])
    scale = (MLA_NOPE + MLA_ROPE) ** -0.5
    n_keys = kx_ref.shape[0]
    for hp in range(HEAD_PAIRS):
        s_pair, m_pair, p_pair, den_pair = (ref.at[hp % 2] for ref in scratch)
        for half in range(2):
            sl = slice(LANES * (2 * hp + half), LANES * (2 * hp + half + 1))
            qh = q[:, sl]
            if rope:
                qh = _rope128(qh, cos_ref[...], sin_ref[...])
            qh = (qh * scale).astype(BF16)
            _store_scores(s_pair, m_pair, slice(half * TILE, (half + 1) * TILE), slice(None),
                          _dot_nt(qh, kx_ref[:, sl]), True)
        v2 = vx_ref[:, LANES * hp:LANES * (hp + 1)]
        o_ref[:, LANES * hp:LANES * (hp + 1)] = _pair_softmax_pv(
            s_pair, m_pair, p_pair, den_pair, [(0, n_keys)], [(v2, False)]).astype(BF16)


def _mla(p, ckv, kr, cache_ckv, cache_kr, layer, q_norm, w_uq, w_k, w_v, rope_tabs, n_seq, seq):
    tiles = seq // TILE
    n_cache = 0 if cache_ckv is None else cache_ckv.shape[2]
    rope = rope_tabs is not None
    in_specs = [_p_block(lambda b, t: b * tiles + t, C_DQ),
                pl.BlockSpec((seq, MLA_KV_LORA), lambda b, t: (b, 0)),
                pl.BlockSpec((seq, LANES), lambda b, t: (b, 0))]
    args = [p, ckv, kr]
    if n_cache:
        in_specs += [pl.BlockSpec((None, None, n_cache, MLA_KV_LORA), lambda b, t: (b, layer, 0, 0)),
                     pl.BlockSpec((None, None, n_cache, LANES), lambda b, t: (b, layer, 0, 0))]
        args += [cache_ckv, cache_kr]
    in_specs += [_const_spec((None, 1, MLA_Q_LORA), (layer, 0, 0)),
                 _const_spec((None, MLA_Q_LORA, MLA_HEADS * LANES), (layer, 0, 0)),
                 _const_spec((None, MLA_KV_LORA, MLA_HEADS * LANES), (layer, 0, 0)),
                 _const_spec((None, MLA_KV_LORA, MLA_HEADS * MLA_V), (layer, 0, 0))]
    args += [q_norm, w_uq, w_k, w_v]
    if rope:
        in_specs += [pl.BlockSpec((TILE, LANES), lambda b, t: (t, 0))] * 2
        args += list(rope_tabs)
    n_keys = n_cache + seq
    return pl.pallas_call(
        functools.partial(_mla_kernel, n_cache=n_cache, n_lat=seq, rope=rope),
        grid=(n_seq, tiles), in_specs=in_specs,
        out_specs=pl.BlockSpec((TILE, BRANCH_WIDTH), lambda b, t: (b * tiles + t, 0)),
        out_shape=jax.ShapeDtypeStruct((n_seq * seq, BRANCH_WIDTH), BF16),
        scratch_shapes=[pltpu.VMEM((n_keys, MLA_HEADS * LANES), BF16), pltpu.VMEM((n_keys, MLA_HEADS * MLA_V), BF16)]
        + _pair_scratch(n_keys),
        compiler_params=_params(("parallel", "arbitrary")), name="mla",
    )(*args)


def _merge_kernel(x_ref, mod_ref, yna_ref, ymla_ref, naz_ref, su_ref, sv_ref, sz_ref, mz_ref, mg0_ref, mg1_ref, mg2_ref,
                  sw_ref, sb_ref, wb_ref, wo_ref, fg_ref, o_ref, *, final):
    tm = x_ref.shape[0]

    def tokens(ref, *idx):
        return ref[(slice(None),) + idx].reshape(tm, CHUNK)

    sv = tokens(sv_ref)
    rows = []
    for c0 in range(0, tm, SGU_CHUNK):
        cols = [_dot(sw_ref[g], sv[c0:c0 + SGU_CHUNK, LANES * g:LANES * (g + 1)]) for g in range(SGU_GROUPS)]
        rows.append(jnp.concatenate(cols, axis=1) + sb_ref[...])
    y_sgu = tokens(su_ref).astype(F32) * jnp.concatenate(rows, axis=0)

    gated = (yna_ref[...] * tokens(naz_ref),
             (y_sgu * tokens(sz_ref).astype(F32)).astype(BF16),
             ymla_ref[...] * tokens(mz_ref))
    merged = None
    for k, mg_ref in enumerate((mg0_ref, mg1_ref, mg2_ref)):
        gate_k = jnp.concatenate([tokens(mg_ref, j) for j in range(mg_ref.shape[1])], axis=1)
        term = gate_k.astype(F32) * _dot(gated[k], wb_ref[k])
        merged = term if merged is None else merged + term
    out = _dot(merged.astype(BF16), wo_ref[...])
    gate = mod_ref[:, 2 * D_MODEL:3 * D_MODEL]
    xn = x_ref[...] + gate * out
    if final:
        xn = _rms(xn, fg_ref[...])
    o_ref[...] = xn


def _merge(x, mod, mod_row, layer, p, y_na, y_mla, sgu_w, sgu_b, w_branch, w_out, final_g, final, tm):
    n_tok = x.shape[0]
    n = tm // TILE
    half = pl.BlockSpec((tm, BRANCH_WIDTH), lambda i: (i, 0))
    full = lambda c: pl.BlockSpec((tm, D_MODEL), lambda i: (i, c))
    chunk = lambda col: pl.BlockSpec((n, None, TILE, CHUNK), lambda i: (i, col // CHUNK, 0, 0))
    per_gate = D_MODEL // CHUNK
    gates = lambda k: pl.BlockSpec((n, per_gate, TILE, CHUNK), lambda i: (i, C_MG // D_MODEL + k, 0, 0))
    in_specs = [full(0),
                pl.BlockSpec((None, None, 1, 3 * D_MODEL), lambda i: (layer, mod_row(i * tm), 0, 0)),
                half, half,
                chunk(C_NAZ), chunk(C_SU), chunk(C_SV), chunk(C_SZ), chunk(C_MZ),
                gates(0), gates(1), gates(2),
                _const_spec((None, SGU_GROUPS, SGU_CHUNK, SGU_CHUNK), (layer, 0, 0, 0)),
                _const_spec((None, SGU_CHUNK, BRANCH_WIDTH), (layer, 0, 0)),
                _const_spec((None, N_BRANCH, BRANCH_WIDTH, D_MODEL), (layer, 0, 0, 0)),
                _const_spec((None, D_MODEL, D_MODEL), (layer, 0, 0)),
                _const_spec((1, D_MODEL), (0, 0))]
    return pl.pallas_call(
        functools.partial(_merge_kernel, final=final),
        grid=(n_tok // tm,), in_specs=in_specs,
        out_specs=full(0), out_shape=jax.ShapeDtypeStruct((n_tok, D_MODEL), F32),
        compiler_params=_params(("parallel",)), name="merge",
    )(x, mod, y_na, y_mla, p, p, p, p, p, p, p, p, sgu_w, sgu_b, w_branch, w_out, final_g)


SRC_DQ = 7 * BRANCH_WIDTH
SRC_DKV = SRC_DQ + MLA_Q_LORA
SRC_KR = SRC_DKV + MLA_KV_LORA
SRC_MZ = SRC_KR + MLA_ROPE
PACK_PLAIN = C_MZ // CHUNK
PACK_SHIFTED = (C_DQ - C_MZ) // CHUNK
PACK_SHIFT = SRC_MZ % CHUNK
assert SRC_DQ % CHUNK == 0 and (SRC_MZ - PACK_SHIFT) // CHUNK == PACK_PLAIN + 1
assert SRC_DQ + CHUNK + (C_KR - P_COLS) == SRC_KR and SRC_KR + MLA_ROPE == SRC_MZ


def _pack_w_in_kernel(a_ref, b_ref, o_ref):
    j = pl.program_id(1)

    def emit(rows):
        o_ref[...] = rows.T.astype(BF16)

    @pl.when(jnp.logical_or(j < PACK_PLAIN, j == PACK_PLAIN + PACK_SHIFTED))
    def _aligned():
        emit(a_ref[...])

    @pl.when(jnp.logical_and(j >= PACK_PLAIN, j < PACK_PLAIN + PACK_SHIFTED))
    def _shifted():
        emit(jnp.concatenate([a_ref[PACK_SHIFT:, :], b_ref[:PACK_SHIFT, :]], axis=0))

    @pl.when(j == PACK_PLAIN + PACK_SHIFTED + 1)
    def _tail():
        n_dkv = C_KR - P_COLS
        zeros = lambda n: jnp.zeros((n, D_MODEL), F32)
        emit(jnp.concatenate([a_ref[:n_dkv, :], zeros(ROPE_LANE0), a_ref[n_dkv:n_dkv + MLA_ROPE, :],
                              zeros(LANES - ROPE_LANE0 - MLA_ROPE), zeros(CHUNK - n_dkv - LANES)], axis=0))


def _pack_w_in(w_in_t):
    n_chunks = pl.cdiv(W_COLS, CHUNK)
    tail_src = SRC_DQ // CHUNK + 1

    def a_block(j):
        return jnp.where(j < PACK_PLAIN, j,
                         jnp.where(j < PACK_PLAIN + PACK_SHIFTED, j + 1,
                                   jnp.where(j == PACK_PLAIN + PACK_SHIFTED, SRC_DQ // CHUNK, tail_src)))

    def b_block(j):
        shifted = jnp.logical_and(j >= PACK_PLAIN, j < PACK_PLAIN + PACK_SHIFTED)
        return jnp.where(shifted, j + 2, PACK_PLAIN + 2)

    return pl.pallas_call(
        _pack_w_in_kernel, grid=(DEPTH, n_chunks),
        in_specs=[pl.BlockSpec((None, CHUNK, D_MODEL), lambda l, j: (l, a_block(j), 0)),
                  pl.BlockSpec((None, CHUNK, D_MODEL), lambda l, j: (l, b_block(j), 0))],
        out_specs=pl.BlockSpec((None, D_MODEL, CHUNK), lambda l, j: (l, 0, j)),
        out_shape=jax.ShapeDtypeStruct((DEPTH, D_MODEL, W_COLS), BF16),
        compiler_params=_params(("parallel", "arbitrary")), name="pack_w_in",
    )(w_in_t, w_in_t)


def _pack_params(w_in, sgu_w, sgu_b, mla_w_uq, mla_w_ukv, w_branch, w_out):
    w_packed = _pack_w_in(jnp.swapaxes(w_in, 1, 2))
    uq = mla_w_uq.reshape(DEPTH, MLA_Q_LORA, MLA_HEADS, MLA_NOPE + MLA_ROPE)
    uq = jnp.pad(uq, ((0, 0), (0, 0), (0, 0), (0, LANES - MLA_NOPE - MLA_ROPE)))
    uq = uq.reshape(DEPTH, MLA_Q_LORA, MLA_HEADS * LANES).astype(BF16)
    ukv = mla_w_ukv.reshape(DEPTH, MLA_KV_LORA, MLA_HEADS, MLA_NOPE + MLA_V)
    w_k = jnp.pad(ukv[..., :MLA_NOPE], ((0, 0), (0, 0), (0, 0), (0, LANES - MLA_NOPE)))
    w_k = w_k.reshape(DEPTH, MLA_KV_LORA, MLA_HEADS * LANES).astype(BF16)
    w_v = ukv[..., MLA_NOPE:].reshape(DEPTH, MLA_KV_LORA, MLA_HEADS * MLA_V).astype(BF16)
    sgu_bias = jnp.repeat(jnp.swapaxes(sgu_b, 1, 2), BRANCH_WIDTH // SGU_GROUPS, axis=2)
    return w_packed, uq, w_k, w_v, sgu_w.astype(BF16), sgu_bias, w_branch.astype(BF16), w_out.astype(BF16)


def _rope_tables(n_tokens):
    pos = jnp.arange(n_tokens, dtype=jnp.int32)
    row = (pos // GRID_W).astype(F32)
    col = (pos % GRID_W).astype(F32)
    n_freq = MLA_ROPE // 4
    inv = ROPE_THETA ** (-jnp.arange(n_freq, dtype=F32) / n_freq)
    ang = jnp.concatenate([row[:, None] * inv, col[:, None] * inv], axis=-1)
    cos, sin = jnp.cos(ang), jnp.sin(ang)
    pad_l, pad_r = ROPE_LANE0, LANES - ROPE_LANE0 - MLA_ROPE
    cos_t = jnp.pad(jnp.concatenate([cos, cos], axis=1), ((0, 0), (pad_l, pad_r)), constant_values=1.0)
    sin_t = jnp.pad(jnp.concatenate([-sin, sin], axis=1), ((0, 0), (pad_l, pad_r)))
    return cos_t, sin_t


def kernel(x_prompt, x_sample, cache_na_k, cache_na_v, cache_mla_ckv, cache_mla_krope, c, c_ctx, norm_g, w_mod, b_mod, w_in, na_rpb, sgu_w, sgu_b, mla_q_norm, mla_w_uq, mla_kv_norm, mla_w_ukv, w_branch, w_out, final_norm_g):
    n_ctx, ctx_len, _ = x_prompt.shape
    n_lat, lat_len, _ = x_sample.shape
    past = cache_na_k.shape[2]
    assert ctx_len == TILE and lat_len % TILE == 0 and lat_len // GRID_W == 16 and past == TILE

    w_packed, w_uq, w_k, w_v, sgu_wb, sgu_bias, w_br, w_o = _pack_params(
        w_in, sgu_w, sgu_b, mla_w_uq, mla_w_ukv, w_branch, w_out)
    rope_tabs = _rope_tables(lat_len)
    tab = _na_bias_tables(na_rpb)
    norm_g3 = norm_g.reshape(DEPTH, 1, D_MODEL)
    kv_norm3 = mla_kv_norm.reshape(DEPTH, 1, MLA_KV_LORA)
    q_norm3 = mla_q_norm.reshape(DEPTH, 1, MLA_Q_LORA)
    final_g = final_norm_g.reshape(1, D_MODEL)

    cond_rows = 16
    cond = jnp.concatenate([c, c_ctx[None, :], jnp.zeros((cond_rows - n_lat - 1, D_MODEL), F32)], axis=0)
    mod = _modulation(cond, w_mod, b_mod).reshape(DEPTH, cond_rows, 1, 3 * D_MODEL)
    ctx_row = lambda token: n_lat
    lat_row = lambda token: token // lat_len

    cache_k = jnp.transpose(cache_na_k, (0, 1, 3, 4, 2)).reshape(n_lat, DEPTH, BRANCH_WIDTH, past)
    cache_v = jnp.transpose(cache_na_v, (0, 1, 3, 4, 2)).reshape(n_lat, DEPTH, BRANCH_WIDTH, past)
    cache_kr = jnp.pad(cache_mla_krope, ((0, 0), (0, 0), (0, 0), (ROPE_LANE0, LANES - ROPE_LANE0 - MLA_ROPE)))

    xp = x_prompt.reshape(n_ctx * ctx_len, D_MODEL)
    xs = x_sample.reshape(n_lat * lat_len, D_MODEL)
    state = None
    for l in range(DEPTH):
        final = l == DEPTH - 1
        p, ckv, kr, *state = _inproj(xp, mod, ctx_row, l, norm_g3, w_packed, kv_norm3, None, state, True, TILE)
        y_na = _na_ctx(p, n_ctx)
        y_mla = _mla(p, ckv, kr, None, None, l, q_norm3, w_uq, w_k, w_v, None, n_ctx, ctx_len)
        xp = _merge(xp, mod, ctx_row, l, p, y_na, y_mla, sgu_wb, sgu_bias, w_br, w_o, final_g, final, MERGE_TILE)
        p, ckv, kr = _inproj(xs, mod, lat_row, l, norm_g3, w_packed, kv_norm3, rope_tabs, None, False, TILE)
        y_na = _na_lat(p, cache_k, cache_v, tab, l, n_lat, lat_len)
        y_mla = _mla(p, ckv, kr, cache_mla_ckv, cache_kr, l, q_norm3, w_uq, w_k, w_v, rope_tabs, n_lat, lat_len)
        xs = _merge(xs, mod, lat_row, l, p, y_na, y_mla, sgu_wb, sgu_bias, w_br, w_o, final_g, final, MERGE_TILE)

    s_k, s_v, s_ckv, s_kr = state

    def heads_last(s):
        return jnp.transpose(s.reshape(n_ctx, DEPTH, NA_HEADS, NA_HEAD_DIM, ctx_len), (0, 1, 4, 2, 3))

    return (xp.reshape(n_ctx, ctx_len, D_MODEL), xs.reshape(n_lat, lat_len, D_MODEL),
            heads_last(s_k), heads_last(s_v), s_ckv, jnp.swapaxes(s_kr, 2, 3))
```

```python
import functools

import numpy as np
import jax
import jax.numpy as jnp
from jax import lax
from jax.experimental import pallas as pl
from jax.experimental.pallas import tpu as pltpu

F32 = jnp.float32
BF16 = jnp.bfloat16

D_MODEL = 1024
DEPTH = 2
GRID_W = 64
BRANCH_WIDTH = 512
N_BRANCH = 3
NA_HEADS = 8
NA_HEAD_DIM = 64
NA_WIN_H = 8
NA_WIN_W = 16
SGU_GROUPS = 4
SGU_CHUNK = 128
MLA_HEADS = 8
MLA_NOPE = 64
MLA_ROPE = 32
MLA_V = 64
MLA_Q_LORA = 384
MLA_KV_LORA = 256
ROPE_THETA = 10000.0
EPS = 1e-6
NEG_INF = -1e30

LANES = 128
TILE = 256
MERGE_TILE = 512
CTX_SEQS_PER_STEP = 2
STRIP = 16
HEAD_PAIRS = NA_HEADS // 2
NA_KEY_ROWS = 12
NA_LOC_KEYS = NA_KEY_ROWS * GRID_W
ROWS_PER_TILE = TILE // GRID_W
TAB_PAD = 4
TAB_BLOCKS = 24
ROPE_LANE0 = MLA_NOPE

C_NAZ, C_SU, C_SV, C_SZ, C_MZ, C_MG = (512 * i for i in range(6))
C_NAQ = C_MG + N_BRANCH * D_MODEL
C_NAK, C_NAV, C_DQ = C_NAQ + 512, C_NAQ + 1024, C_NAQ + 1536
MERGE_CHUNKS = C_NAQ // 512
C_DKV = C_DQ + MLA_Q_LORA
C_KR = C_DKV + MLA_KV_LORA
W_COLS = C_KR + LANES
P_COLS = C_DQ + 512
CHUNK = 512

VMEM_LIMIT = 56 * 1024 * 1024


def _dot(a, b):
    return jnp.dot(a, b, preferred_element_type=F32)


def _dot_nt(a, b):
    return lax.dot_general(a, b, (((1,), (1,)), ((), ())), preferred_element_type=F32)


def _rms(x, g):
    return x * lax.rsqrt(jnp.mean(x * x, axis=-1, keepdims=True) + EPS) * g


def _center_norm(x):
    c = x - jnp.mean(x, axis=-1, keepdims=True)
    return c * lax.rsqrt(jnp.mean(c * c, axis=-1, keepdims=True) + EPS)


def _rope128(x, cos, sin):
    lane = lax.broadcasted_iota(jnp.int32, x.shape, 1)
    swapped = jnp.where(lane < ROPE_LANE0 + MLA_ROPE // 2,
                        pltpu.roll(x, LANES - MLA_ROPE // 2, axis=1),
                        pltpu.roll(x, MLA_ROPE // 2, axis=1))
    return x * cos + swapped * sin


def _store_scores(s_ref, m_ref, rows, cols, scores, first):
    s_ref[rows, cols] = scores
    block_max = jnp.broadcast_to(scores.max(axis=-1, keepdims=True), (scores.shape[0], LANES))
    m_ref[rows, :] = block_max if first else jnp.maximum(m_ref[rows, :], block_max)


def _pair_softmax_pv(s_ref, m_ref, p_ref, den_ref, segments, values):
    for i in range(s_ref.shape[0] // STRIP):
        r = slice(i * STRIP, (i + 1) * STRIP)
        m = m_ref[r, :]
        den = None
        for c0, c1 in segments:
            e = jnp.exp(s_ref[r, c0:c1] - jnp.concatenate([m] * ((c1 - c0) // LANES), axis=1))
            d = e.sum(axis=-1, keepdims=True)
            den = d if den is None else den + d
            p_ref[r, c0:c1] = e.astype(BF16)
        den_ref[r, :] = jnp.broadcast_to(den, (STRIP, LANES))
    o = None
    for (c0, c1), (v, transposed) in zip(segments, values):
        t = _dot_nt(p_ref[:, c0:c1], v) if transposed else _dot(p_ref[:, c0:c1], v)
        o = t if o is None else o + t
    o = o / den_ref[...]
    lane = lax.broadcasted_iota(jnp.int32, (TILE, LANES), 1)
    return jnp.where(lane < LANES // 2, o[:TILE], o[TILE:])


def _pair_scratch(n_keys):
    return [pltpu.VMEM((2, 2 * TILE, n_keys), F32), pltpu.VMEM((2, 2 * TILE, LANES), F32),
            pltpu.VMEM((2, 2 * TILE, n_keys), BF16), pltpu.VMEM((2, 2 * TILE, LANES), F32)]


def _const_spec(shape, index):
    return pl.BlockSpec(shape, lambda *_: index, pipeline_mode=pl.Buffered(1))


def _params(semantics):
    return pltpu.CompilerParams(dimension_semantics=semantics, vmem_limit_bytes=VMEM_LIMIT)


def _mod_kernel(c_ref, w_ref, b_ref, o_ref):
    c = c_ref[...]
    s = c * jax.nn.sigmoid(c)
    o_ref[...] = jnp.dot(s, w_ref[...], preferred_element_type=F32, precision=lax.Precision.HIGHEST) + b_ref[...]


def _modulation(cond, w_mod, b_mod):
    rows = cond.shape[0]
    return pl.pallas_call(
        _mod_kernel,
        grid=(DEPTH, 3),
        in_specs=[pl.BlockSpec((rows, D_MODEL), lambda l, j: (0, 0)),
                  pl.BlockSpec((None, D_MODEL, D_MODEL), lambda l, j: (l, 0, j)),
                  pl.BlockSpec((None, 1, D_MODEL), lambda l, j: (l, 0, j))],
        out_specs=pl.BlockSpec((None, rows, D_MODEL), lambda l, j: (l, 0, j)),
        out_shape=jax.ShapeDtypeStruct((DEPTH, rows, 3 * D_MODEL), F32),
        compiler_params=_params(("parallel", "parallel")),
        name="modulation",
    )(cond, w_mod, b_mod.reshape(DEPTH, 1, 3 * D_MODEL))


def _inproj_kernel(*refs, emit_state, rope, n_alias):
    x0_ref, x_next_ref, mod0_ref, mod_next_ref, g_ref, w_ref, kvn_ref = refs[:7]
    refs = refs[7:]
    if rope:
        cos_ref, sin_ref = refs[:2]
        refs = refs[2:]
    refs = refs[n_alias:]
    p_ref, ckv_ref, kr_ref = refs[:3]
    if emit_state:
        sk_ref, sv_ref, sckv_ref, skr_ref = refs[3:7]
    h_ref = refs[-1]
    step = pl.program_id(0)
    tm = x0_ref.shape[0]

    def normalised(x_ref, mod_ref):
        shift = mod_ref[:, 0:D_MODEL]
        scale = mod_ref[:, D_MODEL:2 * D_MODEL]
        return (_rms(x_ref[...], g_ref[...]) * (1.0 + scale) + shift).astype(BF16)

    @pl.when(step == 0)
    def _first_tile():
        h_ref[0] = normalised(x0_ref, mod0_ref)

    h = h_ref[step % 2]

    def tiles(a):
        return a.reshape(tm // TILE, TILE, a.shape[-1])

    ckv = _rms(_dot(h, w_ref[:, C_DKV:C_KR]), kvn_ref[...])
    ckv_ref[...] = ckv.astype(BF16)
    kr = _dot(h, w_ref[:, C_KR:W_COLS])
    if emit_state:
        sckv_ref[...] = ckv
        skr_ref[...] = kr.T[ROPE_LANE0:ROPE_LANE0 + MLA_ROPE, :]
    if rope:
        kr = _rope128(kr, cos_ref[...], sin_ref[...])
    kr_ref[...] = kr.astype(BF16)

    h_ref[(step + 1) % 2] = normalised(x_next_ref, mod_next_ref)

    for c0 in range(0, P_COLS, CHUNK):
        acc = _dot(h, w_ref[:, c0:c0 + CHUNK])
        if emit_state and c0 == C_NAK:
            sk_ref[...] = acc.T
        if emit_state and c0 == C_NAV:
            sv_ref[...] = acc.T
        if c0 in (C_NAZ, C_SZ, C_MZ):
            acc = acc * jax.nn.sigmoid(acc)
        elif c0 == C_SU:
            acc = jax.nn.gelu(acc)
        elif c0 == C_SV:
            acc = _center_norm(jax.nn.gelu(acc))
        elif C_MG <= c0 < C_NAQ:
            acc = jax.nn.sigmoid(acc)
        p_ref[:, c0 // CHUNK] = tiles(acc.astype(BF16))


def _inproj(x, mod, mod_row, layer, norm_g, w_packed, kv_norm, rope_tabs, state_in, emit_state, tm):
    n_tok = x.shape[0]
    n_tiles = n_tok // tm
    rope = rope_tabs is not None
    next_tile = lambda i: jnp.minimum(i + 1, n_tiles - 1)
    in_specs = [_const_spec((tm, D_MODEL), (0, 0)),
                pl.BlockSpec((tm, D_MODEL), lambda i: (next_tile(i), 0)),
                _const_spec((None, None, 1, 3 * D_MODEL), (layer, mod_row(0), 0, 0)),
                pl.BlockSpec((None, None, 1, 3 * D_MODEL), lambda i: (layer, mod_row(next_tile(i) * tm), 0, 0)),
                _const_spec((None, 1, D_MODEL), (layer, 0, 0)),
                _const_spec((None, D_MODEL, W_COLS), (layer, 0, 0)),
                _const_spec((None, 1, MLA_KV_LORA), (layer, 0, 0))]
    args = [x, x, mod, mod, norm_g, w_packed, kv_norm]
    if rope:
        tiles_per_seq = rope_tabs[0].shape[0] // tm
        in_specs += [pl.BlockSpec((tm, LANES), lambda i: (i % tiles_per_seq, 0))] * 2
        args += list(rope_tabs)
    out_specs = [pl.BlockSpec((tm // TILE, P_COLS // CHUNK, TILE, CHUNK), lambda i: (i, 0, 0, 0)),
                 pl.BlockSpec((tm, MLA_KV_LORA), lambda i: (i, 0)),
                 pl.BlockSpec((tm, LANES), lambda i: (i, 0))]
    out_shape = [jax.ShapeDtypeStruct((n_tok // TILE, P_COLS // CHUNK, TILE, CHUNK), BF16),
                 jax.ShapeDtypeStruct((n_tok, MLA_KV_LORA), BF16),
                 jax.ShapeDtypeStruct((n_tok, LANES), BF16)]
    aliases = {}
    if emit_state:
        assert tm == TILE
        for rows, cols in ((BRANCH_WIDTH, TILE), (BRANCH_WIDTH, TILE), (TILE, MLA_KV_LORA), (MLA_ROPE, TILE)):
            out_specs.append(pl.BlockSpec((None, None, rows, cols), lambda i: (i, layer, 0, 0)))
            out_shape.append(jax.ShapeDtypeStruct((n_tok // TILE, DEPTH, rows, cols), F32))
        if state_in is not None:
            first = len(args)
            in_specs += [pl.BlockSpec(memory_space=pl.ANY)] * 4
            args += list(state_in)
            aliases = {first + k: 3 + k for k in range(4)}
    kern = functools.partial(_inproj_kernel, emit_state=emit_state, rope=rope, n_alias=len(aliases))
    return pl.pallas_call(
        kern, grid=(n_tiles,), in_specs=in_specs, out_specs=out_specs, out_shape=out_shape,
        scratch_shapes=[pltpu.VMEM((2, tm, D_MODEL), BF16)],
        input_output_aliases=aliases, compiler_params=_params(("arbitrary",)),
        name="inproj",
    )(*args)


def _na_heads(q_ref, o_ref, scratch, keys_fn, values_fn, bias_fn=None):
    lane = lax.broadcasted_iota(jnp.int32, (TILE, LANES), 1)
    low = lane < NA_HEAD_DIM
    for hp in range(HEAD_PAIRS):
        sl = slice(LANES * hp, LANES * (hp + 1))
        q2 = q_ref[:, sl] * (NA_HEAD_DIM ** -0.5)
        zero = jnp.zeros_like(q2)
        q_stack = jnp.concatenate([jnp.where(low, q2, zero), jnp.where(low, zero, q2)], axis=0)
        s_pair, m_pair, p_pair, den_pair = (ref.at[hp % 2] for ref in scratch)
        segments = []
        c0 = 0
        for j, (k, transposed) in enumerate(keys_fn(sl)):
            n_keys = k.shape[1] if transposed else k.shape[0]
            scores = _dot(q_stack, k) if transposed else _dot_nt(q_stack, k)
            if j == 0 and bias_fn is not None:
                scores = scores + bias_fn(2 * hp)
            _store_scores(s_pair, m_pair, slice(None), slice(c0, c0 + n_keys), scores, j == 0)
            segments.append((c0, c0 + n_keys))
            c0 += n_keys
        o_ref[:, sl] = _pair_softmax_pv(s_pair, m_pair, p_pair, den_pair, segments, values_fn(sl)).astype(BF16)


def _na_ctx_kernel(q_ref, k_ref, v_ref, o_ref, *scratch):
    for s in range(q_ref.shape[0]):
        _na_heads(q_ref.at[s], o_ref.at[s], scratch,
                  lambda sl, s=s: [(k_ref[s, :, sl], False)], lambda sl, s=s: [(v_ref[s, :, sl], False)])


def _na_band_row0(tile_row0, n_rows):
    return min(max(tile_row0 - NA_WIN_H // 2, 0), n_rows - NA_KEY_ROWS)


def _na_band_plans(seq):
    n_rows = seq // GRID_W
    plans = []
    for t in range(seq // TILE):
        band0 = _na_band_row0(t * ROWS_PER_TILE, n_rows)
        plan = []
        for r in range(ROWS_PER_TILE):
            rq = t * ROWS_PER_TILE + r
            row_lo = min(max(rq - NA_WIN_H // 2, 0), n_rows - NA_WIN_H)
            plan.append((band0 - rq + NA_WIN_H - 1 + TAB_PAD, row_lo - band0))
        plans.append(tuple(plan))
    return tuple(plans)


def _na_lat_kernel(q_ref, k_ref, v_ref, kc_ref, vc_ref, tab_ref, o_ref, bias_ref, *scratch, plans, n_rows):
    t = pl.program_id(0)
    first_half = lax.broadcasted_iota(jnp.int32, (GRID_W, LANES), 1) < GRID_W
    masked = jnp.full((GRID_W, LANES), NEG_INF, F32)

    def block_pair(h, blk):
        if blk % 2 == 0:
            return tab_ref[h, blk // 2]
        return jnp.where(first_half, pltpu.roll(tab_ref[h, blk // 2], GRID_W, axis=1),
                         pltpu.roll(tab_ref[h, blk // 2 + 1], GRID_W, axis=1))

    for plan in sorted(set(plans)):
        is_tile = functools.reduce(jnp.logical_or, [t == i for i, p in enumerate(plans) if p == plan])

        @pl.when(jnp.logical_and(pl.program_id(1) == 0, is_tile))
        def _build_bias(plan=plan):
            for r, (blk0, j0) in enumerate(plan):
                for j in range(NA_KEY_ROWS // 2):
                    in_window = [j0 <= 2 * j + half < j0 + NA_WIN_H for half in range(2)]
                    for h in range(NA_HEADS):
                        if not any(in_window):
                            piece = masked
                        elif all(in_window):
                            piece = block_pair(h, blk0 + 2 * j)
                        else:
                            keep = first_half if in_window[0] else jnp.logical_not(first_half)
                            piece = jnp.where(keep, block_pair(h, blk0 + 2 * j), NEG_INF)
                        bias_ref[h, r * GRID_W:(r + 1) * GRID_W, LANES * j:LANES * (j + 1)] = piece

    band_tile0 = jnp.clip(t * ROWS_PER_TILE - NA_WIN_H // 2, 0, n_rows - NA_KEY_ROWS) // ROWS_PER_TILE
    for plan in sorted(set(plans)):
        first_row = min(j0 for _, j0 in plan)
        last_row = max(j0 for _, j0 in plan) + NA_WIN_H
        skip = first_row // ROWS_PER_TILE
        n_tiles = -(-last_row // ROWS_PER_TILE) - skip
        key_tiles = pl.ds(band_tile0 + skip, n_tiles)
        bias_lanes = slice(skip * TILE, (skip + n_tiles) * TILE)

        @pl.when(functools.reduce(jnp.logical_or, [t == i for i, p in enumerate(plans) if p == plan]))
        def _attend(key_tiles=key_tiles, bias_lanes=bias_lanes, n_keys=n_tiles * TILE):
            def band(ref, sl):
                return ref[key_tiles, :, sl].reshape(n_keys, LANES)

            def bias(head0):
                return bias_ref[head0:head0 + 2, :, bias_lanes].reshape(2 * TILE, n_keys)

            _na_heads(q_ref, o_ref, scratch,
                      lambda sl: [(band(k_ref, sl), False), (kc_ref[sl, :].astype(BF16), True)],
                      lambda sl: [(band(v_ref, sl), False), (vc_ref[sl, :].astype(BF16), True)], bias)


def _p_block(tile_of, col):
    return pl.BlockSpec((None, None, TILE, CHUNK), lambda *g: (tile_of(*g), col // CHUNK, 0, 0))


def _na_ctx(p, n_seq):
    per_step = CTX_SEQS_PER_STEP
    blk = lambda col: pl.BlockSpec((per_step, None, TILE, CHUNK), lambda i: (i, col // CHUNK, 0, 0))
    y = pl.pallas_call(
        _na_ctx_kernel, grid=(n_seq // per_step,),
        in_specs=[blk(C_NAQ), blk(C_NAK), blk(C_NAV)],
        out_specs=pl.BlockSpec((per_step, TILE, BRANCH_WIDTH), lambda i: (i, 0, 0)),
        out_shape=jax.ShapeDtypeStruct((n_seq, TILE, BRANCH_WIDTH), BF16),
        scratch_shapes=_pair_scratch(TILE),
        compiler_params=_params(("parallel",)), name="na_ctx",
    )(p, p, p)
    return y.reshape(n_seq * TILE, BRANCH_WIDTH)


def _na_lat(p, cache_k, cache_v, tab, layer, n_seq, seq):
    tiles = seq // TILE
    n_past = cache_k.shape[3]
    kv = lambda col: pl.BlockSpec((tiles, None, TILE, CHUNK), lambda t, b: (b, col // CHUNK, 0, 0))
    cache = pl.BlockSpec((None, None, BRANCH_WIDTH, n_past), lambda t, b: (b, layer, 0, 0))
    table = _const_spec((None, NA_HEADS, TAB_BLOCKS // 2, GRID_W, LANES), (layer, 0, 0, 0, 0))
    return pl.pallas_call(
        functools.partial(_na_lat_kernel, plans=_na_band_plans(seq), n_rows=seq // GRID_W),
        grid=(tiles, n_seq),
        in_specs=[_p_block(lambda t, b: b * tiles + t, C_NAQ),
                  kv(C_NAK), kv(C_NAV), cache, cache, table],
        out_specs=pl.BlockSpec((TILE, BRANCH_WIDTH), lambda t, b: (b * tiles + t, 0)),
        out_shape=jax.ShapeDtypeStruct((n_seq * seq, BRANCH_WIDTH), BF16),
        scratch_shapes=[pltpu.VMEM((NA_HEADS, TILE, NA_LOC_KEYS), F32)]
        + _pair_scratch(NA_LOC_KEYS + n_past),
        compiler_params=_params(("parallel", "arbitrary")), name="na_lat",
    )(p, p, p, cache_k, cache_v, tab)


def _na_bias_tables(rpb):
    cq = np.arange(GRID_W)[:, None]
    ck = np.arange(GRID_W)[None, :]
    col_lo = np.clip(cq - NA_WIN_W // 2, 0, GRID_W - NA_WIN_W)
    ok = (ck >= col_lo) & (ck < col_lo + NA_WIN_W)
    n_b = 2 * NA_WIN_W - 1
    n_rel = 2 * NA_WIN_H - 1
    onehot = ((ck - cq + NA_WIN_W - 1)[:, None, :] == np.arange(n_b)[None, :, None]) & ok[:, None, :]
    expand = np.zeros((GRID_W, 2, n_b, 2, GRID_W), np.float32)
    expand[:, 0, :, 0, :] = onehot
    expand[:, 1, :, 1, :] = onehot
    expand = expand.reshape(GRID_W, 2 * n_b, LANES)
    rows = jnp.pad(rpb, ((0, 0), (0, 0), (TAB_PAD, TAB_BLOCKS - n_rel - TAB_PAD), (0, 0)))
    real = np.zeros(TAB_BLOCKS, bool)
    real[TAB_PAD:TAB_PAD + n_rel] = True
    pairs = rows.reshape(DEPTH, NA_HEADS, TAB_BLOCKS // 2, 2 * n_b)
    tab = jnp.einsum('lhek,ukc->lheuc', pairs, expand, precision=lax.Precision.HIGHEST)
    keep = real.reshape(-1, 1, 2, 1) & ok[None, :, None, :]
    return jnp.where(keep.reshape(TAB_BLOCKS // 2, GRID_W, LANES), tab, NEG_INF)


def _mla_kernel(*refs, n_cache, n_lat, rope):
    dq_ref, ckv_ref, kr_ref = refs[:3]
    refs = refs[3:]
    if n_cache:
        cckv_ref, ckr_ref = refs[:2]
        refs = refs[2:]
    qn_ref, wuq_ref, wk_ref, wv_ref = refs[:4]
    refs = refs[4:]
    if rope:
        cos_ref, sin_ref = refs[:2]
        refs = refs[2:]
    o_ref, kx_ref, vx_ref = refs[:3]
    scratch = refs[3:]

    @pl.when(pl.program_id(1) == 0)
    def _expand_keys():
        def fill(r0, ckv, kr):
            n = ckv.shape[0]
            kk = _dot(ckv, wk_ref[...])
            for h in range(MLA_HEADS):
                sl = slice(LANES * h, LANES * (h + 1))
                kx_ref[r0:r0 + n, sl] = (kk[:, sl] + kr).astype(BF16)
            vx_ref[r0:r0 + n, :] = _dot(ckv, wv_ref[...]).astype(BF16)

        if n_cache:
            fill(0, cckv_ref[...].astype(BF16), ckr_ref[...])
        for r0 in range(0, n_lat, TILE):
            fill(n_cache + r0, ckv_ref[r0:r0 + TILE, :], kr_ref[r0:r0 + TILE, :].astype(F32))

    dqn = _rms(dq_ref[:, :MLA_Q_LORA].astype(F32), qn_ref[...])
    q = _dot(dqn.astype(BF16), wuq_ref[...])
    scale = (MLA_NOPE + MLA_ROPE) ** -0.5
    n_keys = kx_ref.shape[0]
    for hp in range(HEAD_PAIRS):
        s_pair, m_pair, p_pair, den_pair = (ref.at[hp % 2] for ref in scratch)
        for half in range(2):
            sl = slice(LANES * (2 * hp + half), LANES * (2 * hp + half + 1))
            qh = q[:, sl]
            if rope:
                qh = _rope128(qh, cos_ref[...], sin_ref[...])
            qh = (qh * scale).astype(BF16)
            _store_scores(s_pair, m_pair, slice(half * TILE, (half + 1) * TILE), slice(None),
                          _dot_nt(qh, kx_ref[:, sl]), True)
        v2 = vx_ref[:, LANES * hp:LANES * (hp + 1)]
        o_ref[:, LANES * hp:LANES * (hp + 1)] = _pair_softmax_pv(
            s_pair, m_pair, p_pair, den_pair, [(0, n_keys)], [(v2, False)]).astype(BF16)


def _mla(p, ckv, kr, cache_ckv, cache_kr, layer, q_norm, w_uq, w_k, w_v, rope_tabs, n_seq, seq):
    tiles = seq // TILE
    n_cache = 0 if cache_ckv is None else cache_ckv.shape[2]
    rope = rope_tabs is not None
    in_specs = [_p_block(lambda b, t: b * tiles + t, C_DQ),
                pl.BlockSpec((seq, MLA_KV_LORA), lambda b, t: (b, 0)),
                pl.BlockSpec((seq, LANES), lambda b, t: (b, 0))]
    args = [p, ckv, kr]
    if n_cache:
        in_specs += [pl.BlockSpec((None, None, n_cache, MLA_KV_LORA), lambda b, t: (b, layer, 0, 0)),
                     pl.BlockSpec((None, None, n_cache, LANES), lambda b, t: (b, layer, 0, 0))]
        args += [cache_ckv, cache_kr]
    in_specs += [_const_spec((None, 1, MLA_Q_LORA), (layer, 0, 0)),
                 _const_spec((None, MLA_Q_LORA, MLA_HEADS * LANES), (layer, 0, 0)),
                 _const_spec((None, MLA_KV_LORA, MLA_HEADS * LANES), (layer, 0, 0)),
                 _const_spec((None, MLA_KV_LORA, MLA_HEADS * MLA_V), (layer, 0, 0))]
    args += [q_norm, w_uq, w_k, w_v]
    if rope:
        in_specs += [pl.BlockSpec((TILE, LANES), lambda b, t: (t, 0))] * 2
        args += list(rope_tabs)
    n_keys = n_cache + seq
    return pl.pallas_call(
        functools.partial(_mla_kernel, n_cache=n_cache, n_lat=seq, rope=rope),
        grid=(n_seq, tiles), in_specs=in_specs,
        out_specs=pl.BlockSpec((TILE, BRANCH_WIDTH), lambda b, t: (b * tiles + t, 0)),
        out_shape=jax.ShapeDtypeStruct((n_seq * seq, BRANCH_WIDTH), BF16),
        scratch_shapes=[pltpu.VMEM((n_keys, MLA_HEADS * LANES), BF16), pltpu.VMEM((n_keys, MLA_HEADS * MLA_V), BF16)]
        + _pair_scratch(n_keys),
        compiler_params=_params(("parallel", "arbitrary")), name="mla",
    )(*args)


def _merge_kernel(x_ref, mod_ref, yna_ref, ymla_ref, pm_ref, sw_ref, sb_ref, wb_ref, wo_ref, fg_ref, o_ref, *, final):
    tm = x_ref.shape[0]

    def tokens(col):
        return pm_ref[:, col // CHUNK].reshape(tm, CHUNK)

    sv = tokens(C_SV)
    rows = []
    for c0 in range(0, tm, SGU_CHUNK):
        cols = [_dot(sw_ref[g], sv[c0:c0 + SGU_CHUNK, LANES * g:LANES * (g + 1)]) for g in range(SGU_GROUPS)]
        rows.append(jnp.concatenate(cols, axis=1) + sb_ref[...])
    y_sgu = tokens(C_SU).astype(F32) * jnp.concatenate(rows, axis=0)

    gated = (yna_ref[...] * tokens(C_NAZ),
             (y_sgu * tokens(C_SZ).astype(F32)).astype(BF16),
             ymla_ref[...] * tokens(C_MZ))
    merged = None
    for k in range(N_BRANCH):
        gate_k = jnp.concatenate([tokens(C_MG + k * D_MODEL + c) for c in range(0, D_MODEL, CHUNK)], axis=1)
        term = gate_k.astype(F32) * _dot(gated[k], wb_ref[k])
        merged = term if merged is None else merged + term
    out = _dot(merged.astype(BF16), wo_ref[...])
    gate = mod_ref[:, 2 * D_MODEL:3 * D_MODEL]
    xn = x_ref[...] + gate * out
    if final:
        xn = _rms(xn, fg_ref[...])
    o_ref[...] = xn


def _merge(x, mod, mod_row, layer, p, y_na, y_mla, sgu_w, sgu_b, w_branch, w_out, final_g, final, tm):
    n_tok = x.shape[0]
    half = pl.BlockSpec((tm, BRANCH_WIDTH), lambda i: (i, 0))
    full = lambda c: pl.BlockSpec((tm, D_MODEL), lambda i: (i, c))
    in_specs = [full(0),
                pl.BlockSpec((None, None, 1, 3 * D_MODEL), lambda i: (layer, mod_row(i * tm), 0, 0)),
                half, half,
                pl.BlockSpec((tm // TILE, MERGE_CHUNKS, TILE, CHUNK), lambda i: (i, 0, 0, 0)),
                _const_spec((None, SGU_GROUPS, SGU_CHUNK, SGU_CHUNK), (layer, 0, 0, 0)),
                _const_spec((None, SGU_CHUNK, BRANCH_WIDTH), (layer, 0, 0)),
                _const_spec((None, N_BRANCH, BRANCH_WIDTH, D_MODEL), (layer, 0, 0, 0)),
                _const_spec((None, D_MODEL, D_MODEL), (layer, 0, 0)),
                _const_spec((1, D_MODEL), (0, 0))]
    return pl.pallas_call(
        functools.partial(_merge_kernel, final=final),
        grid=(n_tok // tm,), in_specs=in_specs,
        out_specs=full(0), out_shape=jax.ShapeDtypeStruct((n_tok, D_MODEL), F32),
        compiler_params=_params(("parallel",)), name="merge",
    )(x, mod, y_na, y_mla, p, sgu_w, sgu_b, w_branch, w_out, final_g)


SRC_DQ = 7 * BRANCH_WIDTH
SRC_DKV = SRC_DQ + MLA_Q_LORA
SRC_KR = SRC_DKV + MLA_KV_LORA
SRC_MZ = SRC_KR + MLA_ROPE
SRC_NAZ = 3 * BRANCH_WIDTH
PACK_SHIFT = SRC_MZ % CHUNK
J_SHIFTED = C_MZ // CHUNK
J_Q = C_NAQ // CHUNK
J_DQ = C_DQ // CHUNK
assert SRC_NAZ % CHUNK == 0 and SRC_DQ % CHUNK == 0
assert SRC_DQ + CHUNK + (C_KR - P_COLS) == SRC_KR and SRC_KR + MLA_ROPE == SRC_MZ


def _pack_w_in_kernel(a_ref, b_ref, o_ref):
    j = pl.program_id(1)

    def emit(rows):
        o_ref[...] = rows.T.astype(BF16)

    @pl.when(jnp.logical_or(j < J_SHIFTED, jnp.logical_and(j >= J_Q, j <= J_DQ)))
    def _aligned():
        emit(a_ref[...])

    @pl.when(jnp.logical_and(j >= J_SHIFTED, j < J_Q))
    def _shifted():
        emit(jnp.concatenate([a_ref[PACK_SHIFT:, :], b_ref[:PACK_SHIFT, :]], axis=0))

    @pl.when(j == J_DQ + 1)
    def _tail():
        n_dkv = C_KR - P_COLS
        zeros = lambda n: jnp.zeros((n, D_MODEL), F32)
        emit(jnp.concatenate([a_ref[:n_dkv, :], zeros(ROPE_LANE0), a_ref[n_dkv:n_dkv + MLA_ROPE, :],
                              zeros(LANES - ROPE_LANE0 - MLA_ROPE), zeros(CHUNK - n_dkv - LANES)], axis=0))


def _pack_w_in(w_in_t):
    n_chunks = pl.cdiv(W_COLS, CHUNK)
    shifted0 = (SRC_MZ - PACK_SHIFT) // CHUNK

    def a_block(j):
        return jnp.where(j < J_SHIFTED, j + SRC_NAZ // CHUNK,
                         jnp.where(j < J_Q, j - J_SHIFTED + shifted0,
                                   jnp.where(j < J_DQ, j - J_Q,
                                             jnp.where(j == J_DQ, SRC_DQ // CHUNK, SRC_DQ // CHUNK + 1))))

    def b_block(j):
        shifted = jnp.logical_and(j >= J_SHIFTED, j < J_Q)
        return jnp.where(shifted, j - J_SHIFTED + shifted0 + 1, shifted0 + 1)

    return pl.pallas_call(
        _pack_w_in_kernel, grid=(DEPTH, n_chunks),
        in_specs=[pl.BlockSpec((None, CHUNK, D_MODEL), lambda l, j: (l, a_block(j), 0)),
                  pl.BlockSpec((None, CHUNK, D_MODEL), lambda l, j: (l, b_block(j), 0))],
        out_specs=pl.BlockSpec((None, D_MODEL, CHUNK), lambda l, j: (l, 0, j)),
        out_shape=jax.ShapeDtypeStruct((DEPTH, D_MODEL, W_COLS), BF16),
        compiler_params=_params(("parallel", "arbitrary")), name="pack_w_in",
    )(w_in_t, w_in_t)


def _pack_params(w_in, sgu_w, sgu_b, mla_w_uq, mla_w_ukv, w_branch, w_out):
    w_packed = _pack_w_in(jnp.swapaxes(w_in, 1, 2))
    uq = mla_w_uq.reshape(DEPTH, MLA_Q_LORA, MLA_HEADS, MLA_NOPE + MLA_ROPE)
    uq = jnp.pad(uq, ((0, 0), (0, 0), (0, 0), (0, LANES - MLA_NOPE - MLA_ROPE)))
    uq = uq.reshape(DEPTH, MLA_Q_LORA, MLA_HEADS * LANES).astype(BF16)
    ukv = mla_w_ukv.reshape(DEPTH, MLA_KV_LORA, MLA_HEADS, MLA_NOPE + MLA_V)
    w_k = jnp.pad(ukv[..., :MLA_NOPE], ((0, 0), (0, 0), (0, 0), (0, LANES - MLA_NOPE)))
    w_k = w_k.reshape(DEPTH, MLA_KV_LORA, MLA_HEADS * LANES).astype(BF16)
    w_v = ukv[..., MLA_NOPE:].reshape(DEPTH, MLA_KV_LORA, MLA_HEADS * MLA_V).astype(BF16)
    sgu_bias = jnp.repeat(jnp.swapaxes(sgu_b, 1, 2), BRANCH_WIDTH // SGU_GROUPS, axis=2)
    return w_packed, uq, w_k, w_v, sgu_w.astype(BF16), sgu_bias, w_branch.astype(BF16), w_out.astype(BF16)


def _rope_tables(n_tokens):
    pos = jnp.arange(n_tokens, dtype=jnp.int32)
    row = (pos // GRID_W).astype(F32)
    col = (pos % GRID_W).astype(F32)
    n_freq = MLA_ROPE // 4
    inv = ROPE_THETA ** (-jnp.arange(n_freq, dtype=F32) / n_freq)
    ang = jnp.concatenate([row[:, None] * inv, col[:, None] * inv], axis=-1)
    cos, sin = jnp.cos(ang), jnp.sin(ang)
    pad_l, pad_r = ROPE_LANE0, LANES - ROPE_LANE0 - MLA_ROPE
    cos_t = jnp.pad(jnp.concatenate([cos, cos], axis=1), ((0, 0), (pad_l, pad_r)), constant_values=1.0)
    sin_t = jnp.pad(jnp.concatenate([-sin, sin], axis=1), ((0, 0), (pad_l, pad_r)))
    return cos_t, sin_t


def kernel(x_prompt, x_sample, cache_na_k, cache_na_v, cache_mla_ckv, cache_mla_krope, c, c_ctx, norm_g, w_mod, b_mod, w_in, na_rpb, sgu_w, sgu_b, mla_q_norm, mla_w_uq, mla_kv_norm, mla_w_ukv, w_branch, w_out, final_norm_g):
    n_ctx, ctx_len, _ = x_prompt.shape
    n_lat, lat_len, _ = x_sample.shape
    past = cache_na_k.shape[2]
    assert ctx_len == TILE and lat_len % TILE == 0 and lat_len // GRID_W == 16 and past == TILE

    w_packed, w_uq, w_k, w_v, sgu_wb, sgu_bias, w_br, w_o = _pack_params(
        w_in, sgu_w, sgu_b, mla_w_uq, mla_w_ukv, w_branch, w_out)
    rope_tabs = _rope_tables(lat_len)
    tab = _na_bias_tables(na_rpb)
    norm_g3 = norm_g.reshape(DEPTH, 1, D_MODEL)
    kv_norm3 = mla_kv_norm.reshape(DEPTH, 1, MLA_KV_LORA)
    q_norm3 = mla_q_norm.reshape(DEPTH, 1, MLA_Q_LORA)
    final_g = final_norm_g.reshape(1, D_MODEL)

    cond_rows = 16
    cond = jnp.concatenate([c, c_ctx[None, :], jnp.zeros((cond_rows - n_lat - 1, D_MODEL), F32)], axis=0)
    mod = _modulation(cond, w_mod, b_mod).reshape(DEPTH, cond_rows, 1, 3 * D_MODEL)
    ctx_row = lambda token: n_lat
    lat_row = lambda token: token // lat_len

    cache_k = jnp.transpose(cache_na_k, (0, 1, 3, 4, 2)).reshape(n_lat, DEPTH, BRANCH_WIDTH, past)
    cache_v = jnp.transpose(cache_na_v, (0, 1, 3, 4, 2)).reshape(n_lat, DEPTH, BRANCH_WIDTH, past)
    cache_kr = jnp.pad(cache_mla_krope, ((0, 0), (0, 0), (0, 0), (ROPE_LANE0, LANES - ROPE_LANE0 - MLA_ROPE)))

    xp = x_prompt.reshape(n_ctx * ctx_len, D_MODEL)
    xs = x_sample.reshape(n_lat * lat_len, D_MODEL)
    state = None
    for l in range(DEPTH):
        final = l == DEPTH - 1
        p, ckv, kr, *state = _inproj(xp, mod, ctx_row, l, norm_g3, w_packed, kv_norm3, None, state, True, TILE)
        y_na = _na_ctx(p, n_ctx)
        y_mla = _mla(p, ckv, kr, None, None, l, q_norm3, w_uq, w_k, w_v, None, n_ctx, ctx_len)
        xp = _merge(xp, mod, ctx_row, l, p, y_na, y_mla, sgu_wb, sgu_bias, w_br, w_o, final_g, final, MERGE_TILE)
        p, ckv, kr = _inproj(xs, mod, lat_row, l, norm_g3, w_packed, kv_norm3, rope_tabs, None, False, TILE)
        y_na = _na_lat(p, cache_k, cache_v, tab, l, n_lat, lat_len)
        y_mla = _mla(p, ckv, kr, cache_mla_ckv, cache_kr, l, q_norm3, w_uq, w_k, w_v, rope_tabs, n_lat, lat_len)
        xs = _merge(xs, mod, lat_row, l, p, y_na, y_mla, sgu_wb, sgu_bias, w_br, w_o, final_g, final, MERGE_TILE)

    s_k, s_v, s_ckv, s_kr = state

    def heads_last(s):
        return jnp.transpose(s.reshape(n_ctx, DEPTH, NA_HEADS, NA_HEAD_DIM, ctx_len), (0, 1, 4, 2, 3))

    return (xp.reshape(n_ctx, ctx_len, D_MODEL), xs.reshape(n_lat, lat_len, D_MODEL),
            heads_last(s_k), heads_last(s_v), s_ckv, jnp.swapaxes(s_kr, 2, 3))
```

```python
import functools

import numpy as np
import jax
import jax.numpy as jnp
from jax import lax
from jax.experimental import pallas as pl
from jax.experimental.pallas import tpu as pltpu

F32 = jnp.float32
BF16 = jnp.bfloat16

D_MODEL = 1024
DEPTH = 2
GRID_W = 64
BRANCH_WIDTH = 512
N_BRANCH = 3
NA_HEADS = 8
NA_HEAD_DIM = 64
NA_WIN_H = 8
NA_WIN_W = 16
SGU_GROUPS = 4
SGU_CHUNK = 128
MLA_HEADS = 8
MLA_NOPE = 64
MLA_ROPE = 32
MLA_V = 64
MLA_Q_LORA = 384
MLA_KV_LORA = 256
ROPE_THETA = 10000.0
EPS = 1e-6
NEG_INF = -1e30

LANES = 128
TILE = 256
MERGE_TILE = 512
CTX_SEQS_PER_STEP = 4
STRIP = 16
HEAD_PAIRS = NA_HEADS // 2
SCRATCH_SETS = 2
NA_KEY_ROWS = 12
NA_LOC_KEYS = NA_KEY_ROWS * GRID_W
ROWS_PER_TILE = TILE // GRID_W
TAB_PAD = 4
TAB_BLOCKS = 24
ROPE_LANE0 = MLA_NOPE

C_NAZ, C_SU, C_SV, C_SZ, C_MZ, C_MG = (512 * i for i in range(6))
C_NAQ = C_MG + N_BRANCH * D_MODEL
C_NAK, C_NAV, C_DQ = C_NAQ + 512, C_NAQ + 1024, C_NAQ + 1536
MERGE_CHUNKS = C_NAQ // 512
C_DKV = C_DQ + MLA_Q_LORA
C_KR = C_DKV + MLA_KV_LORA
W_COLS = C_KR + LANES
P_COLS = C_DQ + 512
CHUNK = 512

VMEM_LIMIT = 56 * 1024 * 1024


def _dot(a, b):
    return jnp.dot(a, b, preferred_element_type=F32)


def _dot_nt(a, b):
    return lax.dot_general(a, b, (((1,), (1,)), ((), ())), preferred_element_type=F32)


def _rms(x, g):
    return x * lax.rsqrt(jnp.mean(x * x, axis=-1, keepdims=True) + EPS) * g


def _center_norm(x):
    c = x - jnp.mean(x, axis=-1, keepdims=True)
    return c * lax.rsqrt(jnp.mean(c * c, axis=-1, keepdims=True) + EPS)


def _rope128(x, cos, sin):
    lane = lax.broadcasted_iota(jnp.int32, x.shape, 1)
    swapped = jnp.where(lane < ROPE_LANE0 + MLA_ROPE // 2,
                        pltpu.roll(x, LANES - MLA_ROPE // 2, axis=1),
                        pltpu.roll(x, MLA_ROPE // 2, axis=1))
    return x * cos + swapped * sin


def _store_scores(s_ref, m_ref, rows, cols, scores, first):
    s_ref[rows, cols] = scores
    block_max = jnp.broadcast_to(scores.max(axis=-1, keepdims=True), (scores.shape[0], LANES))
    m_ref[rows, :] = block_max if first else jnp.maximum(m_ref[rows, :], block_max)


def _pair_softmax_pv(s_ref, m_ref, p_ref, den_ref, segments, values):
    for i in range(s_ref.shape[0] // STRIP):
        r = slice(i * STRIP, (i + 1) * STRIP)
        m = m_ref[r, :]
        den = None
        for c0, c1 in segments:
            e = jnp.exp(s_ref[r, c0:c1] - jnp.concatenate([m] * ((c1 - c0) // LANES), axis=1))
            d = e.sum(axis=-1, keepdims=True)
            den = d if den is None else den + d
            p_ref[r, c0:c1] = e.astype(BF16)
        den_ref[r, :] = jnp.broadcast_to(den, (STRIP, LANES))
    o = None
    for (c0, c1), (v, transposed) in zip(segments, values):
        t = _dot_nt(p_ref[:, c0:c1], v) if transposed else _dot(p_ref[:, c0:c1], v)
        o = t if o is None else o + t
    o = o / den_ref[...]
    lane = lax.broadcasted_iota(jnp.int32, (TILE, LANES), 1)
    return jnp.where(lane < LANES // 2, o[:TILE], o[TILE:])


def _pair_scratch(n_keys):
    return [pltpu.VMEM((SCRATCH_SETS, 2 * TILE, n_keys), F32), pltpu.VMEM((SCRATCH_SETS, 2 * TILE, LANES), F32),
            pltpu.VMEM((SCRATCH_SETS, 2 * TILE, n_keys), BF16), pltpu.VMEM((SCRATCH_SETS, 2 * TILE, LANES), F32)]


def _const_spec(shape, index):
    return pl.BlockSpec(shape, lambda *_: index, pipeline_mode=pl.Buffered(1))


def _params(semantics):
    return pltpu.CompilerParams(dimension_semantics=semantics, vmem_limit_bytes=VMEM_LIMIT)


def _mod_kernel(c_ref, w_ref, b_ref, o_ref):
    c = c_ref[...]
    s = c * jax.nn.sigmoid(c)
    o_ref[...] = jnp.dot(s, w_ref[...], preferred_element_type=F32, precision=lax.Precision.HIGHEST) + b_ref[...]


def _modulation(cond, w_mod, b_mod):
    rows = cond.shape[0]
    return pl.pallas_call(
        _mod_kernel,
        grid=(DEPTH, 3),
        in_specs=[pl.BlockSpec((rows, D_MODEL), lambda l, j: (0, 0)),
                  pl.BlockSpec((None, D_MODEL, D_MODEL), lambda l, j: (l, 0, j)),
                  pl.BlockSpec((None, 1, D_MODEL), lambda l, j: (l, 0, j))],
        out_specs=pl.BlockSpec((None, rows, D_MODEL), lambda l, j: (l, 0, j)),
        out_shape=jax.ShapeDtypeStruct((DEPTH, rows, 3 * D_MODEL), F32),
        compiler_params=_params(("parallel", "parallel")),
        name="modulation",
    )(cond, w_mod, b_mod.reshape(DEPTH, 1, 3 * D_MODEL))


def _inproj_kernel(*refs, emit_state, rope, n_alias):
    x0_ref, x_next_ref, mod0_ref, mod_next_ref, g_ref, w_ref, kvn_ref = refs[:7]
    refs = refs[7:]
    if rope:
        cos_ref, sin_ref = refs[:2]
        refs = refs[2:]
    refs = refs[n_alias:]
    p_ref, ckv_ref, kr_ref = refs[:3]
    if emit_state:
        sk_ref, sv_ref, sckv_ref, skr_ref = refs[3:7]
    h_ref = refs[-1]
    step = pl.program_id(0)
    tm = x0_ref.shape[0]

    def normalised(x_ref, mod_ref):
        shift = mod_ref[:, 0:D_MODEL]
        scale = mod_ref[:, D_MODEL:2 * D_MODEL]
        return (_rms(x_ref[...], g_ref[...]) * (1.0 + scale) + shift).astype(BF16)

    @pl.when(step == 0)
    def _first_tile():
        h_ref[0] = normalised(x0_ref, mod0_ref)

    h = h_ref[step % 2]

    def tiles(a):
        return a.reshape(tm // TILE, TILE, a.shape[-1])

    ckv = _rms(_dot(h, w_ref[:, C_DKV:C_KR]), kvn_ref[...])
    ckv_ref[...] = ckv.astype(BF16)
    kr = _dot(h, w_ref[:, C_KR:W_COLS])
    if emit_state:
        sckv_ref[...] = ckv
        skr_ref[...] = kr.T[ROPE_LANE0:ROPE_LANE0 + MLA_ROPE, :]
    if rope:
        kr = _rope128(kr, cos_ref[...], sin_ref[...])
    kr_ref[...] = kr.astype(BF16)

    h_ref[(step + 1) % 2] = normalised(x_next_ref, mod_next_ref)

    for c0 in range(0, P_COLS, CHUNK):
        acc = _dot(h, w_ref[:, c0:c0 + CHUNK])
        if emit_state and c0 == C_NAK:
            sk_ref[...] = acc.T
        if emit_state and c0 == C_NAV:
            sv_ref[...] = acc.T
        if c0 in (C_NAZ, C_SZ, C_MZ):
            acc = acc * jax.nn.sigmoid(acc)
        elif c0 == C_SU:
            acc = jax.nn.gelu(acc)
        elif c0 == C_SV:
            acc = _center_norm(jax.nn.gelu(acc))
        elif C_MG <= c0 < C_NAQ:
            acc = jax.nn.sigmoid(acc)
        p_ref[:, c0 // CHUNK] = tiles(acc.astype(BF16))


def _inproj(x, mod, mod_row, layer, norm_g, w_packed, kv_norm, rope_tabs, state_in, emit_state, tm):
    n_tok = x.shape[0]
    n_tiles = n_tok // tm
    rope = rope_tabs is not None
    next_tile = lambda i: jnp.minimum(i + 1, n_tiles - 1)
    in_specs = [_const_spec((tm, D_MODEL), (0, 0)),
                pl.BlockSpec((tm, D_MODEL), lambda i: (next_tile(i), 0)),
                _const_spec((None, None, 1, 3 * D_MODEL), (layer, mod_row(0), 0, 0)),
                pl.BlockSpec((None, None, 1, 3 * D_MODEL), lambda i: (layer, mod_row(next_tile(i) * tm), 0, 0)),
                _const_spec((None, 1, D_MODEL), (layer, 0, 0)),
                _const_spec((None, D_MODEL, W_COLS), (layer, 0, 0)),
                _const_spec((None, 1, MLA_KV_LORA), (layer, 0, 0))]
    args = [x, x, mod, mod, norm_g, w_packed, kv_norm]
    if rope:
        tiles_per_seq = rope_tabs[0].shape[0] // tm
        in_specs += [pl.BlockSpec((tm, LANES), lambda i: (i % tiles_per_seq, 0))] * 2
        args += list(rope_tabs)
    out_specs = [pl.BlockSpec((tm // TILE, P_COLS // CHUNK, TILE, CHUNK), lambda i: (i, 0, 0, 0)),
                 pl.BlockSpec((tm, MLA_KV_LORA), lambda i: (i, 0)),
                 pl.BlockSpec((tm, LANES), lambda i: (i, 0))]
    out_shape = [jax.ShapeDtypeStruct((n_tok // TILE, P_COLS // CHUNK, TILE, CHUNK), BF16),
                 jax.ShapeDtypeStruct((n_tok, MLA_KV_LORA), BF16),
                 jax.ShapeDtypeStruct((n_tok, LANES), BF16)]
    aliases = {}
    if emit_state:
        assert tm == TILE
        for rows, cols in ((BRANCH_WIDTH, TILE), (BRANCH_WIDTH, TILE), (TILE, MLA_KV_LORA), (MLA_ROPE, TILE)):
            out_specs.append(pl.BlockSpec((None, None, rows, cols), lambda i: (i, layer, 0, 0)))
            out_shape.append(jax.ShapeDtypeStruct((n_tok // TILE, DEPTH, rows, cols), F32))
        if state_in is not None:
            first = len(args)
            in_specs += [pl.BlockSpec(memory_space=pl.ANY)] * 4
            args += list(state_in)
            aliases = {first + k: 3 + k for k in range(4)}
    kern = functools.partial(_inproj_kernel, emit_state=emit_state, rope=rope, n_alias=len(aliases))
    return pl.pallas_call(
        kern, grid=(n_tiles,), in_specs=in_specs, out_specs=out_specs, out_shape=out_shape,
        scratch_shapes=[pltpu.VMEM((2, tm, D_MODEL), BF16)],
        input_output_aliases=aliases, compiler_params=_params(("arbitrary",)),
        name="inproj",
    )(*args)


def _na_heads(q_ref, o_ref, scratch, keys_fn, values_fn, bias_fn=None):
    lane = lax.broadcasted_iota(jnp.int32, (TILE, LANES), 1)
    low = lane < NA_HEAD_DIM
    for hp in range(HEAD_PAIRS):
        sl = slice(LANES * hp, LANES * (hp + 1))
        q2 = q_ref[:, sl] * (NA_HEAD_DIM ** -0.5)
        zero = jnp.zeros_like(q2)
        q_stack = jnp.concatenate([jnp.where(low, q2, zero), jnp.where(low, zero, q2)], axis=0)
        s_pair, m_pair, p_pair, den_pair = (ref.at[hp % SCRATCH_SETS] for ref in scratch)
        segments = []
        c0 = 0
        for j, (k, transposed) in enumerate(keys_fn(sl)):
            n_keys = k.shape[1] if transposed else k.shape[0]
            scores = _dot(q_stack, k) if transposed else _dot_nt(q_stack, k)
            if j == 0 and bias_fn is not None:
                scores = scores + bias_fn(2 * hp)
            _store_scores(s_pair, m_pair, slice(None), slice(c0, c0 + n_keys), scores, j == 0)
            segments.append((c0, c0 + n_keys))
            c0 += n_keys
        o_ref[:, sl] = _pair_softmax_pv(s_pair, m_pair, p_pair, den_pair, segments, values_fn(sl)).astype(BF16)


def _na_ctx_kernel(q_ref, k_ref, v_ref, o_ref, *scratch):
    for s in range(q_ref.shape[0]):
        _na_heads(q_ref.at[s], o_ref.at[s], scratch,
                  lambda sl, s=s: [(k_ref[s, :, sl], False)], lambda sl, s=s: [(v_ref[s, :, sl], False)])


def _na_band_row0(tile_row0, n_rows):
    return min(max(tile_row0 - NA_WIN_H // 2, 0), n_rows - NA_KEY_ROWS)


def _na_band_plans(seq):
    n_rows = seq // GRID_W
    plans = []
    for t in range(seq // TILE):
        band0 = _na_band_row0(t * ROWS_PER_TILE, n_rows)
        plan = []
        for r in range(ROWS_PER_TILE):
            rq = t * ROWS_PER_TILE + r
            row_lo = min(max(rq - NA_WIN_H // 2, 0), n_rows - NA_WIN_H)
            plan.append((band0 - rq + NA_WIN_H - 1 + TAB_PAD, row_lo - band0))
        plans.append(tuple(plan))
    return tuple(plans)


def _na_lat_kernel(q_ref, k_ref, v_ref, kc_ref, vc_ref, tab_ref, o_ref, bias_ref, *scratch, plans, n_rows):
    t = pl.program_id(0)
    first_half = lax.broadcasted_iota(jnp.int32, (GRID_W, LANES), 1) < GRID_W
    masked = jnp.full((GRID_W, LANES), NEG_INF, F32)

    def block_pair(h, blk):
        if blk % 2 == 0:
            return tab_ref[h, blk // 2]
        return jnp.where(first_half, pltpu.roll(tab_ref[h, blk // 2], GRID_W, axis=1),
                         pltpu.roll(tab_ref[h, blk // 2 + 1], GRID_W, axis=1))

    for plan in sorted(set(plans)):
        is_tile = functools.reduce(jnp.logical_or, [t == i for i, p in enumerate(plans) if p == plan])

        @pl.when(jnp.logical_and(pl.program_id(1) == 0, is_tile))
        def _build_bias(plan=plan):
            for r, (blk0, j0) in enumerate(plan):
                for j in range(NA_KEY_ROWS // 2):
                    in_window = [j0 <= 2 * j + half < j0 + NA_WIN_H for half in range(2)]
                    for h in range(NA_HEADS):
                        if not any(in_window):
                            piece = masked
                        elif all(in_window):
                            piece = block_pair(h, blk0 + 2 * j)
                        else:
                            keep = first_half if in_window[0] else jnp.logical_not(first_half)
                            piece = jnp.where(keep, block_pair(h, blk0 + 2 * j), NEG_INF)
                        bias_ref[h, r * GRID_W:(r + 1) * GRID_W, LANES * j:LANES * (j + 1)] = piece

    band_tile0 = jnp.clip(t * ROWS_PER_TILE - NA_WIN_H // 2, 0, n_rows - NA_KEY_ROWS) // ROWS_PER_TILE
    for plan in sorted(set(plans)):
        first_row = min(j0 for _, j0 in plan)
        last_row = max(j0 for _, j0 in plan) + NA_WIN_H
        skip = first_row // ROWS_PER_TILE
        n_tiles = -(-last_row // ROWS_PER_TILE) - skip
        key_tiles = pl.ds(band_tile0 + skip, n_tiles)
        bias_lanes = slice(skip * TILE, (skip + n_tiles) * TILE)

        @pl.when(functools.reduce(jnp.logical_or, [t == i for i, p in enumerate(plans) if p == plan]))
        def _attend(key_tiles=key_tiles, bias_lanes=bias_lanes, n_keys=n_tiles * TILE):
            def band(ref, sl):
                return ref[key_tiles, :, sl].reshape(n_keys, LANES)

            def bias(head0):
                return bias_ref[head0:head0 + 2, :, bias_lanes].reshape(2 * TILE, n_keys)

            _na_heads(q_ref, o_ref, scratch,
                      lambda sl: [(band(k_ref, sl), False), (kc_ref[sl, :].astype(BF16), True)],
                      lambda sl: [(band(v_ref, sl), False), (vc_ref[sl, :].astype(BF16), True)], bias)


def _p_block(tile_of, col):
    return pl.BlockSpec((None, None, TILE, CHUNK), lambda *g: (tile_of(*g), col // CHUNK, 0, 0))


def _na_ctx(p, n_seq):
    per_step = CTX_SEQS_PER_STEP
    blk = lambda col: pl.BlockSpec((per_step, None, TILE, CHUNK), lambda i: (i, col // CHUNK, 0, 0))
    y = pl.pallas_call(
        _na_ctx_kernel, grid=(n_seq // per_step,),
        in_specs=[blk(C_NAQ), blk(C_NAK), blk(C_NAV)],
        out_specs=pl.BlockSpec((per_step, TILE, BRANCH_WIDTH), lambda i: (i, 0, 0)),
        out_shape=jax.ShapeDtypeStruct((n_seq, TILE, BRANCH_WIDTH), BF16),
        scratch_shapes=_pair_scratch(TILE),
        compiler_params=_params(("parallel",)), name="na_ctx",
    )(p, p, p)
    return y.reshape(n_seq * TILE, BRANCH_WIDTH)


def _na_lat(p, cache_k, cache_v, tab, layer, n_seq, seq):
    tiles = seq // TILE
    n_past = cache_k.shape[3]
    kv = lambda col: pl.BlockSpec((tiles, None, TILE, CHUNK), lambda t, b: (b, col // CHUNK, 0, 0))
    cache = pl.BlockSpec((None, None, BRANCH_WIDTH, n_past), lambda t, b: (b, layer, 0, 0))
    table = _const_spec((None, NA_HEADS, TAB_BLOCKS // 2, GRID_W, LANES), (layer, 0, 0, 0, 0))
    return pl.pallas_call(
        functools.partial(_na_lat_kernel, plans=_na_band_plans(seq), n_rows=seq // GRID_W),
        grid=(tiles, n_seq),
        in_specs=[_p_block(lambda t, b: b * tiles + t, C_NAQ),
                  kv(C_NAK), kv(C_NAV), cache, cache, table],
        out_specs=pl.BlockSpec((TILE, BRANCH_WIDTH), lambda t, b: (b * tiles + t, 0)),
        out_shape=jax.ShapeDtypeStruct((n_seq * seq, BRANCH_WIDTH), BF16),
        scratch_shapes=[pltpu.VMEM((NA_HEADS, TILE, NA_LOC_KEYS), F32)]
        + _pair_scratch(NA_LOC_KEYS + n_past),
        compiler_params=_params(("parallel", "arbitrary")), name="na_lat",
    )(p, p, p, cache_k, cache_v, tab)


def _na_bias_tables(rpb):
    cq = np.arange(GRID_W)[:, None]
    ck = np.arange(GRID_W)[None, :]
    col_lo = np.clip(cq - NA_WIN_W // 2, 0, GRID_W - NA_WIN_W)
    ok = (ck >= col_lo) & (ck < col_lo + NA_WIN_W)
    n_b = 2 * NA_WIN_W - 1
    n_rel = 2 * NA_WIN_H - 1
    onehot = ((ck - cq + NA_WIN_W - 1)[:, None, :] == np.arange(n_b)[None, :, None]) & ok[:, None, :]
    expand = np.zeros((GRID_W, 2, n_b, 2, GRID_W), np.float32)
    expand[:, 0, :, 0, :] = onehot
    expand[:, 1, :, 1, :] = onehot
    expand = expand.reshape(GRID_W, 2 * n_b, LANES)
    rows = jnp.pad(rpb, ((0, 0), (0, 0), (TAB_PAD, TAB_BLOCKS - n_rel - TAB_PAD), (0, 0)))
    real = np.zeros(TAB_BLOCKS, bool)
    real[TAB_PAD:TAB_PAD + n_rel] = True
    pairs = rows.reshape(DEPTH, NA_HEADS, TAB_BLOCKS // 2, 2 * n_b)
    tab = jnp.einsum('lhek,ukc->lheuc', pairs, expand, precision=lax.Precision.HIGHEST)
    keep = real.reshape(-1, 1, 2, 1) & ok[None, :, None, :]
    return jnp.where(keep.reshape(TAB_BLOCKS // 2, GRID_W, LANES), tab, NEG_INF)


def _mla_kernel(*refs, n_cache, n_lat, rope, seqs, tiled):
    dq_ref, ckv_ref, kr_ref = refs[:3]
    refs = refs[3:]
    if n_cache:
        cckv_ref, ckr_ref = refs[:2]
        refs = refs[2:]
    qn_ref, wuq_ref, wk_ref, wv_ref = refs[:4]
    refs = refs[4:]
    if rope:
        cos_ref, sin_ref = refs[:2]
        refs = refs[2:]
    o_ref, kx_all, vx_all = refs[:3]
    scratch = refs[3:]
    scale = (MLA_NOPE + MLA_ROPE) ** -0.5
    n_keys = kx_all.shape[1]

    for s in range(seqs):
        kx_ref, vx_ref = kx_all.at[s], vx_all.at[s]

        def expand_keys(s=s, kx_ref=kx_ref, vx_ref=vx_ref):
            def fill(r0, ckv, kr):
                n = ckv.shape[0]
                kk = _dot(ckv, wk_ref[...])
                for h in range(MLA_HEADS):
                    sl = slice(LANES * h, LANES * (h + 1))
                    kx_ref[r0:r0 + n, sl] = (kk[:, sl] + kr).astype(BF16)
                vx_ref[r0:r0 + n, :] = _dot(ckv, wv_ref[...]).astype(BF16)

            if n_cache:
                fill(0, cckv_ref[...].astype(BF16), ckr_ref[...])
            for r0 in range(s * n_lat, (s + 1) * n_lat, TILE):
                fill(n_cache + r0 - s * n_lat, ckv_ref[r0:r0 + TILE, :], kr_ref[r0:r0 + TILE, :].astype(F32))

        if tiled:
            pl.when(pl.program_id(1) == 0)(expand_keys)
        else:
            expand_keys()

        dqn = _rms(dq_ref[s, :, :MLA_Q_LORA].astype(F32), qn_ref[...])
        q = _dot(dqn.astype(BF16), wuq_ref[...])
        for hp in range(HEAD_PAIRS):
            s_pair, m_pair, p_pair, den_pair = (ref.at[hp % SCRATCH_SETS] for ref in scratch)
            for half in range(2):
                sl = slice(LANES * (2 * hp + half), LANES * (2 * hp + half + 1))
                qh = q[:, sl]
                if rope:
                    qh = _rope128(qh, cos_ref[...], sin_ref[...])
                qh = (qh * scale).astype(BF16)
                _store_scores(s_pair, m_pair, slice(half * TILE, (half + 1) * TILE), slice(None),
                              _dot_nt(qh, kx_ref[:, sl]), True)
            v2 = vx_ref[:, LANES * hp:LANES * (hp + 1)]
            o_ref[s * TILE:(s + 1) * TILE, LANES * hp:LANES * (hp + 1)] = _pair_softmax_pv(
                s_pair, m_pair, p_pair, den_pair, [(0, n_keys)], [(v2, False)]).astype(BF16)


def _mla(p, ckv, kr, cache_ckv, cache_kr, layer, q_norm, w_uq, w_k, w_v, rope_tabs, n_seq, seq):
    tiles = seq // TILE
    seqs = CTX_SEQS_PER_STEP if tiles == 1 else 1
    n_cache = 0 if cache_ckv is None else cache_ckv.shape[2]
    rope = rope_tabs is not None
    in_specs = [pl.BlockSpec((seqs, None, TILE, CHUNK), lambda b, t: (b * tiles + t, C_DQ // CHUNK, 0, 0)),
                pl.BlockSpec((seqs * seq, MLA_KV_LORA), lambda b, t: (b, 0)),
                pl.BlockSpec((seqs * seq, LANES), lambda b, t: (b, 0))]
    args = [p, ckv, kr]
    if n_cache:
        in_specs += [pl.BlockSpec((None, None, n_cache, MLA_KV_LORA), lambda b, t: (b, layer, 0, 0)),
                     pl.BlockSpec((None, None, n_cache, LANES), lambda b, t: (b, layer, 0, 0))]
        args += [cache_ckv, cache_kr]
    in_specs += [_const_spec((None, 1, MLA_Q_LORA), (layer, 0, 0)),
                 _const_spec((None, MLA_Q_LORA, MLA_HEADS * LANES), (layer, 0, 0)),
                 _const_spec((None, MLA_KV_LORA, MLA_HEADS * LANES), (layer, 0, 0)),
                 _const_spec((None, MLA_KV_LORA, MLA_HEADS * MLA_V), (layer, 0, 0))]
    args += [q_norm, w_uq, w_k, w_v]
    if rope:
        in_specs += [pl.BlockSpec((TILE, LANES), lambda b, t: (t, 0))] * 2
        args += list(rope_tabs)
    n_keys = n_cache + seq
    return pl.pallas_call(
        functools.partial(_mla_kernel, n_cache=n_cache, n_lat=seq, rope=rope, seqs=seqs, tiled=tiles > 1),
        grid=(n_seq // seqs, tiles), in_specs=in_specs,
        out_specs=pl.BlockSpec((seqs * TILE, BRANCH_WIDTH), lambda b, t: (b * tiles + t, 0)),
        out_shape=jax.ShapeDtypeStruct((n_seq * seq, BRANCH_WIDTH), BF16),
        scratch_shapes=[pltpu.VMEM((seqs, n_keys, MLA_HEADS * LANES), BF16),
                        pltpu.VMEM((seqs, n_keys, MLA_HEADS * MLA_V), BF16)] + _pair_scratch(n_keys),
        compiler_params=_params(("parallel", "arbitrary")), name="mla",
    )(*args)


def _merge_kernel(x_ref, mod_ref, yna_ref, ymla_ref, pm_ref, sw_ref, sb_ref, wb_ref, wo_ref, fg_ref, o_ref, *, final):
    tm = x_ref.shape[0]

    def tokens(col):
        return pm_ref[:, col // CHUNK].reshape(tm, CHUNK)

    sv = tokens(C_SV)
    rows = []
    for c0 in range(0, tm, SGU_CHUNK):
        cols = [_dot(sw_ref[g], sv[c0:c0 + SGU_CHUNK, LANES * g:LANES * (g + 1)]) for g in range(SGU_GROUPS)]
        rows.append(jnp.concatenate(cols, axis=1) + sb_ref[...])
    y_sgu = tokens(C_SU).astype(F32) * jnp.concatenate(rows, axis=0)

    gated = (yna_ref[...] * tokens(C_NAZ),
             (y_sgu * tokens(C_SZ).astype(F32)).astype(BF16),
             ymla_ref[...] * tokens(C_MZ))
    merged = None
    for k in range(N_BRANCH):
        gate_k = jnp.concatenate([tokens(C_MG + k * D_MODEL + c) for c in range(0, D_MODEL, CHUNK)], axis=1)
        term = gate_k.astype(F32) * _dot(gated[k], wb_ref[k])
        merged = term if merged is None else merged + term
    out = _dot(merged.astype(BF16), wo_ref[...])
    gate = mod_ref[:, 2 * D_MODEL:3 * D_MODEL]
    xn = x_ref[...] + gate * out
    if final:
        xn = _rms(xn, fg_ref[...])
    o_ref[...] = xn


def _merge(x, mod, mod_row, layer, p, y_na, y_mla, sgu_w, sgu_b, w_branch, w_out, final_g, final, tm):
    n_tok = x.shape[0]
    half = pl.BlockSpec((tm, BRANCH_WIDTH), lambda i: (i, 0))
    full = lambda c: pl.BlockSpec((tm, D_MODEL), lambda i: (i, c))
    in_specs = [full(0),
                pl.BlockSpec((None, None, 1, 3 * D_MODEL), lambda i: (layer, mod_row(i * tm), 0, 0)),
                half, half,
                pl.BlockSpec((tm // TILE, MERGE_CHUNKS, TILE, CHUNK), lambda i: (i, 0, 0, 0)),
                _const_spec((None, SGU_GROUPS, SGU_CHUNK, SGU_CHUNK), (layer, 0, 0, 0)),
                _const_spec((None, SGU_CHUNK, BRANCH_WIDTH), (layer, 0, 0)),
                _const_spec((None, N_BRANCH, BRANCH_WIDTH, D_MODEL), (layer, 0, 0, 0)),
                _const_spec((None, D_MODEL, D_MODEL), (layer, 0, 0)),
                _const_spec((1, D_MODEL), (0, 0))]
    return pl.pallas_call(
        functools.partial(_merge_kernel, final=final),
        grid=(n_tok // tm,), in_specs=in_specs,
        out_specs=full(0), out_shape=jax.ShapeDtypeStruct((n_tok, D_MODEL), F32),
        compiler_params=_params(("parallel",)), name="merge",
    )(x, mod, y_na, y_mla, p, sgu_w, sgu_b, w_branch, w_out, final_g)


SRC_DQ = 7 * BRANCH_WIDTH
SRC_DKV = SRC_DQ + MLA_Q_LORA
SRC_KR = SRC_DKV + MLA_KV_LORA
SRC_MZ = SRC_KR + MLA_ROPE
SRC_NAZ = 3 * BRANCH_WIDTH
PACK_SHIFT = SRC_MZ % CHUNK
J_SHIFTED = C_MZ // CHUNK
J_Q = C_NAQ // CHUNK
J_DQ = C_DQ // CHUNK
assert SRC_NAZ % CHUNK == 0 and SRC_DQ % CHUNK == 0
assert SRC_DQ + CHUNK + (C_KR - P_COLS) == SRC_KR and SRC_KR + MLA_ROPE == SRC_MZ


def _pack_w_in_kernel(a_ref, b_ref, o_ref):
    j = pl.program_id(1)

    def emit(rows):
        o_ref[...] = rows.T.astype(BF16)

    @pl.when(jnp.logical_or(j < J_SHIFTED, jnp.logical_and(j >= J_Q, j <= J_DQ)))
    def _aligned():
        emit(a_ref[...])

    @pl.when(jnp.logical_and(j >= J_SHIFTED, j < J_Q))
    def _shifted():
        emit(jnp.concatenate([a_ref[PACK_SHIFT:, :], b_ref[:PACK_SHIFT, :]], axis=0))

    @pl.when(j == J_DQ + 1)
    def _tail():
        n_dkv = C_KR - P_COLS
        zeros = lambda n: jnp.zeros((n, D_MODEL), F32)
        emit(jnp.concatenate([a_ref[:n_dkv, :], zeros(ROPE_LANE0), a_ref[n_dkv:n_dkv + MLA_ROPE, :],
                              zeros(LANES - ROPE_LANE0 - MLA_ROPE), zeros(CHUNK - n_dkv - LANES)], axis=0))


def _pack_w_in(w_in_t):
    n_chunks = pl.cdiv(W_COLS, CHUNK)
    shifted0 = (SRC_MZ - PACK_SHIFT) // CHUNK

    def a_block(j):
        return jnp.where(j < J_SHIFTED, j + SRC_NAZ // CHUNK,
                         jnp.where(j < J_Q, j - J_SHIFTED + shifted0,
                                   jnp.where(j < J_DQ, j - J_Q,
                                             jnp.where(j == J_DQ, SRC_DQ // CHUNK, SRC_DQ // CHUNK + 1))))

    def b_block(j):
        shifted = jnp.logical_and(j >= J_SHIFTED, j < J_Q)
        return jnp.where(shifted, j - J_SHIFTED + shifted0 + 1, shifted0 + 1)

    return pl.pallas_call(
        _pack_w_in_kernel, grid=(DEPTH, n_chunks),
        in_specs=[pl.BlockSpec((None, CHUNK, D_MODEL), lambda l, j: (l, a_block(j), 0)),
                  pl.BlockSpec((None, CHUNK, D_MODEL), lambda l, j: (l, b_block(j), 0))],
        out_specs=pl.BlockSpec((None, D_MODEL, CHUNK), lambda l, j: (l, 0, j)),
        out_shape=jax.ShapeDtypeStruct((DEPTH, D_MODEL, W_COLS), BF16),
        compiler_params=_params(("parallel", "arbitrary")), name="pack_w_in",
    )(w_in_t, w_in_t)


def _pack_params(w_in, sgu_w, sgu_b, mla_w_uq, mla_w_ukv, w_branch, w_out):
    w_packed = _pack_w_in(jnp.swapaxes(w_in, 1, 2))
    uq = mla_w_uq.reshape(DEPTH, MLA_Q_LORA, MLA_HEADS, MLA_NOPE + MLA_ROPE)
    uq = jnp.pad(uq, ((0, 0), (0, 0), (0, 0), (0, LANES - MLA_NOPE - MLA_ROPE)))
    uq = uq.reshape(DEPTH, MLA_Q_LORA, MLA_HEADS * LANES).astype(BF16)
    ukv = mla_w_ukv.reshape(DEPTH, MLA_KV_LORA, MLA_HEADS, MLA_NOPE + MLA_V)
    w_k = jnp.pad(ukv[..., :MLA_NOPE], ((0, 0), (0, 0), (0, 0), (0, LANES - MLA_NOPE)))
    w_k = w_k.reshape(DEPTH, MLA_KV_LORA, MLA_HEADS * LANES).astype(BF16)
    w_v = ukv[..., MLA_NOPE:].reshape(DEPTH, MLA_KV_LORA, MLA_HEADS * MLA_V).astype(BF16)
    sgu_bias = jnp.repeat(jnp.swapaxes(sgu_b, 1, 2), BRANCH_WIDTH // SGU_GROUPS, axis=2)
    return w_packed, uq, w_k, w_v, sgu_w.astype(BF16), sgu_bias, w_branch.astype(BF16), w_out.astype(BF16)


def _rope_tables(n_tokens):
    pos = jnp.arange(n_tokens, dtype=jnp.int32)
    row = (pos // GRID_W).astype(F32)
    col = (pos % GRID_W).astype(F32)
    n_freq = MLA_ROPE // 4
    inv = ROPE_THETA ** (-jnp.arange(n_freq, dtype=F32) / n_freq)
    ang = jnp.concatenate([row[:, None] * inv, col[:, None] * inv], axis=-1)
    cos, sin = jnp.cos(ang), jnp.sin(ang)
    pad_l, pad_r = ROPE_LANE0, LANES - ROPE_LANE0 - MLA_ROPE
    cos_t = jnp.pad(jnp.concatenate([cos, cos], axis=1), ((0, 0), (pad_l, pad_r)), constant_values=1.0)
    sin_t = jnp.pad(jnp.concatenate([-sin, sin], axis=1), ((0, 0), (pad_l, pad_r)))
    return cos_t, sin_t


def kernel(x_prompt, x_sample, cache_na_k, cache_na_v, cache_mla_ckv, cache_mla_krope, c, c_ctx, norm_g, w_mod, b_mod, w_in, na_rpb, sgu_w, sgu_b, mla_q_norm, mla_w_uq, mla_kv_norm, mla_w_ukv, w_branch, w_out, final_norm_g):
    n_ctx, ctx_len, _ = x_prompt.shape
    n_lat, lat_len, _ = x_sample.shape
    past = cache_na_k.shape[2]
    assert ctx_len == TILE and lat_len % TILE == 0 and lat_len // GRID_W == 16 and past == TILE

    w_packed, w_uq, w_k, w_v, sgu_wb, sgu_bias, w_br, w_o = _pack_params(
        w_in, sgu_w, sgu_b, mla_w_uq, mla_w_ukv, w_branch, w_out)
    rope_tabs = _rope_tables(lat_len)
    tab = _na_bias_tables(na_rpb)
    norm_g3 = norm_g.reshape(DEPTH, 1, D_MODEL)
    kv_norm3 = mla_kv_norm.reshape(DEPTH, 1, MLA_KV_LORA)
    q_norm3 = mla_q_norm.reshape(DEPTH, 1, MLA_Q_LORA)
    final_g = final_norm_g.reshape(1, D_MODEL)

    cond_rows = 16
    cond = jnp.concatenate([c, c_ctx[None, :], jnp.zeros((cond_rows - n_lat - 1, D_MODEL), F32)], axis=0)
    mod = _modulation(cond, w_mod, b_mod).reshape(DEPTH, cond_rows, 1, 3 * D_MODEL)
    ctx_row = lambda token: n_lat
    lat_row = lambda token: token // lat_len

    cache_k = jnp.transpose(cache_na_k, (0, 1, 3, 4, 2)).reshape(n_lat, DEPTH, BRANCH_WIDTH, past)
    cache_v = jnp.transpose(cache_na_v, (0, 1, 3, 4, 2)).reshape(n_lat, DEPTH, BRANCH_WIDTH, past)
    cache_kr = jnp.pad(cache_mla_krope, ((0, 0), (0, 0), (0, 0), (ROPE_LANE0, LANES - ROPE_LANE0 - MLA_ROPE)))

    xp = x_prompt.reshape(n_ctx * ctx_len, D_MODEL)
    xs = x_sample.reshape(n_lat * lat_len, D_MODEL)
    state = None
    for l in range(DEPTH):
        final = l == DEPTH - 1
        p, ckv, kr, *state = _inproj(xp, mod, ctx_row, l, norm_g3, w_packed, kv_norm3, None, state, True, TILE)
        y_na = _na_ctx(p, n_ctx)
        y_mla = _mla(p, ckv, kr, None, None, l, q_norm3, w_uq, w_k, w_v, None, n_ctx, ctx_len)
        xp = _merge(xp, mod, ctx_row, l, p, y_na, y_mla, sgu_wb, sgu_bias, w_br, w_o, final_g, final, MERGE_TILE)
        p, ckv, kr = _inproj(xs, mod, lat_row, l, norm_g3, w_packed, kv_norm3, rope_tabs, None, False, TILE)
        y_na = _na_lat(p, cache_k, cache_v, tab, l, n_lat, lat_len)
        y_mla = _mla(p, ckv, kr, cache_mla_ckv, cache_kr, l, q_norm3, w_uq, w_k, w_v, rope_tabs, n_lat, lat_len)
        xs = _merge(xs, mod, lat_row, l, p, y_na, y_mla, sgu_wb, sgu_bias, w_br, w_o, final_g, final, MERGE_TILE)

    s_k, s_v, s_ckv, s_kr = state

    def heads_last(s):
        return jnp.transpose(s.reshape(n_ctx, DEPTH, NA_HEADS, NA_HEAD_DIM, ctx_len), (0, 1, 4, 2, 3))

    return (xp.reshape(n_ctx, ctx_len, D_MODEL), xs.reshape(n_lat, lat_len, D_MODEL),
            heads_last(s_k), heads_last(s_v), s_ckv, jnp.swapaxes(s_kr, 2, 3))
```

```python
import functools

import numpy as np
import jax
import jax.numpy as jnp
from jax import lax
from jax.experimental import pallas as pl
from jax.experimental.pallas import tpu as pltpu

F32 = jnp.float32
BF16 = jnp.bfloat16

D_MODEL = 1024
DEPTH = 2
GRID_W = 64
BRANCH_WIDTH = 512
N_BRANCH = 3
NA_HEADS = 8
NA_HEAD_DIM = 64
NA_WIN_H = 8
NA_WIN_W = 16
SGU_GROUPS = 4
SGU_CHUNK = 128
MLA_HEADS = 8
MLA_NOPE = 64
MLA_ROPE = 32
MLA_V = 64
MLA_Q_LORA = 384
MLA_KV_LORA = 256
ROPE_THETA = 10000.0
EPS = 1e-6
NEG_INF = -1e30

LANES = 128
TILE = 256
MERGE_TILE = 512
CTX_SEQS_PER_STEP = 4
LAT_TILES_PER_STEP = 4
LAT_SEQS_PER_STEP = 2
STRIP = 16
HEAD_PAIRS = NA_HEADS // 2
SCRATCH_SETS = 2
NA_KEY_ROWS = 12
NA_LOC_KEYS = NA_KEY_ROWS * GRID_W
ROWS_PER_TILE = TILE // GRID_W
TAB_PAD = 4
TAB_BLOCKS = 24
ROPE_LANE0 = MLA_NOPE

C_NAZ, C_SU, C_SV, C_SZ, C_MZ, C_MG = (512 * i for i in range(6))
C_NAQ = C_MG + N_BRANCH * D_MODEL
C_NAK, C_NAV, C_DQ = C_NAQ + 512, C_NAQ + 1024, C_NAQ + 1536
MERGE_CHUNKS = C_NAQ // 512
C_DKV = C_DQ + MLA_Q_LORA
C_KR = C_DKV + MLA_KV_LORA
W_COLS = C_KR + LANES
P_COLS = C_DQ + 512
CHUNK = 512

VMEM_LIMIT = 56 * 1024 * 1024


def _dot(a, b):
    return jnp.dot(a, b, preferred_element_type=F32)


def _dot_nt(a, b):
    return lax.dot_general(a, b, (((1,), (1,)), ((), ())), preferred_element_type=F32)


def _rms(x, g):
    return x * lax.rsqrt(jnp.mean(x * x, axis=-1, keepdims=True) + EPS) * g


def _center_norm(x):
    c = x - jnp.mean(x, axis=-1, keepdims=True)
    return c * lax.rsqrt(jnp.mean(c * c, axis=-1, keepdims=True) + EPS)


def _rope128(x, cos, sin):
    lane = lax.broadcasted_iota(jnp.int32, x.shape, 1)
    swapped = jnp.where(lane < ROPE_LANE0 + MLA_ROPE // 2,
                        pltpu.roll(x, LANES - MLA_ROPE // 2, axis=1),
                        pltpu.roll(x, MLA_ROPE // 2, axis=1))
    return x * cos + swapped * sin


def _store_scores(s_ref, m_ref, rows, cols, scores, first):
    s_ref[rows, cols] = scores
    block_max = jnp.broadcast_to(scores.max(axis=-1, keepdims=True), (scores.shape[0], LANES))
    m_ref[rows, :] = block_max if first else jnp.maximum(m_ref[rows, :], block_max)


def _pair_softmax_pv(s_ref, m_ref, p_ref, den_ref, segments, values):
    for i in range(s_ref.shape[0] // STRIP):
        r = slice(i * STRIP, (i + 1) * STRIP)
        m = m_ref[r, :]
        den = None
        for c0, c1 in segments:
            e = jnp.exp(s_ref[r, c0:c1] - jnp.concatenate([m] * ((c1 - c0) // LANES), axis=1))
            d = e.sum(axis=-1, keepdims=True)
            den = d if den is None else den + d
            p_ref[r, c0:c1] = e.astype(BF16)
        den_ref[r, :] = jnp.broadcast_to(den, (STRIP, LANES))
    o = None
    for (c0, c1), (v, transposed) in zip(segments, values):
        t = _dot_nt(p_ref[:, c0:c1], v) if transposed else _dot(p_ref[:, c0:c1], v)
        o = t if o is None else o + t
    o = o / den_ref[...]
    lane = lax.broadcasted_iota(jnp.int32, (TILE, LANES), 1)
    return jnp.where(lane < LANES // 2, o[:TILE], o[TILE:])


def _pair_scratch(n_keys):
    return [pltpu.VMEM((SCRATCH_SETS, 2 * TILE, n_keys), F32), pltpu.VMEM((SCRATCH_SETS, 2 * TILE, LANES), F32),
            pltpu.VMEM((SCRATCH_SETS, 2 * TILE, n_keys), BF16), pltpu.VMEM((SCRATCH_SETS, 2 * TILE, LANES), F32)]


def _const_spec(shape, index):
    return pl.BlockSpec(shape, lambda *_: index, pipeline_mode=pl.Buffered(1))


def _params(semantics):
    return pltpu.CompilerParams(dimension_semantics=semantics, vmem_limit_bytes=VMEM_LIMIT)


def _mod_kernel(c_ref, w_ref, b_ref, o_ref):
    c = c_ref[...]
    s = c * jax.nn.sigmoid(c)
    o_ref[...] = jnp.dot(s, w_ref[...], preferred_element_type=F32, precision=lax.Precision.HIGHEST) + b_ref[...]


def _modulation(cond, w_mod, b_mod):
    rows = cond.shape[0]
    return pl.pallas_call(
        _mod_kernel,
        grid=(DEPTH, 3),
        in_specs=[pl.BlockSpec((rows, D_MODEL), lambda l, j: (0, 0)),
                  pl.BlockSpec((None, D_MODEL, D_MODEL), lambda l, j: (l, 0, j)),
                  pl.BlockSpec((None, 1, D_MODEL), lambda l, j: (l, 0, j))],
        out_specs=pl.BlockSpec((None, rows, D_MODEL), lambda l, j: (l, 0, j)),
        out_shape=jax.ShapeDtypeStruct((DEPTH, rows, 3 * D_MODEL), F32),
        compiler_params=_params(("parallel", "parallel")),
        name="modulation",
    )(cond, w_mod, b_mod.reshape(DEPTH, 1, 3 * D_MODEL))


def _inproj_kernel(*refs, emit_state, rope, n_alias):
    x0_ref, x_next_ref, mod0_ref, mod_next_ref, g_ref, w_ref, kvn_ref = refs[:7]
    refs = refs[7:]
    if rope:
        cos_ref, sin_ref = refs[:2]
        refs = refs[2:]
    refs = refs[n_alias:]
    p_ref, ckv_ref, kr_ref = refs[:3]
    if emit_state:
        sk_ref, sv_ref, sckv_ref, skr_ref = refs[3:7]
    h_ref = refs[-1]
    step = pl.program_id(0)
    tm = x0_ref.shape[0]

    def normalised(x_ref, mod_ref):
        shift = mod_ref[:, 0:D_MODEL]
        scale = mod_ref[:, D_MODEL:2 * D_MODEL]
        return (_rms(x_ref[...], g_ref[...]) * (1.0 + scale) + shift).astype(BF16)

    @pl.when(step == 0)
    def _first_tile():
        h_ref[0] = normalised(x0_ref, mod0_ref)

    h = h_ref[step % 2]

    def tiles(a):
        return a.reshape(tm // TILE, TILE, a.shape[-1])

    ckv = _rms(_dot(h, w_ref[:, C_DKV:C_KR]), kvn_ref[...])
    ckv_ref[...] = ckv.astype(BF16)
    kr = _dot(h, w_ref[:, C_KR:W_COLS])
    if emit_state:
        sckv_ref[...] = ckv
        skr_ref[...] = kr.T[ROPE_LANE0:ROPE_LANE0 + MLA_ROPE, :]
    if rope:
        kr = _rope128(kr, cos_ref[...], sin_ref[...])
    kr_ref[...] = kr.astype(BF16)

    h_ref[(step + 1) % 2] = normalised(x_next_ref, mod_next_ref)

    for c0 in range(0, P_COLS, CHUNK):
        acc = _dot(h, w_ref[:, c0:c0 + CHUNK])
        if emit_state and c0 == C_NAK:
            sk_ref[...] = acc.T
        if emit_state and c0 == C_NAV:
            sv_ref[...] = acc.T
        if c0 in (C_NAZ, C_SZ, C_MZ):
            acc = acc * jax.nn.sigmoid(acc)
        elif c0 == C_SU:
            acc = jax.nn.gelu(acc)
        elif c0 == C_SV:
            acc = _center_norm(jax.nn.gelu(acc))
        elif C_MG <= c0 < C_NAQ:
            acc = jax.nn.sigmoid(acc)
        p_ref[:, c0 // CHUNK] = tiles(acc.astype(BF16))


def _inproj(x, mod, mod_row, layer, norm_g, w_packed, kv_norm, rope_tabs, state_in, emit_state, tm):
    n_tok = x.shape[0]
    n_tiles = n_tok // tm
    rope = rope_tabs is not None
    next_tile = lambda i: jnp.minimum(i + 1, n_tiles - 1)
    in_specs = [_const_spec((tm, D_MODEL), (0, 0)),
                pl.BlockSpec((tm, D_MODEL), lambda i: (next_tile(i), 0)),
                _const_spec((None, None, 1, 3 * D_MODEL), (layer, mod_row(0), 0, 0)),
                pl.BlockSpec((None, None, 1, 3 * D_MODEL), lambda i: (layer, mod_row(next_tile(i) * tm), 0, 0)),
                _const_spec((None, 1, D_MODEL), (layer, 0, 0)),
                _const_spec((None, D_MODEL, W_COLS), (layer, 0, 0)),
                _const_spec((None, 1, MLA_KV_LORA), (layer, 0, 0))]
    args = [x, x, mod, mod, norm_g, w_packed, kv_norm]
    if rope:
        tiles_per_seq = rope_tabs[0].shape[0] // tm
        in_specs += [pl.BlockSpec((tm, LANES), lambda i: (i % tiles_per_seq, 0))] * 2
        args += list(rope_tabs)
    out_specs = [pl.BlockSpec((tm // TILE, P_COLS // CHUNK, TILE, CHUNK), lambda i: (i, 0, 0, 0)),
                 pl.BlockSpec((tm, MLA_KV_LORA), lambda i: (i, 0)),
                 pl.BlockSpec((tm, LANES), lambda i: (i, 0))]
    out_shape = [jax.ShapeDtypeStruct((n_tok // TILE, P_COLS // CHUNK, TILE, CHUNK), BF16),
                 jax.ShapeDtypeStruct((n_tok, MLA_KV_LORA), BF16),
                 jax.ShapeDtypeStruct((n_tok, LANES), BF16)]
    aliases = {}
    if emit_state:
        assert tm == TILE
        for rows, cols in ((BRANCH_WIDTH, TILE), (BRANCH_WIDTH, TILE), (TILE, MLA_KV_LORA), (MLA_ROPE, TILE)):
            out_specs.append(pl.BlockSpec((None, None, rows, cols), lambda i: (i, layer, 0, 0)))
            out_shape.append(jax.ShapeDtypeStruct((n_tok // TILE, DEPTH, rows, cols), F32))
        if state_in is not None:
            first = len(args)
            in_specs += [pl.BlockSpec(memory_space=pl.ANY)] * 4
            args += list(state_in)
            aliases = {first + k: 3 + k for k in range(4)}
    kern = functools.partial(_inproj_kernel, emit_state=emit_state, rope=rope, n_alias=len(aliases))
    return pl.pallas_call(
        kern, grid=(n_tiles,), in_specs=in_specs, out_specs=out_specs, out_shape=out_shape,
        scratch_shapes=[pltpu.VMEM((2, tm, D_MODEL), BF16)],
        input_output_aliases=aliases, compiler_params=_params(("arbitrary",)),
        name="inproj",
    )(*args)


def _na_heads(q_ref, o_ref, scratch, keys_fn, values_fn, bias_fn=None):
    lane = lax.broadcasted_iota(jnp.int32, (TILE, LANES), 1)
    low = lane < NA_HEAD_DIM
    for hp in range(HEAD_PAIRS):
        sl = slice(LANES * hp, LANES * (hp + 1))
        q2 = q_ref[:, sl] * (NA_HEAD_DIM ** -0.5)
        zero = jnp.zeros_like(q2)
        q_stack = jnp.concatenate([jnp.where(low, q2, zero), jnp.where(low, zero, q2)], axis=0)
        s_pair, m_pair, p_pair, den_pair = (ref.at[hp % SCRATCH_SETS] for ref in scratch)
        segments = []
        c0 = 0
        for j, (k, transposed) in enumerate(keys_fn(sl)):
            n_keys = k.shape[1] if transposed else k.shape[0]
            scores = _dot(q_stack, k) if transposed else _dot_nt(q_stack, k)
            if j == 0 and bias_fn is not None:
                scores = scores + bias_fn(2 * hp)
            _store_scores(s_pair, m_pair, slice(None), slice(c0, c0 + n_keys), scores, j == 0)
            segments.append((c0, c0 + n_keys))
            c0 += n_keys
        o_ref[:, sl] = _pair_softmax_pv(s_pair, m_pair, p_pair, den_pair, segments, values_fn(sl)).astype(BF16)


def _na_ctx_kernel(q_ref, k_ref, v_ref, o_ref, *scratch):
    for s in range(q_ref.shape[0]):
        _na_heads(q_ref.at[s], o_ref.at[s], scratch,
                  lambda sl, s=s: [(k_ref[s, :, sl], False)], lambda sl, s=s: [(v_ref[s, :, sl], False)])


def _na_band_row0(tile_row0, n_rows):
    return min(max(tile_row0 - NA_WIN_H // 2, 0), n_rows - NA_KEY_ROWS)


def _na_band_plans(seq):
    n_rows = seq // GRID_W
    plans = []
    for t in range(seq // TILE):
        band0 = _na_band_row0(t * ROWS_PER_TILE, n_rows)
        plan = []
        for r in range(ROWS_PER_TILE):
            rq = t * ROWS_PER_TILE + r
            row_lo = min(max(rq - NA_WIN_H // 2, 0), n_rows - NA_WIN_H)
            plan.append((band0 - rq + NA_WIN_H - 1 + TAB_PAD, row_lo - band0))
        plans.append(tuple(plan))
    return tuple(plans)


def _na_lat_kernel(q_ref, k_ref, v_ref, kc_ref, vc_ref, tab_ref, o_ref, bias_ref, *scratch, plans, n_rows):
    t = pl.program_id(0)
    first_half = lax.broadcasted_iota(jnp.int32, (GRID_W, LANES), 1) < GRID_W
    masked = jnp.full((GRID_W, LANES), NEG_INF, F32)

    def block_pair(h, blk):
        if blk % 2 == 0:
            return tab_ref[h, blk // 2]
        return jnp.where(first_half, pltpu.roll(tab_ref[h, blk // 2], GRID_W, axis=1),
                         pltpu.roll(tab_ref[h, blk // 2 + 1], GRID_W, axis=1))

    for plan in sorted(set(plans)):
        is_tile = functools.reduce(jnp.logical_or, [t == i for i, p in enumerate(plans) if p == plan])

        @pl.when(jnp.logical_and(pl.program_id(1) == 0, is_tile))
        def _build_bias(plan=plan):
            for r, (blk0, j0) in enumerate(plan):
                for j in range(NA_KEY_ROWS // 2):
                    in_window = [j0 <= 2 * j + half < j0 + NA_WIN_H for half in range(2)]
                    for h in range(NA_HEADS):
                        if not any(in_window):
                            piece = masked
                        elif all(in_window):
                            piece = block_pair(h, blk0 + 2 * j)
                        else:
                            keep = first_half if in_window[0] else jnp.logical_not(first_half)
                            piece = jnp.where(keep, block_pair(h, blk0 + 2 * j), NEG_INF)
                        bias_ref[h, r * GRID_W:(r + 1) * GRID_W, LANES * j:LANES * (j + 1)] = piece

    band_tile0 = jnp.clip(t * ROWS_PER_TILE - NA_WIN_H // 2, 0, n_rows - NA_KEY_ROWS) // ROWS_PER_TILE
    for plan in sorted(set(plans)):
        first_row = min(j0 for _, j0 in plan)
        last_row = max(j0 for _, j0 in plan) + NA_WIN_H
        skip = first_row // ROWS_PER_TILE
        n_tiles = -(-last_row // ROWS_PER_TILE) - skip
        key_tiles = pl.ds(band_tile0 + skip, n_tiles)
        bias_lanes = slice(skip * TILE, (skip + n_tiles) * TILE)

        @pl.when(functools.reduce(jnp.logical_or, [t == i for i, p in enumerate(plans) if p == plan]))
        def _attend(key_tiles=key_tiles, bias_lanes=bias_lanes, n_keys=n_tiles * TILE):
            def band(ref, s, sl):
                return ref[s, key_tiles, :, sl].reshape(n_keys, LANES)

            def bias(head0):
                return bias_ref[head0:head0 + 2, :, bias_lanes].reshape(2 * TILE, n_keys)

            for s in range(q_ref.shape[0]):
                _na_heads(q_ref.at[s], o_ref.at[s], scratch,
                          lambda sl, s=s: [(band(k_ref, s, sl), False), (kc_ref[s, sl, :].astype(BF16), True)],
                          lambda sl, s=s: [(band(v_ref, s, sl), False), (vc_ref[s, sl, :].astype(BF16), True)], bias)


def _p_block(tile_of, col):
    return pl.BlockSpec((None, None, TILE, CHUNK), lambda *g: (tile_of(*g), col // CHUNK, 0, 0))


def _na_ctx(p, n_seq):
    per_step = CTX_SEQS_PER_STEP
    blk = lambda col: pl.BlockSpec((per_step, None, TILE, CHUNK), lambda i: (i, col // CHUNK, 0, 0))
    y = pl.pallas_call(
        _na_ctx_kernel, grid=(n_seq // per_step,),
        in_specs=[blk(C_NAQ), blk(C_NAK), blk(C_NAV)],
        out_specs=pl.BlockSpec((per_step, TILE, BRANCH_WIDTH), lambda i: (i, 0, 0)),
        out_shape=jax.ShapeDtypeStruct((n_seq, TILE, BRANCH_WIDTH), BF16),
        scratch_shapes=_pair_scratch(TILE),
        compiler_params=_params(("parallel",)), name="na_ctx",
    )(p, p, p)
    return y.reshape(n_seq * TILE, BRANCH_WIDTH)


def _na_lat(p, cache_k, cache_v, tab, layer, n_seq, seq):
    tiles = seq // TILE
    n_past = cache_k.shape[3]
    per_step = LAT_SEQS_PER_STEP
    p5 = p.reshape(n_seq, tiles, P_COLS // CHUNK, TILE, CHUNK)
    kv = lambda col: pl.BlockSpec((per_step, tiles, None, TILE, CHUNK), lambda t, b: (b, 0, col // CHUNK, 0, 0))
    cache = pl.BlockSpec((per_step, None, BRANCH_WIDTH, n_past), lambda t, b: (b, layer, 0, 0))
    table = _const_spec((None, NA_HEADS, TAB_BLOCKS // 2, GRID_W, LANES), (layer, 0, 0, 0, 0))
    y = pl.pallas_call(
        functools.partial(_na_lat_kernel, plans=_na_band_plans(seq), n_rows=seq // GRID_W),
        grid=(tiles, n_seq // per_step),
        in_specs=[pl.BlockSpec((per_step, None, None, TILE, CHUNK), lambda t, b: (b, t, C_NAQ // CHUNK, 0, 0)),
                  kv(C_NAK), kv(C_NAV), cache, cache, table],
        out_specs=pl.BlockSpec((per_step, None, TILE, BRANCH_WIDTH), lambda t, b: (b, t, 0, 0)),
        out_shape=jax.ShapeDtypeStruct((n_seq, tiles, TILE, BRANCH_WIDTH), BF16),
        scratch_shapes=[pltpu.VMEM((NA_HEADS, TILE, NA_LOC_KEYS), F32)]
        + _pair_scratch(NA_LOC_KEYS + n_past),
        compiler_params=_params(("parallel", "arbitrary")), name="na_lat",
    )(p5, p5, p5, cache_k, cache_v, tab)
    return y.reshape(n_seq * seq, BRANCH_WIDTH)


def _na_bias_tables(rpb):
    cq = np.arange(GRID_W)[:, None]
    ck = np.arange(GRID_W)[None, :]
    col_lo = np.clip(cq - NA_WIN_W // 2, 0, GRID_W - NA_WIN_W)
    ok = (ck >= col_lo) & (ck < col_lo + NA_WIN_W)
    n_b = 2 * NA_WIN_W - 1
    n_rel = 2 * NA_WIN_H - 1
    onehot = ((ck - cq + NA_WIN_W - 1)[:, None, :] == np.arange(n_b)[None, :, None]) & ok[:, None, :]
    expand = np.zeros((GRID_W, 2, n_b, 2, GRID_W), np.float32)
    expand[:, 0, :, 0, :] = onehot
    expand[:, 1, :, 1, :] = onehot
    expand = expand.reshape(GRID_W, 2 * n_b, LANES)
    rows = jnp.pad(rpb, ((0, 0), (0, 0), (TAB_PAD, TAB_BLOCKS - n_rel - TAB_PAD), (0, 0)))
    real = np.zeros(TAB_BLOCKS, bool)
    real[TAB_PAD:TAB_PAD + n_rel] = True
    pairs = rows.reshape(DEPTH, NA_HEADS, TAB_BLOCKS // 2, 2 * n_b)
    tab = jnp.einsum('lhek,ukc->lheuc', pairs, expand, precision=lax.Precision.HIGHEST)
    keep = real.reshape(-1, 1, 2, 1) & ok[None, :, None, :]
    return jnp.where(keep.reshape(TAB_BLOCKS // 2, GRID_W, LANES), tab, NEG_INF)


def _mla_kernel(*refs, n_cache, n_lat, rope, seqs, tiled):
    dq_ref, ckv_ref, kr_ref = refs[:3]
    refs = refs[3:]
    if n_cache:
        cckv_ref, ckr_ref = refs[:2]
        refs = refs[2:]
    qn_ref, wuq_ref, wk_ref, wv_ref = refs[:4]
    refs = refs[4:]
    if rope:
        cos_ref, sin_ref = refs[:2]
        refs = refs[2:]
    o_ref, kx_all, vx_all = refs[:3]
    scratch = refs[3:]
    scale = (MLA_NOPE + MLA_ROPE) ** -0.5
    n_keys = kx_all.shape[1]

    for s in range(seqs):
        key_set = 0 if tiled else s
        kx_ref, vx_ref = kx_all.at[key_set], vx_all.at[key_set]

        def expand_keys(key_set=key_set, kx_ref=kx_ref, vx_ref=vx_ref):
            def fill(r0, ckv, kr):
                n = ckv.shape[0]
                kk = _dot(ckv, wk_ref[...])
                for h in range(MLA_HEADS):
                    sl = slice(LANES * h, LANES * (h + 1))
                    kx_ref[r0:r0 + n, sl] = (kk[:, sl] + kr).astype(BF16)
                vx_ref[r0:r0 + n, :] = _dot(ckv, wv_ref[...]).astype(BF16)

            if n_cache:
                fill(0, cckv_ref[...].astype(BF16), ckr_ref[...])
            for r0 in range(key_set * n_lat, (key_set + 1) * n_lat, TILE):
                fill(n_cache + r0 - key_set * n_lat, ckv_ref[r0:r0 + TILE, :], kr_ref[r0:r0 + TILE, :].astype(F32))

        if not tiled:
            expand_keys()
        elif s == 0:
            pl.when(pl.program_id(1) == 0)(expand_keys)

        tile_rows = slice(s * TILE, (s + 1) * TILE)
        dqn = _rms(dq_ref[s, :, :MLA_Q_LORA].astype(F32), qn_ref[...])
        q = _dot(dqn.astype(BF16), wuq_ref[...])
        for hp in range(HEAD_PAIRS):
            s_pair, m_pair, p_pair, den_pair = (ref.at[hp % SCRATCH_SETS] for ref in scratch)
            for half in range(2):
                sl = slice(LANES * (2 * hp + half), LANES * (2 * hp + half + 1))
                qh = q[:, sl]
                if rope:
                    qh = _rope128(qh, cos_ref[tile_rows, :], sin_ref[tile_rows, :])
                qh = (qh * scale).astype(BF16)
                _store_scores(s_pair, m_pair, slice(half * TILE, (half + 1) * TILE), slice(None),
                              _dot_nt(qh, kx_ref[:, sl]), True)
            v2 = vx_ref[:, LANES * hp:LANES * (hp + 1)]
            o_ref[tile_rows, LANES * hp:LANES * (hp + 1)] = _pair_softmax_pv(
                s_pair, m_pair, p_pair, den_pair, [(0, n_keys)], [(v2, False)]).astype(BF16)


def _mla(p, ckv, kr, cache_ckv, cache_kr, layer, q_norm, w_uq, w_k, w_v, rope_tabs, n_seq, seq):
    tiled = seq > TILE
    seqs = LAT_TILES_PER_STEP if tiled else CTX_SEQS_PER_STEP
    tiles = seq // (seqs * TILE) if tiled else 1
    key_rows = seq if tiled else seqs * seq
    n_cache = 0 if cache_ckv is None else cache_ckv.shape[2]
    rope = rope_tabs is not None
    in_specs = [pl.BlockSpec((seqs, None, TILE, CHUNK), lambda b, t: (b * tiles + t, C_DQ // CHUNK, 0, 0)),
                pl.BlockSpec((key_rows, MLA_KV_LORA), lambda b, t: (b, 0)),
                pl.BlockSpec((key_rows, LANES), lambda b, t: (b, 0))]
    args = [p, ckv, kr]
    if n_cache:
        in_specs += [pl.BlockSpec((None, None, n_cache, MLA_KV_LORA), lambda b, t: (b, layer, 0, 0)),
                     pl.BlockSpec((None, None, n_cache, LANES), lambda b, t: (b, layer, 0, 0))]
        args += [cache_ckv, cache_kr]
    in_specs += [_const_spec((None, 1, MLA_Q_LORA), (layer, 0, 0)),
                 _const_spec((None, MLA_Q_LORA, MLA_HEADS * LANES), (layer, 0, 0)),
                 _const_spec((None, MLA_KV_LORA, MLA_HEADS * LANES), (layer, 0, 0)),
                 _const_spec((None, MLA_KV_LORA, MLA_HEADS * MLA_V), (layer, 0, 0))]
    args += [q_norm, w_uq, w_k, w_v]
    if rope:
        in_specs += [pl.BlockSpec((seqs * TILE, LANES), lambda b, t: (t, 0))] * 2
        args += list(rope_tabs)
    n_keys = n_cache + seq
    key_sets = 1 if tiled else seqs
    return pl.pallas_call(
        functools.partial(_mla_kernel, n_cache=n_cache, n_lat=seq, rope=rope, seqs=seqs, tiled=tiled),
        grid=(n_seq if tiled else n_seq // seqs, tiles), in_specs=in_specs,
        out_specs=pl.BlockSpec((seqs * TILE, BRANCH_WIDTH), lambda b, t: (b * tiles + t, 0)),
        out_shape=jax.ShapeDtypeStruct((n_seq * seq, BRANCH_WIDTH), BF16),
        scratch_shapes=[pltpu.VMEM((key_sets, n_keys, MLA_HEADS * LANES), BF16),
                        pltpu.VMEM((key_sets, n_keys, MLA_HEADS * MLA_V), BF16)] + _pair_scratch(n_keys),
        compiler_params=_params(("parallel", "arbitrary")), name="mla",
    )(*args)


def _merge_kernel(x_ref, mod_ref, yna_ref, ymla_ref, pm_ref, sw_ref, sb_ref, wb_ref, wo_ref, fg_ref, o_ref, *, final):
    tm = x_ref.shape[0]

    def tokens(col):
        return pm_ref[:, col // CHUNK].reshape(tm, CHUNK)

    sv = tokens(C_SV)
    rows = []
    for c0 in range(0, tm, SGU_CHUNK):
        cols = [_dot(sw_ref[g], sv[c0:c0 + SGU_CHUNK, LANES * g:LANES * (g + 1)]) for g in range(SGU_GROUPS)]
        rows.append(jnp.concatenate(cols, axis=1) + sb_ref[...])
    y_sgu = tokens(C_SU).astype(F32) * jnp.concatenate(rows, axis=0)

    gated = (yna_ref[...] * tokens(C_NAZ),
             (y_sgu * tokens(C_SZ).astype(F32)).astype(BF16),
             ymla_ref[...] * tokens(C_MZ))
    merged = None
    for k in range(N_BRANCH):
        gate_k = jnp.concatenate([tokens(C_MG + k * D_MODEL + c) for c in range(0, D_MODEL, CHUNK)], axis=1)
        term = gate_k.astype(F32) * _dot(gated[k], wb_ref[k])
        merged = term if merged is None else merged + term
    out = _dot(merged.astype(BF16), wo_ref[...])
    gate = mod_ref[:, 2 * D_MODEL:3 * D_MODEL]
    xn = x_ref[...] + gate * out
    if final:
        xn = _rms(xn, fg_ref[...])
    o_ref[...] = xn


def _merge(x, mod, mod_row, layer, p, y_na, y_mla, sgu_w, sgu_b, w_branch, w_out, final_g, final, tm):
    n_tok = x.shape[0]
    half = pl.BlockSpec((tm, BRANCH_WIDTH), lambda i: (i, 0))
    full = lambda c: pl.BlockSpec((tm, D_MODEL), lambda i: (i, c))
    in_specs = [full(0),
                pl.BlockSpec((None, None, 1, 3 * D_MODEL), lambda i: (layer, mod_row(i * tm), 0, 0)),
                half, half,
                pl.BlockSpec((tm // TILE, MERGE_CHUNKS, TILE, CHUNK), lambda i: (i, 0, 0, 0)),
                _const_spec((None, SGU_GROUPS, SGU_CHUNK, SGU_CHUNK), (layer, 0, 0, 0)),
                _const_spec((None, SGU_CHUNK, BRANCH_WIDTH), (layer, 0, 0)),
                _const_spec((None, N_BRANCH, BRANCH_WIDTH, D_MODEL), (layer, 0, 0, 0)),
                _const_spec((None, D_MODEL, D_MODEL), (layer, 0, 0)),
                _const_spec((1, D_MODEL), (0, 0))]
    return pl.pallas_call(
        functools.partial(_merge_kernel, final=final),
        grid=(n_tok // tm,), in_specs=in_specs,
        out_specs=full(0), out_shape=jax.ShapeDtypeStruct((n_tok, D_MODEL), F32),
        compiler_params=_params(("parallel",)), name="merge",
    )(x, mod, y_na, y_mla, p, sgu_w, sgu_b, w_branch, w_out, final_g)


SRC_DQ = 7 * BRANCH_WIDTH
SRC_DKV = SRC_DQ + MLA_Q_LORA
SRC_KR = SRC_DKV + MLA_KV_LORA
SRC_MZ = SRC_KR + MLA_ROPE
SRC_NAZ = 3 * BRANCH_WIDTH
PACK_SHIFT = SRC_MZ % CHUNK
J_SHIFTED = C_MZ // CHUNK
J_Q = C_NAQ // CHUNK
J_DQ = C_DQ // CHUNK
assert SRC_NAZ % CHUNK == 0 and SRC_DQ % CHUNK == 0
assert SRC_DQ + CHUNK + (C_KR - P_COLS) == SRC_KR and SRC_KR + MLA_ROPE == SRC_MZ


def _pack_w_in_kernel(a_ref, b_ref, o_ref):
    j = pl.program_id(1)

    def emit(rows):
        o_ref[...] = rows.T.astype(BF16)

    @pl.when(jnp.logical_or(j < J_SHIFTED, jnp.logical_and(j >= J_Q, j <= J_DQ)))
    def _aligned():
        emit(a_ref[...])

    @pl.when(jnp.logical_and(j >= J_SHIFTED, j < J_Q))
    def _shifted():
        emit(jnp.concatenate([a_ref[PACK_SHIFT:, :], b_ref[:PACK_SHIFT, :]], axis=0))

    @pl.when(j == J_DQ + 1)
    def _tail():
        n_dkv = C_KR - P_COLS
        zeros = lambda n: jnp.zeros((n, D_MODEL), F32)
        emit(jnp.concatenate([a_ref[:n_dkv, :], zeros(ROPE_LANE0), a_ref[n_dkv:n_dkv + MLA_ROPE, :],
                              zeros(LANES - ROPE_LANE0 - MLA_ROPE), zeros(CHUNK - n_dkv - LANES)], axis=0))


def _pack_w_in(w_in_t):
    n_chunks = pl.cdiv(W_COLS, CHUNK)
    shifted0 = (SRC_MZ - PACK_SHIFT) // CHUNK

    def a_block(j):
        return jnp.where(j < J_SHIFTED, j + SRC_NAZ // CHUNK,
                         jnp.where(j < J_Q, j - J_SHIFTED + shifted0,
                                   jnp.where(j < J_DQ, j - J_Q,
                                             jnp.where(j == J_DQ, SRC_DQ // CHUNK, SRC_DQ // CHUNK + 1))))

    def b_block(j):
        shifted = jnp.logical_and(j >= J_SHIFTED, j < J_Q)
        return jnp.where(shifted, j - J_SHIFTED + shifted0 + 1, shifted0 + 1)

    return pl.pallas_call(
        _pack_w_in_kernel, grid=(DEPTH, n_chunks),
        in_specs=[pl.BlockSpec((None, CHUNK, D_MODEL), lambda l, j: (l, a_block(j), 0)),
                  pl.BlockSpec((None, CHUNK, D_MODEL), lambda l, j: (l, b_block(j), 0))],
        out_specs=pl.BlockSpec((None, D_MODEL, CHUNK), lambda l, j: (l, 0, j)),
        out_shape=jax.ShapeDtypeStruct((DEPTH, D_MODEL, W_COLS), BF16),
        compiler_params=_params(("parallel", "arbitrary")), name="pack_w_in",
    )(w_in_t, w_in_t)


def _pack_params(w_in, sgu_w, sgu_b, mla_w_uq, mla_w_ukv, w_branch, w_out):
    w_packed = _pack_w_in(jnp.swapaxes(w_in, 1, 2))
    uq = mla_w_uq.reshape(DEPTH, MLA_Q_LORA, MLA_HEADS, MLA_NOPE + MLA_ROPE)
    uq = jnp.pad(uq, ((0, 0), (0, 0), (0, 0), (0, LANES - MLA_NOPE - MLA_ROPE)))
    uq = uq.reshape(DEPTH, MLA_Q_LORA, MLA_HEADS * LANES).astype(BF16)
    ukv = mla_w_ukv.reshape(DEPTH, MLA_KV_LORA, MLA_HEADS, MLA_NOPE + MLA_V)
    w_k = jnp.pad(ukv[..., :MLA_NOPE], ((0, 0), (0, 0), (0, 0), (0, LANES - MLA_NOPE)))
    w_k = w_k.reshape(DEPTH, MLA_KV_LORA, MLA_HEADS * LANES).astype(BF16)
    w_v = ukv[..., MLA_NOPE:].reshape(DEPTH, MLA_KV_LORA, MLA_HEADS * MLA_V).astype(BF16)
    sgu_bias = jnp.repeat(jnp.swapaxes(sgu_b, 1, 2), BRANCH_WIDTH // SGU_GROUPS, axis=2)
    return w_packed, uq, w_k, w_v, sgu_w.astype(BF16), sgu_bias, w_branch.astype(BF16), w_out.astype(BF16)


def _rope_tables(n_tokens):
    pos = jnp.arange(n_tokens, dtype=jnp.int32)
    row = (pos // GRID_W).astype(F32)
    col = (pos % GRID_W).astype(F32)
    n_freq = MLA_ROPE // 4
    inv = ROPE_THETA ** (-jnp.arange(n_freq, dtype=F32) / n_freq)
    ang = jnp.concatenate([row[:, None] * inv, col[:, None] * inv], axis=-1)
    cos, sin = jnp.cos(ang), jnp.sin(ang)
    pad_l, pad_r = ROPE_LANE0, LANES - ROPE_LANE0 - MLA_ROPE
    cos_t = jnp.pad(jnp.concatenate([cos, cos], axis=1), ((0, 0), (pad_l, pad_r)), constant_values=1.0)
    sin_t = jnp.pad(jnp.concatenate([-sin, sin], axis=1), ((0, 0), (pad_l, pad_r)))
    return cos_t, sin_t


def kernel(x_prompt, x_sample, cache_na_k, cache_na_v, cache_mla_ckv, cache_mla_krope, c, c_ctx, norm_g, w_mod, b_mod, w_in, na_rpb, sgu_w, sgu_b, mla_q_norm, mla_w_uq, mla_kv_norm, mla_w_ukv, w_branch, w_out, final_norm_g):
    n_ctx, ctx_len, _ = x_prompt.shape
    n_lat, lat_len, _ = x_sample.shape
    past = cache_na_k.shape[2]
    assert ctx_len == TILE and lat_len % TILE == 0 and lat_len // GRID_W == 16 and past == TILE

    w_packed, w_uq, w_k, w_v, sgu_wb, sgu_bias, w_br, w_o = _pack_params(
        w_in, sgu_w, sgu_b, mla_w_uq, mla_w_ukv, w_branch, w_out)
    rope_tabs = _rope_tables(lat_len)
    tab = _na_bias_tables(na_rpb)
    norm_g3 = norm_g.reshape(DEPTH, 1, D_MODEL)
    kv_norm3 = mla_kv_norm.reshape(DEPTH, 1, MLA_KV_LORA)
    q_norm3 = mla_q_norm.reshape(DEPTH, 1, MLA_Q_LORA)
    final_g = final_norm_g.reshape(1, D_MODEL)

    cond_rows = 16
    cond = jnp.concatenate([c, c_ctx[None, :], jnp.zeros((cond_rows - n_lat - 1, D_MODEL), F32)], axis=0)
    mod = _modulation(cond, w_mod, b_mod).reshape(DEPTH, cond_rows, 1, 3 * D_MODEL)
    ctx_row = lambda token: n_lat
    lat_row = lambda token: token // lat_len

    cache_k = jnp.transpose(cache_na_k, (0, 1, 3, 4, 2)).reshape(n_lat, DEPTH, BRANCH_WIDTH, past)
    cache_v = jnp.transpose(cache_na_v, (0, 1, 3, 4, 2)).reshape(n_lat, DEPTH, BRANCH_WIDTH, past)
    cache_kr = jnp.pad(cache_mla_krope, ((0, 0), (0, 0), (0, 0), (ROPE_LANE0, LANES - ROPE_LANE0 - MLA_ROPE)))

    xp = x_prompt.reshape(n_ctx * ctx_len, D_MODEL)
    xs = x_sample.reshape(n_lat * lat_len, D_MODEL)
    state = None
    for l in range(DEPTH):
        final = l == DEPTH - 1
        p, ckv, kr, *state = _inproj(xp, mod, ctx_row, l, norm_g3, w_packed, kv_norm3, None, state, True, TILE)
        y_na = _na_ctx(p, n_ctx)
        y_mla = _mla(p, ckv, kr, None, None, l, q_norm3, w_uq, w_k, w_v, None, n_ctx, ctx_len)
        xp = _merge(xp, mod, ctx_row, l, p, y_na, y_mla, sgu_wb, sgu_bias, w_br, w_o, final_g, final, MERGE_TILE)
        p, ckv, kr = _inproj(xs, mod, lat_row, l, norm_g3, w_packed, kv_norm3, rope_tabs, None, False, TILE)
        y_na = _na_lat(p, cache_k, cache_v, tab, l, n_lat, lat_len)
        y_mla = _mla(p, ckv, kr, cache_mla_ckv, cache_kr, l, q_norm3, w_uq, w_k, w_v, rope_tabs, n_lat, lat_len)
        xs = _merge(xs, mod, lat_row, l, p, y_na, y_mla, sgu_wb, sgu_bias, w_br, w_o, final_g, final, MERGE_TILE)

    s_k, s_v, s_ckv, s_kr = state

    def heads_last(s):
        return jnp.transpose(s.reshape(n_ctx, DEPTH, NA_HEADS, NA_HEAD_DIM, ctx_len), (0, 1, 4, 2, 3))

    return (xp.reshape(n_ctx, ctx_len, D_MODEL), xs.reshape(n_lat, lat_len, D_MODEL),
            heads_last(s_k), heads_last(s_v), s_ckv, jnp.swapaxes(s_kr, 2, 3))
```

```python
import functools

import numpy as np
import jax
import jax.numpy as jnp
from jax import lax
from jax.experimental import pallas as pl
from jax.experimental.pallas import tpu as pltpu

F32 = jnp.float32
BF16 = jnp.bfloat16

D_MODEL = 1024
DEPTH = 2
GRID_W = 64
BRANCH_WIDTH = 512
N_BRANCH = 3
NA_HEADS = 8
NA_HEAD_DIM = 64
NA_WIN_H = 8
NA_WIN_W = 16
SGU_GROUPS = 4
SGU_CHUNK = 128
MLA_HEADS = 8
MLA_NOPE = 64
MLA_ROPE = 32
MLA_V = 64
MLA_Q_LORA = 384
MLA_KV_LORA = 256
ROPE_THETA = 10000.0
EPS = 1e-6
NEG_INF = -1e30

LANES = 128
TILE = 256
MERGE_TILE = 512
CTX_SEQS_PER_STEP = 4
LAT_TILES_PER_STEP = 4
LAT_SEQS_PER_STEP = 2
STRIP = 16
HEAD_PAIRS = NA_HEADS // 2
SCRATCH_SETS = 2
NA_KEY_ROWS = 12
NA_LOC_KEYS = NA_KEY_ROWS * GRID_W
ROWS_PER_TILE = TILE // GRID_W
TAB_PAD = 4
TAB_BLOCKS = 24
ROPE_LANE0 = MLA_NOPE

C_NAZ, C_SU, C_SV, C_SZ, C_MZ, C_MG = (512 * i for i in range(6))
C_NAQ = C_MG + N_BRANCH * D_MODEL
C_NAK, C_NAV, C_DQ = C_NAQ + 512, C_NAQ + 1024, C_NAQ + 1536
MERGE_CHUNKS = C_NAQ // 512
C_DKV = C_DQ + MLA_Q_LORA
C_KR = C_DKV + MLA_KV_LORA
W_COLS = C_KR + LANES
P_COLS = C_DQ + 512
CHUNK = 512

VMEM_LIMIT = 56 * 1024 * 1024


def _dot(a, b):
    return jnp.dot(a, b, preferred_element_type=F32)


def _dot_nt(a, b):
    return lax.dot_general(a, b, (((1,), (1,)), ((), ())), preferred_element_type=F32)


def _rms(x, g):
    return x * lax.rsqrt(jnp.mean(x * x, axis=-1, keepdims=True) + EPS) * g


def _center_norm(x):
    c = x - jnp.mean(x, axis=-1, keepdims=True)
    return c * lax.rsqrt(jnp.mean(c * c, axis=-1, keepdims=True) + EPS)


def _rope128(x, cos, sin):
    lane = lax.broadcasted_iota(jnp.int32, x.shape, 1)
    swapped = jnp.where(lane < ROPE_LANE0 + MLA_ROPE // 2,
                        pltpu.roll(x, LANES - MLA_ROPE // 2, axis=1),
                        pltpu.roll(x, MLA_ROPE // 2, axis=1))
    return x * cos + swapped * sin


def _store_scores(s_ref, m_ref, rows, cols, scores, first):
    s_ref[rows, cols] = scores
    block_max = jnp.broadcast_to(scores.max(axis=-1, keepdims=True), (scores.shape[0], LANES))
    m_ref[rows, :] = block_max if first else jnp.maximum(m_ref[rows, :], block_max)


def _pair_softmax_pv(s_ref, m_ref, p_ref, den_ref, segments, values):
    for i in range(s_ref.shape[0] // STRIP):
        r = slice(i * STRIP, (i + 1) * STRIP)
        m = m_ref[r, :]
        den = None
        for c0, c1 in segments:
            e = jnp.exp(s_ref[r, c0:c1] - jnp.concatenate([m] * ((c1 - c0) // LANES), axis=1))
            d = e.sum(axis=-1, keepdims=True)
            den = d if den is None else den + d
            p_ref[r, c0:c1] = e.astype(BF16)
        den_ref[r, :] = jnp.broadcast_to(den, (STRIP, LANES))
    o = None
    for (c0, c1), (v, transposed) in zip(segments, values):
        t = _dot_nt(p_ref[:, c0:c1], v) if transposed else _dot(p_ref[:, c0:c1], v)
        o = t if o is None else o + t
    o = o / den_ref[...]
    lane = lax.broadcasted_iota(jnp.int32, (TILE, LANES), 1)
    return jnp.where(lane < LANES // 2, o[:TILE], o[TILE:])


def _pair_scratch(n_keys):
    return [pltpu.VMEM((SCRATCH_SETS, 2 * TILE, n_keys), F32), pltpu.VMEM((SCRATCH_SETS, 2 * TILE, LANES), F32),
            pltpu.VMEM((SCRATCH_SETS, 2 * TILE, n_keys), BF16), pltpu.VMEM((SCRATCH_SETS, 2 * TILE, LANES), F32)]


def _const_spec(shape, index):
    return pl.BlockSpec(shape, lambda *_: index, pipeline_mode=pl.Buffered(1))


def _params(semantics):
    return pltpu.CompilerParams(dimension_semantics=semantics, vmem_limit_bytes=VMEM_LIMIT)


def _mod_kernel(c_ref, w_ref, b_ref, o_ref):
    c = c_ref[...]
    s = c * jax.nn.sigmoid(c)
    o_ref[...] = jnp.dot(s, w_ref[...], preferred_element_type=F32, precision=lax.Precision.HIGHEST) + b_ref[...]


def _modulation(cond, w_mod, b_mod):
    rows = cond.shape[0]
    return pl.pallas_call(
        _mod_kernel,
        grid=(DEPTH, 3),
        in_specs=[pl.BlockSpec((rows, D_MODEL), lambda l, j: (0, 0)),
                  pl.BlockSpec((None, D_MODEL, D_MODEL), lambda l, j: (l, 0, j)),
                  pl.BlockSpec((None, 1, D_MODEL), lambda l, j: (l, 0, j))],
        out_specs=pl.BlockSpec((None, rows, D_MODEL), lambda l, j: (l, 0, j)),
        out_shape=jax.ShapeDtypeStruct((DEPTH, rows, 3 * D_MODEL), F32),
        compiler_params=_params(("parallel", "parallel")),
        name="modulation",
    )(cond, w_mod, b_mod.reshape(DEPTH, 1, 3 * D_MODEL))


def _inproj_kernel(*refs, emit_state, rope, n_alias):
    x0_ref, x_odd_ref, x_next_ref, mod0_ref, mod_odd_ref, mod_next_ref, g_ref, w_ref, kvn_ref = refs[:9]
    refs = refs[9:]
    if rope:
        cos_ref, sin_ref = refs[:2]
        refs = refs[2:]
    refs = refs[n_alias:]
    p_ref, ckv_ref, kr_ref = refs[:3]
    if emit_state:
        sk_ref, sv_ref, sckv_ref, skr_ref = refs[3:7]
    h_ref = refs[-1]

    def normalised(x_ref, mod_ref):
        shift = mod_ref[:, 0:D_MODEL]
        scale = mod_ref[:, D_MODEL:2 * D_MODEL]
        return (_rms(x_ref[...], g_ref[...]) * (1.0 + scale) + shift).astype(BF16)

    @pl.when(pl.program_id(0) == 0)
    def _first_tile():
        h_ref[0] = normalised(x0_ref, mod0_ref)

    def project(s, prepare_next):
        rows = slice(s * TILE, (s + 1) * TILE)
        h = h_ref[s]
        ckv = _rms(_dot(h, w_ref[:, C_DKV:C_KR]), kvn_ref[...])
        ckv_ref[rows, :] = ckv.astype(BF16)
        kr = _dot(h, w_ref[:, C_KR:W_COLS])
        if emit_state:
            sckv_ref[s] = ckv
            skr_ref[s] = kr.T[ROPE_LANE0:ROPE_LANE0 + MLA_ROPE, :]
        if rope:
            kr = _rope128(kr, cos_ref[rows, :], sin_ref[rows, :])
        kr_ref[rows, :] = kr.astype(BF16)

        prepare_next()

        for c0 in range(0, P_COLS, CHUNK):
            acc = _dot(h, w_ref[:, c0:c0 + CHUNK])
            if emit_state and c0 == C_NAK:
                sk_ref[s] = acc.T
            if emit_state and c0 == C_NAV:
                sv_ref[s] = acc.T
            if c0 in (C_NAZ, C_SZ, C_MZ):
                acc = acc * jax.nn.sigmoid(acc)
            elif c0 == C_SU:
                acc = jax.nn.gelu(acc)
            elif c0 == C_SV:
                acc = _center_norm(jax.nn.gelu(acc))
            elif C_MG <= c0 < C_NAQ:
                acc = jax.nn.sigmoid(acc)
            p_ref[s, c0 // CHUNK] = acc.astype(BF16)

    def prepare_odd():
        h_ref[1] = normalised(x_odd_ref, mod_odd_ref)

    def prepare_next_even():
        h_ref[0] = normalised(x_next_ref, mod_next_ref)

    project(0, prepare_odd)
    project(1, prepare_next_even)


def _inproj(x, mod, mod_row, layer, norm_g, w_packed, kv_norm, rope_tabs, state_in, emit_state):
    n_tok = x.shape[0]
    n_tiles = n_tok // TILE
    n_steps = n_tiles // 2
    rope = rope_tabs is not None
    odd_tile = lambda i: 2 * i + 1
    next_tile = lambda i: jnp.minimum(2 * i + 2, n_tiles - 1)
    x_tile = lambda tile_of: pl.BlockSpec((TILE, D_MODEL), lambda i: (tile_of(i), 0))
    mod_of = lambda tile_of: pl.BlockSpec((None, None, 1, 3 * D_MODEL),
                                          lambda i: (layer, mod_row(tile_of(i) * TILE), 0, 0))
    in_specs = [_const_spec((TILE, D_MODEL), (0, 0)), x_tile(odd_tile), x_tile(next_tile),
                _const_spec((None, None, 1, 3 * D_MODEL), (layer, mod_row(0), 0, 0)), mod_of(odd_tile), mod_of(next_tile),
                _const_spec((None, 1, D_MODEL), (layer, 0, 0)),
                _const_spec((None, D_MODEL, W_COLS), (layer, 0, 0)),
                _const_spec((None, 1, MLA_KV_LORA), (layer, 0, 0))]
    args = [x, x, x, mod, mod, mod, norm_g, w_packed, kv_norm]
    if rope:
        steps_per_seq = rope_tabs[0].shape[0] // (2 * TILE)
        in_specs += [pl.BlockSpec((2 * TILE, LANES), lambda i: (i % steps_per_seq, 0))] * 2
        args += list(rope_tabs)
    out_specs = [pl.BlockSpec((2, P_COLS // CHUNK, TILE, CHUNK), lambda i: (i, 0, 0, 0)),
                 pl.BlockSpec((2 * TILE, MLA_KV_LORA), lambda i: (i, 0)),
                 pl.BlockSpec((2 * TILE, LANES), lambda i: (i, 0))]
    out_shape = [jax.ShapeDtypeStruct((n_tiles, P_COLS // CHUNK, TILE, CHUNK), BF16),
                 jax.ShapeDtypeStruct((n_tok, MLA_KV_LORA), BF16),
                 jax.ShapeDtypeStruct((n_tok, LANES), BF16)]
    aliases = {}
    if emit_state:
        for rows, cols in ((BRANCH_WIDTH, TILE), (BRANCH_WIDTH, TILE), (TILE, MLA_KV_LORA), (MLA_ROPE, TILE)):
            out_specs.append(pl.BlockSpec((2, None, rows, cols), lambda i: (i, layer, 0, 0)))
            out_shape.append(jax.ShapeDtypeStruct((n_tiles, DEPTH, rows, cols), F32))
        if state_in is not None:
            first = len(args)
            in_specs += [pl.BlockSpec(memory_space=pl.ANY)] * 4
            args += list(state_in)
            aliases = {first + k: 3 + k for k in range(4)}
    kern = functools.partial(_inproj_kernel, emit_state=emit_state, rope=rope, n_alias=len(aliases))
    return pl.pallas_call(
        kern, grid=(n_steps,), in_specs=in_specs, out_specs=out_specs, out_shape=out_shape,
        scratch_shapes=[pltpu.VMEM((2, TILE, D_MODEL), BF16)],
        input_output_aliases=aliases, compiler_params=_params(("arbitrary",)),
        name="inproj",
    )(*args)


def _na_heads(q_ref, o_ref, scratch, keys_fn, values_fn, bias_fn=None):
    lane = lax.broadcasted_iota(jnp.int32, (TILE, LANES), 1)
    low = lane < NA_HEAD_DIM
    for hp in range(HEAD_PAIRS):
        sl = slice(LANES * hp, LANES * (hp + 1))
        q2 = q_ref[:, sl] * (NA_HEAD_DIM ** -0.5)
        zero = jnp.zeros_like(q2)
        q_stack = jnp.concatenate([jnp.where(low, q2, zero), jnp.where(low, zero, q2)], axis=0)
        s_pair, m_pair, p_pair, den_pair = (ref.at[hp % SCRATCH_SETS] for ref in scratch)
        segments = []
        c0 = 0
        for j, (k, transposed) in enumerate(keys_fn(sl)):
            n_keys = k.shape[1] if transposed else k.shape[0]
            scores = _dot(q_stack, k) if transposed else _dot_nt(q_stack, k)
            if j == 0 and bias_fn is not None:
                scores = scores + bias_fn(2 * hp)
            _store_scores(s_pair, m_pair, slice(None), slice(c0, c0 + n_keys), scores, j == 0)
            segments.append((c0, c0 + n_keys))
            c0 += n_keys
        o_ref[:, sl] = _pair_softmax_pv(s_pair, m_pair, p_pair, den_pair, segments, values_fn(sl)).astype(BF16)


def _na_ctx_kernel(q_ref, k_ref, v_ref, o_ref, *scratch):
    for s in range(q_ref.shape[0]):
        _na_heads(q_ref.at[s], o_ref.at[s], scratch,
                  lambda sl, s=s: [(k_ref[s, :, sl], False)], lambda sl, s=s: [(v_ref[s, :, sl], False)])


def _na_band_row0(tile_row0, n_rows):
    return min(max(tile_row0 - NA_WIN_H // 2, 0), n_rows - NA_KEY_ROWS)


def _na_band_plans(seq):
    n_rows = seq // GRID_W
    plans = []
    for t in range(seq // TILE):
        band0 = _na_band_row0(t * ROWS_PER_TILE, n_rows)
        plan = []
        for r in range(ROWS_PER_TILE):
            rq = t * ROWS_PER_TILE + r
            row_lo = min(max(rq - NA_WIN_H // 2, 0), n_rows - NA_WIN_H)
            plan.append((band0 - rq + NA_WIN_H - 1 + TAB_PAD, row_lo - band0))
        plans.append(tuple(plan))
    return tuple(plans)


def _na_lat_kernel(q_ref, k_ref, v_ref, kc_ref, vc_ref, tab_ref, o_ref, bias_ref, *scratch, plans, n_rows):
    t = pl.program_id(0)
    first_half = lax.broadcasted_iota(jnp.int32, (GRID_W, LANES), 1) < GRID_W
    masked = jnp.full((GRID_W, LANES), NEG_INF, F32)

    def block_pair(h, blk):
        if blk % 2 == 0:
            return tab_ref[h, blk // 2]
        return jnp.where(first_half, pltpu.roll(tab_ref[h, blk // 2], GRID_W, axis=1),
                         pltpu.roll(tab_ref[h, blk // 2 + 1], GRID_W, axis=1))

    for plan in sorted(set(plans)):
        is_tile = functools.reduce(jnp.logical_or, [t == i for i, p in enumerate(plans) if p == plan])

        @pl.when(jnp.logical_and(pl.program_id(1) == 0, is_tile))
        def _build_bias(plan=plan):
            for r, (blk0, j0) in enumerate(plan):
                for j in range(NA_KEY_ROWS // 2):
                    in_window = [j0 <= 2 * j + half < j0 + NA_WIN_H for half in range(2)]
                    for h in range(NA_HEADS):
                        if not any(in_window):
                            piece = masked
                        elif all(in_window):
                            piece = block_pair(h, blk0 + 2 * j)
                        else:
                            keep = first_half if in_window[0] else jnp.logical_not(first_half)
                            piece = jnp.where(keep, block_pair(h, blk0 + 2 * j), NEG_INF)
                        bias_ref[h, r * GRID_W:(r + 1) * GRID_W, LANES * j:LANES * (j + 1)] = piece

    band_tile0 = jnp.clip(t * ROWS_PER_TILE - NA_WIN_H // 2, 0, n_rows - NA_KEY_ROWS) // ROWS_PER_TILE
    for plan in sorted(set(plans)):
        first_row = min(j0 for _, j0 in plan)
        last_row = max(j0 for _, j0 in plan) + NA_WIN_H
        skip = first_row // ROWS_PER_TILE
        n_tiles = -(-last_row // ROWS_PER_TILE) - skip
        key_tiles = pl.ds(band_tile0 + skip, n_tiles)
        bias_lanes = slice(skip * TILE, (skip + n_tiles) * TILE)

        @pl.when(functools.reduce(jnp.logical_or, [t == i for i, p in enumerate(plans) if p == plan]))
        def _attend(key_tiles=key_tiles, bias_lanes=bias_lanes, n_keys=n_tiles * TILE):
            def band(ref, s, sl):
                return ref[s, key_tiles, :, sl].reshape(n_keys, LANES)

            def bias(head0):
                return bias_ref[head0:head0 + 2, :, bias_lanes].reshape(2 * TILE, n_keys)

            for s in range(q_ref.shape[0]):
                _na_heads(q_ref.at[s], o_ref.at[s], scratch,
                          lambda sl, s=s: [(band(k_ref, s, sl), False), (kc_ref[s, sl, :].astype(BF16), True)],
                          lambda sl, s=s: [(band(v_ref, s, sl), False), (vc_ref[s, sl, :].astype(BF16), True)], bias)


def _p_block(tile_of, col):
    return pl.BlockSpec((None, None, TILE, CHUNK), lambda *g: (tile_of(*g), col // CHUNK, 0, 0))


def _na_ctx(p, n_seq):
    per_step = CTX_SEQS_PER_STEP
    blk = lambda col: pl.BlockSpec((per_step, None, TILE, CHUNK), lambda i: (i, col // CHUNK, 0, 0))
    y = pl.pallas_call(
        _na_ctx_kernel, grid=(n_seq // per_step,),
        in_specs=[blk(C_NAQ), blk(C_NAK), blk(C_NAV)],
        out_specs=pl.BlockSpec((per_step, TILE, BRANCH_WIDTH), lambda i: (i, 0, 0)),
        out_shape=jax.ShapeDtypeStruct((n_seq, TILE, BRANCH_WIDTH), BF16),
        scratch_shapes=_pair_scratch(TILE),
        compiler_params=_params(("parallel",)), name="na_ctx",
    )(p, p, p)
    return y.reshape(n_seq * TILE, BRANCH_WIDTH)


def _na_lat(p, cache_k, cache_v, tab, layer, n_seq, seq):
    tiles = seq // TILE
    n_past = cache_k.shape[3]
    per_step = LAT_SEQS_PER_STEP
    p5 = p.reshape(n_seq, tiles, P_COLS // CHUNK, TILE, CHUNK)
    kv = lambda col: pl.BlockSpec((per_step, tiles, None, TILE, CHUNK), lambda t, b: (b, 0, col // CHUNK, 0, 0))
    cache = pl.BlockSpec((per_step, None, BRANCH_WIDTH, n_past), lambda t, b: (b, layer, 0, 0))
    table = _const_spec((None, NA_HEADS, TAB_BLOCKS // 2, GRID_W, LANES), (layer, 0, 0, 0, 0))
    y = pl.pallas_call(
        functools.partial(_na_lat_kernel, plans=_na_band_plans(seq), n_rows=seq // GRID_W),
        grid=(tiles, n_seq // per_step),
        in_specs=[pl.BlockSpec((per_step, None, None, TILE, CHUNK), lambda t, b: (b, t, C_NAQ // CHUNK, 0, 0)),
                  kv(C_NAK), kv(C_NAV), cache, cache, table],
        out_specs=pl.BlockSpec((per_step, None, TILE, BRANCH_WIDTH), lambda t, b: (b, t, 0, 0)),
        out_shape=jax.ShapeDtypeStruct((n_seq, tiles, TILE, BRANCH_WIDTH), BF16),
        scratch_shapes=[pltpu.VMEM((NA_HEADS, TILE, NA_LOC_KEYS), F32)]
        + _pair_scratch(NA_LOC_KEYS + n_past),
        compiler_params=_params(("parallel", "arbitrary")), name="na_lat",
    )(p5, p5, p5, cache_k, cache_v, tab)
    return y.reshape(n_seq * seq, BRANCH_WIDTH)


def _na_bias_tables(rpb):
    cq = np.arange(GRID_W)[:, None]
    ck = np.arange(GRID_W)[None, :]
    col_lo = np.clip(cq - NA_WIN_W // 2, 0, GRID_W - NA_WIN_W)
    ok = (ck >= col_lo) & (ck < col_lo + NA_WIN_W)
    n_b = 2 * NA_WIN_W - 1
    n_rel = 2 * NA_WIN_H - 1
    onehot = ((ck - cq + NA_WIN_W - 1)[:, None, :] == np.arange(n_b)[None, :, None]) & ok[:, None, :]
    expand = np.zeros((GRID_W, 2, n_b, 2, GRID_W), np.float32)
    expand[:, 0, :, 0, :] = onehot
    expand[:, 1, :, 1, :] = onehot
    expand = expand.reshape(GRID_W, 2 * n_b, LANES)
    rows = jnp.pad(rpb, ((0, 0), (0, 0), (TAB_PAD, TAB_BLOCKS - n_rel - TAB_PAD), (0, 0)))
    real = np.zeros(TAB_BLOCKS, bool)
    real[TAB_PAD:TAB_PAD + n_rel] = True
    pairs = rows.reshape(DEPTH, NA_HEADS, TAB_BLOCKS // 2, 2 * n_b)
    tab = jnp.einsum('lhek,ukc->lheuc', pairs, expand, precision=lax.Precision.HIGHEST)
    keep = real.reshape(-1, 1, 2, 1) & ok[None, :, None, :]
    return jnp.where(keep.reshape(TAB_BLOCKS // 2, GRID_W, LANES), tab, NEG_INF)


def _mla_kernel(*refs, n_cache, n_lat, rope, seqs, tiled):
    dq_ref, ckv_ref, kr_ref = refs[:3]
    refs = refs[3:]
    if n_cache:
        cckv_ref, ckr_ref = refs[:2]
        refs = refs[2:]
    qn_ref, wuq_ref, wk_ref, wv_ref = refs[:4]
    refs = refs[4:]
    if rope:
        cos_ref, sin_ref = refs[:2]
        refs = refs[2:]
    o_ref, kx_all, vx_all = refs[:3]
    scratch = refs[3:]
    scale = (MLA_NOPE + MLA_ROPE) ** -0.5
    n_keys = kx_all.shape[1]

    for s in range(seqs):
        key_set = 0 if tiled else s
        kx_ref, vx_ref = kx_all.at[key_set], vx_all.at[key_set]

        def expand_keys(key_set=key_set, kx_ref=kx_ref, vx_ref=vx_ref):
            def fill(r0, ckv, kr):
                n = ckv.shape[0]
                kk = _dot(ckv, wk_ref[...])
                for h in range(MLA_HEADS):
                    sl = slice(LANES * h, LANES * (h + 1))
                    kx_ref[r0:r0 + n, sl] = (kk[:, sl] + kr).astype(BF16)
                vx_ref[r0:r0 + n, :] = _dot(ckv, wv_ref[...]).astype(BF16)

            if n_cache:
                fill(0, cckv_ref[...].astype(BF16), ckr_ref[...])
            for r0 in range(key_set * n_lat, (key_set + 1) * n_lat, TILE):
                fill(n_cache + r0 - key_set * n_lat, ckv_ref[r0:r0 + TILE, :], kr_ref[r0:r0 + TILE, :].astype(F32))

        if not tiled:
            expand_keys()
        elif s == 0:
            pl.when(pl.program_id(1) == 0)(expand_keys)

        tile_rows = slice(s * TILE, (s + 1) * TILE)
        dqn = _rms(dq_ref[s, :, :MLA_Q_LORA].astype(F32), qn_ref[...])
        q = _dot(dqn.astype(BF16), wuq_ref[...])
        for hp in range(HEAD_PAIRS):
            s_pair, m_pair, p_pair, den_pair = (ref.at[hp % SCRATCH_SETS] for ref in scratch)
            for half in range(2):
                sl = slice(LANES * (2 * hp + half), LANES * (2 * hp + half + 1))
                qh = q[:, sl]
                if rope:
                    qh = _rope128(qh, cos_ref[tile_rows, :], sin_ref[tile_rows, :])
                qh = (qh * scale).astype(BF16)
                _store_scores(s_pair, m_pair, slice(half * TILE, (half + 1) * TILE), slice(None),
                              _dot_nt(qh, kx_ref[:, sl]), True)
            v2 = vx_ref[:, LANES * hp:LANES * (hp + 1)]
            o_ref[tile_rows, LANES * hp:LANES * (hp + 1)] = _pair_softmax_pv(
                s_pair, m_pair, p_pair, den_pair, [(0, n_keys)], [(v2, False)]).astype(BF16)


def _mla(p, ckv, kr, cache_ckv, cache_kr, layer, q_norm, w_uq, w_k, w_v, rope_tabs, n_seq, seq):
    tiled = seq > TILE
    seqs = LAT_TILES_PER_STEP if tiled else CTX_SEQS_PER_STEP
    tiles = seq // (seqs * TILE) if tiled else 1
    key_rows = seq if tiled else seqs * seq
    n_cache = 0 if cache_ckv is None else cache_ckv.shape[2]
    rope = rope_tabs is not None
    in_specs = [pl.BlockSpec((seqs, None, TILE, CHUNK), lambda b, t: (b * tiles + t, C_DQ // CHUNK, 0, 0)),
                pl.BlockSpec((key_rows, MLA_KV_LORA), lambda b, t: (b, 0)),
                pl.BlockSpec((key_rows, LANES), lambda b, t: (b, 0))]
    args = [p, ckv, kr]
    if n_cache:
        in_specs += [pl.BlockSpec((None, None, n_cache, MLA_KV_LORA), lambda b, t: (b, layer, 0, 0)),
                     pl.BlockSpec((None, None, n_cache, LANES), lambda b, t: (b, layer, 0, 0))]
        args += [cache_ckv, cache_kr]
    in_specs += [_const_spec((None, 1, MLA_Q_LORA), (layer, 0, 0)),
                 _const_spec((None, MLA_Q_LORA, MLA_HEADS * LANES), (layer, 0, 0)),
                 _const_spec((None, MLA_KV_LORA, MLA_HEADS * LANES), (layer, 0, 0)),
                 _const_spec((None, MLA_KV_LORA, MLA_HEADS * MLA_V), (layer, 0, 0))]
    args += [q_norm, w_uq, w_k, w_v]
    if rope:
        in_specs += [pl.BlockSpec((seqs * TILE, LANES), lambda b, t: (t, 0))] * 2
        args += list(rope_tabs)
    n_keys = n_cache + seq
    key_sets = 1 if tiled else seqs
    return pl.pallas_call(
        functools.partial(_mla_kernel, n_cache=n_cache, n_lat=seq, rope=rope, seqs=seqs, tiled=tiled),
        grid=(n_seq if tiled else n_seq // seqs, tiles), in_specs=in_specs,
        out_specs=pl.BlockSpec((seqs * TILE, BRANCH_WIDTH), lambda b, t: (b * tiles + t, 0)),
        out_shape=jax.ShapeDtypeStruct((n_seq * seq, BRANCH_WIDTH), BF16),
        scratch_shapes=[pltpu.VMEM((key_sets, n_keys, MLA_HEADS * LANES), BF16),
                        pltpu.VMEM((key_sets, n_keys, MLA_HEADS * MLA_V), BF16)] + _pair_scratch(n_keys),
        compiler_params=_params(("parallel", "arbitrary")), name="mla",
    )(*args)


def _merge_kernel(x_ref, mod_ref, yna_ref, ymla_ref, pm_ref, sw_ref, sb_ref, wb_ref, wo_ref, fg_ref, o_ref, *, final):
    tm = x_ref.shape[0]

    def tokens(col):
        return pm_ref[:, col // CHUNK].reshape(tm, CHUNK)

    sv = tokens(C_SV)
    rows = []
    for c0 in range(0, tm, SGU_CHUNK):
        cols = [_dot(sw_ref[g], sv[c0:c0 + SGU_CHUNK, LANES * g:LANES * (g + 1)]) for g in range(SGU_GROUPS)]
        rows.append(jnp.concatenate(cols, axis=1) + sb_ref[...])
    y_sgu = tokens(C_SU).astype(F32) * jnp.concatenate(rows, axis=0)

    gated = (yna_ref[...] * tokens(C_NAZ),
             (y_sgu * tokens(C_SZ).astype(F32)).astype(BF16),
             ymla_ref[...] * tokens(C_MZ))
    merged = None
    for k in range(N_BRANCH):
        gate_k = jnp.concatenate([tokens(C_MG + k * D_MODEL + c) for c in range(0, D_MODEL, CHUNK)], axis=1)
        term = gate_k.astype(F32) * _dot(gated[k], wb_ref[k])
        merged = term if merged is None else merged + term
    out = _dot(merged.astype(BF16), wo_ref[...])
    gate = mod_ref[:, 2 * D_MODEL:3 * D_MODEL]
    xn = x_ref[...] + gate * out
    if final:
        xn = _rms(xn, fg_ref[...])
    o_ref[...] = xn


def _merge(x, mod, mod_row, layer, p, y_na, y_mla, sgu_w, sgu_b, w_branch, w_out, final_g, final, tm):
    n_tok = x.shape[0]
    half = pl.BlockSpec((tm, BRANCH_WIDTH), lambda i: (i, 0))
    full = lambda c: pl.BlockSpec((tm, D_MODEL), lambda i: (i, c))
    in_specs = [full(0),
                pl.BlockSpec((None, None, 1, 3 * D_MODEL), lambda i: (layer, mod_row(i * tm), 0, 0)),
                half, half,
                pl.BlockSpec((tm // TILE, MERGE_CHUNKS, TILE, CHUNK), lambda i: (i, 0, 0, 0)),
                _const_spec((None, SGU_GROUPS, SGU_CHUNK, SGU_CHUNK), (layer, 0, 0, 0)),
                _const_spec((None, SGU_CHUNK, BRANCH_WIDTH), (layer, 0, 0)),
                _const_spec((None, N_BRANCH, BRANCH_WIDTH, D_MODEL), (layer, 0, 0, 0)),
                _const_spec((None, D_MODEL, D_MODEL), (layer, 0, 0)),
                _const_spec((1, D_MODEL), (0, 0))]
    return pl.pallas_call(
        functools.partial(_merge_kernel, final=final),
        grid=(n_tok // tm,), in_specs=in_specs,
        out_specs=full(0), out_shape=jax.ShapeDtypeStruct((n_tok, D_MODEL), F32),
        compiler_params=_params(("parallel",)), name="merge",
    )(x, mod, y_na, y_mla, p, sgu_w, sgu_b, w_branch, w_out, final_g)


SRC_DQ = 7 * BRANCH_WIDTH
SRC_DKV = SRC_DQ + MLA_Q_LORA
SRC_KR = SRC_DKV + MLA_KV_LORA
SRC_MZ = SRC_KR + MLA_ROPE
SRC_NAZ = 3 * BRANCH_WIDTH
PACK_SHIFT = SRC_MZ % CHUNK
J_SHIFTED = C_MZ // CHUNK
J_Q = C_NAQ // CHUNK
J_DQ = C_DQ // CHUNK
assert SRC_NAZ % CHUNK == 0 and SRC_DQ % CHUNK == 0
assert SRC_DQ + CHUNK + (C_KR - P_COLS) == SRC_KR and SRC_KR + MLA_ROPE == SRC_MZ


def _pack_w_in_kernel(a_ref, b_ref, o_ref):
    j = pl.program_id(1)

    def emit(rows):
        o_ref[...] = rows.T.astype(BF16)

    @pl.when(jnp.logical_or(j < J_SHIFTED, jnp.logical_and(j >= J_Q, j <= J_DQ)))
    def _aligned():
        emit(a_ref[...])

    @pl.when(jnp.logical_and(j >= J_SHIFTED, j < J_Q))
    def _shifted():
        emit(jnp.concatenate([a_ref[PACK_SHIFT:, :], b_ref[:PACK_SHIFT, :]], axis=0))

    @pl.when(j == J_DQ + 1)
    def _tail():
        n_dkv = C_KR - P_COLS
        zeros = lambda n: jnp.zeros((n, D_MODEL), F32)
        emit(jnp.concatenate([a_ref[:n_dkv, :], zeros(ROPE_LANE0), a_ref[n_dkv:n_dkv + MLA_ROPE, :],
                              zeros(LANES - ROPE_LANE0 - MLA_ROPE), zeros(CHUNK - n_dkv - LANES)], axis=0))


def _pack_w_in(w_in_t):
    n_chunks = pl.cdiv(W_COLS, CHUNK)
    shifted0 = (SRC_MZ - PACK_SHIFT) // CHUNK

    def a_block(j):
        return jnp.where(j < J_SHIFTED, j + SRC_NAZ // CHUNK,
                         jnp.where(j < J_Q, j - J_SHIFTED + shifted0,
                                   jnp.where(j < J_DQ, j - J_Q,
                                             jnp.where(j == J_DQ, SRC_DQ // CHUNK, SRC_DQ // CHUNK + 1))))

    def b_block(j):
        shifted = jnp.logical_and(j >= J_SHIFTED, j < J_Q)
        return jnp.where(shifted, j - J_SHIFTED + shifted0 + 1, shifted0 + 1)

    return pl.pallas_call(
        _pack_w_in_kernel, grid=(DEPTH, n_chunks),
        in_specs=[pl.BlockSpec((None, CHUNK, D_MODEL), lambda l, j: (l, a_block(j), 0)),
                  pl.BlockSpec((None, CHUNK, D_MODEL), lambda l, j: (l, b_block(j), 0))],
        out_specs=pl.BlockSpec((None, D_MODEL, CHUNK), lambda l, j: (l, 0, j)),
        out_shape=jax.ShapeDtypeStruct((DEPTH, D_MODEL, W_COLS), BF16),
        compiler_params=_params(("parallel", "arbitrary")), name="pack_w_in",
    )(w_in_t, w_in_t)


def _pack_params(w_in, sgu_w, sgu_b, mla_w_uq, mla_w_ukv, w_branch, w_out):
    w_packed = _pack_w_in(jnp.swapaxes(w_in, 1, 2))
    uq = mla_w_uq.reshape(DEPTH, MLA_Q_LORA, MLA_HEADS, MLA_NOPE + MLA_ROPE)
    uq = jnp.pad(uq, ((0, 0), (0, 0), (0, 0), (0, LANES - MLA_NOPE - MLA_ROPE)))
    uq = uq.reshape(DEPTH, MLA_Q_LORA, MLA_HEADS * LANES).astype(BF16)
    ukv = mla_w_ukv.reshape(DEPTH, MLA_KV_LORA, MLA_HEADS, MLA_NOPE + MLA_V)
    w_k = jnp.pad(ukv[..., :MLA_NOPE], ((0, 0), (0, 0), (0, 0), (0, LANES - MLA_NOPE)))
    w_k = w_k.reshape(DEPTH, MLA_KV_LORA, MLA_HEADS * LANES).astype(BF16)
    w_v = ukv[..., MLA_NOPE:].reshape(DEPTH, MLA_KV_LORA, MLA_HEADS * MLA_V).astype(BF16)
    sgu_bias = jnp.repeat(jnp.swapaxes(sgu_b, 1, 2), BRANCH_WIDTH // SGU_GROUPS, axis=2)
    return w_packed, uq, w_k, w_v, sgu_w.astype(BF16), sgu_bias, w_branch.astype(BF16), w_out.astype(BF16)


def _rope_tables(n_tokens):
    pos = jnp.arange(n_tokens, dtype=jnp.int32)
    row = (pos // GRID_W).astype(F32)
    col = (pos % GRID_W).astype(F32)
    n_freq = MLA_ROPE // 4
    inv = ROPE_THETA ** (-jnp.arange(n_freq, dtype=F32) / n_freq)
    ang = jnp.concatenate([row[:, None] * inv, col[:, None] * inv], axis=-1)
    cos, sin = jnp.cos(ang), jnp.sin(ang)
    pad_l, pad_r = ROPE_LANE0, LANES - ROPE_LANE0 - MLA_ROPE
    cos_t = jnp.pad(jnp.concatenate([cos, cos], axis=1), ((0, 0), (pad_l, pad_r)), constant_values=1.0)
    sin_t = jnp.pad(jnp.concatenate([-sin, sin], axis=1), ((0, 0), (pad_l, pad_r)))
    return cos_t, sin_t


def kernel(x_prompt, x_sample, cache_na_k, cache_na_v, cache_mla_ckv, cache_mla_krope, c, c_ctx, norm_g, w_mod, b_mod, w_in, na_rpb, sgu_w, sgu_b, mla_q_norm, mla_w_uq, mla_kv_norm, mla_w_ukv, w_branch, w_out, final_norm_g):
    n_ctx, ctx_len, _ = x_prompt.shape
    n_lat, lat_len, _ = x_sample.shape
    past = cache_na_k.shape[2]
    assert ctx_len == TILE and lat_len % TILE == 0 and lat_len // GRID_W == 16 and past == TILE

    w_packed, w_uq, w_k, w_v, sgu_wb, sgu_bias, w_br, w_o = _pack_params(
        w_in, sgu_w, sgu_b, mla_w_uq, mla_w_ukv, w_branch, w_out)
    rope_tabs = _rope_tables(lat_len)
    tab = _na_bias_tables(na_rpb)
    norm_g3 = norm_g.reshape(DEPTH, 1, D_MODEL)
    kv_norm3 = mla_kv_norm.reshape(DEPTH, 1, MLA_KV_LORA)
    q_norm3 = mla_q_norm.reshape(DEPTH, 1, MLA_Q_LORA)
    final_g = final_norm_g.reshape(1, D_MODEL)

    cond_rows = 16
    cond = jnp.concatenate([c, c_ctx[None, :], jnp.zeros((cond_rows - n_lat - 1, D_MODEL), F32)], axis=0)
    mod = _modulation(cond, w_mod, b_mod).reshape(DEPTH, cond_rows, 1, 3 * D_MODEL)
    ctx_row = lambda token: n_lat
    lat_row = lambda token: token // lat_len

    cache_k = jnp.transpose(cache_na_k, (0, 1, 3, 4, 2)).reshape(n_lat, DEPTH, BRANCH_WIDTH, past)
    cache_v = jnp.transpose(cache_na_v, (0, 1, 3, 4, 2)).reshape(n_lat, DEPTH, BRANCH_WIDTH, past)
    cache_kr = jnp.pad(cache_mla_krope, ((0, 0), (0, 0), (0, 0), (ROPE_LANE0, LANES - ROPE_LANE0 - MLA_ROPE)))

    xp = x_prompt.reshape(n_ctx * ctx_len, D_MODEL)
    xs = x_sample.reshape(n_lat * lat_len, D_MODEL)
    state = None
    for l in range(DEPTH):
        final = l == DEPTH - 1
        p, ckv, kr, *state = _inproj(xp, mod, ctx_row, l, norm_g3, w_packed, kv_norm3, None, state, True)
        y_na = _na_ctx(p, n_ctx)
        y_mla = _mla(p, ckv, kr, None, None, l, q_norm3, w_uq, w_k, w_v, None, n_ctx, ctx_len)
        xp = _merge(xp, mod, ctx_row, l, p, y_na, y_mla, sgu_wb, sgu_bias, w_br, w_o, final_g, final, MERGE_TILE)
        p, ckv, kr = _inproj(xs, mod, lat_row, l, norm_g3, w_packed, kv_norm3, rope_tabs, None, False)
        y_na = _na_lat(p, cache_k, cache_v, tab, l, n_lat, lat_len)
        y_mla = _mla(p, ckv, kr, cache_mla_ckv, cache_kr, l, q_norm3, w_uq, w_k, w_v, rope_tabs, n_lat, lat_len)
        xs = _merge(xs, mod, lat_row, l, p, y_na, y_mla, sgu_wb, sgu_bias, w_br, w_o, final_g, final, MERGE_TILE)

    s_k, s_v, s_ckv, s_kr = state

    def heads_last(s):
        return jnp.transpose(s.reshape(n_ctx, DEPTH, NA_HEADS, NA_HEAD_DIM, ctx_len), (0, 1, 4, 2, 3))

    return (xp.reshape(n_ctx, ctx_len, D_MODEL), xs.reshape(n_lat, lat_len, D_MODEL),
            heads_last(s_k), heads_last(s_v), s_ckv, jnp.swapaxes(s_kr, 2, 3))
```

```python
import functools

import numpy as np
import jax
import jax.numpy as jnp
from jax import lax
from jax.experimental import pallas as pl
from jax.experimental.pallas import tpu as pltpu

F32 = jnp.float32
BF16 = jnp.bfloat16

D_MODEL = 1024
DEPTH = 2
GRID_W = 64
BRANCH_WIDTH = 512
N_BRANCH = 3
NA_HEADS = 8
NA_HEAD_DIM = 64
NA_WIN_H = 8
NA_WIN_W = 16
SGU_GROUPS = 4
SGU_CHUNK = 128
MLA_HEADS = 8
MLA_NOPE = 64
MLA_ROPE = 32
MLA_V = 64
MLA_Q_LORA = 384
MLA_KV_LORA = 256
ROPE_THETA = 10000.0
EPS = 1e-6
NEG_INF = -1e30

LANES = 128
TILE = 256
MERGE_TILE = 512
CTX_SEQS_PER_STEP = 4
LAT_TILES_PER_STEP = 4
LAT_SEQS_PER_STEP = 2
STRIP = 16
HEAD_PAIRS = NA_HEADS // 2
SCRATCH_SETS = 2
NA_KEY_ROWS = 12
NA_LOC_KEYS = NA_KEY_ROWS * GRID_W
ROWS_PER_TILE = TILE // GRID_W
TAB_PAD = 4
TAB_BLOCKS = 24
ROPE_LANE0 = MLA_NOPE

C_NAZ, C_SU, C_SV, C_SZ, C_MZ, C_MG = (512 * i for i in range(6))
C_NAQ = C_MG + N_BRANCH * D_MODEL
C_NAK, C_NAV, C_DQ = C_NAQ + 512, C_NAQ + 1024, C_NAQ + 1536
MERGE_CHUNKS = C_NAQ // 512
C_DKV = C_DQ + MLA_Q_LORA
C_KR = C_DKV + MLA_KV_LORA
W_COLS = C_KR + LANES
P_COLS = C_DQ + 512
CHUNK = 512

VMEM_LIMIT = 56 * 1024 * 1024


def _dot(a, b):
    return jnp.dot(a, b, preferred_element_type=F32)


def _dot_nt(a, b):
    return lax.dot_general(a, b, (((1,), (1,)), ((), ())), preferred_element_type=F32)


def _rms(x, g):
    return x * lax.rsqrt(jnp.mean(x * x, axis=-1, keepdims=True) + EPS) * g


def _center_norm(x):
    c = x - jnp.mean(x, axis=-1, keepdims=True)
    return c * lax.rsqrt(jnp.mean(c * c, axis=-1, keepdims=True) + EPS)


def _rope128(x, cos, sin):
    lane = lax.broadcasted_iota(jnp.int32, x.shape, 1)
    swapped = jnp.where(lane < ROPE_LANE0 + MLA_ROPE // 2,
                        pltpu.roll(x, LANES - MLA_ROPE // 2, axis=1),
                        pltpu.roll(x, MLA_ROPE // 2, axis=1))
    return x * cos + swapped * sin


def _store_scores(s_ref, m_ref, rows, cols, scores, first):
    s_ref[rows, cols] = scores
    block_max = jnp.broadcast_to(scores.max(axis=-1, keepdims=True), (scores.shape[0], LANES))
    m_ref[rows, :] = block_max if first else jnp.maximum(m_ref[rows, :], block_max)


def _pair_softmax_pv(s_ref, m_ref, p_ref, den_ref, segments, values):
    for i in range(s_ref.shape[0] // STRIP):
        r = slice(i * STRIP, (i + 1) * STRIP)
        m = m_ref[r, :]
        den = None
        for c0, c1 in segments:
            e = jnp.exp(s_ref[r, c0:c1] - jnp.concatenate([m] * ((c1 - c0) // LANES), axis=1))
            d = e.sum(axis=-1, keepdims=True)
            den = d if den is None else den + d
            p_ref[r, c0:c1] = e.astype(BF16)
        den_ref[r, :] = jnp.broadcast_to(den, (STRIP, LANES))
    o = None
    for (c0, c1), (v, transposed) in zip(segments, values):
        t = _dot_nt(p_ref[:, c0:c1], v) if transposed else _dot(p_ref[:, c0:c1], v)
        o = t if o is None else o + t
    o = o / den_ref[...]
    lane = lax.broadcasted_iota(jnp.int32, (TILE, LANES), 1)
    return jnp.where(lane < LANES // 2, o[:TILE], o[TILE:])


def _pair_scratch(n_keys):
    return [pltpu.VMEM((SCRATCH_SETS, 2 * TILE, n_keys), F32), pltpu.VMEM((SCRATCH_SETS, 2 * TILE, LANES), F32),
            pltpu.VMEM((SCRATCH_SETS, 2 * TILE, n_keys), BF16), pltpu.VMEM((SCRATCH_SETS, 2 * TILE, LANES), F32)]


def _const_spec(shape, index):
    return pl.BlockSpec(shape, lambda *_: index, pipeline_mode=pl.Buffered(1))


def _params(semantics):
    return pltpu.CompilerParams(dimension_semantics=semantics, vmem_limit_bytes=VMEM_LIMIT)


def _mod_kernel(c_ref, w_ref, b_ref, o_ref):
    c = c_ref[...]
    s = c * jax.nn.sigmoid(c)
    o_ref[...] = jnp.dot(s, w_ref[...], preferred_element_type=F32, precision=lax.Precision.HIGHEST) + b_ref[...]


def _modulation(cond, w_mod, b_mod):
    rows = cond.shape[0]
    return pl.pallas_call(
        _mod_kernel,
        grid=(DEPTH, 3),
        in_specs=[pl.BlockSpec((rows, D_MODEL), lambda l, j: (0, 0)),
                  pl.BlockSpec((None, D_MODEL, D_MODEL), lambda l, j: (l, 0, j)),
                  pl.BlockSpec((None, 1, D_MODEL), lambda l, j: (l, 0, j))],
        out_specs=pl.BlockSpec((None, rows, D_MODEL), lambda l, j: (l, 0, j)),
        out_shape=jax.ShapeDtypeStruct((DEPTH, rows, 3 * D_MODEL), F32),
        compiler_params=_params(("parallel", "parallel")),
        name="modulation",
    )(cond, w_mod, b_mod.reshape(DEPTH, 1, 3 * D_MODEL))


def _inproj_kernel(*refs, emit_state, rope, n_alias):
    x0_ref, x_next_ref, mod0_ref, mod_next_ref, g_ref, w_ref, kvn_ref = refs[:7]
    refs = refs[7:]
    if rope:
        cos_ref, sin_ref = refs[:2]
        refs = refs[2:]
    refs = refs[n_alias:]
    p_ref, ckv_ref, kr_ref = refs[:3]
    if emit_state:
        sk_ref, sv_ref, sckv_ref, skr_ref = refs[3:7]
    h_ref = refs[-1]
    step = pl.program_id(0)
    tm = x0_ref.shape[0]

    def normalised(x_ref, mod_ref):
        shift = mod_ref[:, 0:D_MODEL]
        scale = mod_ref[:, D_MODEL:2 * D_MODEL]
        return (_rms(x_ref[...], g_ref[...]) * (1.0 + scale) + shift).astype(BF16)

    @pl.when(step == 0)
    def _first_tile():
        h_ref[0] = normalised(x0_ref, mod0_ref)

    h = h_ref[step % 2]

    def tiles(a):
        return a.reshape(tm // TILE, TILE, a.shape[-1])

    ckv = _rms(_dot(h, w_ref[:, C_DKV:C_KR]), kvn_ref[...])
    ckv_ref[...] = ckv.astype(BF16)
    kr = _dot(h, w_ref[:, C_KR:W_COLS])
    if emit_state:
        sckv_ref[...] = ckv
        skr_ref[...] = kr.T[ROPE_LANE0:ROPE_LANE0 + MLA_ROPE, :]
    if rope:
        kr = _rope128(kr, cos_ref[...], sin_ref[...])
    kr_ref[...] = kr.astype(BF16)

    h_ref[(step + 1) % 2] = normalised(x_next_ref, mod_next_ref)

    for c0 in range(0, P_COLS, CHUNK):
        acc = _dot(h, w_ref[:, c0:c0 + CHUNK])
        if emit_state and c0 == C_NAK:
            sk_ref[...] = acc.T
        if emit_state and c0 == C_NAV:
            sv_ref[...] = acc.T
        if c0 in (C_NAZ, C_SZ, C_MZ):
            acc = acc * jax.nn.sigmoid(acc)
        elif c0 == C_SU:
            acc = jax.nn.gelu(acc)
        elif c0 == C_SV:
            acc = _center_norm(jax.nn.gelu(acc))
        elif C_MG <= c0 < C_NAQ:
            acc = jax.nn.sigmoid(acc)
        p_ref[:, c0 // CHUNK] = tiles(acc.astype(BF16))


def _inproj(x, mod, mod_row, layer, norm_g, w_packed, kv_norm, rope_tabs, state_in, emit_state, tm):
    n_tok = x.shape[0]
    n_tiles = n_tok // tm
    rope = rope_tabs is not None
    next_tile = lambda i: jnp.minimum(i + 1, n_tiles - 1)
    in_specs = [_const_spec((tm, D_MODEL), (0, 0)),
                pl.BlockSpec((tm, D_MODEL), lambda i: (next_tile(i), 0)),
                _const_spec((None, None, 1, 3 * D_MODEL), (layer, mod_row(0), 0, 0)),
                pl.BlockSpec((None, None, 1, 3 * D_MODEL), lambda i: (layer, mod_row(next_tile(i) * tm), 0, 0)),
                _const_spec((None, 1, D_MODEL), (layer, 0, 0)),
                _const_spec((None, D_MODEL, W_COLS), (layer, 0, 0)),
                _const_spec((None, 1, MLA_KV_LORA), (layer, 0, 0))]
    args = [x, x, mod, mod, norm_g, w_packed, kv_norm]
    if rope:
        tiles_per_seq = rope_tabs[0].shape[0] // tm
        in_specs += [pl.BlockSpec((tm, LANES), lambda i: (i % tiles_per_seq, 0))] * 2
        args += list(rope_tabs)
    out_specs = [pl.BlockSpec((tm // TILE, P_COLS // CHUNK, TILE, CHUNK), lambda i: (i, 0, 0, 0)),
                 pl.BlockSpec((tm, MLA_KV_LORA), lambda i: (i, 0)),
                 pl.BlockSpec((tm, LANES), lambda i: (i, 0))]
    out_shape = [jax.ShapeDtypeStruct((n_tok // TILE, P_COLS // CHUNK, TILE, CHUNK), BF16),
                 jax.ShapeDtypeStruct((n_tok, MLA_KV_LORA), BF16),
                 jax.ShapeDtypeStruct((n_tok, LANES), BF16)]
    aliases = {}
    if emit_state:
        assert tm == TILE
        for rows, cols in ((BRANCH_WIDTH, TILE), (BRANCH_WIDTH, TILE), (TILE, MLA_KV_LORA), (MLA_ROPE, TILE)):
            out_specs.append(pl.BlockSpec((None, None, rows, cols), lambda i: (i, layer, 0, 0)))
            out_shape.append(jax.ShapeDtypeStruct((n_tok // TILE, DEPTH, rows, cols), F32))
        if state_in is not None:
            first = len(args)
            in_specs += [pl.BlockSpec(memory_space=pl.ANY)] * 4
            args += list(state_in)
            aliases = {first + k: 3 + k for k in range(4)}
    kern = functools.partial(_inproj_kernel, emit_state=emit_state, rope=rope, n_alias=len(aliases))
    return pl.pallas_call(
        kern, grid=(n_tiles,), in_specs=in_specs, out_specs=out_specs, out_shape=out_shape,
        scratch_shapes=[pltpu.VMEM((2, tm, D_MODEL), BF16)],
        input_output_aliases=aliases, compiler_params=_params(("arbitrary",)),
        name="inproj",
    )(*args)


def _na_heads(q_ref, o_ref, scratch, keys_fn, values_fn, bias_fn=None):
    lane = lax.broadcasted_iota(jnp.int32, (TILE, LANES), 1)
    low = lane < NA_HEAD_DIM
    for hp in range(HEAD_PAIRS):
        sl = slice(LANES * hp, LANES * (hp + 1))
        q2 = q_ref[:, sl] * (NA_HEAD_DIM ** -0.5)
        zero = jnp.zeros_like(q2)
        q_stack = jnp.concatenate([jnp.where(low, q2, zero), jnp.where(low, zero, q2)], axis=0)
        s_pair, m_pair, p_pair, den_pair = (ref.at[hp % SCRATCH_SETS] for ref in scratch)
        segments = []
        c0 = 0
        for j, (k, transposed) in enumerate(keys_fn(sl)):
            n_keys = k.shape[1] if transposed else k.shape[0]
            scores = _dot(q_stack, k) if transposed else _dot_nt(q_stack, k)
            if j == 0 and bias_fn is not None:
                scores = scores + bias_fn(2 * hp)
            _store_scores(s_pair, m_pair, slice(None), slice(c0, c0 + n_keys), scores, j == 0)
            segments.append((c0, c0 + n_keys))
            c0 += n_keys
        o_ref[:, sl] = _pair_softmax_pv(s_pair, m_pair, p_pair, den_pair, segments, values_fn(sl)).astype(BF16)


def _na_ctx_kernel(q_ref, k_ref, v_ref, o_ref, *scratch):
    for s in range(q_ref.shape[0]):
        _na_heads(q_ref.at[s], o_ref.at[s], scratch,
                  lambda sl, s=s: [(k_ref[s, :, sl], False)], lambda sl, s=s: [(v_ref[s, :, sl], False)])


def _na_band_row0(tile_row0, n_rows):
    return min(max(tile_row0 - NA_WIN_H // 2, 0), n_rows - NA_KEY_ROWS)


def _na_band_plans(seq):
    n_rows = seq // GRID_W
    plans = []
    for t in range(seq // TILE):
        band0 = _na_band_row0(t * ROWS_PER_TILE, n_rows)
        plan = []
        for r in range(ROWS_PER_TILE):
            rq = t * ROWS_PER_TILE + r
            row_lo = min(max(rq - NA_WIN_H // 2, 0), n_rows - NA_WIN_H)
            plan.append((band0 - rq + NA_WIN_H - 1 + TAB_PAD, row_lo - band0))
        plans.append(tuple(plan))
    return tuple(plans)


def _na_lat_kernel(q_ref, k_ref, v_ref, kc_ref, vc_ref, tab_ref, o_ref, bias_ref, *scratch, plans, n_rows):
    t = pl.program_id(0)
    first_half = lax.broadcasted_iota(jnp.int32, (GRID_W, LANES), 1) < GRID_W
    masked = jnp.full((GRID_W, LANES), NEG_INF, F32)

    def block_pair(h, blk):
        if blk % 2 == 0:
            return tab_ref[h, blk // 2]
        return jnp.where(first_half, pltpu.roll(tab_ref[h, blk // 2], GRID_W, axis=1),
                         pltpu.roll(tab_ref[h, blk // 2 + 1], GRID_W, axis=1))

    for plan in sorted(set(plans)):
        is_tile = functools.reduce(jnp.logical_or, [t == i for i, p in enumerate(plans) if p == plan])

        @pl.when(jnp.logical_and(pl.program_id(1) == 0, is_tile))
        def _build_bias(plan=plan):
            for r, (blk0, j0) in enumerate(plan):
                for j in range(NA_KEY_ROWS // 2):
                    in_window = [j0 <= 2 * j + half < j0 + NA_WIN_H for half in range(2)]
                    for h in range(NA_HEADS):
                        if not any(in_window):
                            piece = masked
                        elif all(in_window):
                            piece = block_pair(h, blk0 + 2 * j)
                        else:
                            keep = first_half if in_window[0] else jnp.logical_not(first_half)
                            piece = jnp.where(keep, block_pair(h, blk0 + 2 * j), NEG_INF)
                        bias_ref[h, r * GRID_W:(r + 1) * GRID_W, LANES * j:LANES * (j + 1)] = piece

    band_tile0 = jnp.clip(t * ROWS_PER_TILE - NA_WIN_H // 2, 0, n_rows - NA_KEY_ROWS) // ROWS_PER_TILE
    for plan in sorted(set(plans)):
        first_row = min(j0 for _, j0 in plan)
        last_row = max(j0 for _, j0 in plan) + NA_WIN_H
        skip = first_row // ROWS_PER_TILE
        n_tiles = -(-last_row // ROWS_PER_TILE) - skip
        key_tiles = pl.ds(band_tile0 + skip, n_tiles)
        bias_lanes = slice(skip * TILE, (skip + n_tiles) * TILE)

        @pl.when(functools.reduce(jnp.logical_or, [t == i for i, p in enumerate(plans) if p == plan]))
        def _attend(key_tiles=key_tiles, bias_lanes=bias_lanes, n_keys=n_tiles * TILE):
            def band(ref, s, sl):
                return ref[s, key_tiles, :, sl].reshape(n_keys, LANES)

            def bias(head0):
                return bias_ref[head0:head0 + 2, :, bias_lanes].reshape(2 * TILE, n_keys)

            for s in range(q_ref.shape[0]):
                _na_heads(q_ref.at[s], o_ref.at[s], scratch,
                          lambda sl, s=s: [(band(k_ref, s, sl), False), (kc_ref[s, sl, :].astype(BF16), True)],
                          lambda sl, s=s: [(band(v_ref, s, sl), False), (vc_ref[s, sl, :].astype(BF16), True)], bias)


def _p_block(tile_of, col):
    return pl.BlockSpec((None, None, TILE, CHUNK), lambda *g: (tile_of(*g), col // CHUNK, 0, 0))


def _na_ctx(p, n_seq):
    per_step = CTX_SEQS_PER_STEP
    blk = lambda col: pl.BlockSpec((per_step, None, TILE, CHUNK), lambda i: (i, col // CHUNK, 0, 0))
    y = pl.pallas_call(
        _na_ctx_kernel, grid=(n_seq // per_step,),
        in_specs=[blk(C_NAQ), blk(C_NAK), blk(C_NAV)],
        out_specs=pl.BlockSpec((per_step, TILE, BRANCH_WIDTH), lambda i: (i, 0, 0)),
        out_shape=jax.ShapeDtypeStruct((n_seq, TILE, BRANCH_WIDTH), BF16),
        scratch_shapes=_pair_scratch(TILE),
        compiler_params=_params(("parallel",)), name="na_ctx",
    )(p, p, p)
    return y.reshape(n_seq * TILE, BRANCH_WIDTH)


def _na_lat(p, cache_k, cache_v, tab, layer, n_seq, seq):
    tiles = seq // TILE
    n_past = cache_k.shape[3]
    per_step = LAT_SEQS_PER_STEP
    p5 = p.reshape(n_seq, tiles, P_COLS // CHUNK, TILE, CHUNK)
    kv = lambda col: pl.BlockSpec((per_step, tiles, None, TILE, CHUNK), lambda t, b: (b, 0, col // CHUNK, 0, 0))
    cache = pl.BlockSpec((per_step, None, BRANCH_WIDTH, n_past), lambda t, b: (b, layer, 0, 0))
    table = _const_spec((None, NA_HEADS, TAB_BLOCKS // 2, GRID_W, LANES), (layer, 0, 0, 0, 0))
    y = pl.pallas_call(
        functools.partial(_na_lat_kernel, plans=_na_band_plans(seq), n_rows=seq // GRID_W),
        grid=(tiles, n_seq // per_step),
        in_specs=[pl.BlockSpec((per_step, None, None, TILE, CHUNK), lambda t, b: (b, t, C_NAQ // CHUNK, 0, 0)),
                  kv(C_NAK), kv(C_NAV), cache, cache, table],
        out_specs=pl.BlockSpec((per_step, None, TILE, BRANCH_WIDTH), lambda t, b: (b, t, 0, 0)),
        out_shape=jax.ShapeDtypeStruct((n_seq, tiles, TILE, BRANCH_WIDTH), BF16),
        scratch_shapes=[pltpu.VMEM((NA_HEADS, TILE, NA_LOC_KEYS), F32)]
        + _pair_scratch(NA_LOC_KEYS + n_past),
        compiler_params=_params(("parallel", "arbitrary")), name="na_lat",
    )(p5, p5, p5, cache_k, cache_v, tab)
    return y.reshape(n_seq * seq, BRANCH_WIDTH)


def _na_bias_tables(rpb):
    cq = np.arange(GRID_W)[:, None]
    ck = np.arange(GRID_W)[None, :]
    col_lo = np.clip(cq - NA_WIN_W // 2, 0, GRID_W - NA_WIN_W)
    ok = (ck >= col_lo) & (ck < col_lo + NA_WIN_W)
    n_b = 2 * NA_WIN_W - 1
    n_rel = 2 * NA_WIN_H - 1
    onehot = ((ck - cq + NA_WIN_W - 1)[:, None, :] == np.arange(n_b)[None, :, None]) & ok[:, None, :]
    expand = np.zeros((GRID_W, 2, n_b, 2, GRID_W), np.float32)
    expand[:, 0, :, 0, :] = onehot
    expand[:, 1, :, 1, :] = onehot
    expand = expand.reshape(GRID_W, 2 * n_b, LANES)
    rows = jnp.pad(rpb, ((0, 0), (0, 0), (TAB_PAD, TAB_BLOCKS - n_rel - TAB_PAD), (0, 0)))
    real = np.zeros(TAB_BLOCKS, bool)
    real[TAB_PAD:TAB_PAD + n_rel] = True
    pairs = rows.reshape(DEPTH, NA_HEADS, TAB_BLOCKS // 2, 2 * n_b)
    tab = jnp.einsum('lhek,ukc->lheuc', pairs, expand, precision=lax.Precision.HIGHEST)
    keep = real.reshape(-1, 1, 2, 1) & ok[None, :, None, :]
    return jnp.where(keep.reshape(TAB_BLOCKS // 2, GRID_W, LANES), tab, NEG_INF)


def _mla_kernel(*refs, n_cache, n_lat, rope, seqs, tiled):
    dq_ref, ckv_ref, kr_ref = refs[:3]
    refs = refs[3:]
    if n_cache:
        cckv_ref, ckr_ref = refs[:2]
        refs = refs[2:]
    qn_ref, wuq_ref, wk_ref, wv_ref = refs[:4]
    refs = refs[4:]
    if rope:
        cos_ref, sin_ref = refs[:2]
        refs = refs[2:]
    o_ref, kx_all, vx_all = refs[:3]
    scratch = refs[3:]
    scale = (MLA_NOPE + MLA_ROPE) ** -0.5
    n_keys = kx_all.shape[1]

    for s in range(seqs):
        key_set = 0 if tiled else s
        kx_ref, vx_ref = kx_all.at[key_set], vx_all.at[key_set]

        def expand_keys(key_set=key_set, kx_ref=kx_ref, vx_ref=vx_ref):
            def fill(r0, ckv, kr):
                n = ckv.shape[0]
                kk = _dot(ckv, wk_ref[...])
                for h in range(MLA_HEADS):
                    sl = slice(LANES * h, LANES * (h + 1))
                    kx_ref[r0:r0 + n, sl] = (kk[:, sl] + kr).astype(BF16)
                vx_ref[r0:r0 + n, :] = _dot(ckv, wv_ref[...]).astype(BF16)

            if n_cache:
                fill(0, cckv_ref[...].astype(BF16), ckr_ref[...])
            for r0 in range(key_set * n_lat, (key_set + 1) * n_lat, TILE):
                fill(n_cache + r0 - key_set * n_lat, ckv_ref[r0:r0 + TILE, :], kr_ref[r0:r0 + TILE, :].astype(F32))

        if not tiled:
            expand_keys()
        elif s == 0:
            pl.when(pl.program_id(1) == 0)(expand_keys)

        tile_rows = slice(s * TILE, (s + 1) * TILE)
        dqn = _rms(dq_ref[s, :, :MLA_Q_LORA].astype(F32), qn_ref[...])
        q = _dot(dqn.astype(BF16), wuq_ref[...])
        for hp in range(HEAD_PAIRS):
            s_pair, m_pair, p_pair, den_pair = (ref.at[hp % SCRATCH_SETS] for ref in scratch)
            for half in range(2):
                sl = slice(LANES * (2 * hp + half), LANES * (2 * hp + half + 1))
                qh = q[:, sl]
                if rope:
                    qh = _rope128(qh, cos_ref[tile_rows, :], sin_ref[tile_rows, :])
                qh = (qh * scale).astype(BF16)
                _store_scores(s_pair, m_pair, slice(half * TILE, (half + 1) * TILE), slice(None),
                              _dot_nt(qh, kx_ref[:, sl]), True)
            v2 = vx_ref[:, LANES * hp:LANES * (hp + 1)]
            o_ref[tile_rows, LANES * hp:LANES * (hp + 1)] = _pair_softmax_pv(
                s_pair, m_pair, p_pair, den_pair, [(0, n_keys)], [(v2, False)]).astype(BF16)


def _mla(p, ckv, kr, cache_ckv, cache_kr, layer, q_norm, w_uq, w_k, w_v, rope_tabs, n_seq, seq):
    tiled = seq > TILE
    seqs = LAT_TILES_PER_STEP if tiled else CTX_SEQS_PER_STEP
    tiles = seq // (seqs * TILE) if tiled else 1
    key_rows = seq if tiled else seqs * seq
    n_cache = 0 if cache_ckv is None else cache_ckv.shape[2]
    rope = rope_tabs is not None
    in_specs = [pl.BlockSpec((seqs, None, TILE, CHUNK), lambda b, t: (b * tiles + t, C_DQ // CHUNK, 0, 0)),
                pl.BlockSpec((key_rows, MLA_KV_LORA), lambda b, t: (b, 0)),
                pl.BlockSpec((key_rows, LANES), lambda b, t: (b, 0))]
    args = [p, ckv, kr]
    if n_cache:
        in_specs += [pl.BlockSpec((None, None, n_cache, MLA_KV_LORA), lambda b, t: (b, layer, 0, 0)),
                     pl.BlockSpec((None, None, n_cache, LANES), lambda b, t: (b, layer, 0, 0))]
        args += [cache_ckv, cache_kr]
    in_specs += [_const_spec((None, 1, MLA_Q_LORA), (layer, 0, 0)),
                 _const_spec((None, MLA_Q_LORA, MLA_HEADS * LANES), (layer, 0, 0)),
                 _const_spec((None, MLA_KV_LORA, MLA_HEADS * LANES), (layer, 0, 0)),
                 _const_spec((None, MLA_KV_LORA, MLA_HEADS * MLA_V), (layer, 0, 0))]
    args += [q_norm, w_uq, w_k, w_v]
    if rope:
        in_specs += [pl.BlockSpec((seqs * TILE, LANES), lambda b, t: (t, 0))] * 2
        args += list(rope_tabs)
    n_keys = n_cache + seq
    key_sets = 1 if tiled else seqs
    return pl.pallas_call(
        functools.partial(_mla_kernel, n_cache=n_cache, n_lat=seq, rope=rope, seqs=seqs, tiled=tiled),
        grid=(n_seq if tiled else n_seq // seqs, tiles), in_specs=in_specs,
        out_specs=pl.BlockSpec((seqs * TILE, BRANCH_WIDTH), lambda b, t: (b * tiles + t, 0)),
        out_shape=jax.ShapeDtypeStruct((n_seq * seq, BRANCH_WIDTH), BF16),
        scratch_shapes=[pltpu.VMEM((key_sets, n_keys, MLA_HEADS * LANES), BF16),
                        pltpu.VMEM((key_sets, n_keys, MLA_HEADS * MLA_V), BF16)] + _pair_scratch(n_keys),
        compiler_params=_params(("parallel", "arbitrary")), name="mla",
    )(*args)


def _merge_kernel(x_ref, mod_ref, yna_ref, ymla_ref, pm_ref, sw_ref, sb_ref, wb_ref, wo_ref, fg_ref, o_ref, *, final):
    tm = x_ref.shape[0]

    def tokens(col):
        return pm_ref[:, col // CHUNK].reshape(tm, CHUNK)

    sv = tokens(C_SV)
    rows = []
    for c0 in range(0, tm, SGU_CHUNK):
        cols = [_dot(sw_ref[g], sv[c0:c0 + SGU_CHUNK, LANES * g:LANES * (g + 1)]) for g in range(SGU_GROUPS)]
        rows.append(jnp.concatenate(cols, axis=1) + sb_ref[...])
    y_sgu = tokens(C_SU).astype(F32) * jnp.concatenate(rows, axis=0)

    gated = (yna_ref[...] * tokens(C_NAZ),
             (y_sgu * tokens(C_SZ).astype(F32)).astype(BF16),
             ymla_ref[...] * tokens(C_MZ))
    merged = None
    for k in range(N_BRANCH):
        gate_k = jnp.concatenate([tokens(C_MG + k * D_MODEL + c) for c in range(0, D_MODEL, CHUNK)], axis=1)
        term = gate_k.astype(F32) * _dot(gated[k], wb_ref[k])
        merged = term if merged is None else merged + term
    out = _dot(merged.astype(BF16), wo_ref[...])
    gate = mod_ref[:, 2 * D_MODEL:3 * D_MODEL]
    xn = x_ref[...] + gate * out
    if final:
        xn = _rms(xn, fg_ref[...])
    o_ref[...] = xn


def _merge(x, mod, mod_row, layer, p, y_na, y_mla, sgu_w, sgu_b, w_branch, w_out, final_g, final, tm):
    n_tok = x.shape[0]
    half = pl.BlockSpec((tm, BRANCH_WIDTH), lambda i: (i, 0))
    full = lambda c: pl.BlockSpec((tm, D_MODEL), lambda i: (i, c))
    in_specs = [full(0),
                pl.BlockSpec((None, None, 1, 3 * D_MODEL), lambda i: (layer, mod_row(i * tm), 0, 0)),
                half, half,
                pl.BlockSpec((tm // TILE, MERGE_CHUNKS, TILE, CHUNK), lambda i: (i, 0, 0, 0)),
                _const_spec((None, SGU_GROUPS, SGU_CHUNK, SGU_CHUNK), (layer, 0, 0, 0)),
                _const_spec((None, SGU_CHUNK, BRANCH_WIDTH), (layer, 0, 0)),
                _const_spec((None, N_BRANCH, BRANCH_WIDTH, D_MODEL), (layer, 0, 0, 0)),
                _const_spec((None, D_MODEL, D_MODEL), (layer, 0, 0)),
                _const_spec((1, D_MODEL), (0, 0))]
    return pl.pallas_call(
        functools.partial(_merge_kernel, final=final),
        grid=(n_tok // tm,), in_specs=in_specs,
        out_specs=full(0), out_shape=jax.ShapeDtypeStruct((n_tok, D_MODEL), F32),
        compiler_params=_params(("parallel",)), name="merge",
    )(x, mod, y_na, y_mla, p, sgu_w, sgu_b, w_branch, w_out, final_g)


SRC_DQ = 7 * BRANCH_WIDTH
SRC_DKV = SRC_DQ + MLA_Q_LORA
SRC_KR = SRC_DKV + MLA_KV_LORA
SRC_MZ = SRC_KR + MLA_ROPE
SRC_NAZ = 3 * BRANCH_WIDTH
PACK_SHIFT = SRC_MZ % CHUNK
J_SHIFTED = C_MZ // CHUNK
J_Q = C_NAQ // CHUNK
J_DQ = C_DQ // CHUNK
assert SRC_NAZ % CHUNK == 0 and SRC_DQ % CHUNK == 0
assert SRC_DQ + CHUNK + (C_KR - P_COLS) == SRC_KR and SRC_KR + MLA_ROPE == SRC_MZ


PACK_SHIFTED = J_Q - J_SHIFTED
PACK_TAIL_SRC = (SRC_MZ - PACK_SHIFT) // CHUNK
assert PACK_TAIL_SRC == SRC_DQ // CHUNK + 1


def _pack_w_in_kernel(c_ref, o_ref, carry_ref):
    k = pl.program_id(1)

    def emit(rows):
        o_ref[...] = rows.T.astype(BF16)

    @pl.when(k == 0)
    def _tail():
        n_dkv = C_KR - P_COLS
        zeros = lambda n: jnp.zeros((n, D_MODEL), F32)
        emit(jnp.concatenate([c_ref[:n_dkv, :], zeros(ROPE_LANE0), c_ref[n_dkv:n_dkv + MLA_ROPE, :],
                              zeros(LANES - ROPE_LANE0 - MLA_ROPE), zeros(CHUNK - n_dkv - LANES)], axis=0))
        carry_ref[...] = c_ref[...]

    @pl.when(jnp.logical_and(k >= 1, k <= PACK_SHIFTED))
    def _shifted():
        emit(jnp.concatenate([carry_ref[PACK_SHIFT:, :], c_ref[:PACK_SHIFT, :]], axis=0))
        carry_ref[...] = c_ref[...]

    @pl.when(k > PACK_SHIFTED)
    def _aligned():
        emit(c_ref[...])


def _pack_w_in(w_in_t):
    n_chunks = pl.cdiv(W_COLS, CHUNK)
    assert n_chunks == J_DQ + 2
    first_plain = PACK_SHIFTED + 1
    first_q = first_plain + J_SHIFTED
    last = n_chunks - 1

    def dst_chunk(k):
        return jnp.where(k == 0, J_DQ + 1,
                         jnp.where(k < first_plain, k - 1 + J_SHIFTED,
                                   jnp.where(k < first_q, k - first_plain,
                                             jnp.where(k < last, k - first_q + J_Q, J_DQ))))

    def src_chunk(k):
        return jnp.where(k < first_plain, k + PACK_TAIL_SRC,
                         jnp.where(k < first_q, k - first_plain + SRC_NAZ // CHUNK,
                                   jnp.where(k < last, k - first_q, SRC_DQ // CHUNK)))

    return pl.pallas_call(
        _pack_w_in_kernel, grid=(DEPTH, n_chunks),
        in_specs=[pl.BlockSpec((None, CHUNK, D_MODEL), lambda l, k: (l, src_chunk(k), 0))],
        out_specs=pl.BlockSpec((None, D_MODEL, CHUNK), lambda l, k: (l, 0, dst_chunk(k))),
        out_shape=jax.ShapeDtypeStruct((DEPTH, D_MODEL, W_COLS), BF16),
        scratch_shapes=[pltpu.VMEM((CHUNK, D_MODEL), F32)],
        compiler_params=_params(("parallel", "arbitrary")), name="pack_w_in",
    )(w_in_t)


def _pack_params(w_in, sgu_w, sgu_b, mla_w_uq, mla_w_ukv, w_branch, w_out):
    w_packed = _pack_w_in(jnp.swapaxes(w_in, 1, 2))
    uq = mla_w_uq.reshape(DEPTH, MLA_Q_LORA, MLA_HEADS, MLA_NOPE + MLA_ROPE)
    uq = jnp.pad(uq, ((0, 0), (0, 0), (0, 0), (0, LANES - MLA_NOPE - MLA_ROPE)))
    uq = uq.reshape(DEPTH, MLA_Q_LORA, MLA_HEADS * LANES).astype(BF16)
    ukv = mla_w_ukv.reshape(DEPTH, MLA_KV_LORA, MLA_HEADS, MLA_NOPE + MLA_V)
    w_k = jnp.pad(ukv[..., :MLA_NOPE], ((0, 0), (0, 0), (0, 0), (0, LANES - MLA_NOPE)))
    w_k = w_k.reshape(DEPTH, MLA_KV_LORA, MLA_HEADS * LANES).astype(BF16)
    w_v = ukv[..., MLA_NOPE:].reshape(DEPTH, MLA_KV_LORA, MLA_HEADS * MLA_V).astype(BF16)
    sgu_bias = jnp.repeat(jnp.swapaxes(sgu_b, 1, 2), BRANCH_WIDTH // SGU_GROUPS, axis=2)
    return w_packed, uq, w_k, w_v, sgu_w.astype(BF16), sgu_bias, w_branch.astype(BF16), w_out.astype(BF16)


def _rope_tables(n_tokens):
    pos = jnp.arange(n_tokens, dtype=jnp.int32)
    row = (pos // GRID_W).astype(F32)
    col = (pos % GRID_W).astype(F32)
    n_freq = MLA_ROPE // 4
    inv = ROPE_THETA ** (-jnp.arange(n_freq, dtype=F32) / n_freq)
    ang = jnp.concatenate([row[:, None] * inv, col[:, None] * inv], axis=-1)
    cos, sin = jnp.cos(ang), jnp.sin(ang)
    pad_l, pad_r = ROPE_LANE0, LANES - ROPE_LANE0 - MLA_ROPE
    cos_t = jnp.pad(jnp.concatenate([cos, cos], axis=1), ((0, 0), (pad_l, pad_r)), constant_values=1.0)
    sin_t = jnp.pad(jnp.concatenate([-sin, sin], axis=1), ((0, 0), (pad_l, pad_r)))
    return cos_t, sin_t


def kernel(x_prompt, x_sample, cache_na_k, cache_na_v, cache_mla_ckv, cache_mla_krope, c, c_ctx, norm_g, w_mod, b_mod, w_in, na_rpb, sgu_w, sgu_b, mla_q_norm, mla_w_uq, mla_kv_norm, mla_w_ukv, w_branch, w_out, final_norm_g):
    n_ctx, ctx_len, _ = x_prompt.shape
    n_lat, lat_len, _ = x_sample.shape
    past = cache_na_k.shape[2]
    assert ctx_len == TILE and lat_len % TILE == 0 and lat_len // GRID_W == 16 and past == TILE

    w_packed, w_uq, w_k, w_v, sgu_wb, sgu_bias, w_br, w_o = _pack_params(
        w_in, sgu_w, sgu_b, mla_w_uq, mla_w_ukv, w_branch, w_out)
    rope_tabs = _rope_tables(lat_len)
    tab = _na_bias_tables(na_rpb)
    norm_g3 = norm_g.reshape(DEPTH, 1, D_MODEL)
    kv_norm3 = mla_kv_norm.reshape(DEPTH, 1, MLA_KV_LORA)
    q_norm3 = mla_q_norm.reshape(DEPTH, 1, MLA_Q_LORA)
    final_g = final_norm_g.reshape(1, D_MODEL)

    cond_rows = 16
    cond = jnp.concatenate([c, c_ctx[None, :], jnp.zeros((cond_rows - n_lat - 1, D_MODEL), F32)], axis=0)
    mod = _modulation(cond, w_mod, b_mod).reshape(DEPTH, cond_rows, 1, 3 * D_MODEL)
    ctx_row = lambda token: n_lat
    lat_row = lambda token: token // lat_len

    cache_k = jnp.transpose(cache_na_k, (0, 1, 3, 4, 2)).reshape(n_lat, DEPTH, BRANCH_WIDTH, past)
    cache_v = jnp.transpose(cache_na_v, (0, 1, 3, 4, 2)).reshape(n_lat, DEPTH, BRANCH_WIDTH, past)
    cache_kr = jnp.pad(cache_mla_krope, ((0, 0), (0, 0), (0, 0), (ROPE_LANE0, LANES - ROPE_LANE0 - MLA_ROPE)))

    xp = x_prompt.reshape(n_ctx * ctx_len, D_MODEL)
    xs = x_sample.reshape(n_lat * lat_len, D_MODEL)
    state = None
    for l in range(DEPTH):
        final = l == DEPTH - 1
        p, ckv, kr, *state = _inproj(xp, mod, ctx_row, l, norm_g3, w_packed, kv_norm3, None, state, True, TILE)
        y_na = _na_ctx(p, n_ctx)
        y_mla = _mla(p, ckv, kr, None, None, l, q_norm3, w_uq, w_k, w_v, None, n_ctx, ctx_len)
        xp = _merge(xp, mod, ctx_row, l, p, y_na, y_mla, sgu_wb, sgu_bias, w_br, w_o, final_g, final, MERGE_TILE)
        p, ckv, kr = _inproj(xs, mod, lat_row, l, norm_g3, w_packed, kv_norm3, rope_tabs, None, False, TILE)
        y_na = _na_lat(p, cache_k, cache_v, tab, l, n_lat, lat_len)
        y_mla = _mla(p, ckv, kr, cache_mla_ckv, cache_kr, l, q_norm3, w_uq, w_k, w_v, rope_tabs, n_lat, lat_len)
        xs = _merge(xs, mod, lat_row, l, p, y_na, y_mla, sgu_wb, sgu_bias, w_br, w_o, final_g, final, MERGE_TILE)

    s_k, s_v, s_ckv, s_kr = state

    def heads_last(s):
        return jnp.transpose(s.reshape(n_ctx, DEPTH, NA_HEADS, NA_HEAD_DIM, ctx_len), (0, 1, 4, 2, 3))

    return (xp.reshape(n_ctx, ctx_len, D_MODEL), xs.reshape(n_lat, lat_len, D_MODEL),
            heads_last(s_k), heads_last(s_v), s_ckv, jnp.swapaxes(s_kr, 2, 3))
```

```python
import functools

import numpy as np
import jax
import jax.numpy as jnp
from jax import lax
from jax.experimental import pallas as pl
from jax.experimental.pallas import tpu as pltpu

F32 = jnp.float32
BF16 = jnp.bfloat16

D_MODEL = 1024
DEPTH = 2
GRID_W = 64
BRANCH_WIDTH = 512
N_BRANCH = 3
NA_HEADS = 8
NA_HEAD_DIM = 64
NA_WIN_H = 8
NA_WIN_W = 16
SGU_GROUPS = 4
SGU_CHUNK = 128
MLA_HEADS = 8
MLA_NOPE = 64
MLA_ROPE = 32
MLA_V = 64
MLA_Q_LORA = 384
MLA_KV_LORA = 256
ROPE_THETA = 10000.0
EPS = 1e-6
NEG_INF = -1e30

LANES = 128
TILE = 256
MERGE_TILE = 512
CTX_SEQS_PER_STEP = 4
LAT_TILES_PER_STEP = 4
LAT_SEQS_PER_STEP = 2
STRIP = 16
HEAD_PAIRS = NA_HEADS // 2
SCRATCH_SETS = 2
NA_KEY_ROWS = 12
NA_LOC_KEYS = NA_KEY_ROWS * GRID_W
ROWS_PER_TILE = TILE // GRID_W
TAB_PAD = 4
TAB_BLOCKS = 24
ROPE_LANE0 = MLA_NOPE

C_NAZ, C_SU, C_SV, C_SZ, C_MZ, C_MG = (512 * i for i in range(6))
C_NAQ = C_MG + N_BRANCH * D_MODEL
C_NAK, C_NAV, C_DQ = C_NAQ + 512, C_NAQ + 1024, C_NAQ + 1536
MERGE_CHUNKS = C_NAQ // 512
C_DKV = C_DQ + MLA_Q_LORA
C_KR = C_DKV + MLA_KV_LORA
W_COLS = C_KR + LANES
P_COLS = C_DQ + 512
CHUNK = 512

VMEM_LIMIT = 56 * 1024 * 1024


def _dot(a, b):
    return jnp.dot(a, b, preferred_element_type=F32)


def _dot_nt(a, b):
    return lax.dot_general(a, b, (((1,), (1,)), ((), ())), preferred_element_type=F32)


def _rms(x, g):
    return x * lax.rsqrt(jnp.mean(x * x, axis=-1, keepdims=True) + EPS) * g


def _center_norm(x):
    c = x - jnp.mean(x, axis=-1, keepdims=True)
    return c * lax.rsqrt(jnp.mean(c * c, axis=-1, keepdims=True) + EPS)


def _rope128(x, cos, sin):
    lane = lax.broadcasted_iota(jnp.int32, x.shape, 1)
    swapped = jnp.where(lane < ROPE_LANE0 + MLA_ROPE // 2,
                        pltpu.roll(x, LANES - MLA_ROPE // 2, axis=1),
                        pltpu.roll(x, MLA_ROPE // 2, axis=1))
    return x * cos + swapped * sin


def _store_scores(s_ref, m_ref, rows, cols, scores, first):
    s_ref[rows, cols] = scores
    block_max = jnp.broadcast_to(scores.max(axis=-1, keepdims=True), (scores.shape[0], LANES))
    m_ref[rows, :] = block_max if first else jnp.maximum(m_ref[rows, :], block_max)


def _pair_softmax_pv(s_ref, m_ref, p_ref, den_ref, segments, values):
    for i in range(s_ref.shape[0] // STRIP):
        r = slice(i * STRIP, (i + 1) * STRIP)
        m = m_ref[r, :]
        den = None
        for c0, c1 in segments:
            e = jnp.exp(s_ref[r, c0:c1] - jnp.concatenate([m] * ((c1 - c0) // LANES), axis=1))
            d = e.sum(axis=-1, keepdims=True)
            den = d if den is None else den + d
            p_ref[r, c0:c1] = e.astype(BF16)
        den_ref[r, :] = jnp.broadcast_to(den, (STRIP, LANES))
    o = None
    for (c0, c1), (v, transposed) in zip(segments, values):
        t = _dot_nt(p_ref[:, c0:c1], v) if transposed else _dot(p_ref[:, c0:c1], v)
        o = t if o is None else o + t
    o = o / den_ref[...]
    lane = lax.broadcasted_iota(jnp.int32, (TILE, LANES), 1)
    return jnp.where(lane < LANES // 2, o[:TILE], o[TILE:])


def _pair_scratch(n_keys):
    return [pltpu.VMEM((SCRATCH_SETS, 2 * TILE, n_keys), F32), pltpu.VMEM((SCRATCH_SETS, 2 * TILE, LANES), F32),
            pltpu.VMEM((SCRATCH_SETS, 2 * TILE, n_keys), BF16), pltpu.VMEM((SCRATCH_SETS, 2 * TILE, LANES), F32)]


def _const_spec(shape, index):
    return pl.BlockSpec(shape, lambda *_: index, pipeline_mode=pl.Buffered(1))


def _params(semantics):
    return pltpu.CompilerParams(dimension_semantics=semantics, vmem_limit_bytes=VMEM_LIMIT)


def _mod_kernel(c_ref, w_ref, b_ref, o_ref):
    c = c_ref[...]
    s = c * jax.nn.sigmoid(c)
    o_ref[...] = jnp.dot(s, w_ref[...], preferred_element_type=F32, precision=lax.Precision.HIGHEST) + b_ref[...]


def _modulation(cond, w_mod, b_mod):
    rows = cond.shape[0]
    return pl.pallas_call(
        _mod_kernel,
        grid=(DEPTH, 3),
        in_specs=[pl.BlockSpec((rows, D_MODEL), lambda l, j: (0, 0)),
                  pl.BlockSpec((None, D_MODEL, D_MODEL), lambda l, j: (l, 0, j)),
                  pl.BlockSpec((None, 1, D_MODEL), lambda l, j: (l, 0, j))],
        out_specs=pl.BlockSpec((None, rows, D_MODEL), lambda l, j: (l, 0, j)),
        out_shape=jax.ShapeDtypeStruct((DEPTH, rows, 3 * D_MODEL), F32),
        compiler_params=_params(("parallel", "parallel")),
        name="modulation",
    )(cond, w_mod, b_mod.reshape(DEPTH, 1, 3 * D_MODEL))


def _inproj_kernel(*refs, emit_state, rope, n_alias):
    x0_ref, x_next_ref, mod0_ref, mod_next_ref, g_ref, w_ref, kvn_ref = refs[:7]
    refs = refs[7:]
    if rope:
        cos_ref, sin_ref = refs[:2]
        refs = refs[2:]
    refs = refs[n_alias:]
    p_ref, ckv_ref, kr_ref = refs[:3]
    if emit_state:
        sk_ref, sv_ref, sckv_ref, skr_ref = refs[3:7]
    h_ref = refs[-1]
    step = pl.program_id(0)
    tm = x0_ref.shape[0]

    def normalised(x_ref, mod_ref):
        shift = mod_ref[:, 0:D_MODEL]
        scale = mod_ref[:, D_MODEL:2 * D_MODEL]
        return (_rms(x_ref[...], g_ref[...]) * (1.0 + scale) + shift).astype(BF16)

    @pl.when(step == 0)
    def _first_tile():
        h_ref[0] = normalised(x0_ref, mod0_ref)

    h = h_ref[step % 2]

    def tiles(a):
        return a.reshape(tm // TILE, TILE, a.shape[-1])

    ckv = _rms(_dot(h, w_ref[:, C_DKV:C_KR]), kvn_ref[...])
    ckv_ref[...] = ckv.astype(BF16)
    kr = _dot(h, w_ref[:, C_KR:W_COLS])
    if emit_state:
        sckv_ref[...] = ckv
        skr_ref[...] = kr.T[ROPE_LANE0:ROPE_LANE0 + MLA_ROPE, :]
    if rope:
        kr = _rope128(kr, cos_ref[...], sin_ref[...])
    kr_ref[...] = kr.astype(BF16)

    h_ref[(step + 1) % 2] = normalised(x_next_ref, mod_next_ref)

    for c0 in range(0, P_COLS, CHUNK):
        acc = _dot(h, w_ref[:, c0:c0 + CHUNK])
        if emit_state and c0 == C_NAK:
            sk_ref[...] = acc.T
        if emit_state and c0 == C_NAV:
            sv_ref[...] = acc.T
        if c0 in (C_NAZ, C_SZ, C_MZ):
            acc = acc * jax.nn.sigmoid(acc)
        elif c0 == C_SU:
            acc = jax.nn.gelu(acc)
        elif c0 == C_SV:
            acc = _center_norm(jax.nn.gelu(acc))
        elif C_MG <= c0 < C_NAQ:
            acc = jax.nn.sigmoid(acc)
        p_ref[:, c0 // CHUNK] = tiles(acc.astype(BF16))


def _inproj(x, mod, mod_row, layer, norm_g, w_packed, kv_norm, rope_tabs, state_in, emit_state, tm):
    n_tok = x.shape[0]
    n_tiles = n_tok // tm
    rope = rope_tabs is not None
    next_tile = lambda i: jnp.minimum(i + 1, n_tiles - 1)
    in_specs = [_const_spec((tm, D_MODEL), (0, 0)),
                pl.BlockSpec((tm, D_MODEL), lambda i: (next_tile(i), 0)),
                _const_spec((None, None, 1, 3 * D_MODEL), (layer, mod_row(0), 0, 0)),
                pl.BlockSpec((None, None, 1, 3 * D_MODEL), lambda i: (layer, mod_row(next_tile(i) * tm), 0, 0)),
                _const_spec((None, 1, D_MODEL), (layer, 0, 0)),
                _const_spec((None, D_MODEL, W_COLS), (layer, 0, 0)),
                _const_spec((None, 1, MLA_KV_LORA), (layer, 0, 0))]
    args = [x, x, mod, mod, norm_g, w_packed, kv_norm]
    if rope:
        tiles_per_seq = rope_tabs[0].shape[0] // tm
        in_specs += [pl.BlockSpec((tm, LANES), lambda i: (i % tiles_per_seq, 0))] * 2
        args += list(rope_tabs)
    out_specs = [pl.BlockSpec((tm // TILE, P_COLS // CHUNK, TILE, CHUNK), lambda i: (i, 0, 0, 0)),
                 pl.BlockSpec((tm, MLA_KV_LORA), lambda i: (i, 0)),
                 pl.BlockSpec((tm, LANES), lambda i: (i, 0))]
    out_shape = [jax.ShapeDtypeStruct((n_tok // TILE, P_COLS // CHUNK, TILE, CHUNK), BF16),
                 jax.ShapeDtypeStruct((n_tok, MLA_KV_LORA), BF16),
                 jax.ShapeDtypeStruct((n_tok, LANES), BF16)]
    aliases = {}
    if emit_state:
        assert tm == TILE
        for rows, cols in ((BRANCH_WIDTH, TILE), (BRANCH_WIDTH, TILE), (TILE, MLA_KV_LORA), (MLA_ROPE, TILE)):
            out_specs.append(pl.BlockSpec((None, None, rows, cols), lambda i: (i, layer, 0, 0)))
            out_shape.append(jax.ShapeDtypeStruct((n_tok // TILE, DEPTH, rows, cols), F32))
        if state_in is not None:
            first = len(args)
            in_specs += [pl.BlockSpec(memory_space=pl.ANY)] * 4
            args += list(state_in)
            aliases = {first + k: 3 + k for k in range(4)}
    kern = functools.partial(_inproj_kernel, emit_state=emit_state, rope=rope, n_alias=len(aliases))
    return pl.pallas_call(
        kern, grid=(n_tiles,), in_specs=in_specs, out_specs=out_specs, out_shape=out_shape,
        scratch_shapes=[pltpu.VMEM((2, tm, D_MODEL), BF16)],
        input_output_aliases=aliases, compiler_params=_params(("arbitrary",)),
        name="inproj",
    )(*args)


def _na_heads(q_ref, o_ref, scratch, keys_fn, values_fn, bias_fn=None):
    lane = lax.broadcasted_iota(jnp.int32, (TILE, LANES), 1)
    low = lane < NA_HEAD_DIM
    for hp in range(HEAD_PAIRS):
        sl = slice(LANES * hp, LANES * (hp + 1))
        q2 = q_ref[:, sl] * (NA_HEAD_DIM ** -0.5)
        zero = jnp.zeros_like(q2)
        q_stack = jnp.concatenate([jnp.where(low, q2, zero), jnp.where(low, zero, q2)], axis=0)
        s_pair, m_pair, p_pair, den_pair = (ref.at[hp % SCRATCH_SETS] for ref in scratch)
        segments = []
        c0 = 0
        for j, (k, transposed) in enumerate(keys_fn(sl)):
            n_keys = k.shape[1] if transposed else k.shape[0]
            scores = _dot(q_stack, k) if transposed else _dot_nt(q_stack, k)
            if j == 0 and bias_fn is not None:
                scores = scores + bias_fn(2 * hp)
            _store_scores(s_pair, m_pair, slice(None), slice(c0, c0 + n_keys), scores, j == 0)
            segments.append((c0, c0 + n_keys))
            c0 += n_keys
        o_ref[:, sl] = _pair_softmax_pv(s_pair, m_pair, p_pair, den_pair, segments, values_fn(sl)).astype(BF16)


def _na_ctx_kernel(q_ref, k_ref, v_ref, o_ref, *scratch):
    for s in range(q_ref.shape[0]):
        _na_heads(q_ref.at[s], o_ref.at[s], scratch,
                  lambda sl, s=s: [(k_ref[s, :, sl], False)], lambda sl, s=s: [(v_ref[s, :, sl], False)])


def _na_band_row0(tile_row0, n_rows):
    return min(max(tile_row0 - NA_WIN_H // 2, 0), n_rows - NA_KEY_ROWS)


def _na_band_plans(seq):
    n_rows = seq // GRID_W
    plans = []
    for t in range(seq // TILE):
        band0 = _na_band_row0(t * ROWS_PER_TILE, n_rows)
        plan = []
        for r in range(ROWS_PER_TILE):
            rq = t * ROWS_PER_TILE + r
            row_lo = min(max(rq - NA_WIN_H // 2, 0), n_rows - NA_WIN_H)
            plan.append((band0 - rq + NA_WIN_H - 1 + TAB_PAD, row_lo - band0))
        plans.append(tuple(plan))
    return tuple(plans)


def _na_lat_kernel(q_ref, k_ref, v_ref, kc_ref, vc_ref, tab_ref, o_ref, bias_ref, *scratch, plans, n_rows):
    t = pl.program_id(0)
    first_half = lax.broadcasted_iota(jnp.int32, (GRID_W, LANES), 1) < GRID_W
    masked = jnp.full((GRID_W, LANES), NEG_INF, F32)

    def block_pair(h, blk):
        if blk % 2 == 0:
            return tab_ref[h, blk // 2]
        return jnp.where(first_half, pltpu.roll(tab_ref[h, blk // 2], GRID_W, axis=1),
                         pltpu.roll(tab_ref[h, blk // 2 + 1], GRID_W, axis=1))

    for plan in sorted(set(plans)):
        is_tile = functools.reduce(jnp.logical_or, [t == i for i, p in enumerate(plans) if p == plan])

        @pl.when(jnp.logical_and(pl.program_id(1) == 0, is_tile))
        def _build_bias(plan=plan):
            for r, (blk0, j0) in enumerate(plan):
                for j in range(NA_KEY_ROWS // 2):
                    in_window = [j0 <= 2 * j + half < j0 + NA_WIN_H for half in range(2)]
                    for h in range(NA_HEADS):
                        if not any(in_window):
                            piece = masked
                        elif all(in_window):
                            piece = block_pair(h, blk0 + 2 * j)
                        else:
                            keep = first_half if in_window[0] else jnp.logical_not(first_half)
                            piece = jnp.where(keep, block_pair(h, blk0 + 2 * j), NEG_INF)
                        bias_ref[h, r * GRID_W:(r + 1) * GRID_W, LANES * j:LANES * (j + 1)] = piece

    band_tile0 = jnp.clip(t * ROWS_PER_TILE - NA_WIN_H // 2, 0, n_rows - NA_KEY_ROWS) // ROWS_PER_TILE
    for plan in sorted(set(plans)):
        first_row = min(j0 for _, j0 in plan)
        last_row = max(j0 for _, j0 in plan) + NA_WIN_H
        skip = first_row // ROWS_PER_TILE
        n_tiles = -(-last_row // ROWS_PER_TILE) - skip
        key_tiles = pl.ds(band_tile0 + skip, n_tiles)
        bias_lanes = slice(skip * TILE, (skip + n_tiles) * TILE)

        @pl.when(functools.reduce(jnp.logical_or, [t == i for i, p in enumerate(plans) if p == plan]))
        def _attend(key_tiles=key_tiles, bias_lanes=bias_lanes, n_keys=n_tiles * TILE):
            def band(ref, s, sl):
                return ref[s, key_tiles, :, sl].reshape(n_keys, LANES)

            def bias(head0):
                return bias_ref[head0:head0 + 2, :, bias_lanes].reshape(2 * TILE, n_keys)

            for s in range(q_ref.shape[0]):
                _na_heads(q_ref.at[s], o_ref.at[s], scratch,
                          lambda sl, s=s: [(band(k_ref, s, sl), False), (kc_ref[s, sl, :].astype(BF16), True)],
                          lambda sl, s=s: [(band(v_ref, s, sl), False), (vc_ref[s, sl, :].astype(BF16), True)], bias)


def _p_block(tile_of, col):
    return pl.BlockSpec((None, None, TILE, CHUNK), lambda *g: (tile_of(*g), col // CHUNK, 0, 0))


def _na_ctx(p, n_seq):
    per_step = CTX_SEQS_PER_STEP
    blk = lambda col: pl.BlockSpec((per_step, None, TILE, CHUNK), lambda i: (i, col // CHUNK, 0, 0))
    y = pl.pallas_call(
        _na_ctx_kernel, grid=(n_seq // per_step,),
        in_specs=[blk(C_NAQ), blk(C_NAK), blk(C_NAV)],
        out_specs=pl.BlockSpec((per_step, TILE, BRANCH_WIDTH), lambda i: (i, 0, 0)),
        out_shape=jax.ShapeDtypeStruct((n_seq, TILE, BRANCH_WIDTH), BF16),
        scratch_shapes=_pair_scratch(TILE),
        compiler_params=_params(("parallel",)), name="na_ctx",
    )(p, p, p)
    return y.reshape(n_seq * TILE, BRANCH_WIDTH)


def _na_lat(p, cache_k, cache_v, tab, layer, n_seq, seq):
    tiles = seq // TILE
    n_past = cache_k.shape[3]
    per_step = LAT_SEQS_PER_STEP
    p5 = p.reshape(n_seq, tiles, P_COLS // CHUNK, TILE, CHUNK)
    kv = lambda col: pl.BlockSpec((per_step, tiles, None, TILE, CHUNK), lambda t, b: (b, 0, col // CHUNK, 0, 0))
    cache = pl.BlockSpec((per_step, None, BRANCH_WIDTH, n_past), lambda t, b: (b, layer, 0, 0))
    table = _const_spec((None, NA_HEADS, TAB_BLOCKS // 2, GRID_W, LANES), (layer, 0, 0, 0, 0))
    y = pl.pallas_call(
        functools.partial(_na_lat_kernel, plans=_na_band_plans(seq), n_rows=seq // GRID_W),
        grid=(tiles, n_seq // per_step),
        in_specs=[pl.BlockSpec((per_step, None, None, TILE, CHUNK), lambda t, b: (b, t, C_NAQ // CHUNK, 0, 0)),
                  kv(C_NAK), kv(C_NAV), cache, cache, table],
        out_specs=pl.BlockSpec((per_step, None, TILE, BRANCH_WIDTH), lambda t, b: (b, t, 0, 0)),
        out_shape=jax.ShapeDtypeStruct((n_seq, tiles, TILE, BRANCH_WIDTH), BF16),
        scratch_shapes=[pltpu.VMEM((NA_HEADS, TILE, NA_LOC_KEYS), F32)]
        + _pair_scratch(NA_LOC_KEYS + n_past),
        compiler_params=_params(("parallel", "arbitrary")), name="na_lat",
    )(p5, p5, p5, cache_k, cache_v, tab)
    return y.reshape(n_seq * seq, BRANCH_WIDTH)


def _na_bias_tables(rpb):
    cq = np.arange(GRID_W)[:, None]
    ck = np.arange(GRID_W)[None, :]
    col_lo = np.clip(cq - NA_WIN_W // 2, 0, GRID_W - NA_WIN_W)
    ok = (ck >= col_lo) & (ck < col_lo + NA_WIN_W)
    n_b = 2 * NA_WIN_W - 1
    n_rel = 2 * NA_WIN_H - 1
    onehot = ((ck - cq + NA_WIN_W - 1)[:, None, :] == np.arange(n_b)[None, :, None]) & ok[:, None, :]
    expand = np.zeros((GRID_W, 2, n_b, 2, GRID_W), np.float32)
    expand[:, 0, :, 0, :] = onehot
    expand[:, 1, :, 1, :] = onehot
    expand = expand.reshape(GRID_W, 2 * n_b, LANES)
    rows = jnp.pad(rpb, ((0, 0), (0, 0), (TAB_PAD, TAB_BLOCKS - n_rel - TAB_PAD), (0, 0)))
    real = np.zeros(TAB_BLOCKS, bool)
    real[TAB_PAD:TAB_PAD + n_rel] = True
    pairs = rows.reshape(DEPTH, NA_HEADS, TAB_BLOCKS // 2, 2 * n_b)
    tab = jnp.einsum('lhek,ukc->lheuc', pairs, expand, precision=lax.Precision.HIGHEST)
    keep = real.reshape(-1, 1, 2, 1) & ok[None, :, None, :]
    return jnp.where(keep.reshape(TAB_BLOCKS // 2, GRID_W, LANES), tab, NEG_INF)


def _mla_kernel(*refs, n_cache, n_lat, rope, seqs, tiled):
    dq_ref, ckv_ref, kr_ref = refs[:3]
    refs = refs[3:]
    if n_cache:
        cckv_ref, ckr_ref = refs[:2]
        refs = refs[2:]
    qn_ref, wuq_ref, wk_ref, wv_ref = refs[:4]
    refs = refs[4:]
    if rope:
        cos_ref, sin_ref = refs[:2]
        refs = refs[2:]
    o_ref, kx_all, vx_all = refs[:3]
    scratch = refs[3:]
    scale = (MLA_NOPE + MLA_ROPE) ** -0.5
    n_keys = kx_all.shape[1]

    for s in range(seqs):
        key_set = 0 if tiled else s
        kx_ref, vx_ref = kx_all.at[key_set], vx_all.at[key_set]

        def expand_keys(key_set=key_set, kx_ref=kx_ref, vx_ref=vx_ref):
            def fill(r0, ckv, kr):
                n = ckv.shape[0]
                kk = _dot(ckv, wk_ref[...])
                for h in range(MLA_HEADS):
                    sl = slice(LANES * h, LANES * (h + 1))
                    kx_ref[r0:r0 + n, sl] = (kk[:, sl] + kr).astype(BF16)
                vx_ref[r0:r0 + n, :] = _dot(ckv, wv_ref[...]).astype(BF16)

            if n_cache:
                fill(0, cckv_ref[...].astype(BF16), ckr_ref[...])
            for r0 in range(key_set * n_lat, (key_set + 1) * n_lat, TILE):
                fill(n_cache + r0 - key_set * n_lat, ckv_ref[r0:r0 + TILE, :], kr_ref[r0:r0 + TILE, :].astype(F32))

        if not tiled:
            expand_keys()
        elif s == 0:
            pl.when(pl.program_id(1) == 0)(expand_keys)

        tile_rows = slice(s * TILE, (s + 1) * TILE)
        dqn = _rms(dq_ref[s, :, :MLA_Q_LORA].astype(F32), qn_ref[...])
        q = _dot(dqn.astype(BF16), wuq_ref[...])
        for hp in range(HEAD_PAIRS):
            s_pair, m_pair, p_pair, den_pair = (ref.at[hp % SCRATCH_SETS] for ref in scratch)
            for half in range(2):
                sl = slice(LANES * (2 * hp + half), LANES * (2 * hp + half + 1))
                qh = q[:, sl]
                if rope:
                    qh = _rope128(qh, cos_ref[tile_rows, :], sin_ref[tile_rows, :])
                qh = (qh * scale).astype(BF16)
                _store_scores(s_pair, m_pair, slice(half * TILE, (half + 1) * TILE), slice(None),
                              _dot_nt(qh, kx_ref[:, sl]), True)
            v2 = vx_ref[:, LANES * hp:LANES * (hp + 1)]
            o_ref[tile_rows, LANES * hp:LANES * (hp + 1)] = _pair_softmax_pv(
                s_pair, m_pair, p_pair, den_pair, [(0, n_keys)], [(v2, False)]).astype(BF16)


def _mla(p, ckv, kr, cache_ckv, cache_kr, layer, q_norm, w_uq, w_k, w_v, rope_tabs, n_seq, seq):
    tiled = seq > TILE
    seqs = LAT_TILES_PER_STEP if tiled else CTX_SEQS_PER_STEP
    tiles = seq // (seqs * TILE) if tiled else 1
    key_rows = seq if tiled else seqs * seq
    n_cache = 0 if cache_ckv is None else cache_ckv.shape[2]
    rope = rope_tabs is not None
    in_specs = [pl.BlockSpec((seqs, None, TILE, CHUNK), lambda b, t: (b * tiles + t, C_DQ // CHUNK, 0, 0)),
                pl.BlockSpec((key_rows, MLA_KV_LORA), lambda b, t: (b, 0)),
                pl.BlockSpec((key_rows, LANES), lambda b, t: (b, 0))]
    args = [p, ckv, kr]
    if n_cache:
        in_specs += [pl.BlockSpec((None, None, n_cache, MLA_KV_LORA), lambda b, t: (b, layer, 0, 0)),
                     pl.BlockSpec((None, None, n_cache, LANES), lambda b, t: (b, layer, 0, 0))]
        args += [cache_ckv, cache_kr]
    in_specs += [_const_spec((None, 1, MLA_Q_LORA), (layer, 0, 0)),
                 _const_spec((None, MLA_Q_LORA, MLA_HEADS * LANES), (layer, 0, 0)),
                 _const_spec((None, MLA_KV_LORA, MLA_HEADS * LANES), (layer, 0, 0)),
                 _const_spec((None, MLA_KV_LORA, MLA_HEADS * MLA_V), (layer, 0, 0))]
    args += [q_norm, w_uq, w_k, w_v]
    if rope:
        in_specs += [pl.BlockSpec((seqs * TILE, LANES), lambda b, t: (t, 0))] * 2
        args += list(rope_tabs)
    n_keys = n_cache + seq
    key_sets = 1 if tiled else seqs
    return pl.pallas_call(
        functools.partial(_mla_kernel, n_cache=n_cache, n_lat=seq, rope=rope, seqs=seqs, tiled=tiled),
        grid=(n_seq if tiled else n_seq // seqs, tiles), in_specs=in_specs,
        out_specs=pl.BlockSpec((seqs * TILE, BRANCH_WIDTH), lambda b, t: (b * tiles + t, 0)),
        out_shape=jax.ShapeDtypeStruct((n_seq * seq, BRANCH_WIDTH), BF16),
        scratch_shapes=[pltpu.VMEM((key_sets, n_keys, MLA_HEADS * LANES), BF16),
                        pltpu.VMEM((key_sets, n_keys, MLA_HEADS * MLA_V), BF16)] + _pair_scratch(n_keys),
        compiler_params=_params(("parallel", "arbitrary")), name="mla",
    )(*args)


def _merge_kernel(x_ref, mod_ref, yna_ref, ymla_ref, pm_ref, sw_ref, sb_ref, wb32_ref, wo32_ref, fg_ref, o_ref,
                  wb_ref, wo_ref, *, final):
    tm = x_ref.shape[0]

    @pl.when(pl.program_id(0) == 0)
    def _cast_weights():
        wb_ref[...] = wb32_ref[...].astype(BF16)
        wo_ref[...] = wo32_ref[...].astype(BF16)

    def tokens(col):
        return pm_ref[:, col // CHUNK].reshape(tm, CHUNK)

    sv = tokens(C_SV)
    rows = []
    for c0 in range(0, tm, SGU_CHUNK):
        cols = [_dot(sw_ref[g], sv[c0:c0 + SGU_CHUNK, LANES * g:LANES * (g + 1)]) for g in range(SGU_GROUPS)]
        rows.append(jnp.concatenate(cols, axis=1) + sb_ref[...])
    y_sgu = tokens(C_SU).astype(F32) * jnp.concatenate(rows, axis=0)

    gated = (yna_ref[...] * tokens(C_NAZ),
             (y_sgu * tokens(C_SZ).astype(F32)).astype(BF16),
             ymla_ref[...] * tokens(C_MZ))
    merged = None
    for k in range(N_BRANCH):
        gate_k = jnp.concatenate([tokens(C_MG + k * D_MODEL + c) for c in range(0, D_MODEL, CHUNK)], axis=1)
        term = gate_k.astype(F32) * _dot(gated[k], wb_ref[k])
        merged = term if merged is None else merged + term
    out = _dot(merged.astype(BF16), wo_ref[...])
    gate = mod_ref[:, 2 * D_MODEL:3 * D_MODEL]
    xn = x_ref[...] + gate * out
    if final:
        xn = _rms(xn, fg_ref[...])
    o_ref[...] = xn


def _merge(x, mod, mod_row, layer, p, y_na, y_mla, sgu_w, sgu_b, w_branch, w_out, final_g, final, tm):
    n_tok = x.shape[0]
    half = pl.BlockSpec((tm, BRANCH_WIDTH), lambda i: (i, 0))
    full = lambda c: pl.BlockSpec((tm, D_MODEL), lambda i: (i, c))
    in_specs = [full(0),
                pl.BlockSpec((None, None, 1, 3 * D_MODEL), lambda i: (layer, mod_row(i * tm), 0, 0)),
                half, half,
                pl.BlockSpec((tm // TILE, MERGE_CHUNKS, TILE, CHUNK), lambda i: (i, 0, 0, 0)),
                _const_spec((None, SGU_GROUPS, SGU_CHUNK, SGU_CHUNK), (layer, 0, 0, 0)),
                _const_spec((None, SGU_CHUNK, BRANCH_WIDTH), (layer, 0, 0)),
                _const_spec((None, N_BRANCH, BRANCH_WIDTH, D_MODEL), (layer, 0, 0, 0)),
                _const_spec((None, D_MODEL, D_MODEL), (layer, 0, 0)),
                _const_spec((1, D_MODEL), (0, 0))]
    return pl.pallas_call(
        functools.partial(_merge_kernel, final=final),
        grid=(n_tok // tm,), in_specs=in_specs,
        out_specs=full(0), out_shape=jax.ShapeDtypeStruct((n_tok, D_MODEL), F32),
        scratch_shapes=[pltpu.VMEM((N_BRANCH, BRANCH_WIDTH, D_MODEL), BF16), pltpu.VMEM((D_MODEL, D_MODEL), BF16)],
        compiler_params=_params(("arbitrary",)), name="merge",
    )(x, mod, y_na, y_mla, p, sgu_w, sgu_b, w_branch, w_out, final_g)


SRC_DQ = 7 * BRANCH_WIDTH
SRC_DKV = SRC_DQ + MLA_Q_LORA
SRC_KR = SRC_DKV + MLA_KV_LORA
SRC_MZ = SRC_KR + MLA_ROPE
SRC_NAZ = 3 * BRANCH_WIDTH
PACK_SHIFT = SRC_MZ % CHUNK
J_SHIFTED = C_MZ // CHUNK
J_Q = C_NAQ // CHUNK
J_DQ = C_DQ // CHUNK
assert SRC_NAZ % CHUNK == 0 and SRC_DQ % CHUNK == 0
assert SRC_DQ + CHUNK + (C_KR - P_COLS) == SRC_KR and SRC_KR + MLA_ROPE == SRC_MZ


PACK_SHIFTED = J_Q - J_SHIFTED
PACK_TAIL_SRC = (SRC_MZ - PACK_SHIFT) // CHUNK
assert PACK_TAIL_SRC == SRC_DQ // CHUNK + 1


def _pack_w_in_kernel(c_ref, o_ref, carry_ref):
    k = pl.program_id(1)

    def emit(rows):
        o_ref[...] = rows.T.astype(BF16)

    @pl.when(k == 0)
    def _tail():
        n_dkv = C_KR - P_COLS
        zeros = lambda n: jnp.zeros((n, D_MODEL), F32)
        emit(jnp.concatenate([c_ref[:n_dkv, :], zeros(ROPE_LANE0), c_ref[n_dkv:n_dkv + MLA_ROPE, :],
                              zeros(LANES - ROPE_LANE0 - MLA_ROPE), zeros(CHUNK - n_dkv - LANES)], axis=0))
        carry_ref[...] = c_ref[...]

    @pl.when(jnp.logical_and(k >= 1, k <= PACK_SHIFTED))
    def _shifted():
        emit(jnp.concatenate([carry_ref[PACK_SHIFT:, :], c_ref[:PACK_SHIFT, :]], axis=0))
        carry_ref[...] = c_ref[...]

    @pl.when(k > PACK_SHIFTED)
    def _aligned():
        emit(c_ref[...])


def _pack_w_in(w_in_t):
    n_chunks = pl.cdiv(W_COLS, CHUNK)
    assert n_chunks == J_DQ + 2
    first_plain = PACK_SHIFTED + 1
    first_q = first_plain + J_SHIFTED
    last = n_chunks - 1

    def dst_chunk(k):
        return jnp.where(k == 0, J_DQ + 1,
                         jnp.where(k < first_plain, k - 1 + J_SHIFTED,
                                   jnp.where(k < first_q, k - first_plain,
                                             jnp.where(k < last, k - first_q + J_Q, J_DQ))))

    def src_chunk(k):
        return jnp.where(k < first_plain, k + PACK_TAIL_SRC,
                         jnp.where(k < first_q, k - first_plain + SRC_NAZ // CHUNK,
                                   jnp.where(k < last, k - first_q, SRC_DQ // CHUNK)))

    return pl.pallas_call(
        _pack_w_in_kernel, grid=(DEPTH, n_chunks),
        in_specs=[pl.BlockSpec((None, CHUNK, D_MODEL), lambda l, k: (l, src_chunk(k), 0))],
        out_specs=pl.BlockSpec((None, D_MODEL, CHUNK), lambda l, k: (l, 0, dst_chunk(k))),
        out_shape=jax.ShapeDtypeStruct((DEPTH, D_MODEL, W_COLS), BF16),
        scratch_shapes=[pltpu.VMEM((CHUNK, D_MODEL), F32)],
        compiler_params=_params(("parallel", "arbitrary")), name="pack_w_in",
    )(w_in_t)


def _pack_params(w_in, sgu_w, sgu_b, mla_w_uq, mla_w_ukv, w_branch, w_out):
    w_packed = _pack_w_in(jnp.swapaxes(w_in, 1, 2))
    uq = mla_w_uq.reshape(DEPTH, MLA_Q_LORA, MLA_HEADS, MLA_NOPE + MLA_ROPE)
    uq = jnp.pad(uq, ((0, 0), (0, 0), (0, 0), (0, LANES - MLA_NOPE - MLA_ROPE)))
    uq = uq.reshape(DEPTH, MLA_Q_LORA, MLA_HEADS * LANES).astype(BF16)
    ukv = mla_w_ukv.reshape(DEPTH, MLA_KV_LORA, MLA_HEADS, MLA_NOPE + MLA_V)
    w_k = jnp.pad(ukv[..., :MLA_NOPE], ((0, 0), (0, 0), (0, 0), (0, LANES - MLA_NOPE)))
    w_k = w_k.reshape(DEPTH, MLA_KV_LORA, MLA_HEADS * LANES).astype(BF16)
    w_v = ukv[..., MLA_NOPE:].reshape(DEPTH, MLA_KV_LORA, MLA_HEADS * MLA_V).astype(BF16)
    sgu_bias = jnp.repeat(jnp.swapaxes(sgu_b, 1, 2), BRANCH_WIDTH // SGU_GROUPS, axis=2)
    return w_packed, uq, w_k, w_v, sgu_w.astype(BF16), sgu_bias, w_branch, w_out


def _rope_tables(n_tokens):
    pos = jnp.arange(n_tokens, dtype=jnp.int32)
    row = (pos // GRID_W).astype(F32)
    col = (pos % GRID_W).astype(F32)
    n_freq = MLA_ROPE // 4
    inv = ROPE_THETA ** (-jnp.arange(n_freq, dtype=F32) / n_freq)
    ang = jnp.concatenate([row[:, None] * inv, col[:, None] * inv], axis=-1)
    cos, sin = jnp.cos(ang), jnp.sin(ang)
    pad_l, pad_r = ROPE_LANE0, LANES - ROPE_LANE0 - MLA_ROPE
    cos_t = jnp.pad(jnp.concatenate([cos, cos], axis=1), ((0, 0), (pad_l, pad_r)), constant_values=1.0)
    sin_t = jnp.pad(jnp.concatenate([-sin, sin], axis=1), ((0, 0), (pad_l, pad_r)))
    return cos_t, sin_t


def kernel(x_prompt, x_sample, cache_na_k, cache_na_v, cache_mla_ckv, cache_mla_krope, c, c_ctx, norm_g, w_mod, b_mod, w_in, na_rpb, sgu_w, sgu_b, mla_q_norm, mla_w_uq, mla_kv_norm, mla_w_ukv, w_branch, w_out, final_norm_g):
    n_ctx, ctx_len, _ = x_prompt.shape
    n_lat, lat_len, _ = x_sample.shape
    past = cache_na_k.shape[2]
    assert ctx_len == TILE and lat_len % TILE == 0 and lat_len // GRID_W == 16 and past == TILE

    w_packed, w_uq, w_k, w_v, sgu_wb, sgu_bias, w_br, w_o = _pack_params(
        w_in, sgu_w, sgu_b, mla_w_uq, mla_w_ukv, w_branch, w_out)
    rope_tabs = _rope_tables(lat_len)
    tab = _na_bias_tables(na_rpb)
    norm_g3 = norm_g.reshape(DEPTH, 1, D_MODEL)
    kv_norm3 = mla_kv_norm.reshape(DEPTH, 1, MLA_KV_LORA)
    q_norm3 = mla_q_norm.reshape(DEPTH, 1, MLA_Q_LORA)
    final_g = final_norm_g.reshape(1, D_MODEL)

    cond_rows = 16
    cond = jnp.concatenate([c, c_ctx[None, :], jnp.zeros((cond_rows - n_lat - 1, D_MODEL), F32)], axis=0)
    mod = _modulation(cond, w_mod, b_mod).reshape(DEPTH, cond_rows, 1, 3 * D_MODEL)
    ctx_row = lambda token: n_lat
    lat_row = lambda token: token // lat_len

    cache_k = jnp.transpose(cache_na_k, (0, 1, 3, 4, 2)).reshape(n_lat, DEPTH, BRANCH_WIDTH, past)
    cache_v = jnp.transpose(cache_na_v, (0, 1, 3, 4, 2)).reshape(n_lat, DEPTH, BRANCH_WIDTH, past)
    cache_kr = jnp.pad(cache_mla_krope, ((0, 0), (0, 0), (0, 0), (ROPE_LANE0, LANES - ROPE_LANE0 - MLA_ROPE)))

    xp = x_prompt.reshape(n_ctx * ctx_len, D_MODEL)
    xs = x_sample.reshape(n_lat * lat_len, D_MODEL)
    state = None
    for l in range(DEPTH):
        final = l == DEPTH - 1
        p, ckv, kr, *state = _inproj(xp, mod, ctx_row, l, norm_g3, w_packed, kv_norm3, None, state, True, TILE)
        y_na = _na_ctx(p, n_ctx)
        y_mla = _mla(p, ckv, kr, None, None, l, q_norm3, w_uq, w_k, w_v, None, n_ctx, ctx_len)
        xp = _merge(xp, mod, ctx_row, l, p, y_na, y_mla, sgu_wb, sgu_bias, w_br, w_o, final_g, final, MERGE_TILE)
        p, ckv, kr = _inproj(xs, mod, lat_row, l, norm_g3, w_packed, kv_norm3, rope_tabs, None, False, TILE)
        y_na = _na_lat(p, cache_k, cache_v, tab, l, n_lat, lat_len)
        y_mla = _mla(p, ckv, kr, cache_mla_ckv, cache_kr, l, q_norm3, w_uq, w_k, w_v, rope_tabs, n_lat, lat_len)
        xs = _merge(xs, mod, lat_row, l, p, y_na, y_mla, sgu_wb, sgu_bias, w_br, w_o, final_g, final, MERGE_TILE)

    s_k, s_v, s_ckv, s_kr = state

    def heads_last(s):
        return jnp.transpose(s.reshape(n_ctx, DEPTH, NA_HEADS, NA_HEAD_DIM, ctx_len), (0, 1, 4, 2, 3))

    return (xp.reshape(n_ctx, ctx_len, D_MODEL), xs.reshape(n_lat, lat_len, D_MODEL),
            heads_last(s_k), heads_last(s_v), s_ckv, jnp.swapaxes(s_kr, 2, 3))
```

```python
import functools

import numpy as np
import jax
import jax.numpy as jnp
from jax import lax
from jax.experimental import pallas as pl
from jax.experimental.pallas import tpu as pltpu

F32 = jnp.float32
BF16 = jnp.bfloat16

D_MODEL = 1024
DEPTH = 2
GRID_W = 64
BRANCH_WIDTH = 512
N_BRANCH = 3
NA_HEADS = 8
NA_HEAD_DIM = 64
NA_WIN_H = 8
NA_WIN_W = 16
SGU_GROUPS = 4
SGU_CHUNK = 128
MLA_HEADS = 8
MLA_NOPE = 64
MLA_ROPE = 32
MLA_V = 64
MLA_Q_LORA = 384
MLA_KV_LORA = 256
ROPE_THETA = 10000.0
EPS = 1e-6
NEG_INF = -1e30

LANES = 128
TILE = 256
MERGE_TILE = 512
CTX_SEQS_PER_STEP = 8
LAT_TILES_PER_STEP = 4
LAT_SEQS_PER_STEP = 2
STRIP = 16
HEAD_PAIRS = NA_HEADS // 2
SCRATCH_SETS = 2
NA_KEY_ROWS = 12
NA_LOC_KEYS = NA_KEY_ROWS * GRID_W
ROWS_PER_TILE = TILE // GRID_W
TAB_PAD = 4
TAB_BLOCKS = 24
ROPE_LANE0 = MLA_NOPE

C_NAZ, C_SU, C_SV, C_SZ, C_MZ, C_MG = (512 * i for i in range(6))
C_NAQ = C_MG + N_BRANCH * D_MODEL
C_NAK, C_NAV, C_DQ = C_NAQ + 512, C_NAQ + 1024, C_NAQ + 1536
MERGE_CHUNKS = C_NAQ // 512
C_DKV = C_DQ + MLA_Q_LORA
C_KR = C_DKV + MLA_KV_LORA
W_COLS = C_KR + LANES
P_COLS = C_DQ + 512
CHUNK = 512

VMEM_LIMIT = 56 * 1024 * 1024


def _dot(a, b):
    return jnp.dot(a, b, preferred_element_type=F32)


def _dot_nt(a, b):
    return lax.dot_general(a, b, (((1,), (1,)), ((), ())), preferred_element_type=F32)


def _rms(x, g):
    return x * lax.rsqrt(jnp.mean(x * x, axis=-1, keepdims=True) + EPS) * g


def _center_norm(x):
    c = x - jnp.mean(x, axis=-1, keepdims=True)
    return c * lax.rsqrt(jnp.mean(c * c, axis=-1, keepdims=True) + EPS)


def _rope128(x, cos, sin):
    lane = lax.broadcasted_iota(jnp.int32, x.shape, 1)
    swapped = jnp.where(lane < ROPE_LANE0 + MLA_ROPE // 2,
                        pltpu.roll(x, LANES - MLA_ROPE // 2, axis=1),
                        pltpu.roll(x, MLA_ROPE // 2, axis=1))
    return x * cos + swapped * sin


def _store_scores(s_ref, m_ref, rows, cols, scores, first):
    s_ref[rows, cols] = scores
    block_max = jnp.broadcast_to(scores.max(axis=-1, keepdims=True), (scores.shape[0], LANES))
    m_ref[rows, :] = block_max if first else jnp.maximum(m_ref[rows, :], block_max)


def _pair_softmax_pv(s_ref, m_ref, p_ref, den_ref, segments, values):
    for i in range(s_ref.shape[0] // STRIP):
        r = slice(i * STRIP, (i + 1) * STRIP)
        m = m_ref[r, :]
        den = None
        for c0, c1 in segments:
            e = jnp.exp(s_ref[r, c0:c1] - jnp.concatenate([m] * ((c1 - c0) // LANES), axis=1))
            d = e.sum(axis=-1, keepdims=True)
            den = d if den is None else den + d
            p_ref[r, c0:c1] = e.astype(BF16)
        den_ref[r, :] = jnp.broadcast_to(den, (STRIP, LANES))
    o = None
    for (c0, c1), (v, transposed) in zip(segments, values):
        t = _dot_nt(p_ref[:, c0:c1], v) if transposed else _dot(p_ref[:, c0:c1], v)
        o = t if o is None else o + t
    o = o / den_ref[...]
    lane = lax.broadcasted_iota(jnp.int32, (TILE, LANES), 1)
    return jnp.where(lane < LANES // 2, o[:TILE], o[TILE:])


def _pair_scratch(n_keys):
    return [pltpu.VMEM((SCRATCH_SETS, 2 * TILE, n_keys), F32), pltpu.VMEM((SCRATCH_SETS, 2 * TILE, LANES), F32),
            pltpu.VMEM((SCRATCH_SETS, 2 * TILE, n_keys), BF16), pltpu.VMEM((SCRATCH_SETS, 2 * TILE, LANES), F32)]


def _const_spec(shape, index):
    return pl.BlockSpec(shape, lambda *_: index, pipeline_mode=pl.Buffered(1))


def _params(semantics):
    return pltpu.CompilerParams(dimension_semantics=semantics, vmem_limit_bytes=VMEM_LIMIT)


def _mod_kernel(c_ref, w_ref, b_ref, o_ref):
    c = c_ref[...]
    s = c * jax.nn.sigmoid(c)
    o_ref[...] = jnp.dot(s, w_ref[...], preferred_element_type=F32, precision=lax.Precision.HIGHEST) + b_ref[...]


def _modulation(cond, w_mod, b_mod):
    rows = cond.shape[0]
    return pl.pallas_call(
        _mod_kernel,
        grid=(DEPTH, 3),
        in_specs=[pl.BlockSpec((rows, D_MODEL), lambda l, j: (0, 0)),
                  pl.BlockSpec((None, D_MODEL, D_MODEL), lambda l, j: (l, 0, j)),
                  pl.BlockSpec((None, 1, D_MODEL), lambda l, j: (l, 0, j))],
        out_specs=pl.BlockSpec((None, rows, D_MODEL), lambda l, j: (l, 0, j)),
        out_shape=jax.ShapeDtypeStruct((DEPTH, rows, 3 * D_MODEL), F32),
        compiler_params=_params(("parallel", "parallel")),
        name="modulation",
    )(cond, w_mod, b_mod.reshape(DEPTH, 1, 3 * D_MODEL))


def _inproj_kernel(*refs, emit_state, rope, n_alias):
    x0_ref, x_next_ref, mod0_ref, mod_next_ref, g_ref, w_ref, kvn_ref = refs[:7]
    refs = refs[7:]
    if rope:
        cos_ref, sin_ref = refs[:2]
        refs = refs[2:]
    refs = refs[n_alias:]
    p_ref, ckv_ref, kr_ref = refs[:3]
    if emit_state:
        sk_ref, sv_ref, sckv_ref, skr_ref = refs[3:7]
    h_ref = refs[-1]
    step = pl.program_id(0)
    tm = x0_ref.shape[0]

    def normalised(x_ref, mod_ref):
        shift = mod_ref[:, 0:D_MODEL]
        scale = mod_ref[:, D_MODEL:2 * D_MODEL]
        return (_rms(x_ref[...], g_ref[...]) * (1.0 + scale) + shift).astype(BF16)

    @pl.when(step == 0)
    def _first_tile():
        h_ref[0] = normalised(x0_ref, mod0_ref)

    h = h_ref[step % 2]

    def tiles(a):
        return a.reshape(tm // TILE, TILE, a.shape[-1])

    ckv = _rms(_dot(h, w_ref[:, C_DKV:C_KR]), kvn_ref[...])
    ckv_ref[...] = ckv.astype(BF16)
    kr = _dot(h, w_ref[:, C_KR:W_COLS])
    if emit_state:
        sckv_ref[...] = ckv
        skr_ref[...] = kr.T[ROPE_LANE0:ROPE_LANE0 + MLA_ROPE, :]
    if rope:
        kr = _rope128(kr, cos_ref[...], sin_ref[...])
    kr_ref[...] = kr.astype(BF16)

    h_ref[(step + 1) % 2] = normalised(x_next_ref, mod_next_ref)

    for c0 in range(0, P_COLS, CHUNK):
        acc = _dot(h, w_ref[:, c0:c0 + CHUNK])
        if emit_state and c0 == C_NAK:
            sk_ref[...] = acc.T
        if emit_state and c0 == C_NAV:
            sv_ref[...] = acc.T
        if c0 in (C_NAZ, C_SZ, C_MZ):
            acc = acc * jax.nn.sigmoid(acc)
        elif c0 == C_SU:
            acc = jax.nn.gelu(acc)
        elif c0 == C_SV:
            acc = _center_norm(jax.nn.gelu(acc))
        elif C_MG <= c0 < C_NAQ:
            acc = jax.nn.sigmoid(acc)
        p_ref[:, c0 // CHUNK] = tiles(acc.astype(BF16))


def _inproj(x, mod, mod_row, layer, norm_g, w_packed, kv_norm, rope_tabs, state_in, emit_state, tm):
    n_tok = x.shape[0]
    n_tiles = n_tok // tm
    rope = rope_tabs is not None
    next_tile = lambda i: jnp.minimum(i + 1, n_tiles - 1)
    in_specs = [_const_spec((tm, D_MODEL), (0, 0)),
                pl.BlockSpec((tm, D_MODEL), lambda i: (next_tile(i), 0)),
                _const_spec((None, None, 1, 3 * D_MODEL), (layer, mod_row(0), 0, 0)),
                pl.BlockSpec((None, None, 1, 3 * D_MODEL), lambda i: (layer, mod_row(next_tile(i) * tm), 0, 0)),
                _const_spec((None, 1, D_MODEL), (layer, 0, 0)),
                _const_spec((None, D_MODEL, W_COLS), (layer, 0, 0)),
                _const_spec((None, 1, MLA_KV_LORA), (layer, 0, 0))]
    args = [x, x, mod, mod, norm_g, w_packed, kv_norm]
    if rope:
        tiles_per_seq = rope_tabs[0].shape[0] // tm
        in_specs += [pl.BlockSpec((tm, LANES), lambda i: (i % tiles_per_seq, 0))] * 2
        args += list(rope_tabs)
    out_specs = [pl.BlockSpec((tm // TILE, P_COLS // CHUNK, TILE, CHUNK), lambda i: (i, 0, 0, 0)),
                 pl.BlockSpec((tm, MLA_KV_LORA), lambda i: (i, 0)),
                 pl.BlockSpec((tm, LANES), lambda i: (i, 0))]
    out_shape = [jax.ShapeDtypeStruct((n_tok // TILE, P_COLS // CHUNK, TILE, CHUNK), BF16),
                 jax.ShapeDtypeStruct((n_tok, MLA_KV_LORA), BF16),
                 jax.ShapeDtypeStruct((n_tok, LANES), BF16)]
    aliases = {}
    if emit_state:
        assert tm == TILE
        for rows, cols in ((BRANCH_WIDTH, TILE), (BRANCH_WIDTH, TILE), (TILE, MLA_KV_LORA), (MLA_ROPE, TILE)):
            out_specs.append(pl.BlockSpec((None, None, rows, cols), lambda i: (i, layer, 0, 0)))
            out_shape.append(jax.ShapeDtypeStruct((n_tok // TILE, DEPTH, rows, cols), F32))
        if state_in is not None:
            first = len(args)
            in_specs += [pl.BlockSpec(memory_space=pl.ANY)] * 4
            args += list(state_in)
            aliases = {first + k: 3 + k for k in range(4)}
    kern = functools.partial(_inproj_kernel, emit_state=emit_state, rope=rope, n_alias=len(aliases))
    return pl.pallas_call(
        kern, grid=(n_tiles,), in_specs=in_specs, out_specs=out_specs, out_shape=out_shape,
        scratch_shapes=[pltpu.VMEM((2, tm, D_MODEL), BF16)],
        input_output_aliases=aliases, compiler_params=_params(("arbitrary",)),
        name="inproj",
    )(*args)


def _na_heads(q_ref, o_ref, scratch, keys_fn, values_fn, bias_fn=None):
    lane = lax.broadcasted_iota(jnp.int32, (TILE, LANES), 1)
    low = lane < NA_HEAD_DIM
    for hp in range(HEAD_PAIRS):
        sl = slice(LANES * hp, LANES * (hp + 1))
        q2 = q_ref[:, sl] * (NA_HEAD_DIM ** -0.5)
        zero = jnp.zeros_like(q2)
        q_stack = jnp.concatenate([jnp.where(low, q2, zero), jnp.where(low, zero, q2)], axis=0)
        s_pair, m_pair, p_pair, den_pair = (ref.at[hp % SCRATCH_SETS] for ref in scratch)
        segments = []
        c0 = 0
        for j, (k, transposed) in enumerate(keys_fn(sl)):
            n_keys = k.shape[1] if transposed else k.shape[0]
            scores = _dot(q_stack, k) if transposed else _dot_nt(q_stack, k)
            if j == 0 and bias_fn is not None:
                scores = scores + bias_fn(2 * hp)
            _store_scores(s_pair, m_pair, slice(None), slice(c0, c0 + n_keys), scores, j == 0)
            segments.append((c0, c0 + n_keys))
            c0 += n_keys
        o_ref[:, sl] = _pair_softmax_pv(s_pair, m_pair, p_pair, den_pair, segments, values_fn(sl)).astype(BF16)


def _na_ctx_kernel(q_ref, k_ref, v_ref, o_ref, *scratch):
    for s in range(q_ref.shape[0]):
        _na_heads(q_ref.at[s], o_ref.at[s], scratch,
                  lambda sl, s=s: [(k_ref[s, :, sl], False)], lambda sl, s=s: [(v_ref[s, :, sl], False)])


def _na_band_row0(tile_row0, n_rows):
    return min(max(tile_row0 - NA_WIN_H // 2, 0), n_rows - NA_KEY_ROWS)


def _na_band_plans(seq):
    n_rows = seq // GRID_W
    plans = []
    for t in range(seq // TILE):
        band0 = _na_band_row0(t * ROWS_PER_TILE, n_rows)
        plan = []
        for r in range(ROWS_PER_TILE):
            rq = t * ROWS_PER_TILE + r
            row_lo = min(max(rq - NA_WIN_H // 2, 0), n_rows - NA_WIN_H)
            plan.append((band0 - rq + NA_WIN_H - 1 + TAB_PAD, row_lo - band0))
        plans.append(tuple(plan))
    return tuple(plans)


def _na_lat_kernel(q_ref, k_ref, v_ref, kc_ref, vc_ref, tab_ref, o_ref, bias_ref, *scratch, plans, n_rows):
    t = pl.program_id(0)
    first_half = lax.broadcasted_iota(jnp.int32, (GRID_W, LANES), 1) < GRID_W
    masked = jnp.full((GRID_W, LANES), NEG_INF, F32)

    def block_pair(h, blk):
        if blk % 2 == 0:
            return tab_ref[h, blk // 2]
        return jnp.where(first_half, pltpu.roll(tab_ref[h, blk // 2], GRID_W, axis=1),
                         pltpu.roll(tab_ref[h, blk // 2 + 1], GRID_W, axis=1))

    for plan in sorted(set(plans)):
        is_tile = functools.reduce(jnp.logical_or, [t == i for i, p in enumerate(plans) if p == plan])

        @pl.when(jnp.logical_and(pl.program_id(1) == 0, is_tile))
        def _build_bias(plan=plan):
            for r, (blk0, j0) in enumerate(plan):
                for j in range(NA_KEY_ROWS // 2):
                    in_window = [j0 <= 2 * j + half < j0 + NA_WIN_H for half in range(2)]
                    for h in range(NA_HEADS):
                        if not any(in_window):
                            piece = masked
                        elif all(in_window):
                            piece = block_pair(h, blk0 + 2 * j)
                        else:
                            keep = first_half if in_window[0] else jnp.logical_not(first_half)
                            piece = jnp.where(keep, block_pair(h, blk0 + 2 * j), NEG_INF)
                        bias_ref[h, r * GRID_W:(r + 1) * GRID_W, LANES * j:LANES * (j + 1)] = piece

    band_tile0 = jnp.clip(t * ROWS_PER_TILE - NA_WIN_H // 2, 0, n_rows - NA_KEY_ROWS) // ROWS_PER_TILE
    for plan in sorted(set(plans)):
        first_row = min(j0 for _, j0 in plan)
        last_row = max(j0 for _, j0 in plan) + NA_WIN_H
        skip = first_row // ROWS_PER_TILE
        n_tiles = -(-last_row // ROWS_PER_TILE) - skip
        key_tiles = pl.ds(band_tile0 + skip, n_tiles)
        bias_lanes = slice(skip * TILE, (skip + n_tiles) * TILE)

        @pl.when(functools.reduce(jnp.logical_or, [t == i for i, p in enumerate(plans) if p == plan]))
        def _attend(key_tiles=key_tiles, bias_lanes=bias_lanes, n_keys=n_tiles * TILE):
            def band(ref, s, sl):
                return ref[s, key_tiles, :, sl].reshape(n_keys, LANES)

            def bias(head0):
                return bias_ref[head0:head0 + 2, :, bias_lanes].reshape(2 * TILE, n_keys)

            for s in range(q_ref.shape[0]):
                _na_heads(q_ref.at[s], o_ref.at[s], scratch,
                          lambda sl, s=s: [(band(k_ref, s, sl), False), (kc_ref[s, sl, :].astype(BF16), True)],
                          lambda sl, s=s: [(band(v_ref, s, sl), False), (vc_ref[s, sl, :].astype(BF16), True)], bias)


def _p_block(tile_of, col):
    return pl.BlockSpec((None, None, TILE, CHUNK), lambda *g: (tile_of(*g), col // CHUNK, 0, 0))


def _na_ctx(p, n_seq):
    per_step = CTX_SEQS_PER_STEP
    blk = lambda col: pl.BlockSpec((per_step, None, TILE, CHUNK), lambda i: (i, col // CHUNK, 0, 0))
    y = pl.pallas_call(
        _na_ctx_kernel, grid=(n_seq // per_step,),
        in_specs=[blk(C_NAQ), blk(C_NAK), blk(C_NAV)],
        out_specs=pl.BlockSpec((per_step, TILE, BRANCH_WIDTH), lambda i: (i, 0, 0)),
        out_shape=jax.ShapeDtypeStruct((n_seq, TILE, BRANCH_WIDTH), BF16),
        scratch_shapes=_pair_scratch(TILE),
        compiler_params=_params(("parallel",)), name="na_ctx",
    )(p, p, p)
    return y.reshape(n_seq * TILE, BRANCH_WIDTH)


def _na_lat(p, cache_k, cache_v, tab, layer, n_seq, seq):
    tiles = seq // TILE
    n_past = cache_k.shape[3]
    per_step = LAT_SEQS_PER_STEP
    p5 = p.reshape(n_seq, tiles, P_COLS // CHUNK, TILE, CHUNK)
    kv = lambda col: pl.BlockSpec((per_step, tiles, None, TILE, CHUNK), lambda t, b: (b, 0, col // CHUNK, 0, 0))
    cache = pl.BlockSpec((per_step, None, BRANCH_WIDTH, n_past), lambda t, b: (b, layer, 0, 0))
    table = _const_spec((None, NA_HEADS, TAB_BLOCKS // 2, GRID_W, LANES), (layer, 0, 0, 0, 0))
    y = pl.pallas_call(
        functools.partial(_na_lat_kernel, plans=_na_band_plans(seq), n_rows=seq // GRID_W),
        grid=(tiles, n_seq // per_step),
        in_specs=[pl.BlockSpec((per_step, None, None, TILE, CHUNK), lambda t, b: (b, t, C_NAQ // CHUNK, 0, 0)),
                  kv(C_NAK), kv(C_NAV), cache, cache, table],
        out_specs=pl.BlockSpec((per_step, None, TILE, BRANCH_WIDTH), lambda t, b: (b, t, 0, 0)),
        out_shape=jax.ShapeDtypeStruct((n_seq, tiles, TILE, BRANCH_WIDTH), BF16),
        scratch_shapes=[pltpu.VMEM((NA_HEADS, TILE, NA_LOC_KEYS), F32)]
        + _pair_scratch(NA_LOC_KEYS + n_past),
        compiler_params=_params(("parallel", "arbitrary")), name="na_lat",
    )(p5, p5, p5, cache_k, cache_v, tab)
    return y.reshape(n_seq * seq, BRANCH_WIDTH)


def _na_bias_tables(rpb):
    cq = np.arange(GRID_W)[:, None]
    ck = np.arange(GRID_W)[None, :]
    col_lo = np.clip(cq - NA_WIN_W // 2, 0, GRID_W - NA_WIN_W)
    ok = (ck >= col_lo) & (ck < col_lo + NA_WIN_W)
    n_b = 2 * NA_WIN_W - 1
    n_rel = 2 * NA_WIN_H - 1
    onehot = ((ck - cq + NA_WIN_W - 1)[:, None, :] == np.arange(n_b)[None, :, None]) & ok[:, None, :]
    expand = np.zeros((GRID_W, 2, n_b, 2, GRID_W), np.float32)
    expand[:, 0, :, 0, :] = onehot
    expand[:, 1, :, 1, :] = onehot
    expand = expand.reshape(GRID_W, 2 * n_b, LANES)
    rows = jnp.pad(rpb, ((0, 0), (0, 0), (TAB_PAD, TAB_BLOCKS - n_rel - TAB_PAD), (0, 0)))
    real = np.zeros(TAB_BLOCKS, bool)
    real[TAB_PAD:TAB_PAD + n_rel] = True
    pairs = rows.reshape(DEPTH, NA_HEADS, TAB_BLOCKS // 2, 2 * n_b)
    tab = jnp.einsum('lhek,ukc->lheuc', pairs, expand, precision=lax.Precision.HIGHEST)
    keep = real.reshape(-1, 1, 2, 1) & ok[None, :, None, :]
    return jnp.where(keep.reshape(TAB_BLOCKS // 2, GRID_W, LANES), tab, NEG_INF)


def _mla_kernel(*refs, n_cache, n_lat, rope, seqs, tiled):
    dq_ref, ckv_ref, kr_ref = refs[:3]
    refs = refs[3:]
    if n_cache:
        cckv_ref, ckr_ref = refs[:2]
        refs = refs[2:]
    qn_ref, wuq_ref, wk_ref, wv_ref = refs[:4]
    refs = refs[4:]
    if rope:
        cos_ref, sin_ref = refs[:2]
        refs = refs[2:]
    o_ref, kx_all, vx_all = refs[:3]
    scratch = refs[3:]
    scale = (MLA_NOPE + MLA_ROPE) ** -0.5
    n_keys = kx_all.shape[1]

    for s in range(seqs):
        key_set = 0 if tiled else s
        kx_ref, vx_ref = kx_all.at[key_set], vx_all.at[key_set]

        def expand_keys(key_set=key_set, kx_ref=kx_ref, vx_ref=vx_ref):
            def fill(r0, ckv, kr):
                n = ckv.shape[0]
                kk = _dot(ckv, wk_ref[...])
                for h in range(MLA_HEADS):
                    sl = slice(LANES * h, LANES * (h + 1))
                    kx_ref[r0:r0 + n, sl] = (kk[:, sl] + kr).astype(BF16)
                vx_ref[r0:r0 + n, :] = _dot(ckv, wv_ref[...]).astype(BF16)

            if n_cache:
                fill(0, cckv_ref[...].astype(BF16), ckr_ref[...])
            for r0 in range(key_set * n_lat, (key_set + 1) * n_lat, TILE):
                fill(n_cache + r0 - key_set * n_lat, ckv_ref[r0:r0 + TILE, :], kr_ref[r0:r0 + TILE, :].astype(F32))

        if not tiled:
            expand_keys()
        elif s == 0:
            pl.when(pl.program_id(1) == 0)(expand_keys)

        tile_rows = slice(s * TILE, (s + 1) * TILE)
        dqn = _rms(dq_ref[s, :, :MLA_Q_LORA].astype(F32), qn_ref[...])
        q = _dot(dqn.astype(BF16), wuq_ref[...])
        for hp in range(HEAD_PAIRS):
            s_pair, m_pair, p_pair, den_pair = (ref.at[hp % SCRATCH_SETS] for ref in scratch)
            for half in range(2):
                sl = slice(LANES * (2 * hp + half), LANES * (2 * hp + half + 1))
                qh = q[:, sl]
                if rope:
                    qh = _rope128(qh, cos_ref[tile_rows, :], sin_ref[tile_rows, :])
                qh = (qh * scale).astype(BF16)
                _store_scores(s_pair, m_pair, slice(half * TILE, (half + 1) * TILE), slice(None),
                              _dot_nt(qh, kx_ref[:, sl]), True)
            v2 = vx_ref[:, LANES * hp:LANES * (hp + 1)]
            o_ref[tile_rows, LANES * hp:LANES * (hp + 1)] = _pair_softmax_pv(
                s_pair, m_pair, p_pair, den_pair, [(0, n_keys)], [(v2, False)]).astype(BF16)


def _mla(p, ckv, kr, cache_ckv, cache_kr, layer, q_norm, w_uq, w_k, w_v, rope_tabs, n_seq, seq):
    tiled = seq > TILE
    seqs = LAT_TILES_PER_STEP if tiled else CTX_SEQS_PER_STEP
    tiles = seq // (seqs * TILE) if tiled else 1
    key_rows = seq if tiled else seqs * seq
    n_cache = 0 if cache_ckv is None else cache_ckv.shape[2]
    rope = rope_tabs is not None
    in_specs = [pl.BlockSpec((seqs, None, TILE, CHUNK), lambda b, t: (b * tiles + t, C_DQ // CHUNK, 0, 0)),
                pl.BlockSpec((key_rows, MLA_KV_LORA), lambda b, t: (b, 0)),
                pl.BlockSpec((key_rows, LANES), lambda b, t: (b, 0))]
    args = [p, ckv, kr]
    if n_cache:
        in_specs += [pl.BlockSpec((None, None, n_cache, MLA_KV_LORA), lambda b, t: (b, layer, 0, 0)),
                     pl.BlockSpec((None, None, n_cache, LANES), lambda b, t: (b, layer, 0, 0))]
        args += [cache_ckv, cache_kr]
    in_specs += [_const_spec((None, 1, MLA_Q_LORA), (layer, 0, 0)),
                 _const_spec((None, MLA_Q_LORA, MLA_HEADS * LANES), (layer, 0, 0)),
                 _const_spec((None, MLA_KV_LORA, MLA_HEADS * LANES), (layer, 0, 0)),
                 _const_spec((None, MLA_KV_LORA, MLA_HEADS * MLA_V), (layer, 0, 0))]
    args += [q_norm, w_uq, w_k, w_v]
    if rope:
        in_specs += [pl.BlockSpec((seqs * TILE, LANES), lambda b, t: (t, 0))] * 2
        args += list(rope_tabs)
    n_keys = n_cache + seq
    key_sets = 1 if tiled else seqs
    return pl.pallas_call(
        functools.partial(_mla_kernel, n_cache=n_cache, n_lat=seq, rope=rope, seqs=seqs, tiled=tiled),
        grid=(n_seq if tiled else n_seq // seqs, tiles), in_specs=in_specs,
        out_specs=pl.BlockSpec((seqs * TILE, BRANCH_WIDTH), lambda b, t: (b * tiles + t, 0)),
        out_shape=jax.ShapeDtypeStruct((n_seq * seq, BRANCH_WIDTH), BF16),
        scratch_shapes=[pltpu.VMEM((key_sets, n_keys, MLA_HEADS * LANES), BF16),
                        pltpu.VMEM((key_sets, n_keys, MLA_HEADS * MLA_V), BF16)] + _pair_scratch(n_keys),
        compiler_params=_params(("parallel", "arbitrary")), name="mla",
    )(*args)


def _merge_kernel(x_ref, mod_ref, yna_ref, ymla_ref, pm_ref, sw_ref, sb_ref, wb32_ref, wo32_ref, fg_ref, o_ref,
                  wb_ref, wo_ref, *, final):
    tm = x_ref.shape[0]

    @pl.when(pl.program_id(0) == 0)
    def _cast_weights():
        wb_ref[...] = wb32_ref[...].astype(BF16)
        wo_ref[...] = wo32_ref[...].astype(BF16)

    def tokens(col):
        return pm_ref[:, col // CHUNK].reshape(tm, CHUNK)

    sv = tokens(C_SV)
    rows = []
    for c0 in range(0, tm, SGU_CHUNK):
        cols = [_dot(sw_ref[g], sv[c0:c0 + SGU_CHUNK, LANES * g:LANES * (g + 1)]) for g in range(SGU_GROUPS)]
        rows.append(jnp.concatenate(cols, axis=1) + sb_ref[...])
    y_sgu = tokens(C_SU).astype(F32) * jnp.concatenate(rows, axis=0)

    gated = (yna_ref[...] * tokens(C_NAZ),
             (y_sgu * tokens(C_SZ).astype(F32)).astype(BF16),
             ymla_ref[...] * tokens(C_MZ))
    merged = None
    for k in range(N_BRANCH):
        gate_k = jnp.concatenate([tokens(C_MG + k * D_MODEL + c) for c in range(0, D_MODEL, CHUNK)], axis=1)
        term = gate_k.astype(F32) * _dot(gated[k], wb_ref[k])
        merged = term if merged is None else merged + term
    out = _dot(merged.astype(BF16), wo_ref[...])
    gate = mod_ref[:, 2 * D_MODEL:3 * D_MODEL]
    xn = x_ref[...] + gate * out
    if final:
        xn = _rms(xn, fg_ref[...])
    o_ref[...] = xn


def _merge(x, mod, mod_row, layer, p, y_na, y_mla, sgu_w, sgu_b, w_branch, w_out, final_g, final, tm):
    n_tok = x.shape[0]
    half = pl.BlockSpec((tm, BRANCH_WIDTH), lambda i: (i, 0))
    full = lambda c: pl.BlockSpec((tm, D_MODEL), lambda i: (i, c))
    in_specs = [full(0),
                pl.BlockSpec((None, None, 1, 3 * D_MODEL), lambda i: (layer, mod_row(i * tm), 0, 0)),
                half, half,
                pl.BlockSpec((tm // TILE, MERGE_CHUNKS, TILE, CHUNK), lambda i: (i, 0, 0, 0)),
                _const_spec((None, SGU_GROUPS, SGU_CHUNK, SGU_CHUNK), (layer, 0, 0, 0)),
                _const_spec((None, SGU_CHUNK, BRANCH_WIDTH), (layer, 0, 0)),
                _const_spec((None, N_BRANCH, BRANCH_WIDTH, D_MODEL), (layer, 0, 0, 0)),
                _const_spec((None, D_MODEL, D_MODEL), (layer, 0, 0)),
                _const_spec((1, D_MODEL), (0, 0))]
    return pl.pallas_call(
        functools.partial(_merge_kernel, final=final),
        grid=(n_tok // tm,), in_specs=in_specs,
        out_specs=full(0), out_shape=jax.ShapeDtypeStruct((n_tok, D_MODEL), F32),
        scratch_shapes=[pltpu.VMEM((N_BRANCH, BRANCH_WIDTH, D_MODEL), BF16), pltpu.VMEM((D_MODEL, D_MODEL), BF16)],
        compiler_params=_params(("arbitrary",)), name="merge",
    )(x, mod, y_na, y_mla, p, sgu_w, sgu_b, w_branch, w_out, final_g)


SRC_DQ = 7 * BRANCH_WIDTH
SRC_DKV = SRC_DQ + MLA_Q_LORA
SRC_KR = SRC_DKV + MLA_KV_LORA
SRC_MZ = SRC_KR + MLA_ROPE
SRC_NAZ = 3 * BRANCH_WIDTH
PACK_SHIFT = SRC_MZ % CHUNK
J_SHIFTED = C_MZ // CHUNK
J_Q = C_NAQ // CHUNK
J_DQ = C_DQ // CHUNK
assert SRC_NAZ % CHUNK == 0 and SRC_DQ % CHUNK == 0
assert SRC_DQ + CHUNK + (C_KR - P_COLS) == SRC_KR and SRC_KR + MLA_ROPE == SRC_MZ


PACK_SHIFTED = J_Q - J_SHIFTED
PACK_TAIL_SRC = (SRC_MZ - PACK_SHIFT) // CHUNK
assert PACK_TAIL_SRC == SRC_DQ // CHUNK + 1


def _pack_w_in_kernel(c_ref, o_ref, carry_ref):
    k = pl.program_id(1)

    def emit(rows):
        o_ref[...] = rows.T.astype(BF16)

    @pl.when(k == 0)
    def _tail():
        n_dkv = C_KR - P_COLS
        zeros = lambda n: jnp.zeros((n, D_MODEL), F32)
        emit(jnp.concatenate([c_ref[:n_dkv, :], zeros(ROPE_LANE0), c_ref[n_dkv:n_dkv + MLA_ROPE, :],
                              zeros(LANES - ROPE_LANE0 - MLA_ROPE), zeros(CHUNK - n_dkv - LANES)], axis=0))
        carry_ref[...] = c_ref[...]

    @pl.when(jnp.logical_and(k >= 1, k <= PACK_SHIFTED))
    def _shifted():
        emit(jnp.concatenate([carry_ref[PACK_SHIFT:, :], c_ref[:PACK_SHIFT, :]], axis=0))
        carry_ref[...] = c_ref[...]

    @pl.when(k > PACK_SHIFTED)
    def _aligned():
        emit(c_ref[...])


def _pack_w_in(w_in_t):
    n_chunks = pl.cdiv(W_COLS, CHUNK)
    assert n_chunks == J_DQ + 2
    first_plain = PACK_SHIFTED + 1
    first_q = first_plain + J_SHIFTED
    last = n_chunks - 1

    def dst_chunk(k):
        return jnp.where(k == 0, J_DQ + 1,
                         jnp.where(k < first_plain, k - 1 + J_SHIFTED,
                                   jnp.where(k < first_q, k - first_plain,
                                             jnp.where(k < last, k - first_q + J_Q, J_DQ))))

    def src_chunk(k):
        return jnp.where(k < first_plain, k + PACK_TAIL_SRC,
                         jnp.where(k < first_q, k - first_plain + SRC_NAZ // CHUNK,
                                   jnp.where(k < last, k - first_q, SRC_DQ // CHUNK)))

    return pl.pallas_call(
        _pack_w_in_kernel, grid=(DEPTH, n_chunks),
        in_specs=[pl.BlockSpec((None, CHUNK, D_MODEL), lambda l, k: (l, src_chunk(k), 0))],
        out_specs=pl.BlockSpec((None, D_MODEL, CHUNK), lambda l, k: (l, 0, dst_chunk(k))),
        out_shape=jax.ShapeDtypeStruct((DEPTH, D_MODEL, W_COLS), BF16),
        scratch_shapes=[pltpu.VMEM((CHUNK, D_MODEL), F32)],
        compiler_params=_params(("parallel", "arbitrary")), name="pack_w_in",
    )(w_in_t)


def _pack_params(w_in, sgu_w, sgu_b, mla_w_uq, mla_w_ukv, w_branch, w_out):
    w_packed = _pack_w_in(jnp.swapaxes(w_in, 1, 2))
    uq = mla_w_uq.reshape(DEPTH, MLA_Q_LORA, MLA_HEADS, MLA_NOPE + MLA_ROPE)
    uq = jnp.pad(uq, ((0, 0), (0, 0), (0, 0), (0, LANES - MLA_NOPE - MLA_ROPE)))
    uq = uq.reshape(DEPTH, MLA_Q_LORA, MLA_HEADS * LANES).astype(BF16)
    ukv = mla_w_ukv.reshape(DEPTH, MLA_KV_LORA, MLA_HEADS, MLA_NOPE + MLA_V)
    w_k = jnp.pad(ukv[..., :MLA_NOPE], ((0, 0), (0, 0), (0, 0), (0, LANES - MLA_NOPE)))
    w_k = w_k.reshape(DEPTH, MLA_KV_LORA, MLA_HEADS * LANES).astype(BF16)
    w_v = ukv[..., MLA_NOPE:].reshape(DEPTH, MLA_KV_LORA, MLA_HEADS * MLA_V).astype(BF16)
    sgu_bias = jnp.repeat(jnp.swapaxes(sgu_b, 1, 2), BRANCH_WIDTH // SGU_GROUPS, axis=2)
    return w_packed, uq, w_k, w_v, sgu_w.astype(BF16), sgu_bias, w_branch, w_out


def _rope_tables(n_tokens):
    pos = jnp.arange(n_tokens, dtype=jnp.int32)
    row = (pos // GRID_W).astype(F32)
    col = (pos % GRID_W).astype(F32)
    n_freq = MLA_ROPE // 4
    inv = ROPE_THETA ** (-jnp.arange(n_freq, dtype=F32) / n_freq)
    ang = jnp.concatenate([row[:, None] * inv, col[:, None] * inv], axis=-1)
    cos, sin = jnp.cos(ang), jnp.sin(ang)
    pad_l, pad_r = ROPE_LANE0, LANES - ROPE_LANE0 - MLA_ROPE
    cos_t = jnp.pad(jnp.concatenate([cos, cos], axis=1), ((0, 0), (pad_l, pad_r)), constant_values=1.0)
    sin_t = jnp.pad(jnp.concatenate([-sin, sin], axis=1), ((0, 0), (pad_l, pad_r)))
    return cos_t, sin_t


def kernel(x_prompt, x_sample, cache_na_k, cache_na_v, cache_mla_ckv, cache_mla_krope, c, c_ctx, norm_g, w_mod, b_mod, w_in, na_rpb, sgu_w, sgu_b, mla_q_norm, mla_w_uq, mla_kv_norm, mla_w_ukv, w_branch, w_out, final_norm_g):
    n_ctx, ctx_len, _ = x_prompt.shape
    n_lat, lat_len, _ = x_sample.shape
    past = cache_na_k.shape[2]
    assert ctx_len == TILE and lat_len % TILE == 0 and lat_len // GRID_W == 16 and past == TILE

    w_packed, w_uq, w_k, w_v, sgu_wb, sgu_bias, w_br, w_o = _pack_params(
        w_in, sgu_w, sgu_b, mla_w_uq, mla_w_ukv, w_branch, w_out)
    rope_tabs = _rope_tables(lat_len)
    tab = _na_bias_tables(na_rpb)
    norm_g3 = norm_g.reshape(DEPTH, 1, D_MODEL)
    kv_norm3 = mla_kv_norm.reshape(DEPTH, 1, MLA_KV_LORA)
    q_norm3 = mla_q_norm.reshape(DEPTH, 1, MLA_Q_LORA)
    final_g = final_norm_g.reshape(1, D_MODEL)

    cond_rows = 16
    cond = jnp.concatenate([c, c_ctx[None, :], jnp.zeros((cond_rows - n_lat - 1, D_MODEL), F32)], axis=0)
    mod = _modulation(cond, w_mod, b_mod).reshape(DEPTH, cond_rows, 1, 3 * D_MODEL)
    ctx_row = lambda token: n_lat
    lat_row = lambda token: token // lat_len

    cache_k = jnp.transpose(cache_na_k, (0, 1, 3, 4, 2)).reshape(n_lat, DEPTH, BRANCH_WIDTH, past)
    cache_v = jnp.transpose(cache_na_v, (0, 1, 3, 4, 2)).reshape(n_lat, DEPTH, BRANCH_WIDTH, past)
    cache_kr = jnp.pad(cache_mla_krope, ((0, 0), (0, 0), (0, 0), (ROPE_LANE0, LANES - ROPE_LANE0 - MLA_ROPE)))

    xp = x_prompt.reshape(n_ctx * ctx_len, D_MODEL)
    xs = x_sample.reshape(n_lat * lat_len, D_MODEL)
    state = None
    for l in range(DEPTH):
        final = l == DEPTH - 1
        p, ckv, kr, *state = _inproj(xp, mod, ctx_row, l, norm_g3, w_packed, kv_norm3, None, state, True, TILE)
        y_na = _na_ctx(p, n_ctx)
        y_mla = _mla(p, ckv, kr, None, None, l, q_norm3, w_uq, w_k, w_v, None, n_ctx, ctx_len)
        xp = _merge(xp, mod, ctx_row, l, p, y_na, y_mla, sgu_wb, sgu_bias, w_br, w_o, final_g, final, MERGE_TILE)
        p, ckv, kr = _inproj(xs, mod, lat_row, l, norm_g3, w_packed, kv_norm3, rope_tabs, None, False, TILE)
        y_na = _na_lat(p, cache_k, cache_v, tab, l, n_lat, lat_len)
        y_mla = _mla(p, ckv, kr, cache_mla_ckv, cache_kr, l, q_norm3, w_uq, w_k, w_v, rope_tabs, n_lat, lat_len)
        xs = _merge(xs, mod, lat_row, l, p, y_na, y_mla, sgu_wb, sgu_bias, w_br, w_o, final_g, final, MERGE_TILE)

    s_k, s_v, s_ckv, s_kr = state

    def heads_last(s):
        return jnp.transpose(s.reshape(n_ctx, DEPTH, NA_HEADS, NA_HEAD_DIM, ctx_len), (0, 1, 4, 2, 3))

    return (xp.reshape(n_ctx, ctx_len, D_MODEL), xs.reshape(n_lat, lat_len, D_MODEL),
            heads_last(s_k), heads_last(s_v), s_ckv, jnp.swapaxes(s_kr, 2, 3))
```

```python
import functools

import numpy as np
import jax
import jax.numpy as jnp
from jax import lax
from jax.experimental import pallas as pl
from jax.experimental.pallas import tpu as pltpu

F32 = jnp.float32
BF16 = jnp.bfloat16

D_MODEL = 1024
DEPTH = 2
GRID_W = 64
BRANCH_WIDTH = 512
N_BRANCH = 3
NA_HEADS = 8
NA_HEAD_DIM = 64
NA_WIN_H = 8
NA_WIN_W = 16
SGU_GROUPS = 4
SGU_CHUNK = 128
MLA_HEADS = 8
MLA_NOPE = 64
MLA_ROPE = 32
MLA_V = 64
MLA_Q_LORA = 384
MLA_KV_LORA = 256
ROPE_THETA = 10000.0
EPS = 1e-6
NEG_INF = -1e30

LANES = 128
TILE = 256
MERGE_TILE = 512
CTX_SEQS_PER_STEP = 4
LAT_TILES_PER_STEP = 4
LAT_SEQS_PER_STEP = 2
STRIP = 16
HEAD_PAIRS = NA_HEADS // 2
SCRATCH_SETS = 2
NA_KEY_ROWS = 12
NA_LOC_KEYS = NA_KEY_ROWS * GRID_W
ROWS_PER_TILE = TILE // GRID_W
TAB_PAD = 4
TAB_BLOCKS = 24
ROPE_LANE0 = MLA_NOPE

C_NAZ, C_SU, C_SV, C_SZ, C_MZ, C_MG = (512 * i for i in range(6))
C_NAQ = C_MG + N_BRANCH * D_MODEL
C_NAK, C_NAV, C_DQ = C_NAQ + 512, C_NAQ + 1024, C_NAQ + 1536
MERGE_CHUNKS = C_NAQ // 512
C_DKV = C_DQ + MLA_Q_LORA
C_KR = C_DKV + MLA_KV_LORA
W_COLS = C_KR + LANES
P_COLS = C_DQ + 512
CHUNK = 512

VMEM_LIMIT = 56 * 1024 * 1024


def _dot(a, b):
    return jnp.dot(a, b, preferred_element_type=F32)


def _dot_nt(a, b):
    return lax.dot_general(a, b, (((1,), (1,)), ((), ())), preferred_element_type=F32)


def _rms(x, g):
    return x * lax.rsqrt(jnp.mean(x * x, axis=-1, keepdims=True) + EPS) * g


def _center_norm(x):
    c = x - jnp.mean(x, axis=-1, keepdims=True)
    return c * lax.rsqrt(jnp.mean(c * c, axis=-1, keepdims=True) + EPS)


def _rope128(x, cos, sin):
    lane = lax.broadcasted_iota(jnp.int32, x.shape, 1)
    swapped = jnp.where(lane < ROPE_LANE0 + MLA_ROPE // 2,
                        pltpu.roll(x, LANES - MLA_ROPE // 2, axis=1),
                        pltpu.roll(x, MLA_ROPE // 2, axis=1))
    return x * cos + swapped * sin


def _store_scores(s_ref, m_ref, rows, cols, scores, first):
    s_ref[rows, cols] = scores
    block_max = jnp.broadcast_to(scores.max(axis=-1, keepdims=True), (scores.shape[0], LANES))
    m_ref[rows, :] = block_max if first else jnp.maximum(m_ref[rows, :], block_max)


def _pair_softmax_pv(s_ref, m_ref, p_ref, den_ref, segments, values):
    for i in range(s_ref.shape[0] // STRIP):
        r = slice(i * STRIP, (i + 1) * STRIP)
        m = m_ref[r, :]
        den = None
        for c0, c1 in segments:
            e = jnp.exp(s_ref[r, c0:c1] - jnp.concatenate([m] * ((c1 - c0) // LANES), axis=1))
            d = e.sum(axis=-1, keepdims=True)
            den = d if den is None else den + d
            p_ref[r, c0:c1] = e.astype(BF16)
        den_ref[r, :] = jnp.broadcast_to(den, (STRIP, LANES))
    o = None
    for (c0, c1), (v, transposed) in zip(segments, values):
        t = _dot_nt(p_ref[:, c0:c1], v) if transposed else _dot(p_ref[:, c0:c1], v)
        o = t if o is None else o + t
    o = o / den_ref[...]
    lane = lax.broadcasted_iota(jnp.int32, (TILE, LANES), 1)
    return jnp.where(lane < LANES // 2, o[:TILE], o[TILE:])


def _pair_scratch(n_keys):
    return [pltpu.VMEM((SCRATCH_SETS, 2 * TILE, n_keys), F32), pltpu.VMEM((SCRATCH_SETS, 2 * TILE, LANES), F32),
            pltpu.VMEM((SCRATCH_SETS, 2 * TILE, n_keys), BF16), pltpu.VMEM((SCRATCH_SETS, 2 * TILE, LANES), F32)]


def _const_spec(shape, index):
    return pl.BlockSpec(shape, lambda *_: index, pipeline_mode=pl.Buffered(1))


def _params(semantics):
    return pltpu.CompilerParams(dimension_semantics=semantics, vmem_limit_bytes=VMEM_LIMIT)


def _mod_kernel(c_ref, w_ref, b_ref, o_ref):
    c = c_ref[...]
    s = c * jax.nn.sigmoid(c)
    o_ref[...] = jnp.dot(s, w_ref[...], preferred_element_type=F32, precision=lax.Precision.HIGHEST) + b_ref[...]


def _modulation(cond, w_mod, b_mod):
    rows = cond.shape[0]
    return pl.pallas_call(
        _mod_kernel,
        grid=(DEPTH, 3),
        in_specs=[pl.BlockSpec((rows, D_MODEL), lambda l, j: (0, 0)),
                  pl.BlockSpec((None, D_MODEL, D_MODEL), lambda l, j: (l, 0, j)),
                  pl.BlockSpec((None, 1, D_MODEL), lambda l, j: (l, 0, j))],
        out_specs=pl.BlockSpec((None, rows, D_MODEL), lambda l, j: (l, 0, j)),
        out_shape=jax.ShapeDtypeStruct((DEPTH, rows, 3 * D_MODEL), F32),
        compiler_params=_params(("parallel", "parallel")),
        name="modulation",
    )(cond, w_mod, b_mod.reshape(DEPTH, 1, 3 * D_MODEL))


def _inproj_kernel(*refs, emit_state, rope, n_alias):
    x0_ref, x_next_ref, mod0_ref, mod_next_ref, g_ref, w_ref, kvn_ref = refs[:7]
    refs = refs[7:]
    if rope:
        cos_ref, sin_ref = refs[:2]
        refs = refs[2:]
    refs = refs[n_alias:]
    p_ref, ckv_ref, kr_ref = refs[:3]
    if emit_state:
        sk_ref, sv_ref, sckv_ref, skr_ref = refs[3:7]
    h_ref = refs[-1]
    step = pl.program_id(0)
    tm = x0_ref.shape[0]

    def normalised(x_ref, mod_ref):
        shift = mod_ref[:, 0:D_MODEL]
        scale = mod_ref[:, D_MODEL:2 * D_MODEL]
        return (_rms(x_ref[...], g_ref[...]) * (1.0 + scale) + shift).astype(BF16)

    @pl.when(step == 0)
    def _first_tile():
        h_ref[0] = normalised(x0_ref, mod0_ref)

    h = h_ref[step % 2]

    def tiles(a):
        return a.reshape(tm // TILE, TILE, a.shape[-1])

    ckv = _rms(_dot(h, w_ref[:, C_DKV:C_KR]), kvn_ref[...])
    ckv_ref[...] = ckv.astype(BF16)
    kr = _dot(h, w_ref[:, C_KR:W_COLS])
    if emit_state:
        sckv_ref[...] = ckv
        skr_ref[...] = kr.T[ROPE_LANE0:ROPE_LANE0 + MLA_ROPE, :]
    if rope:
        kr = _rope128(kr, cos_ref[...], sin_ref[...])
    kr_ref[...] = kr.astype(BF16)

    h_ref[(step + 1) % 2] = normalised(x_next_ref, mod_next_ref)

    for c0 in range(0, P_COLS, CHUNK):
        acc = _dot(h, w_ref[:, c0:c0 + CHUNK])
        if emit_state and c0 == C_NAK:
            sk_ref[...] = acc.T
        if emit_state and c0 == C_NAV:
            sv_ref[...] = acc.T
        if c0 in (C_NAZ, C_SZ, C_MZ):
            acc = acc * jax.nn.sigmoid(acc)
        elif c0 == C_SU:
            acc = jax.nn.gelu(acc)
        elif c0 == C_SV:
            acc = _center_norm(jax.nn.gelu(acc))
        elif C_MG <= c0 < C_NAQ:
            acc = jax.nn.sigmoid(acc)
        p_ref[:, c0 // CHUNK] = tiles(acc.astype(BF16))


def _inproj(x, mod, mod_row, layer, norm_g, w_packed, kv_norm, rope_tabs, state_in, emit_state, tm):
    n_tok = x.shape[0]
    n_tiles = n_tok // tm
    rope = rope_tabs is not None
    next_tile = lambda i: jnp.minimum(i + 1, n_tiles - 1)
    in_specs = [_const_spec((tm, D_MODEL), (0, 0)),
                pl.BlockSpec((tm, D_MODEL), lambda i: (next_tile(i), 0)),
                _const_spec((None, None, 1, 3 * D_MODEL), (layer, mod_row(0), 0, 0)),
                pl.BlockSpec((None, None, 1, 3 * D_MODEL), lambda i: (layer, mod_row(next_tile(i) * tm), 0, 0)),
                _const_spec((None, 1, D_MODEL), (layer, 0, 0)),
                _const_spec((None, D_MODEL, W_COLS), (layer, 0, 0)),
                _const_spec((None, 1, MLA_KV_LORA), (layer, 0, 0))]
    args = [x, x, mod, mod, norm_g, w_packed, kv_norm]
    if rope:
        tiles_per_seq = rope_tabs[0].shape[0] // tm
        in_specs += [pl.BlockSpec((tm, LANES), lambda i: (i % tiles_per_seq, 0))] * 2
        args += list(rope_tabs)
    out_specs = [pl.BlockSpec((tm // TILE, P_COLS // CHUNK, TILE, CHUNK), lambda i: (i, 0, 0, 0)),
                 pl.BlockSpec((tm, MLA_KV_LORA), lambda i: (i, 0)),
                 pl.BlockSpec((tm, LANES), lambda i: (i, 0))]
    out_shape = [jax.ShapeDtypeStruct((n_tok // TILE, P_COLS // CHUNK, TILE, CHUNK), BF16),
                 jax.ShapeDtypeStruct((n_tok, MLA_KV_LORA), BF16),
                 jax.ShapeDtypeStruct((n_tok, LANES), BF16)]
    aliases = {}
    if emit_state:
        assert tm == TILE
        for rows, cols in ((BRANCH_WIDTH, TILE), (BRANCH_WIDTH, TILE), (TILE, MLA_KV_LORA), (MLA_ROPE, TILE)):
            out_specs.append(pl.BlockSpec((None, None, rows, cols), lambda i: (i, layer, 0, 0)))
            out_shape.append(jax.ShapeDtypeStruct((n_tok // TILE, DEPTH, rows, cols), F32))
        if state_in is not None:
            first = len(args)
            in_specs += [pl.BlockSpec(memory_space=pl.ANY)] * 4
            args += list(state_in)
            aliases = {first + k: 3 + k for k in range(4)}
    kern = functools.partial(_inproj_kernel, emit_state=emit_state, rope=rope, n_alias=len(aliases))
    return pl.pallas_call(
        kern, grid=(n_tiles,), in_specs=in_specs, out_specs=out_specs, out_shape=out_shape,
        scratch_shapes=[pltpu.VMEM((2, tm, D_MODEL), BF16)],
        input_output_aliases=aliases, compiler_params=_params(("arbitrary",)),
        name="inproj",
    )(*args)


def _na_heads(q_ref, o_ref, scratch, keys_fn, values_fn, bias_fn=None):
    lane = lax.broadcasted_iota(jnp.int32, (TILE, LANES), 1)
    low = lane < NA_HEAD_DIM
    for hp in range(HEAD_PAIRS):
        sl = slice(LANES * hp, LANES * (hp + 1))
        q2 = q_ref[:, sl] * (NA_HEAD_DIM ** -0.5)
        zero = jnp.zeros_like(q2)
        q_stack = jnp.concatenate([jnp.where(low, q2, zero), jnp.where(low, zero, q2)], axis=0)
        s_pair, m_pair, p_pair, den_pair = (ref.at[hp % SCRATCH_SETS] for ref in scratch)
        segments = []
        c0 = 0
        for j, (k, transposed) in enumerate(keys_fn(sl)):
            n_keys = k.shape[1] if transposed else k.shape[0]
            scores = _dot(q_stack, k) if transposed else _dot_nt(q_stack, k)
            if j == 0 and bias_fn is not None:
                scores = scores + bias_fn(2 * hp)
            _store_scores(s_pair, m_pair, slice(None), slice(c0, c0 + n_keys), scores, j == 0)
            segments.append((c0, c0 + n_keys))
            c0 += n_keys
        o_ref[:, sl] = _pair_softmax_pv(s_pair, m_pair, p_pair, den_pair, segments, values_fn(sl)).astype(BF16)


def _na_ctx_kernel(q_ref, k_ref, v_ref, o_ref, *scratch):
    for s in range(q_ref.shape[0]):
        _na_heads(q_ref.at[s], o_ref.at[s], scratch,
                  lambda sl, s=s: [(k_ref[s, :, sl], False)], lambda sl, s=s: [(v_ref[s, :, sl], False)])


def _na_band_row0(tile_row0, n_rows):
    return min(max(tile_row0 - NA_WIN_H // 2, 0), n_rows - NA_KEY_ROWS)


def _na_band_plans(seq):
    n_rows = seq // GRID_W
    plans = []
    for t in range(seq // TILE):
        band0 = _na_band_row0(t * ROWS_PER_TILE, n_rows)
        plan = []
        for r in range(ROWS_PER_TILE):
            rq = t * ROWS_PER_TILE + r
            row_lo = min(max(rq - NA_WIN_H // 2, 0), n_rows - NA_WIN_H)
            plan.append((band0 - rq + NA_WIN_H - 1 + TAB_PAD, row_lo - band0))
        plans.append(tuple(plan))
    return tuple(plans)


def _na_lat_kernel(q_ref, k_ref, v_ref, kc_ref, vc_ref, tab_ref, o_ref, bias_ref, *scratch, plans, n_rows):
    t = pl.program_id(0)
    first_half = lax.broadcasted_iota(jnp.int32, (GRID_W, LANES), 1) < GRID_W
    masked = jnp.full((GRID_W, LANES), NEG_INF, F32)

    def block_pair(h, blk):
        if blk % 2 == 0:
            return tab_ref[h, blk // 2]
        return jnp.where(first_half, pltpu.roll(tab_ref[h, blk // 2], GRID_W, axis=1),
                         pltpu.roll(tab_ref[h, blk // 2 + 1], GRID_W, axis=1))

    for plan in sorted(set(plans)):
        is_tile = functools.reduce(jnp.logical_or, [t == i for i, p in enumerate(plans) if p == plan])

        @pl.when(jnp.logical_and(pl.program_id(1) == 0, is_tile))
        def _build_bias(plan=plan):
            for r, (blk0, j0) in enumerate(plan):
                for j in range(NA_KEY_ROWS // 2):
                    in_window = [j0 <= 2 * j + half < j0 + NA_WIN_H for half in range(2)]
                    for h in range(NA_HEADS):
                        if not any(in_window):
                            piece = masked
                        elif all(in_window):
                            piece = block_pair(h, blk0 + 2 * j)
                        else:
                            keep = first_half if in_window[0] else jnp.logical_not(first_half)
                            piece = jnp.where(keep, block_pair(h, blk0 + 2 * j), NEG_INF)
                        bias_ref[h, r * GRID_W:(r + 1) * GRID_W, LANES * j:LANES * (j + 1)] = piece

    band_tile0 = jnp.clip(t * ROWS_PER_TILE - NA_WIN_H // 2, 0, n_rows - NA_KEY_ROWS) // ROWS_PER_TILE
    for plan in sorted(set(plans)):
        first_row = min(j0 for _, j0 in plan)
        last_row = max(j0 for _, j0 in plan) + NA_WIN_H
        skip = first_row // ROWS_PER_TILE
        n_tiles = -(-last_row // ROWS_PER_TILE) - skip
        key_tiles = pl.ds(band_tile0 + skip, n_tiles)
        bias_lanes = slice(skip * TILE, (skip + n_tiles) * TILE)

        @pl.when(functools.reduce(jnp.logical_or, [t == i for i, p in enumerate(plans) if p == plan]))
        def _attend(key_tiles=key_tiles, bias_lanes=bias_lanes, n_keys=n_tiles * TILE):
            def band(ref, s, sl):
                return ref[s, key_tiles, :, sl].reshape(n_keys, LANES)

            def bias(head0):
                return bias_ref[head0:head0 + 2, :, bias_lanes].reshape(2 * TILE, n_keys)

            for s in range(q_ref.shape[0]):
                _na_heads(q_ref.at[s], o_ref.at[s], scratch,
                          lambda sl, s=s: [(band(k_ref, s, sl), False), (kc_ref[s, sl, :].astype(BF16), True)],
                          lambda sl, s=s: [(band(v_ref, s, sl), False), (vc_ref[s, sl, :].astype(BF16), True)], bias)


def _p_block(tile_of, col):
    return pl.BlockSpec((None, None, TILE, CHUNK), lambda *g: (tile_of(*g), col // CHUNK, 0, 0))


def _na_ctx(p, n_seq):
    per_step = CTX_SEQS_PER_STEP
    blk = lambda col: pl.BlockSpec((per_step, None, TILE, CHUNK), lambda i: (i, col // CHUNK, 0, 0))
    y = pl.pallas_call(
        _na_ctx_kernel, grid=(n_seq // per_step,),
        in_specs=[blk(C_NAQ), blk(C_NAK), blk(C_NAV)],
        out_specs=pl.BlockSpec((per_step, TILE, BRANCH_WIDTH), lambda i: (i, 0, 0)),
        out_shape=jax.ShapeDtypeStruct((n_seq, TILE, BRANCH_WIDTH), BF16),
        scratch_shapes=_pair_scratch(TILE),
        compiler_params=_params(("parallel",)), name="na_ctx",
    )(p, p, p)
    return y.reshape(n_seq * TILE, BRANCH_WIDTH)


def _na_lat(p, cache_k, cache_v, tab, layer, n_seq, seq):
    tiles = seq // TILE
    n_past = cache_k.shape[3]
    per_step = LAT_SEQS_PER_STEP
    p5 = p.reshape(n_seq, tiles, P_COLS // CHUNK, TILE, CHUNK)
    kv = lambda col: pl.BlockSpec((per_step, tiles, None, TILE, CHUNK), lambda t, b: (b, 0, col // CHUNK, 0, 0))
    cache = pl.BlockSpec((per_step, None, BRANCH_WIDTH, n_past), lambda t, b: (b, layer, 0, 0))
    table = _const_spec((None, NA_HEADS, TAB_BLOCKS // 2, GRID_W, LANES), (layer, 0, 0, 0, 0))
    y = pl.pallas_call(
        functools.partial(_na_lat_kernel, plans=_na_band_plans(seq), n_rows=seq // GRID_W),
        grid=(tiles, n_seq // per_step),
        in_specs=[pl.BlockSpec((per_step, None, None, TILE, CHUNK), lambda t, b: (b, t, C_NAQ // CHUNK, 0, 0)),
                  kv(C_NAK), kv(C_NAV), cache, cache, table],
        out_specs=pl.BlockSpec((per_step, None, TILE, BRANCH_WIDTH), lambda t, b: (b, t, 0, 0)),
        out_shape=jax.ShapeDtypeStruct((n_seq, tiles, TILE, BRANCH_WIDTH), BF16),
        scratch_shapes=[pltpu.VMEM((NA_HEADS, TILE, NA_LOC_KEYS), F32)]
        + _pair_scratch(NA_LOC_KEYS + n_past),
        compiler_params=_params(("parallel", "arbitrary")), name="na_lat",
    )(p5, p5, p5, cache_k, cache_v, tab)
    return y.reshape(n_seq * seq, BRANCH_WIDTH)


def _na_bias_tables(rpb):
    cq = np.arange(GRID_W)[:, None]
    ck = np.arange(GRID_W)[None, :]
    col_lo = np.clip(cq - NA_WIN_W // 2, 0, GRID_W - NA_WIN_W)
    ok = (ck >= col_lo) & (ck < col_lo + NA_WIN_W)
    n_b = 2 * NA_WIN_W - 1
    n_rel = 2 * NA_WIN_H - 1
    onehot = ((ck - cq + NA_WIN_W - 1)[:, None, :] == np.arange(n_b)[None, :, None]) & ok[:, None, :]
    expand = np.zeros((GRID_W, 2, n_b, 2, GRID_W), np.float32)
    expand[:, 0, :, 0, :] = onehot
    expand[:, 1, :, 1, :] = onehot
    expand = expand.reshape(GRID_W, 2 * n_b, LANES)
    rows = jnp.pad(rpb, ((0, 0), (0, 0), (TAB_PAD, TAB_BLOCKS - n_rel - TAB_PAD), (0, 0)))
    real = np.zeros(TAB_BLOCKS, bool)
    real[TAB_PAD:TAB_PAD + n_rel] = True
    pairs = rows.reshape(DEPTH, NA_HEADS, TAB_BLOCKS // 2, 2 * n_b)
    tab = jnp.einsum('lhek,ukc->lheuc', pairs, expand, precision=lax.Precision.HIGHEST)
    keep = real.reshape(-1, 1, 2, 1) & ok[None, :, None, :]
    return jnp.where(keep.reshape(TAB_BLOCKS // 2, GRID_W, LANES), tab, NEG_INF)


def _mla_kernel(*refs, n_cache, n_lat, rope, seqs, tiled):
    dq_ref, ckv_ref, kr_ref = refs[:3]
    refs = refs[3:]
    if n_cache:
        cckv_ref, ckr_ref = refs[:2]
        refs = refs[2:]
    qn_ref, wuq_ref, wk_ref, wv_ref = refs[:4]
    refs = refs[4:]
    if rope:
        cos_ref, sin_ref = refs[:2]
        refs = refs[2:]
    o_ref, kx_all, vx_all = refs[:3]
    scratch = refs[3:]
    scale = (MLA_NOPE + MLA_ROPE) ** -0.5
    n_keys = kx_all.shape[1]

    for s in range(seqs):
        key_set = 0 if tiled else s
        kx_ref, vx_ref = kx_all.at[key_set], vx_all.at[key_set]

        def expand_keys(key_set=key_set, kx_ref=kx_ref, vx_ref=vx_ref):
            def fill(r0, ckv, kr):
                n = ckv.shape[0]
                kk = _dot(ckv, wk_ref[...])
                for h in range(MLA_HEADS):
                    sl = slice(LANES * h, LANES * (h + 1))
                    kx_ref[r0:r0 + n, sl] = (kk[:, sl] + kr).astype(BF16)
                vx_ref[r0:r0 + n, :] = _dot(ckv, wv_ref[...]).astype(BF16)

            if n_cache:
                fill(0, cckv_ref[...].astype(BF16), ckr_ref[...])
            for r0 in range(key_set * n_lat, (key_set + 1) * n_lat, TILE):
                fill(n_cache + r0 - key_set * n_lat, ckv_ref[r0:r0 + TILE, :], kr_ref[r0:r0 + TILE, :].astype(F32))

        if not tiled:
            expand_keys()
        elif s == 0:
            pl.when(pl.program_id(1) == 0)(expand_keys)

        tile_rows = slice(s * TILE, (s + 1) * TILE)
        dqn = _rms(dq_ref[s, :, :MLA_Q_LORA].astype(F32), qn_ref[...])
        q = _dot(dqn.astype(BF16), wuq_ref[...])
        for hp in range(HEAD_PAIRS):
            s_pair, m_pair, p_pair, den_pair = (ref.at[hp % SCRATCH_SETS] for ref in scratch)
            for half in range(2):
                sl = slice(LANES * (2 * hp + half), LANES * (2 * hp + half + 1))
                qh = q[:, sl]
                if rope:
                    qh = _rope128(qh, cos_ref[tile_rows, :], sin_ref[tile_rows, :])
                qh = (qh * scale).astype(BF16)
                _store_scores(s_pair, m_pair, slice(half * TILE, (half + 1) * TILE), slice(None),
                              _dot_nt(qh, kx_ref[:, sl]), True)
            v2 = vx_ref[:, LANES * hp:LANES * (hp + 1)]
            o_ref[tile_rows, LANES * hp:LANES * (hp + 1)] = _pair_softmax_pv(
                s_pair, m_pair, p_pair, den_pair, [(0, n_keys)], [(v2, False)]).astype(BF16)


def _mla(p, ckv, kr, cache_ckv, cache_kr, layer, q_norm, w_uq, w_k, w_v, rope_tabs, n_seq, seq):
    tiled = seq > TILE
    seqs = LAT_TILES_PER_STEP if tiled else CTX_SEQS_PER_STEP
    tiles = seq // (seqs * TILE) if tiled else 1
    key_rows = seq if tiled else seqs * seq
    n_cache = 0 if cache_ckv is None else cache_ckv.shape[2]
    rope = rope_tabs is not None
    in_specs = [pl.BlockSpec((seqs, None, TILE, CHUNK), lambda b, t: (b * tiles + t, C_DQ // CHUNK, 0, 0)),
                pl.BlockSpec((key_rows, MLA_KV_LORA), lambda b, t: (b, 0)),
                pl.BlockSpec((key_rows, LANES), lambda b, t: (b, 0))]
    args = [p, ckv, kr]
    if n_cache:
        in_specs += [pl.BlockSpec((None, None, n_cache, MLA_KV_LORA), lambda b, t: (b, layer, 0, 0)),
                     pl.BlockSpec((None, None, n_cache, LANES), lambda b, t: (b, layer, 0, 0))]
        args += [cache_ckv, cache_kr]
    in_specs += [_const_spec((None, 1, MLA_Q_LORA), (layer, 0, 0)),
                 _const_spec((None, MLA_Q_LORA, MLA_HEADS * LANES), (layer, 0, 0)),
                 _const_spec((None, MLA_KV_LORA, MLA_HEADS * LANES), (layer, 0, 0)),
                 _const_spec((None, MLA_KV_LORA, MLA_HEADS * MLA_V), (layer, 0, 0))]
    args += [q_norm, w_uq, w_k, w_v]
    if rope:
        in_specs += [pl.BlockSpec((seqs * TILE, LANES), lambda b, t: (t, 0))] * 2
        args += list(rope_tabs)
    n_keys = n_cache + seq
    key_sets = 1 if tiled else seqs
    return pl.pallas_call(
        functools.partial(_mla_kernel, n_cache=n_cache, n_lat=seq, rope=rope, seqs=seqs, tiled=tiled),
        grid=(n_seq if tiled else n_seq // seqs, tiles), in_specs=in_specs,
        out_specs=pl.BlockSpec((seqs * TILE, BRANCH_WIDTH), lambda b, t: (b * tiles + t, 0)),
        out_shape=jax.ShapeDtypeStruct((n_seq * seq, BRANCH_WIDTH), BF16),
        scratch_shapes=[pltpu.VMEM((key_sets, n_keys, MLA_HEADS * LANES), BF16),
                        pltpu.VMEM((key_sets, n_keys, MLA_HEADS * MLA_V), BF16)] + _pair_scratch(n_keys),
        compiler_params=_params(("parallel", "arbitrary")), name="mla",
    )(*args)


def _merge_kernel(x_ref, mod_ref, yna_ref, ymla_ref, p_hbm, sw_ref, sb_ref, wb32_ref, wo32_ref, fg_ref, o_ref,
                  wb_ref, wo_ref, pm_ring, pm_sem, *, final, n_steps):
    tm = x_ref.shape[0]
    step = pl.program_id(0)
    tiles = tm // TILE

    def fetch(s, slot):
        return pltpu.make_async_copy(p_hbm.at[pl.ds(s * tiles, tiles), pl.ds(0, MERGE_CHUNKS)],
                                     pm_ring.at[slot], pm_sem.at[slot])

    @pl.when(step == 0)
    def _prime():
        fetch(0, 0).start()
        if n_steps > 1:
            fetch(1, 1).start()

    @pl.when(step + 2 < n_steps)
    def _prefetch():
        fetch(step + 2, (step + 2) % 3).start()

    fetch(step, step % 3).wait()
    pm_ref = pm_ring.at[step % 3]

    @pl.when(step == 0)
    def _cast_weights():
        wb_ref[...] = wb32_ref[...].astype(BF16)
        wo_ref[...] = wo32_ref[...].astype(BF16)

    def tokens(col):
        return pm_ref[:, col // CHUNK].reshape(tm, CHUNK)

    sv = tokens(C_SV)
    rows = []
    for c0 in range(0, tm, SGU_CHUNK):
        cols = [_dot(sw_ref[g], sv[c0:c0 + SGU_CHUNK, LANES * g:LANES * (g + 1)]) for g in range(SGU_GROUPS)]
        rows.append(jnp.concatenate(cols, axis=1) + sb_ref[...])
    y_sgu = tokens(C_SU).astype(F32) * jnp.concatenate(rows, axis=0)

    gated = (yna_ref[...] * tokens(C_NAZ),
             (y_sgu * tokens(C_SZ).astype(F32)).astype(BF16),
             ymla_ref[...] * tokens(C_MZ))
    merged = None
    for k in range(N_BRANCH):
        gate_k = jnp.concatenate([tokens(C_MG + k * D_MODEL + c) for c in range(0, D_MODEL, CHUNK)], axis=1)
        term = gate_k.astype(F32) * _dot(gated[k], wb_ref[k])
        merged = term if merged is None else merged + term
    out = _dot(merged.astype(BF16), wo_ref[...])
    gate = mod_ref[:, 2 * D_MODEL:3 * D_MODEL]
    xn = x_ref[...] + gate * out
    if final:
        xn = _rms(xn, fg_ref[...])
    o_ref[...] = xn


def _merge(x, mod, mod_row, layer, p, y_na, y_mla, sgu_w, sgu_b, w_branch, w_out, final_g, final, tm):
    n_tok = x.shape[0]
    half = pl.BlockSpec((tm, BRANCH_WIDTH), lambda i: (i, 0))
    full = lambda c: pl.BlockSpec((tm, D_MODEL), lambda i: (i, c))
    in_specs = [full(0),
                pl.BlockSpec((None, None, 1, 3 * D_MODEL), lambda i: (layer, mod_row(i * tm), 0, 0)),
                half, half,
                pl.BlockSpec(memory_space=pl.ANY),
                _const_spec((None, SGU_GROUPS, SGU_CHUNK, SGU_CHUNK), (layer, 0, 0, 0)),
                _const_spec((None, SGU_CHUNK, BRANCH_WIDTH), (layer, 0, 0)),
                _const_spec((None, N_BRANCH, BRANCH_WIDTH, D_MODEL), (layer, 0, 0, 0)),
                _const_spec((None, D_MODEL, D_MODEL), (layer, 0, 0)),
                _const_spec((1, D_MODEL), (0, 0))]
    return pl.pallas_call(
        functools.partial(_merge_kernel, final=final, n_steps=n_tok // tm),
        grid=(n_tok // tm,), in_specs=in_specs,
        out_specs=full(0), out_shape=jax.ShapeDtypeStruct((n_tok, D_MODEL), F32),
        scratch_shapes=[pltpu.VMEM((N_BRANCH, BRANCH_WIDTH, D_MODEL), BF16), pltpu.VMEM((D_MODEL, D_MODEL), BF16),
                        pltpu.VMEM((3, tm // TILE, MERGE_CHUNKS, TILE, CHUNK), BF16), pltpu.SemaphoreType.DMA((3,))],
        compiler_params=_params(("arbitrary",)), name="merge",
    )(x, mod, y_na, y_mla, p, sgu_w, sgu_b, w_branch, w_out, final_g)


SRC_DQ = 7 * BRANCH_WIDTH
SRC_DKV = SRC_DQ + MLA_Q_LORA
SRC_KR = SRC_DKV + MLA_KV_LORA
SRC_MZ = SRC_KR + MLA_ROPE
SRC_NAZ = 3 * BRANCH_WIDTH
PACK_SHIFT = SRC_MZ % CHUNK
J_SHIFTED = C_MZ // CHUNK
J_Q = C_NAQ // CHUNK
J_DQ = C_DQ // CHUNK
assert SRC_NAZ % CHUNK == 0 and SRC_DQ % CHUNK == 0
assert SRC_DQ + CHUNK + (C_KR - P_COLS) == SRC_KR and SRC_KR + MLA_ROPE == SRC_MZ


PACK_SHIFTED = J_Q - J_SHIFTED
PACK_TAIL_SRC = (SRC_MZ - PACK_SHIFT) // CHUNK
assert PACK_TAIL_SRC == SRC_DQ // CHUNK + 1


def _pack_w_in_kernel(c_ref, o_ref, carry_ref):
    k = pl.program_id(1)

    def emit(rows):
        o_ref[...] = rows.T.astype(BF16)

    @pl.when(k == 0)
    def _tail():
        n_dkv = C_KR - P_COLS
        zeros = lambda n: jnp.zeros((n, D_MODEL), F32)
        emit(jnp.concatenate([c_ref[:n_dkv, :], zeros(ROPE_LANE0), c_ref[n_dkv:n_dkv + MLA_ROPE, :],
                              zeros(LANES - ROPE_LANE0 - MLA_ROPE), zeros(CHUNK - n_dkv - LANES)], axis=0))
        carry_ref[...] = c_ref[...]

    @pl.when(jnp.logical_and(k >= 1, k <= PACK_SHIFTED))
    def _shifted():
        emit(jnp.concatenate([carry_ref[PACK_SHIFT:, :], c_ref[:PACK_SHIFT, :]], axis=0))
        carry_ref[...] = c_ref[...]

    @pl.when(k > PACK_SHIFTED)
    def _aligned():
        emit(c_ref[...])


def _pack_w_in(w_in_t):
    n_chunks = pl.cdiv(W_COLS, CHUNK)
    assert n_chunks == J_DQ + 2
    first_plain = PACK_SHIFTED + 1
    first_q = first_plain + J_SHIFTED
    last = n_chunks - 1

    def dst_chunk(k):
        return jnp.where(k == 0, J_DQ + 1,
                         jnp.where(k < first_plain, k - 1 + J_SHIFTED,
                                   jnp.where(k < first_q, k - first_plain,
                                             jnp.where(k < last, k - first_q + J_Q, J_DQ))))

    def src_chunk(k):
        return jnp.where(k < first_plain, k + PACK_TAIL_SRC,
                         jnp.where(k < first_q, k - first_plain + SRC_NAZ // CHUNK,
                                   jnp.where(k < last, k - first_q, SRC_DQ // CHUNK)))

    return pl.pallas_call(
        _pack_w_in_kernel, grid=(DEPTH, n_chunks),
        in_specs=[pl.BlockSpec((None, CHUNK, D_MODEL), lambda l, k: (l, src_chunk(k), 0))],
        out_specs=pl.BlockSpec((None, D_MODEL, CHUNK), lambda l, k: (l, 0, dst_chunk(k))),
        out_shape=jax.ShapeDtypeStruct((DEPTH, D_MODEL, W_COLS), BF16),
        scratch_shapes=[pltpu.VMEM((CHUNK, D_MODEL), F32)],
        compiler_params=_params(("parallel", "arbitrary")), name="pack_w_in",
    )(w_in_t)


def _pack_params(w_in, sgu_w, sgu_b, mla_w_uq, mla_w_ukv, w_branch, w_out):
    w_packed = _pack_w_in(jnp.swapaxes(w_in, 1, 2))
    uq = mla_w_uq.reshape(DEPTH, MLA_Q_LORA, MLA_HEADS, MLA_NOPE + MLA_ROPE)
    uq = jnp.pad(uq, ((0, 0), (0, 0), (0, 0), (0, LANES - MLA_NOPE - MLA_ROPE)))
    uq = uq.reshape(DEPTH, MLA_Q_LORA, MLA_HEADS * LANES).astype(BF16)
    ukv = mla_w_ukv.reshape(DEPTH, MLA_KV_LORA, MLA_HEADS, MLA_NOPE + MLA_V)
    w_k = jnp.pad(ukv[..., :MLA_NOPE], ((0, 0), (0, 0), (0, 0), (0, LANES - MLA_NOPE)))
    w_k = w_k.reshape(DEPTH, MLA_KV_LORA, MLA_HEADS * LANES).astype(BF16)
    w_v = ukv[..., MLA_NOPE:].reshape(DEPTH, MLA_KV_LORA, MLA_HEADS * MLA_V).astype(BF16)
    sgu_bias = jnp.repeat(jnp.swapaxes(sgu_b, 1, 2), BRANCH_WIDTH // SGU_GROUPS, axis=2)
    return w_packed, uq, w_k, w_v, sgu_w.astype(BF16), sgu_bias, w_branch, w_out


def _rope_tables(n_tokens):
    pos = jnp.arange(n_tokens, dtype=jnp.int32)
    row = (pos // GRID_W).astype(F32)
    col = (pos % GRID_W).astype(F32)
    n_freq = MLA_ROPE // 4
    inv = ROPE_THETA ** (-jnp.arange(n_freq, dtype=F32) / n_freq)
    ang = jnp.concatenate([row[:, None] * inv, col[:, None] * inv], axis=-1)
    cos, sin = jnp.cos(ang), jnp.sin(ang)
    pad_l, pad_r = ROPE_LANE0, LANES - ROPE_LANE0 - MLA_ROPE
    cos_t = jnp.pad(jnp.concatenate([cos, cos], axis=1), ((0, 0), (pad_l, pad_r)), constant_values=1.0)
    sin_t = jnp.pad(jnp.concatenate([-sin, sin], axis=1), ((0, 0), (pad_l, pad_r)))
    return cos_t, sin_t


def kernel(x_prompt, x_sample, cache_na_k, cache_na_v, cache_mla_ckv, cache_mla_krope, c, c_ctx, norm_g, w_mod, b_mod, w_in, na_rpb, sgu_w, sgu_b, mla_q_norm, mla_w_uq, mla_kv_norm, mla_w_ukv, w_branch, w_out, final_norm_g):
    n_ctx, ctx_len, _ = x_prompt.shape
    n_lat, lat_len, _ = x_sample.shape
    past = cache_na_k.shape[2]
    assert ctx_len == TILE and lat_len % TILE == 0 and lat_len // GRID_W == 16 and past == TILE

    w_packed, w_uq, w_k, w_v, sgu_wb, sgu_bias, w_br, w_o = _pack_params(
        w_in, sgu_w, sgu_b, mla_w_uq, mla_w_ukv, w_branch, w_out)
    rope_tabs = _rope_tables(lat_len)
    tab = _na_bias_tables(na_rpb)
    norm_g3 = norm_g.reshape(DEPTH, 1, D_MODEL)
    kv_norm3 = mla_kv_norm.reshape(DEPTH, 1, MLA_KV_LORA)
    q_norm3 = mla_q_norm.reshape(DEPTH, 1, MLA_Q_LORA)
    final_g = final_norm_g.reshape(1, D_MODEL)

    cond_rows = 16
    cond = jnp.concatenate([c, c_ctx[None, :], jnp.zeros((cond_rows - n_lat - 1, D_MODEL), F32)], axis=0)
    mod = _modulation(cond, w_mod, b_mod).reshape(DEPTH, cond_rows, 1, 3 * D_MODEL)
    ctx_row = lambda token: n_lat
    lat_row = lambda token: token // lat_len

    cache_k = jnp.transpose(cache_na_k, (0, 1, 3, 4, 2)).reshape(n_lat, DEPTH, BRANCH_WIDTH, past)
    cache_v = jnp.transpose(cache_na_v, (0, 1, 3, 4, 2)).reshape(n_lat, DEPTH, BRANCH_WIDTH, past)
    cache_kr = jnp.pad(cache_mla_krope, ((0, 0), (0, 0), (0, 0), (ROPE_LANE0, LANES - ROPE_LANE0 - MLA_ROPE)))

    xp = x_prompt.reshape(n_ctx * ctx_len, D_MODEL)
    xs = x_sample.reshape(n_lat * lat_len, D_MODEL)
    state = None
    for l in range(DEPTH):
        final = l == DEPTH - 1
        p, ckv, kr, *state = _inproj(xp, mod, ctx_row, l, norm_g3, w_packed, kv_norm3, None, state, True, TILE)
        y_na = _na_ctx(p, n_ctx)
        y_mla = _mla(p, ckv, kr, None, None, l, q_norm3, w_uq, w_k, w_v, None, n_ctx, ctx_len)
        xp = _merge(xp, mod, ctx_row, l, p, y_na, y_mla, sgu_wb, sgu_bias, w_br, w_o, final_g, final, MERGE_TILE)
        p, ckv, kr = _inproj(xs, mod, lat_row, l, norm_g3, w_packed, kv_norm3, rope_tabs, None, False, TILE)
        y_na = _na_lat(p, cache_k, cache_v, tab, l, n_lat, lat_len)
        y_mla = _mla(p, ckv, kr, cache_mla_ckv, cache_kr, l, q_norm3, w_uq, w_k, w_v, rope_tabs, n_lat, lat_len)
        xs = _merge(xs, mod, lat_row, l, p, y_na, y_mla, sgu_wb, sgu_bias, w_br, w_o, final_g, final, MERGE_TILE)

    s_k, s_v, s_ckv, s_kr = state

    def heads_last(s):
        return jnp.transpose(s.reshape(n_ctx, DEPTH, NA_HEADS, NA_HEAD_DIM, ctx_len), (0, 1, 4, 2, 3))

    return (xp.reshape(n_ctx, ctx_len, D_MODEL), xs.reshape(n_lat, lat_len, D_MODEL),
            heads_last(s_k), heads_last(s_v), s_ckv, jnp.swapaxes(s_kr, 2, 3))
```
